```python
import jax
import jax.numpy as jnp
from jax import lax
import numpy as np

D_MODEL = 1024
BATCH = 4
SEQ = 4096
DEPTH = 1
DEC_BATCH = 128
DEC_SEQ = 1
PAST_LEN = 2048
PAGE_SIZE = 128

HEAD_DIM = 64
DN_HEADS = 8
NSA_HEADS = 8
NSA_KV_HEADS = 2
NSA_GROUP = NSA_HEADS // NSA_KV_HEADS
DN_WIDTH = DN_HEADS * HEAD_DIM
NSA_WIDTH = NSA_HEADS * HEAD_DIM
MIX_WIDTH = DN_WIDTH + NSA_WIDTH
CONV_WIDTH = 4
DN_CONV_DIM = 3 * DN_WIDTH
DN_CHUNK = 64
CMP_BLOCK = 64
SEL_BLOCK = 64
TOP_N = 16
N_LOCAL_BLOCKS = 2
WINDOW = 512
CMP_HIDDEN = 128
Q_BLOCK = 128
NSA_KV_KINDS = 6
MEM_LEN = 256
MEM_HEADS = 4
MEM_HEAD_DIM = D_MODEL // MEM_HEADS
D_FF = 4 * D_MODEL
RMS_EPS = 1e-6
FORCE_SCORE = 1e3
NEG_INF = -1e30
ATTN_SCALE = HEAD_DIM ** -0.5

OFF_DN_QKV = 0
OFF_DN_Z = OFF_DN_QKV + DN_CONV_DIM
OFF_DN_B = OFF_DN_Z + DN_WIDTH
OFF_DN_A = OFF_DN_B + DN_HEADS
OFF_NSA_Q = OFF_DN_A + DN_HEADS
OFF_NSA_KV = OFF_NSA_Q + NSA_WIDTH
OFF_NSA_G = OFF_NSA_KV + NSA_KV_KINDS * NSA_KV_HEADS * HEAD_DIM
IN_COLS = OFF_NSA_G + 3 * NSA_HEADS

kernel_name = 'hymba_deltanet_nsa_decode_step'


def rms_norm(x, gain):
    xf = x.astype(jnp.float32)
    y = xf * lax.rsqrt(jnp.mean(xf * xf, axis=-1, keepdims=True) + RMS_EPS)
    return (y * gain.astype(jnp.float32)).astype(x.dtype)


def masked_softmax(s, mask):
    s = jnp.where(mask, s.astype(jnp.float32), NEG_INF)
    p = jnp.where(mask, jnp.exp(s - jnp.max(s, axis=-1, keepdims=True)), 0.0)
    return p / jnp.maximum(jnp.sum(p, axis=-1, keepdims=True), 1e-30)


def l2_normalize(x):
    return x * lax.rsqrt(jnp.sum(x * x, axis=-1, keepdims=True) + 1e-6)


def causal_depthwise_conv(x_ext, w):
    return lax.conv_general_dilated(
        x_ext, w[:, None, :].astype(x_ext.dtype), window_strides=(1,), padding='VALID',
        dimension_numbers=('NWC', 'WIO', 'NWC'), feature_group_count=x_ext.shape[-1])


def gated_delta_chunked(q, k, v, g, beta, s0):
    B, T, H, _ = q.shape
    C = min(DN_CHUNK, T)
    n = -(-T // C)
    pad = n * C - T

    def to_chunks(a):
        a = jnp.pad(a, [(0, 0), (0, pad)] + [(0, 0)] * (a.ndim - 2))
        a = a.reshape((B, n, C) + a.shape[2:])
        return jnp.swapaxes(jnp.moveaxis(a, 1, 0), 2, 3)

    qc, kc, vc, gc, bc = (to_chunks(a) for a in (q, k, v, g, beta))
    gcum = jnp.cumsum(gc, axis=-1)
    pos = jnp.arange(C)
    incl = pos[:, None] >= pos[None, :]
    strict = pos[:, None] > pos[None, :]
    decay = jnp.exp(jnp.where(incl, gcum[..., :, None] - gcum[..., None, :], NEG_INF))
    kb = kc * bc[..., None]
    a = jnp.where(strict, jnp.einsum('nbhid,nbhjd->nbhij', kb, kc) * decay, 0.0)
    m = a + jnp.eye(C, dtype=a.dtype)
    u = lax.linalg.triangular_solve(m, vc * bc[..., None], left_side=True, lower=True, unit_diagonal=True)
    w = lax.linalg.triangular_solve(m, kb * jnp.exp(gcum)[..., None], left_side=True, lower=True,
                                    unit_diagonal=True)
    aqk = jnp.where(incl, jnp.einsum('nbhid,nbhjd->nbhij', qc, kc) * decay, 0.0)

    def step(S, inp):
        q_i, k_i, u_i, w_i, g_i, aqk_i = inp
        v_new = u_i - jnp.einsum('bhck,bhkv->bhcv', w_i, S)
        o_i = (jnp.einsum('bhck,bhkv->bhcv', q_i * jnp.exp(g_i)[..., None], S)
               + jnp.einsum('bhij,bhjv->bhiv', aqk_i, v_new))
        g_last = g_i[..., -1:]
        S = (S * jnp.exp(g_last)[..., None]
             + jnp.einsum('bhck,bhcv->bhkv', k_i * jnp.exp(g_last - g_i)[..., None], v_new))
        return S, o_i

    s_final, o = lax.scan(step, s0, (qc, kc, u, w, gcum, aqk))
    o = jnp.moveaxis(jnp.swapaxes(o, 2, 3), 0, 1).reshape(B, n * C, H, -1)[:, :T]
    return o, s_final


def deltanet_mixer(p_qkv, p_z, p_b, p_a, conv_state, rec_state, conv_w, a_log, dt_bias, norm_gain):
    B, T, _ = p_qkv.shape
    x_ext = jnp.concatenate([conv_state.astype(p_qkv.dtype), p_qkv], axis=1)
    c = jax.nn.silu(causal_depthwise_conv(x_ext, conv_w).astype(jnp.float32))
    c = c.reshape(B, T, 3, DN_HEADS, HEAD_DIM)
    q = l2_normalize(c[:, :, 0]) * HEAD_DIM ** -0.5
    k = l2_normalize(c[:, :, 1])
    v = c[:, :, 2]
    beta = jax.nn.sigmoid(p_b.astype(jnp.float32))
    g = -jnp.exp(a_log.astype(jnp.float32)) * jax.nn.softplus(
        p_a.astype(jnp.float32) + dt_bias.astype(jnp.float32))
    o, new_rec = gated_delta_chunked(q, k, v, g, beta, rec_state.astype(jnp.float32))
    z = p_z.astype(jnp.float32).reshape(B, T, DN_HEADS, HEAD_DIM)
    o = rms_norm(o, norm_gain) * jax.nn.silu(z)
    return o.reshape(B, T, DN_WIDTH).astype(p_qkv.dtype), x_ext[:, -(CONV_WIDTH - 1):], new_rec


def compress_blocks(k, pe, w1, b1, w2):
    B, T, G, dh = k.shape
    nb = T // CMP_BLOCK
    kb = k[:, :nb * CMP_BLOCK].reshape(B, nb, CMP_BLOCK, G, dh) + pe[:, None, :]
    flat = kb.transpose(0, 1, 3, 2, 4).reshape(B, nb, G, CMP_BLOCK * dh)
    return jax.nn.relu(flat @ w1 + b1) @ w2


def slc_win_block(qg_t, q_pos, sel_idx, kv_slc, kv_win, win_pos):
    B, G, Sq, n = sel_idx.shape
    bi = jnp.arange(B)[:, None, None, None]
    gi = jnp.arange(G)[None, :, None, None]
    kv = kv_slc[bi, gi, sel_idx].reshape(B, G, Sq, n * SEL_BLOCK, 2, HEAD_DIM)
    kpos = (sel_idx[..., None] * SEL_BLOCK + jnp.arange(SEL_BLOCK)).reshape(B, G, Sq, n * SEL_BLOCK)
    s = jnp.einsum('bgrqd,bgqkd->bgrqk', qg_t, kv[..., 0, :], preferred_element_type=jnp.float32) * ATTN_SCALE
    p = masked_softmax(s, (kpos <= q_pos[None, None, :, None])[:, :, None])
    o_slc = jnp.einsum('bgrqk,bgqkd->bqgrd', p, kv[..., 1, :].astype(jnp.float32))
    s = jnp.einsum('bgrqd,bkgd->bgrqk', qg_t, kv_win[:, :, 0], preferred_element_type=jnp.float32) * ATTN_SCALE
    dpos = q_pos[:, None] - win_pos[None, :]
    wmask = (dpos >= 0) & (dpos < WINDOW) & (win_pos >= 0)[None, :]
    p = masked_softmax(s, wmask)
    o_win = jnp.einsum('bgrqk,bkgd->bqgrd', p, kv_win[:, :, 1].astype(jnp.float32))
    return o_slc, o_win


def nsa_mixer(q, gate_logits, kv_all, win_ctx, q_pos, win_pos, banded, cmp_pe, cmp_w1, cmp_b1, cmp_w2):
    B, Sq, _, _ = q.shape
    T = kv_all.shape[1]
    k_cmp = compress_blocks(kv_all[:, :, 0], cmp_pe[0], cmp_w1[0], cmp_b1[0], cmp_w2[0])
    v_cmp = compress_blocks(kv_all[:, :, 1], cmp_pe[1], cmp_w1[1], cmp_b1[1], cmp_w2[1])
    nb = k_cmp.shape[1]
    qg = q.reshape(B, Sq, NSA_KV_HEADS, NSA_GROUP, HEAD_DIM)
    s = jnp.einsum('bqgrd,bngd->bgrqn', qg, k_cmp, preferred_element_type=jnp.float32) * ATTN_SCALE
    blk_end = (jnp.arange(nb) + 1) * CMP_BLOCK - 1
    p = masked_softmax(s, blk_end[None, :] <= q_pos[:, None])
    o_cmp = jnp.einsum('bgrqn,bngd->bqgrd', p, v_cmp.astype(jnp.float32))
    n_sel = -(-T // SEL_BLOCK)
    imp = jnp.pad(p.sum(axis=2), ((0, 0), (0, 0), (0, 0), (0, n_sel - nb)))
    blk = jnp.arange(n_sel)[None, :]
    cur = (q_pos // SEL_BLOCK)[:, None]
    valid = blk <= cur
    forced = valid & ((blk == 0) | (cur - blk < N_LOCAL_BLOCKS))
    score = jnp.where(valid, imp + jnp.where(forced, FORCE_SCORE, 0.0), -1.0)
    _, sel_idx = lax.top_k(score, min(TOP_N, n_sel))
    kv_slc = jnp.pad(kv_all[:, :, 2:4], ((0, 0), (0, n_sel * SEL_BLOCK - T), (0, 0), (0, 0), (0, 0)))
    kv_slc = kv_slc.reshape(B, n_sel, SEL_BLOCK, 2, NSA_KV_HEADS, HEAD_DIM).transpose(0, 4, 1, 2, 3, 5)
    qg_t = qg.transpose(0, 2, 3, 1, 4)
    if banded:
        nqb = Sq // Q_BLOCK
        kv_win_pad = jnp.pad(win_ctx, ((0, 0), (WINDOW, 0), (0, 0), (0, 0), (0, 0)))

        def body(args):
            qb, posb, idxb, start = args
            kvw = lax.dynamic_slice_in_dim(kv_win_pad, start, WINDOW + Q_BLOCK, axis=1)
            wpos = start - WINDOW + jnp.arange(WINDOW + Q_BLOCK)
            return slc_win_block(qb, posb, idxb, kv_slc, kvw, wpos)

        qb = qg_t.reshape(B, NSA_KV_HEADS, NSA_GROUP, nqb, Q_BLOCK, HEAD_DIM).transpose(3, 0, 1, 2, 4, 5)
        posb = q_pos.reshape(nqb, Q_BLOCK)
        idxb = sel_idx.reshape(B, NSA_KV_HEADS, nqb, Q_BLOCK, -1).transpose(2, 0, 1, 3, 4)
        starts = jnp.arange(nqb) * Q_BLOCK
        o_slc, o_win = lax.map(body, (qb, posb, idxb, starts))
        o_slc = jnp.moveaxis(o_slc, 0, 1).reshape(B, Sq, NSA_KV_HEADS, NSA_GROUP, HEAD_DIM)
        o_win = jnp.moveaxis(o_win, 0, 1).reshape(B, Sq, NSA_KV_HEADS, NSA_GROUP, HEAD_DIM)
    else:
        o_slc, o_win = slc_win_block(qg_t, q_pos, sel_idx, kv_slc, win_ctx, win_pos)
    gates = jax.nn.sigmoid(gate_logits.astype(jnp.float32)).reshape(B, Sq, NSA_KV_HEADS, NSA_GROUP, 3)
    o = gates[..., 0:1] * o_cmp + gates[..., 1:2] * o_slc + gates[..., 2:3] * o_win
    return o.reshape(B, Sq, NSA_WIDTH).astype(q.dtype)


def mem_cross_attn(h, mem_kv, w_q, w_o):
    B, T, _ = h.shape
    q = (h @ w_q).reshape(B, T, MEM_HEADS, MEM_HEAD_DIM)
    s = jnp.einsum('bthd,bmhd->bhtm', q, mem_kv[:, :, 0], preferred_element_type=jnp.float32) * MEM_HEAD_DIM ** -0.5
    p = jax.nn.softmax(s, axis=-1)
    o = jnp.einsum('bhtm,bmhd->bthd', p, mem_kv[:, :, 1].astype(jnp.float32))
    return o.reshape(B, T, MEM_HEADS * MEM_HEAD_DIM).astype(h.dtype) @ w_o


def trunk_layer(x, mem_kv, conv_state, rec_state, kv_past, win_past, q_pos, win_pos, win_keep, banded,
                ln_mix, w_in, dn_conv_w, dn_a_log, dn_dt_bias, dn_norm, cmp_pe, cmp_w1, cmp_b1, cmp_w2,
                w_out, ln_mem, w_mem_q, w_mem_o, ln_ffn, w_up, w_down):
    B, T, _ = x.shape
    proj = rms_norm(x, ln_mix) @ w_in
    dn_out, new_conv, new_rec = deltanet_mixer(
        proj[..., OFF_DN_QKV:OFF_DN_Z], proj[..., OFF_DN_Z:OFF_DN_B], proj[..., OFF_DN_B:OFF_DN_A],
        proj[..., OFF_DN_A:OFF_NSA_Q], conv_state, rec_state, dn_conv_w, dn_a_log, dn_dt_bias, dn_norm)
    q = proj[..., OFF_NSA_Q:OFF_NSA_KV].reshape(B, T, NSA_HEADS, HEAD_DIM)
    kv_new = proj[..., OFF_NSA_KV:OFF_NSA_G].reshape(B, T, NSA_KV_KINDS, NSA_KV_HEADS, HEAD_DIM)
    kv_rows = kv_new[:, :, :4]
    win_rows = kv_new[:, :, 4:]
    kv_all = kv_rows if kv_past is None else jnp.concatenate([kv_past.astype(kv_rows.dtype), kv_rows], axis=1)
    win_ctx = win_rows if win_past is None else jnp.concatenate([win_past.astype(win_rows.dtype), win_rows], axis=1)
    nsa_out = nsa_mixer(q, proj[..., OFF_NSA_G:], kv_all, win_ctx, q_pos, win_pos, banded,
                        cmp_pe, cmp_w1, cmp_b1, cmp_w2)
    x = x + jnp.concatenate([dn_out, nsa_out], axis=-1) @ w_out
    x = x + mem_cross_attn(rms_norm(x, ln_mem), mem_kv, w_mem_q, w_mem_o)
    x = x + jnp.square(jax.nn.relu(rms_norm(x, ln_ffn) @ w_up)) @ w_down
    return x, new_conv, new_rec, kv_rows, win_ctx[:, win_ctx.shape[1] - win_keep:]


def setup_inputs(seed: int = 0) -> dict:
    key = jax.random.key(seed)
    ks = jax.random.split(key, 32)
    f32 = jnp.float32

    def nrm(k, shape, scale):
        return jax.random.normal(k, shape, f32) * scale

    def gain(k, shape):
        return 1.0 + 0.05 * jax.random.normal(k, shape, f32)

    n_pages = PAST_LEN // PAGE_SIZE
    n_used = DEC_BATCH * n_pages
    n_phys = n_used + max(1, n_used // 4)
    wbuf = min(WINDOW, PAST_LEN)
    page_table = jax.random.permutation(ks[0], n_phys)[:n_used].reshape(DEC_BATCH, n_pages).astype(jnp.int32)
    dt = jnp.exp(jax.random.uniform(ks[1], (DEPTH, DN_HEADS), f32, float(np.log(1e-3)), float(np.log(1e-1))))
    return {
        'x_prompt': nrm(ks[2], (BATCH, SEQ, D_MODEL), 1.0),
        'x_sample': nrm(ks[3], (DEC_BATCH, DEC_SEQ, D_MODEL), 1.0),
        'mem_prompt': nrm(ks[4], (BATCH, MEM_LEN, D_MODEL), 1.0),
        'cache_nsa_kv': nrm(ks[5], (DEPTH, n_phys, PAGE_SIZE, 4, NSA_KV_HEADS, HEAD_DIM), 1.0),
        'cache_win_kv': nrm(ks[6], (DEPTH, DEC_BATCH, wbuf, 2, NSA_KV_HEADS, HEAD_DIM), 1.0),
        'cache_mem_kv': nrm(ks[7], (DEPTH, DEC_BATCH, MEM_LEN, 2, MEM_HEADS, MEM_HEAD_DIM), 1.0),
        'state_dn_conv': nrm(ks[8], (DEPTH, DEC_BATCH, CONV_WIDTH - 1, DN_CONV_DIM), 1.0),
        'state_dn_rec': nrm(ks[9], (DEPTH, DEC_BATCH, DN_HEADS, HEAD_DIM, HEAD_DIM), HEAD_DIM ** -0.5),
        'page_table': page_table,
        'ln_mix': gain(ks[10], (DEPTH, D_MODEL)),
        'w_in': nrm(ks[11], (DEPTH, D_MODEL, IN_COLS), D_MODEL ** -0.5),
        'dn_conv_w': nrm(ks[12], (DEPTH, CONV_WIDTH, DN_CONV_DIM), CONV_WIDTH ** -0.5),
        'dn_a_log': jnp.log(jax.random.uniform(ks[13], (DEPTH, DN_HEADS), f32, 1.0, 16.0)),
        'dn_dt_bias': dt + jnp.log(-jnp.expm1(-dt)),
        'dn_norm': gain(ks[14], (DEPTH, HEAD_DIM)),
        'cmp_pe': nrm(ks[15], (DEPTH, 2, CMP_BLOCK, HEAD_DIM), 0.5),
        'cmp_w1': nrm(ks[16], (DEPTH, 2, CMP_BLOCK * HEAD_DIM, CMP_HIDDEN), (CMP_BLOCK * HEAD_DIM) ** -0.5),
        'cmp_b1': nrm(ks[17], (DEPTH, 2, CMP_HIDDEN), 0.01),
        'cmp_w2': nrm(ks[18], (DEPTH, 2, CMP_HIDDEN, HEAD_DIM), (2.0 / CMP_HIDDEN) ** 0.5),
        'w_out': nrm(ks[19], (DEPTH, MIX_WIDTH, D_MODEL), MIX_WIDTH ** -0.5),
        'ln_mem': gain(ks[20], (DEPTH, D_MODEL)),
        'ln_memkv': gain(ks[21], (DEPTH, D_MODEL)),
        'w_mem_q': nrm(ks[22], (DEPTH, D_MODEL, MEM_HEADS * MEM_HEAD_DIM), D_MODEL ** -0.5),
        'w_mem_kv': nrm(ks[23], (DEPTH, D_MODEL, 2 * MEM_HEADS * MEM_HEAD_DIM), D_MODEL ** -0.5),
        'w_mem_o': nrm(ks[24], (DEPTH, MEM_HEADS * MEM_HEAD_DIM, D_MODEL), (MEM_HEADS * MEM_HEAD_DIM) ** -0.5),
        'ln_ffn': gain(ks[25], (DEPTH, D_MODEL)),
        'w_up': nrm(ks[26], (DEPTH, D_MODEL, D_FF), D_MODEL ** -0.5),
        'w_down': nrm(ks[27], (DEPTH, D_FF, D_MODEL), D_FF ** -0.5),
        'ln_final': gain(ks[28], (D_MODEL,)),
    }


def reference(x_prompt, x_sample, mem_prompt, cache_nsa_kv, cache_win_kv, cache_mem_kv, state_dn_conv,
              state_dn_rec, page_table, ln_mix, w_in, dn_conv_w, dn_a_log, dn_dt_bias, dn_norm, cmp_pe,
              cmp_w1, cmp_b1, cmp_w2, w_out, ln_mem, ln_memkv, w_mem_q, w_mem_kv, w_mem_o, ln_ffn, w_up,
              w_down, ln_final):
    B, S, _ = x_prompt.shape
    DB, DS, _ = x_sample.shape
    past = page_table.shape[1] * PAGE_SIZE
    wbuf = cache_win_kv.shape[2]
    mem_len = mem_prompt.shape[1]
    q_pos_p = jnp.arange(S)
    q_pos_s = past + jnp.arange(DS)
    win_pos_s = past - wbuf + jnp.arange(wbuf + DS)
    xp, xs = x_prompt, x_sample
    p_nsa, p_win, p_mem, p_conv, p_rec = [], [], [], [], []
    s_nsa, s_win, s_conv, s_rec = [], [], [], []
    for l in range(DEPTH):
        lw = (ln_mix[l], w_in[l], dn_conv_w[l], dn_a_log[l], dn_dt_bias[l], dn_norm[l], cmp_pe[l], cmp_w1[l],
              cmp_b1[l], cmp_w2[l], w_out[l], ln_mem[l], w_mem_q[l], w_mem_o[l], ln_ffn[l], w_up[l], w_down[l])
        mem_kv = (rms_norm(mem_prompt, ln_memkv[l]) @ w_mem_kv[l]).reshape(B, mem_len, 2, MEM_HEADS, MEM_HEAD_DIM)
        xp, conv_n, rec_n, nsa_n, win_n = trunk_layer(
            xp, mem_kv, jnp.zeros((B, CONV_WIDTH - 1, DN_CONV_DIM), xp.dtype),
            jnp.zeros((B, DN_HEADS, HEAD_DIM, HEAD_DIM), jnp.float32), None, None,
            q_pos_p, q_pos_p, min(WINDOW, S), True, *lw)
        p_nsa.append(nsa_n)
        p_win.append(win_n)
        p_mem.append(mem_kv)
        p_conv.append(conv_n)
        p_rec.append(rec_n)
        kv_past = cache_nsa_kv[l][page_table].reshape(DB, past, 4, NSA_KV_HEADS, HEAD_DIM)
        xs, conv_n, rec_n, nsa_n, win_n = trunk_layer(
            xs, cache_mem_kv[l], state_dn_conv[l], state_dn_rec[l], kv_past, cache_win_kv[l],
            q_pos_s, win_pos_s, wbuf, False, *lw)
        s_nsa.append(nsa_n)
        s_win.append(win_n)
        s_conv.append(conv_n)
        s_rec.append(rec_n)
    y_prompt = rms_norm(xp, ln_final)
    y_sample = rms_norm(xs, ln_final)
    return (y_prompt, y_sample, jnp.stack(p_nsa), jnp.stack(p_win), jnp.stack(p_mem), jnp.stack(p_conv),
            jnp.stack(p_rec), jnp.stack(s_nsa), jnp.stack(s_win), jnp.stack(s_conv), jnp.stack(s_rec))
```

```python
import functools

import jax
import jax.numpy as jnp
from jax import lax
from jax.experimental import pallas as pl
from jax.experimental.pallas import tpu as pltpu

F32 = jnp.float32
BF16 = jnp.bfloat16

HEAD_DIM = 64
DN_HEADS = 8
NSA_HEADS = 8
NSA_KV_HEADS = 2
NSA_GROUP = NSA_HEADS // NSA_KV_HEADS
DN_WIDTH = DN_HEADS * HEAD_DIM
NSA_WIDTH = NSA_HEADS * HEAD_DIM
CONV_WIDTH = 4
DN_CONV_DIM = 3 * DN_WIDTH
DN_CHUNK = 64
CMP_BLOCK = 64
SEL_BLOCK = 64
TOP_N = 16
N_LOCAL_BLOCKS = 2
WINDOW = 512
CMP_HIDDEN = 128
NSA_KV_KINDS = 6
MEM_HEADS = 4
PAGE_SIZE = 128
RMS_EPS = 1e-6
FORCE_SCORE = 1e3
NEG_INF = -1e30
ATTN_SCALE = HEAD_DIM ** -0.5

LANES = 128
P_QKV = 0
P_Z = P_QKV + DN_CONV_DIM
P_NQ = P_Z + DN_WIDTH
P_NKV = P_NQ + NSA_WIDTH
P_SMALL = P_NKV + NSA_KV_KINDS * NSA_KV_HEADS * HEAD_DIM
P_COLS = P_SMALL + LANES
KV_COLS = NSA_KV_KINDS * NSA_KV_HEADS * HEAD_DIM
VMEM_LIMIT = 56 * 1024 * 1024
DN_SOLVE_PASSES = 3


def _cp(sem, vmem=VMEM_LIMIT):
    return pltpu.CompilerParams(dimension_semantics=sem, vmem_limit_bytes=vmem)


def _split2(a):
    hi = a.astype(BF16)
    return hi, (a - hi.astype(F32)).astype(BF16)


def _dot(a, b, passes=1):
    if passes == 1:
        return jnp.dot(a.astype(BF16), b.astype(BF16), preferred_element_type=F32)
    ah, al = _split2(a)
    bh, bl = _split2(b)
    return (jnp.dot(ah, bh, preferred_element_type=F32) + jnp.dot(ah, bl, preferred_element_type=F32)
            + jnp.dot(al, bh, preferred_element_type=F32))


def _dot_nt(a, b):
    return lax.dot_general(a.astype(BF16), b.astype(BF16), (((1,), (1,)), ((), ())),
                           preferred_element_type=F32)


def _split3(a):
    hi = a.astype(BF16)
    r1 = a - hi.astype(F32)
    mid = r1.astype(BF16)
    lo = (r1 - mid.astype(F32)).astype(BF16)
    return hi, mid, lo


def _dot_exact_lhs01(a01, b):
    a = a01.astype(BF16)
    hi, mid, lo = _split3(b)
    return (jnp.dot(a, hi, preferred_element_type=F32) + jnp.dot(a, mid, preferred_element_type=F32)
            + jnp.dot(a, lo, preferred_element_type=F32))


def _dot_exact_rhs01(a, b01):
    b = b01.astype(BF16)
    hi, mid, lo = _split3(a)
    return (jnp.dot(hi, b, preferred_element_type=F32) + jnp.dot(mid, b, preferred_element_type=F32)
            + jnp.dot(lo, b, preferred_element_type=F32))


def _rms(x, gain):
    ms = jnp.mean(x * x, axis=-1, keepdims=True)
    return x * lax.rsqrt(ms + RMS_EPS) * gain


def _softplus(x):
    return jnp.maximum(x, 0.0) + jnp.log1p(jnp.exp(-jnp.abs(x)))


def _silu(x):
    return x * jax.nn.sigmoid(x)


def _rms_mm_kernel(x_ref, g_ref, w_ref, o_ref, h_ref):
    @pl.when(pl.program_id(1) == 0)
    def _():
        h_ref[...] = _rms(x_ref[...], g_ref[...]).astype(BF16)

    o_ref[...] = jnp.dot(h_ref[...], w_ref[...], preferred_element_type=F32)


def rms_matmul(x, gain, w_bf16, tm, tn):
    n, d = x.shape
    m = w_bf16.shape[1]
    assert n % tm == 0 and m % tn == 0
    return pl.pallas_call(
        _rms_mm_kernel,
        grid=(n // tm, m // tn),
        in_specs=[pl.BlockSpec((tm, d), lambda i, j: (i, 0)),
                  pl.BlockSpec((1, d), lambda i, j: (0, 0)),
                  pl.BlockSpec((d, tn), lambda i, j: (0, j))],
        out_specs=pl.BlockSpec((tm, tn), lambda i, j: (i, j)),
        out_shape=jax.ShapeDtypeStruct((n, m), F32),
        scratch_shapes=[pltpu.VMEM((tm, d), BF16)],
        compiler_params=_cp(("parallel", "arbitrary")),
        name="rms_matmul",
    )(x, gain.reshape(1, d), w_bf16)


def _mm_res_kernel(*refs, n_in):
    res_ref = refs[0]
    a_refs = refs[1:1 + n_in]
    w_refs = refs[1 + n_in:1 + 2 * n_in]
    o_ref = refs[1 + 2 * n_in]
    acc = res_ref[...]
    for a_ref, w_ref in zip(a_refs, w_refs):
        acc = acc + jnp.dot(a_ref[...].astype(BF16), w_ref[...], preferred_element_type=F32)
    o_ref[...] = acc


def matmul_residual(res, a_list, w_list, tm):
    n, d = res.shape
    assert n % tm == 0
    n_in = len(a_list)
    in_specs = [pl.BlockSpec((tm, d), lambda i: (i, 0))]
    in_specs += [pl.BlockSpec((tm, a.shape[1]), lambda i: (i, 0)) for a in a_list]
    in_specs += [pl.BlockSpec(w.shape, lambda i: (0, 0)) for w in w_list]
    return pl.pallas_call(
        functools.partial(_mm_res_kernel, n_in=n_in),
        grid=(n // tm,),
        in_specs=in_specs,
        out_specs=pl.BlockSpec((tm, d), lambda i: (i, 0)),
        out_shape=jax.ShapeDtypeStruct((n, d), F32),
        compiler_params=_cp(("parallel",)),
        name="matmul_residual",
    )(res, *a_list, *w_list)


def _ffn_kernel(x_ref, g_ref, wu_ref, wd_ref, gf_ref, o_ref, hn_ref, acc_ref, *, final_norm):
    j = pl.program_id(1)

    @pl.when(j == 0)
    def _():
        x = x_ref[...]
        hn_ref[...] = _rms(x, g_ref[...]).astype(BF16)
        acc_ref[...] = x

    u = jnp.dot(hn_ref[...], wu_ref[...], preferred_element_type=F32)
    u = jnp.square(jnp.maximum(u, 0.0)).astype(BF16)
    acc_ref[...] += jnp.dot(u, wd_ref[...], preferred_element_type=F32)

    @pl.when(j == pl.num_programs(1) - 1)
    def _():
        y = acc_ref[...]
        if final_norm:
            y = _rms(y, gf_ref[...])
        o_ref[...] = y


def ffn(x, gain, wu_bf16, wd_bf16, gain_final, final_norm, tm, tf):
    n, d = x.shape
    f = wu_bf16.shape[1]
    assert n % tm == 0 and f % tf == 0
    return pl.pallas_call(
        functools.partial(_ffn_kernel, final_norm=final_norm),
        grid=(n // tm, f // tf),
        in_specs=[pl.BlockSpec((tm, d), lambda i, j: (i, 0)),
                  pl.BlockSpec((1, d), lambda i, j: (0, 0)),
                  pl.BlockSpec((d, tf), lambda i, j: (0, j)),
                  pl.BlockSpec((tf, d), lambda i, j: (j, 0)),
                  pl.BlockSpec((1, d), lambda i, j: (0, 0))],
        out_specs=pl.BlockSpec((tm, d), lambda i, j: (i, 0)),
        out_shape=jax.ShapeDtypeStruct((n, d), F32),
        scratch_shapes=[pltpu.VMEM((tm, d), BF16), pltpu.VMEM((tm, d), F32)],
        compiler_params=_cp(("parallel", "arbitrary")),
        name="ffn",
    )(x, gain.reshape(1, d), wu_bf16, wd_bf16, gain_final.reshape(1, d))


def _mem_attn_kernel(q_ref, kv_ref, o_ref, *, heads, hd):
    scale = hd ** -0.5
    for h in range(heads):
        q = q_ref[0, :, h * hd:(h + 1) * hd]
        k = kv_ref[0, :, h * hd:(h + 1) * hd]
        v = kv_ref[0, :, (heads + h) * hd:(heads + h + 1) * hd]
        s = _dot_nt(q, k) * scale
        p = jnp.exp(s - jnp.max(s, axis=-1, keepdims=True))
        p = p / jnp.sum(p, axis=-1, keepdims=True)
        o_ref[0, :, h * hd:(h + 1) * hd] = _dot(p, v)


def mem_attention(q, kv, tq):
    b, t, d = q.shape
    m = kv.shape[1]
    assert t % tq == 0
    return pl.pallas_call(
        functools.partial(_mem_attn_kernel, heads=MEM_HEADS, hd=d // MEM_HEADS),
        grid=(b, t // tq),
        in_specs=[pl.BlockSpec((1, tq, d), lambda i, j: (i, j, 0)),
                  pl.BlockSpec((1, m, 2 * d), lambda i, j: (i, 0, 0))],
        out_specs=pl.BlockSpec((1, tq, d), lambda i, j: (i, j, 0)),
        out_shape=jax.ShapeDtypeStruct((b, t, d), F32),
        compiler_params=_cp(("parallel", "parallel")),
        name="mem_attention",
    )(q, kv)


def _compress_kernel(x_ref, w1_ref, pe_ref, b1_ref, w2_ref, o_ref, acc_ref, *, row_cols, tt):
    j = pl.program_id(1)

    @pl.when(j == 0)
    def _():
        acc_ref[...] = jnp.zeros_like(acc_ref)

    hid2 = 2 * CMP_HIDDEN
    for kind in range(2):
        part = None
        for t in range(tt):
            c0 = t * row_cols + kind * LANES
            xs = x_ref[:, c0:c0 + LANES] + pe_ref[kind, t:t + 1, :]
            d = jnp.dot(xs.astype(BF16), w1_ref[kind, t], preferred_element_type=F32)
            part = d if part is None else part + d
        acc_ref[:, kind * hid2:(kind + 1) * hid2] += part

    @pl.when(j == pl.num_programs(1) - 1)
    def _():
        h = jnp.maximum(acc_ref[...] + b1_ref[...], 0.0)
        for kind in range(2):
            o_ref[:, kind * LANES:(kind + 1) * LANES] = jnp.dot(
                h[:, kind * hid2:(kind + 1) * hid2].astype(BF16), w2_ref[kind], preferred_element_type=F32)


def compress_blocks(x2, row_cols, cw, bt, tt=8):
    nb = x2.shape[0]
    assert nb % bt == 0 and CMP_BLOCK % tt == 0
    w1bd, pe2, b1bd, w2bd = cw
    return pl.pallas_call(
        functools.partial(_compress_kernel, row_cols=row_cols, tt=tt),
        grid=(nb // bt, CMP_BLOCK // tt),
        in_specs=[pl.BlockSpec((bt, tt * row_cols), lambda i, j: (i, j)),
                  pl.BlockSpec((2, tt, LANES, 2 * CMP_HIDDEN), lambda i, j: (0, j, 0, 0)),
                  pl.BlockSpec((2, tt, LANES), lambda i, j: (0, j, 0)),
                  pl.BlockSpec((1, 4 * CMP_HIDDEN), lambda i, j: (0, 0)),
                  pl.BlockSpec((2, 2 * CMP_HIDDEN, LANES), lambda i, j: (0, 0, 0))],
        out_specs=pl.BlockSpec((bt, 2 * LANES), lambda i, j: (i, 0)),
        out_shape=jax.ShapeDtypeStruct((nb, 2 * LANES), F32),
        scratch_shapes=[pltpu.VMEM((bt, 4 * CMP_HIDDEN), F32)],
        compiler_params=_cp(("parallel", "arbitrary")),
        name="compress_blocks",
    )(x2, w1bd, pe2, b1bd, w2bd)


def _compress_weights(cmp_pe, cmp_w1, cmp_b1, cmp_w2):
    w1r = cmp_w1.reshape(2, CMP_BLOCK, HEAD_DIM, CMP_HIDDEN)
    z = jnp.zeros_like(w1r)
    w1bd = jnp.concatenate([jnp.concatenate([w1r, z], -1), jnp.concatenate([z, w1r], -1)], axis=2).astype(BF16)
    pe2 = jnp.concatenate([cmp_pe, cmp_pe], -1)
    b1bd = jnp.concatenate([cmp_b1[0], cmp_b1[0], cmp_b1[1], cmp_b1[1]]).reshape(1, 4 * CMP_HIDDEN)
    z2 = jnp.zeros_like(cmp_w2)
    w2bd = jnp.concatenate([jnp.concatenate([cmp_w2, z2], -1), jnp.concatenate([z2, cmp_w2], -1)], axis=1).astype(BF16)
    return w1bd, pe2, b1bd, w2bd


def _topk_mask(score, k):
    n = score.shape[-1]
    lane = lax.broadcasted_iota(jnp.int32, score.shape, score.ndim - 1).astype(F32)
    sel = jnp.zeros(score.shape, F32)
    for _ in range(k):
        m = jnp.max(score, axis=-1, keepdims=True)
        idx = jnp.min(jnp.where(score == m, lane, float(n)), axis=-1, keepdims=True)
        hit = lane == idx
        sel = jnp.where(hit, 1.0, sel)
        score = jnp.where(hit, -2.0, score)
    return sel


def _masked_softmax(s, mask):
    s = jnp.where(mask, s, NEG_INF)
    p = jnp.where(mask, jnp.exp(s - jnp.max(s, axis=-1, keepdims=True)), 0.0)
    return p / jnp.maximum(jnp.sum(p, axis=-1, keepdims=True), 1e-30)


def _cmp_topk_kernel(q_ref, kc_ref, vc_ref, ocmp_ref, sel_ref, *, tq, nblk, topn):
    qi = pl.program_id(2)
    qpos = qi * tq + lax.broadcasted_iota(jnp.int32, (tq, nblk), 0)
    blk = lax.broadcasted_iota(jnp.int32, (tq, nblk), 1)
    vis = (blk + 1) * CMP_BLOCK - 1 <= qpos
    kc = kc_ref[0, 0]
    vc = vc_ref[0, 0]
    imp = jnp.zeros((tq, nblk), F32)
    for r in range(NSA_GROUP):
        s = _dot_nt(q_ref[0, 0, r], kc) * ATTN_SCALE
        p = _masked_softmax(s, vis)
        ocmp_ref[0, 0, r] = _dot(p, vc)
        imp = imp + p
    cur = lax.shift_right_logical(qpos, 6)
    valid = blk <= cur
    forced = valid & ((blk == 0) | (cur - blk < N_LOCAL_BLOCKS))
    score = jnp.where(valid, imp + jnp.where(forced, FORCE_SCORE, 0.0), -1.0)
    sel_ref[0, 0] = _topk_mask(score, topn).astype(BF16)


def nsa_cmp_topk(q4, kc, vc, tq):
    b, g, r, sq, dh = q4.shape
    nblk = kc.shape[2]
    assert sq % tq == 0 and SEL_BLOCK == 64
    return pl.pallas_call(
        functools.partial(_cmp_topk_kernel, tq=tq, nblk=nblk, topn=min(TOP_N, nblk)),
        grid=(b, g, sq // tq),
        in_specs=[pl.BlockSpec((1, 1, r, tq, dh), lambda i, j, k: (i, j, 0, k, 0)),
                  pl.BlockSpec((1, 1, nblk, dh), lambda i, j, k: (i, j, 0, 0)),
                  pl.BlockSpec((1, 1, nblk, dh), lambda i, j, k: (i, j, 0, 0))],
        out_specs=[pl.BlockSpec((1, 1, r, tq, dh), lambda i, j, k: (i, j, 0, k, 0)),
                   pl.BlockSpec((1, 1, tq, nblk), lambda i, j, k: (i, j, k, 0))],
        out_shape=[jax.ShapeDtypeStruct((b, g, r, sq, dh), F32),
                   jax.ShapeDtypeStruct((b, g, sq, nblk), BF16)],
        compiler_params=_cp(("parallel", "parallel", "parallel")),
        name="nsa_cmp_topk",
    )(q4, kc, vc)


def _nsa_attn_kernel(q_ref, ks_ref, vs_ref, kw_ref, vw_ref, sel_ref, ocmp_ref, gate_ref, o_ref,
                     m_sc, l_sc, acc_sc, *, tq, tk, nsel):
    qi = pl.program_id(2)
    r = NSA_GROUP
    dh = HEAD_DIM
    q = (q_ref[0, 0] * ATTN_SCALE).astype(BF16).reshape(r * tq, dh)
    qpos = qi * tq + lax.broadcasted_iota(jnp.int32, (tq, 1), 0)

    def reset():
        m_sc[...] = jnp.full(m_sc.shape, NEG_INF, F32)
        l_sc[...] = jnp.zeros(l_sc.shape, F32)
        acc_sc[...] = jnp.zeros(acc_sc.shape, F32)

    def step(k, v, mask):
        s = _dot_nt(q, k).reshape(r, tq, k.shape[0])
        mask3 = mask[None]
        s = jnp.where(mask3, s, NEG_INF)
        m_prev = m_sc[...].reshape(r, tq, 1)
        m_new = jnp.maximum(m_prev, jnp.max(s, axis=-1, keepdims=True))
        alpha = jnp.exp(m_prev - m_new)
        p = jnp.where(mask3, jnp.exp(s - m_new), 0.0)
        l_new = alpha * l_sc[...].reshape(r, tq, 1) + jnp.sum(p, axis=-1, keepdims=True)
        pv = jnp.dot(p.reshape(r * tq, k.shape[0]).astype(BF16), v, preferred_element_type=F32)
        acc_sc[...] = alpha.reshape(r * tq, 1) * acc_sc[...] + pv
        m_sc[...] = m_new.reshape(r * tq, 1)
        l_sc[...] = l_new.reshape(r * tq, 1)

    def result():
        return (acc_sc[...] / jnp.maximum(l_sc[...], 1e-30)).reshape(r, tq, dh)

    reset()
    sel = sel_ref[0, 0]
    n_kt = ((qi + 1) * tq + tk - 1) // tk

    def slc_body(kt, carry):
        k0 = pl.multiple_of(kt * tk, tk)
        kpos = k0 + lax.broadcasted_iota(jnp.int32, (1, tk), 1)
        eblk = lax.broadcasted_iota(jnp.int32, (nsel, tk), 0)
        ecol = k0 + lax.broadcasted_iota(jnp.int32, (nsel, tk), 1)
        expand = jnp.where(lax.shift_right_logical(ecol, 6) == eblk, 1.0, 0.0).astype(BF16)
        chosen = jnp.dot(sel, expand, preferred_element_type=F32) > 0.5
        step(ks_ref[0, 0, pl.ds(k0, tk), :], vs_ref[0, 0, pl.ds(k0, tk), :], chosen & (kpos <= qpos))
        return carry

    lax.fori_loop(0, n_kt, slc_body, 0)
    o_slc = result()

    reset()
    kt_hi = qi + 1
    kt_lo = jnp.maximum(qi - WINDOW // tq, 0)

    def win_body(kt, carry):
        k0 = pl.multiple_of(kt * tq, tq)
        kpos = k0 + lax.broadcasted_iota(jnp.int32, (1, tq), 1)
        dpos = qpos - kpos
        step(kw_ref[0, 0, pl.ds(k0, tq), :], vw_ref[0, 0, pl.ds(k0, tq), :], (dpos >= 0) & (dpos < WINDOW))
        return carry

    lax.fori_loop(kt_lo, kt_hi, win_body, 0)
    o_win = result()

    gates = jax.nn.sigmoid(gate_ref[0, 0])
    for h in range(r):
        o_ref[0, 0, h] = (gates[:, 3 * h:3 * h + 1] * ocmp_ref[0, 0, h]
                          + gates[:, 3 * h + 1:3 * h + 2] * o_slc[h]
                          + gates[:, 3 * h + 2:3 * h + 3] * o_win[h])


def nsa_attention(q4, ks, vs, kw, vw, sel, ocmp, gate_g, tq, tk):
    b, g, r, sq, dh = q4.shape
    t = ks.shape[2]
    nsel = sel.shape[3]
    assert sq == t and sq % tq == 0 and t % tk == 0 and tk % tq == 0 and WINDOW % tq == 0
    kv_spec = pl.BlockSpec((1, 1, t, dh), lambda i, j, k: (i, j, 0, 0))
    q_spec = pl.BlockSpec((1, 1, r, tq, dh), lambda i, j, k: (i, j, 0, k, 0))
    return pl.pallas_call(
        functools.partial(_nsa_attn_kernel, tq=tq, tk=tk, nsel=nsel),
        grid=(b, g, sq // tq),
        in_specs=[q_spec, kv_spec, kv_spec, kv_spec, kv_spec,
                  pl.BlockSpec((1, 1, tq, nsel), lambda i, j, k: (i, j, k, 0)),
                  q_spec,
                  pl.BlockSpec((1, 1, tq, LANES), lambda i, j, k: (i, j, k, 0))],
        out_specs=q_spec,
        out_shape=jax.ShapeDtypeStruct((b, g, r, sq, dh), F32),
        scratch_shapes=[pltpu.VMEM((r * tq, 1), F32), pltpu.VMEM((r * tq, 1), F32),
                        pltpu.VMEM((r * tq, dh), F32)],
        compiler_params=_cp(("parallel", "parallel", "parallel")),
        name="nsa_attention",
    )(q4, ks, vs, kw, vw, sel, ocmp, gate_g)


def _deltanet_kernel(qkv_ref, z_ref, sm_ref, cw_ref, alog_ref, dtb_ref, gn_ref, o_ref, s_out_ref,
                     xbuf, s_sc):
    c = pl.program_id(1)
    ch = DN_CHUNK
    n_pairs = DN_HEADS // 2
    two = 2 * ch

    @pl.when(c == 0)
    def _():
        xbuf[0:8, :] = jnp.zeros((8, DN_CONV_DIM), F32)
        s_sc[...] = jnp.zeros_like(s_sc)

    xbuf[8:8 + ch, :] = qkv_ref[0]
    conv = None
    for w in range(CONV_WIDTH):
        term = xbuf[5 + w:5 + w + ch, :] * cw_ref[w:w + 1, :]
        conv = term if conv is None else conv + term
    xbuf[0:8, :] = xbuf[ch:ch + 8, :]
    act = _silu(conv)

    sm = sm_ref[0]
    beta_all = jax.nn.sigmoid(sm)
    g_all = -jnp.exp(alog_ref[...]) * _softplus(sm + dtb_ref[...])
    ti = lax.broadcasted_iota(jnp.int32, (ch, ch), 0)
    tj = lax.broadcasted_iota(jnp.int32, (ch, ch), 1)
    gcum_all = _dot_exact_lhs01(jnp.where(ti >= tj, 1.0, 0.0), g_all)

    lane = lax.broadcasted_iota(jnp.int32, (ch, LANES), 1)
    lo = lane < HEAD_DIM
    row2 = lax.broadcasted_iota(jnp.int32, (two, two), 0)
    col2 = lax.broadcasted_iota(jnp.int32, (two, two), 1)
    same = (row2 >= ch) == (col2 >= ch)
    incl = same & (row2 >= col2)
    strict = same & (row2 > col2)
    top = lax.broadcasted_iota(jnp.int32, (two, 1), 0) < ch

    def seg_sum(x):
        s_lo = jnp.sum(jnp.where(lo, x, 0.0), axis=-1, keepdims=True)
        s_hi = jnp.sum(jnp.where(lo, 0.0, x), axis=-1, keepdims=True)
        return jnp.where(lo, s_lo, s_hi)

    def stack(x):
        return jnp.concatenate([jnp.where(lo, x, 0.0), jnp.where(lo, 0.0, x)], axis=0)

    def col2x(a, b):
        return jnp.concatenate([jnp.broadcast_to(a, (ch, LANES)), jnp.broadcast_to(b, (ch, LANES))], axis=0)

    for p in range(n_pairs):
        c0 = p * LANES
        qp = act[:, c0:c0 + LANES]
        kp = act[:, DN_WIDTH + c0:DN_WIDTH + c0 + LANES]
        vp = act[:, 2 * DN_WIDTH + c0:2 * DN_WIDTH + c0 + LANES]
        qp = qp * lax.rsqrt(seg_sum(qp * qp) + 1e-6) * (HEAD_DIM ** -0.5)
        kp = kp * lax.rsqrt(seg_sum(kp * kp) + 1e-6)
        q2, k2, v2 = stack(qp), stack(kp), stack(vp)
        h0, h1 = 2 * p, 2 * p + 1
        beta2 = col2x(beta_all[:, h0:h0 + 1], beta_all[:, h1:h1 + 1])
        gc2 = col2x(gcum_all[:, DN_HEADS + h0:DN_HEADS + h0 + 1], gcum_all[:, DN_HEADS + h1:DN_HEADS + h1 + 1])
        gl2 = jnp.where(top, gcum_all[ch - 1:ch, DN_HEADS + h0:DN_HEADS + h0 + 1],
                        gcum_all[ch - 1:ch, DN_HEADS + h1:DN_HEADS + h1 + 1])
        decay = jnp.exp(jnp.where(incl, gc2 - gc2.T, NEG_INF))
        kb2 = k2 * beta2
        a_mat = jnp.where(strict, _dot_nt(kb2, k2) * decay, 0.0)
        aqk = jnp.where(incl, _dot_nt(q2, k2) * decay, 0.0)
        s_old = s_sc[p]
        egc = jnp.exp(gc2)
        x = beta2 * (v2 - egc * _dot(k2, s_old))
        pw = -a_mat
        n_lvl = ch.bit_length() - 1
        for lvl in range(n_lvl):
            x = x + _dot(pw, x, DN_SOLVE_PASSES)
            if lvl + 1 < n_lvl:
                pw = _dot(pw, pw, DN_SOLVE_PASSES)
        o2 = _dot(q2 * egc, s_old) + _dot(aqk, x)
        kdec = k2 * jnp.exp(gl2 - gc2)
        s_sc[p] = s_old * jnp.exp(gl2) + _dot(kdec.T, x)
        o_pair = o2[0:ch] + o2[ch:two]
        zp = z_ref[0, :, c0:c0 + LANES]
        inv = lax.rsqrt(seg_sum(o_pair * o_pair) * (1.0 / HEAD_DIM) + RMS_EPS)
        o_ref[0, :, c0:c0 + LANES] = o_pair * inv * gn_ref[...] * _silu(zp)

    @pl.when(c == pl.num_programs(1) - 1)
    def _():
        s_out_ref[0] = s_sc[...]


def deltanet_prompt(proj3, conv_w, a_log, dt_bias, norm_gain):
    b, t, _ = proj3.shape
    ch = DN_CHUNK
    assert t % ch == 0
    pad = jnp.zeros((LANES - 2 * DN_HEADS,), F32)
    alog_row = jnp.concatenate([jnp.zeros((DN_HEADS,), F32), a_log, pad]).reshape(1, LANES)
    dtb_row = jnp.concatenate([jnp.zeros((DN_HEADS,), F32), dt_bias, pad]).reshape(1, LANES)
    gn_row = jnp.concatenate([norm_gain, norm_gain]).reshape(1, LANES)
    n_pairs = DN_HEADS // 2
    return pl.pallas_call(
        _deltanet_kernel,
        grid=(b, t // ch),
        in_specs=[pl.BlockSpec((1, ch, DN_CONV_DIM), lambda i, j: (i, j, P_QKV // DN_CONV_DIM)),
                  pl.BlockSpec((1, ch, DN_WIDTH), lambda i, j: (i, j, P_Z // DN_WIDTH)),
                  pl.BlockSpec((1, ch, LANES), lambda i, j: (i, j, P_SMALL // LANES)),
                  pl.BlockSpec((CONV_WIDTH, DN_CONV_DIM), lambda i, j: (0, 0)),
                  pl.BlockSpec((1, LANES), lambda i, j: (0, 0)),
                  pl.BlockSpec((1, LANES), lambda i, j: (0, 0)),
                  pl.BlockSpec((1, LANES), lambda i, j: (0, 0))],
        out_specs=[pl.BlockSpec((1, ch, DN_WIDTH), lambda i, j: (i, j, 0)),
                   pl.BlockSpec((1, n_pairs, 2 * ch, LANES), lambda i, j: (i, 0, 0, 0))],
        out_shape=[jax.ShapeDtypeStruct((b, t, DN_WIDTH), F32),
                   jax.ShapeDtypeStruct((b, n_pairs, 2 * ch, LANES), F32)],
        scratch_shapes=[pltpu.VMEM((ch + 8, DN_CONV_DIM), F32), pltpu.VMEM((n_pairs, 2 * ch, LANES), F32)],
        compiler_params=_cp(("parallel", "arbitrary")),
        name="deltanet_prompt",
    )(proj3, proj3, proj3, conv_w, alog_row, dtb_row, gn_row)


def _pairs_to_heads(s_pairs):
    d = HEAD_DIM
    return jnp.stack([s_pairs[:, :, :d, :d], s_pairs[:, :, d:, d:]], axis=2).reshape(
        s_pairs.shape[0], DN_HEADS, d, d)


def _dn_step_prep_kernel(qkv_ref, cs_ref, sm_ref, cw_ref, alog_ref, dtb_ref, ones_ref,
                         q_ref, k_ref, v_ref, sc_ref):
    conv = qkv_ref[...] * cw_ref[CONV_WIDTH - 1:CONV_WIDTH, :]
    for w in range(CONV_WIDTH - 1):
        conv = conv + cs_ref[w] * cw_ref[w:w + 1, :]
    act = _silu(conv)
    q = act[:, 0:DN_WIDTH]
    k = act[:, DN_WIDTH:2 * DN_WIDTH]

    def seg_sum(x):
        return _dot_exact_rhs01(x, ones_ref[...])

    q_ref[...] = q * lax.rsqrt(seg_sum(q * q) + 1e-6) * (HEAD_DIM ** -0.5)
    k_ref[...] = k * lax.rsqrt(seg_sum(k * k) + 1e-6)
    v_ref[...] = act[:, 2 * DN_WIDTH:]
    sm = sm_ref[...]
    g = -jnp.exp(alog_ref[...]) * _softplus(sm + dtb_ref[...])
    lane = lax.broadcasted_iota(jnp.int32, sm.shape, 1)
    sc_ref[...] = jnp.where(lane < DN_HEADS, jax.nn.sigmoid(sm), jnp.exp(g))


def _dn_step_kernel(kt_ref, qt_ref, v_ref, sc_ref, z_ref, gn_ref, s_ref, o_ref, s_out_ref, *, nseq):
    hds, d = DN_HEADS, HEAD_DIM
    for j in range(nseq):
        s_old = s_ref[j]
        kcol = kt_ref[0, :, j:j + 1]
        qcol = qt_ref[0, :, j:j + 1]
        beta = sc_ref[j, :, 0:1]
        eg = sc_ref[j, :, 1:2]
        v = v_ref[j]
        ks = jnp.sum((kcol * s_old).reshape(hds, d, d), axis=1)
        qs = jnp.sum((qcol * s_old).reshape(hds, d, d), axis=1)
        qk = jnp.sum((qcol * kcol).reshape(hds, d, 1), axis=1)
        v_new = beta * (v - eg * ks)
        o = eg * qs + qk * v_new
        inv = lax.rsqrt(jnp.mean(o * o, axis=-1, keepdims=True) + RMS_EPS)
        o_ref[j] = o * inv * gn_ref[...] * _silu(z_ref[j])
        s3 = s_old.reshape(hds, d, d) * eg[:, :, None] + kcol.reshape(hds, d, 1) * v_new[:, None, :]
        s_out_ref[j] = s3.reshape(hds * d, d)


def deltanet_sample(proj_s, conv_state, rec_state, conv_w, a_log, dt_bias, norm_gain, nseq=8):
    n = proj_s.shape[0]
    assert n % nseq == 0
    hds, d = DN_HEADS, HEAD_DIM
    pad = jnp.zeros((LANES - 2 * hds,), F32)
    alog_row = jnp.concatenate([jnp.zeros((hds,), F32), a_log, pad]).reshape(1, LANES)
    dtb_row = jnp.concatenate([jnp.zeros((hds,), F32), dt_bias, pad]).reshape(1, LANES)
    head_of = jnp.arange(DN_WIDTH) // d
    ones_bd = (head_of[:, None] == head_of[None, :]).astype(BF16)
    cs = jnp.transpose(conv_state, (1, 0, 2))
    full = lambda shape: pl.BlockSpec(shape, lambda i: (0,) * len(shape))
    q, k, v, sc = pl.pallas_call(
        _dn_step_prep_kernel,
        grid=(1,),
        in_specs=[pl.BlockSpec((n, DN_CONV_DIM), lambda i: (0, P_QKV // DN_CONV_DIM)),
                  full((CONV_WIDTH - 1, n, DN_CONV_DIM)),
                  pl.BlockSpec((n, LANES), lambda i: (0, P_SMALL // LANES)),
                  full((CONV_WIDTH, DN_CONV_DIM)), full((1, LANES)), full((1, LANES)),
                  full((DN_WIDTH, DN_WIDTH))],
        out_specs=[full((n, DN_WIDTH)), full((n, DN_WIDTH)), full((n, DN_WIDTH)), full((n, LANES))],
        out_shape=[jax.ShapeDtypeStruct((n, DN_WIDTH), F32)] * 3 + [jax.ShapeDtypeStruct((n, LANES), F32)],
        compiler_params=_cp(("arbitrary",)),
        name="dn_step_prep",
    )(proj_s, cs, proj_s, conv_w, alog_row, dtb_row, ones_bd)
    ng = n // nseq
    kt = jnp.transpose(k.reshape(ng, nseq, DN_WIDTH), (0, 2, 1))
    qt = jnp.transpose(q.reshape(ng, nseq, DN_WIDTH), (0, 2, 1))
    sc3 = jnp.stack([sc[:, :hds], sc[:, hds:2 * hds]], axis=-1)
    z = proj_s[:, P_Z:P_Z + DN_WIDTH].reshape(n, hds, d)
    o, s_new = pl.pallas_call(
        functools.partial(_dn_step_kernel, nseq=nseq),
        grid=(ng,),
        in_specs=[pl.BlockSpec((1, DN_WIDTH, nseq), lambda i: (i, 0, 0)),
                  pl.BlockSpec((1, DN_WIDTH, nseq), lambda i: (i, 0, 0)),
                  pl.BlockSpec((nseq, hds, d), lambda i: (i, 0, 0)),
                  pl.BlockSpec((nseq, hds, 2), lambda i: (i, 0, 0)),
                  pl.BlockSpec((nseq, hds, d), lambda i: (i, 0, 0)),
                  pl.BlockSpec((1, d), lambda i: (0, 0)),
                  pl.BlockSpec((nseq, hds * d, d), lambda i: (i, 0, 0))],
        out_specs=[pl.BlockSpec((nseq, hds, d), lambda i: (i, 0, 0)),
                   pl.BlockSpec((nseq, hds * d, d), lambda i: (i, 0, 0))],
        out_shape=[jax.ShapeDtypeStruct((n, hds, d), F32), jax.ShapeDtypeStruct((n, hds * d, d), F32)],
        compiler_params=_cp(("parallel",)),
        name="dn_step",
    )(kt, qt, v.reshape(n, hds, d), sc3, z, norm_gain.reshape(1, d), rec_state.reshape(n, hds * d, d))
    return o.reshape(n, DN_WIDTH), s_new.reshape(n, hds, d, d)


def _nsa_decode_kernel(pt_ref, q_ref, new_ref, gate_ref, win_ref, *refs, n_pages):
    kc_refs = refs[:n_pages]
    pg_refs = refs[n_pages:2 * n_pages]
    o_ref = refs[2 * n_pages]
    kc_sc = refs[2 * n_pages + 1]
    del pt_ref
    dh, r, g2, nh = HEAD_DIM, NSA_GROUP, NSA_KV_HEADS, NSA_HEADS
    past = n_pages * PAGE_SIZE
    nb = past // CMP_BLOCK
    nsel = nb + 1
    wlen = win_ref.shape[1]
    for j in range(n_pages):
        kc_sc[2 * j:2 * j + 2, :] = kc_refs[j][0]
    kcv = kc_sc[...]
    new = new_ref[0]
    q8 = q_ref[0] * ATTN_SCALE
    win = win_ref[0]
    head = lax.broadcasted_iota(jnp.int32, (nh, 1), 0)
    blk = lax.broadcasted_iota(jnp.int32, (nh, LANES), 1)
    kpos = lax.broadcasted_iota(jnp.int32, (1, past + LANES), 1)
    eblk = lax.broadcasted_iota(jnp.int32, (LANES, past + LANES), 0)
    expand = jnp.where(lax.shift_right_logical(kpos, 6) == eblk, 1.0, 0.0).astype(BF16)
    wp = lax.broadcasted_iota(jnp.int32, (1, wlen + LANES), 1)
    dpos = wlen - wp
    wmask = (dpos >= 0) & (dpos < WINDOW) & (past - wlen + wp >= 0)
    outs = []
    for g in range(g2):
        in_g = (head >= g * r) & (head < (g + 1) * r)
        kc = kcv[:, g * dh:(g + 1) * dh]
        vc = kcv[:, (g2 + g) * dh:(g2 + g + 1) * dh]
        s = _dot_nt(q8, kc)
        p = jnp.exp(s - jnp.max(s, axis=-1, keepdims=True))
        p = p / jnp.maximum(jnp.sum(p, axis=-1, keepdims=True), 1e-30)
        o_cmp = _dot(p, vc)
        imp = jnp.sum(jnp.where(in_g, p, 0.0), axis=0, keepdims=True)
        imp = jnp.broadcast_to(jnp.concatenate([imp, jnp.zeros((1, LANES - nb), F32)], axis=1), (nh, LANES))
        valid = blk < nsel
        forced = valid & ((blk == 0) | (nb - blk < N_LOCAL_BLOCKS))
        score = jnp.where(valid, imp + jnp.where(forced, FORCE_SCORE, 0.0), -1.0)
        sel = _topk_mask(score, min(TOP_N, nsel))
        parts = [_dot_nt(q8, pg_refs[j][0, :, g * dh:(g + 1) * dh]) for j in range(n_pages)]
        s_new = jnp.sum(q8 * new[:, (4 + g) * dh:(5 + g) * dh], axis=-1, keepdims=True)
        parts.append(jnp.broadcast_to(s_new, (nh, LANES)))
        s_all = jnp.concatenate(parts, axis=1)
        chosen = jnp.dot(sel.astype(BF16), expand, preferred_element_type=F32) > 0.5
        pm = _masked_softmax(s_all, chosen & (kpos <= past))
        o_slc = pm[:, past:past + 1] * new[:, (6 + g) * dh:(7 + g) * dh]
        for j in range(n_pages):
            o_slc = o_slc + _dot(pm[:, j * PAGE_SIZE:(j + 1) * PAGE_SIZE],
                                 pg_refs[j][0, :, (g2 + g) * dh:(g2 + g + 1) * dh])
        sw = _dot_nt(q8, win[:, g * dh:(g + 1) * dh])
        sw_new = jnp.sum(q8 * new[:, (8 + g) * dh:(9 + g) * dh], axis=-1, keepdims=True)
        pw = _masked_softmax(jnp.concatenate([sw, jnp.broadcast_to(sw_new, (nh, LANES))], axis=1), wmask)
        o_win = pw[:, wlen:wlen + 1] * new[:, (10 + g) * dh:(11 + g) * dh] + _dot(
            pw[:, 0:wlen], win[:, (g2 + g) * dh:(g2 + g + 1) * dh])
        outs.append((in_g, o_cmp, o_slc, o_win))
    gates = jnp.broadcast_to(jax.nn.sigmoid(gate_ref[0]), (nh, LANES))

    def gate(branch):
        return jnp.sum(jnp.where(blk == 3 * head + branch, gates, 0.0), axis=-1, keepdims=True)

    total = jnp.zeros((nh, dh), F32)
    for in_g, o_cmp, o_slc, o_win in outs:
        total = jnp.where(in_g, gate(0) * o_cmp + gate(1) * o_slc + gate(2) * o_win, total)
    o_ref[0] = total


def nsa_decode(page_table, q3, new_row, gate_row, win_cache, kc_phys, cache_pages):
    n, n_pages = page_table.shape
    dh = HEAD_DIM
    wlen = win_cache.shape[1]
    assert n_pages * PAGE_SIZE // CMP_BLOCK + 1 <= LANES and PAGE_SIZE == 2 * CMP_BLOCK

    def page_map(j):
        return lambda i, pt: (pt[i, j], 0, 0)

    def slc_map(j):
        return lambda i, pt: (pt[i, j], 0, 1)

    in_specs = [pl.BlockSpec((1, NSA_HEADS, dh), lambda i, pt: (i, 0, 0)),
                pl.BlockSpec((1, 1, KV_COLS), lambda i, pt: (i, 0, 0)),
                pl.BlockSpec((1, 1, LANES), lambda i, pt: (i, 0, 0)),
                pl.BlockSpec((1, wlen, 4 * dh), lambda i, pt: (i, 0, 0))]
    in_specs += [pl.BlockSpec((1, 2, 4 * dh), page_map(j)) for j in range(n_pages)]
    in_specs += [pl.BlockSpec((1, PAGE_SIZE, 4 * dh), slc_map(j)) for j in range(n_pages)]
    grid_spec = pltpu.PrefetchScalarGridSpec(
        num_scalar_prefetch=1, grid=(n,), in_specs=in_specs,
        out_specs=pl.BlockSpec((1, NSA_HEADS, dh), lambda i, pt: (i, 0, 0)),
        scratch_shapes=[pltpu.VMEM((2 * n_pages, 4 * dh), F32)])
    return pl.pallas_call(
        functools.partial(_nsa_decode_kernel, n_pages=n_pages),
        grid_spec=grid_spec,
        out_shape=jax.ShapeDtypeStruct((n, NSA_HEADS, dh), F32),
        compiler_params=_cp(("arbitrary",)),
        name="nsa_decode",
    )(page_table, q3, new_row, gate_row, win_cache, *([kc_phys] * n_pages), *([cache_pages] * n_pages))


def _permute_w_in(w):
    d = w.shape[0]
    off_b = DN_CONV_DIM + DN_WIDTH
    off_q = off_b + 2 * DN_HEADS
    off_g = off_q + NSA_WIDTH + KV_COLS
    n_g = 3 * NSA_HEADS
    pad = jnp.zeros((d, LANES - 2 * DN_HEADS - n_g), w.dtype)
    return jnp.concatenate([w[:, :off_b], w[:, off_q:off_g], w[:, off_b:off_q], w[:, off_g:off_g + n_g], pad], axis=1)


def _row_tile(n, cap):
    t = min(n, cap)
    while n % t:
        t //= 2
    return t


def _trunk_tail(x2, mixer_dn, mixer_nsa, mem_kv3, bshape, lw, final_norm):
    n, d = x2.shape
    b, t = bshape
    tm = _row_tile(n, 512)
    x2 = matmul_residual(x2, [mixer_dn, mixer_nsa], [lw["w_out_dn"], lw["w_out_nsa"]], tm)
    qm = rms_matmul(x2, lw["ln_mem"], lw["w_mem_q"], tm, 512)
    q3 = qm.reshape(b, t, d)
    sub = 8
    if t < sub:
        q3 = jnp.broadcast_to(q3[:, :1], (b, sub, d))
    att = mem_attention(q3, mem_kv3, _row_tile(q3.shape[1], 512))[:, :t]
    x2 = matmul_residual(x2, [att.reshape(n, d)], [lw["w_mem_o"]], tm)
    return ffn(x2, lw["ln_ffn"], lw["w_up"], lw["w_down"], lw["ln_final"], final_norm, _row_tile(n, 1024), 512)


def _gate_groups(small):
    per = 3 * NSA_GROUP
    g0 = 2 * DN_HEADS
    parts = []
    for g in range(NSA_KV_HEADS):
        gl = small[..., g0 + g * per:g0 + (g + 1) * per]
        parts.append(jnp.concatenate([gl, jnp.zeros(gl.shape[:-1] + (LANES - per,), gl.dtype)], axis=-1))
    return jnp.stack(parts, axis=-2)


def _prompt_layer(xp, mem_prompt, lw, cw, final_norm):
    b, s, d = xp.shape
    n = b * s
    x2 = xp.reshape(n, d)
    proj = rms_matmul(x2, lw["ln_mix"], lw["w_in"], _row_tile(n, 512), 1152)
    proj3 = proj.reshape(b, s, P_COLS)
    dn_out, s_pairs = deltanet_prompt(proj3, lw["dn_conv_w"], lw["dn_a_log"], lw["dn_dt_bias"], lw["dn_norm"])
    p_conv = proj3[:, s - (CONV_WIDTH - 1):, P_QKV:P_QKV + DN_CONV_DIM]
    p_rec = _pairs_to_heads(s_pairs)
    kv = proj3[:, :, P_NKV:P_NKV + KV_COLS]
    kv6 = kv.reshape(b, s, NSA_KV_KINDS, NSA_KV_HEADS, HEAD_DIM)
    nb = s // CMP_BLOCK
    kcv = compress_blocks(kv[:, :nb * CMP_BLOCK].reshape(b * nb, CMP_BLOCK * KV_COLS), KV_COLS, cw,
                          _row_tile(b * nb, 256))
    kcv = kcv.reshape(b, nb, 2, NSA_KV_HEADS, HEAD_DIM)
    kc = jnp.transpose(kcv[:, :, 0], (0, 2, 1, 3))
    vc = jnp.transpose(kcv[:, :, 1], (0, 2, 1, 3))
    q4 = jnp.transpose(proj3[:, :, P_NQ:P_NQ + NSA_WIDTH].reshape(b, s, NSA_KV_HEADS, NSA_GROUP, HEAD_DIM),
                       (0, 2, 3, 1, 4))
    ocmp, sel = nsa_cmp_topk(q4, kc, vc, _row_tile(s, 512))
    kvt = jnp.transpose(kv6, (2, 0, 3, 1, 4)).astype(BF16)
    gate_g = jnp.transpose(_gate_groups(proj3[:, :, P_SMALL:P_SMALL + LANES]), (0, 2, 1, 3))
    o4 = nsa_attention(q4, kvt[2], kvt[3], kvt[4], kvt[5], sel, ocmp, gate_g, 128, 512)
    nsa_out = jnp.transpose(o4, (0, 3, 1, 2, 4)).reshape(n, NSA_WIDTH)
    m = mem_prompt.shape[1]
    mem_kv = rms_matmul(mem_prompt.reshape(b * m, d), lw["ln_memkv"], lw["w_mem_kv"], _row_tile(b * m, 512), 512)
    mem_kv3 = mem_kv.reshape(b, m, 2 * d)
    y = _trunk_tail(x2, dn_out.reshape(n, DN_WIDTH), nsa_out, mem_kv3, (b, s), lw, final_norm)
    wk = min(WINDOW, s)
    return (y.reshape(b, s, d), kv6[:, :, :4], kv6[:, s - wk:, 4:], mem_kv3.reshape(b, m, 2, MEM_HEADS, d // MEM_HEADS),
            p_conv, p_rec)


def _sample_layer(xs, cache_nsa, cache_win, cache_mem, conv_state, rec_state, page_table, lw, cw, final_norm):
    db, ds, d = xs.shape
    assert ds == 1
    n = db
    x2 = xs.reshape(n, d)
    proj = rms_matmul(x2, lw["ln_mix"], lw["w_in"], _row_tile(n, 512), 1152)
    dn_out, s_rec = deltanet_sample(proj, conv_state, rec_state, lw["dn_conv_w"], lw["dn_a_log"], lw["dn_dt_bias"],
                                    lw["dn_norm"])
    s_conv = jnp.concatenate([conv_state[:, 1:], proj[:, None, P_QKV:P_QKV + DN_CONV_DIM]], axis=1)
    n_phys = cache_nsa.shape[0]
    row_cols = 4 * NSA_KV_HEADS * HEAD_DIM
    nblk_phys = n_phys * (PAGE_SIZE // CMP_BLOCK)
    kc_phys = compress_blocks(cache_nsa.reshape(nblk_phys, CMP_BLOCK * row_cols), row_cols, cw,
                              _row_tile(nblk_phys, 256))
    kv_new = proj[:, P_NKV:P_NKV + KV_COLS]
    wlen = cache_win.shape[1]
    o8 = nsa_decode(page_table, proj[:, P_NQ:P_NQ + NSA_WIDTH].reshape(n, NSA_HEADS, HEAD_DIM),
                    kv_new.reshape(n, 1, KV_COLS),
                    _flat_gates(proj[:, P_SMALL:P_SMALL + LANES]).reshape(n, 1, LANES),
                    cache_win.reshape(n, wlen, 4 * HEAD_DIM),
                    kc_phys.reshape(n_phys, PAGE_SIZE // CMP_BLOCK, 4 * HEAD_DIM),
                    cache_nsa.reshape(n_phys, PAGE_SIZE, row_cols))
    kv6 = kv_new.reshape(n, 1, NSA_KV_KINDS, NSA_KV_HEADS, HEAD_DIM)
    s_win = jnp.concatenate([cache_win, kv6[:, :, 4:]], axis=1)[:, 1:]
    y = _trunk_tail(x2, dn_out, o8.reshape(n, NSA_WIDTH), cache_mem.reshape(n, cache_mem.shape[1], 2 * d),
                    (n, 1), lw, final_norm)
    return y.reshape(db, ds, d), kv6[:, :, :4], s_win, s_conv, s_rec


def _flat_gates(small):
    g0 = 2 * DN_HEADS
    n_g = 3 * NSA_HEADS
    gl = small[..., g0:g0 + n_g]
    return jnp.concatenate([gl, jnp.zeros(gl.shape[:-1] + (LANES - n_g,), gl.dtype)], axis=-1)


def kernel(x_prompt, x_sample, mem_prompt, cache_nsa_kv, cache_win_kv, cache_mem_kv, state_dn_conv, state_dn_rec, page_table, ln_mix, w_in, dn_conv_w, dn_a_log, dn_dt_bias, dn_norm, cmp_pe, cmp_w1, cmp_b1, cmp_w2, w_out, ln_mem, ln_memkv, w_mem_q, w_mem_kv, w_mem_o, ln_ffn, w_up, w_down, ln_final):
    depth = w_in.shape[0]
    xp, xs = x_prompt, x_sample
    outs_p = [[] for _ in range(5)]
    outs_s = [[] for _ in range(4)]
    for l in range(depth):
        lw = {
            "ln_mix": ln_mix[l], "w_in": _permute_w_in(w_in[l]).astype(BF16),
            "dn_conv_w": dn_conv_w[l], "dn_a_log": dn_a_log[l], "dn_dt_bias": dn_dt_bias[l], "dn_norm": dn_norm[l],
            "w_out_dn": w_out[l][:DN_WIDTH].astype(BF16), "w_out_nsa": w_out[l][DN_WIDTH:].astype(BF16),
            "ln_mem": ln_mem[l], "ln_memkv": ln_memkv[l], "w_mem_q": w_mem_q[l].astype(BF16),
            "w_mem_kv": w_mem_kv[l].astype(BF16), "w_mem_o": w_mem_o[l].astype(BF16),
            "ln_ffn": ln_ffn[l], "w_up": w_up[l].astype(BF16), "w_down": w_down[l].astype(BF16),
            "ln_final": ln_final,
        }
        cw = _compress_weights(cmp_pe[l], cmp_w1[l], cmp_b1[l], cmp_w2[l])
        last = l == depth - 1
        xp, p_nsa, p_win, p_mem, p_conv, p_rec = _prompt_layer(xp, mem_prompt, lw, cw, last)
        for acc, val in zip(outs_p, (p_nsa, p_win, p_mem, p_conv, p_rec)):
            acc.append(val)
        xs, s_nsa, s_win, s_conv, s_rec = _sample_layer(
            xs, cache_nsa_kv[l], cache_win_kv[l], cache_mem_kv[l], state_dn_conv[l], state_dn_rec[l],
            page_table, lw, cw, last)
        for acc, val in zip(outs_s, (s_nsa, s_win, s_conv, s_rec)):
            acc.append(val)
    return (xp, xs) + tuple(jnp.stack(a) for a in outs_p) + tuple(jnp.stack(a) for a in outs_s)
```

```python
import functools

import jax
import jax.numpy as jnp
from jax import lax
from jax.experimental import pallas as pl
from jax.experimental.pallas import tpu as pltpu

F32 = jnp.float32
BF16 = jnp.bfloat16

HEAD_DIM = 64
DN_HEADS = 8
NSA_HEADS = 8
NSA_KV_HEADS = 2
NSA_GROUP = NSA_HEADS // NSA_KV_HEADS
DN_WIDTH = DN_HEADS * HEAD_DIM
NSA_WIDTH = NSA_HEADS * HEAD_DIM
CONV_WIDTH = 4
DN_CONV_DIM = 3 * DN_WIDTH
DN_CHUNK = 64
CMP_BLOCK = 64
SEL_BLOCK = 64
TOP_N = 16
N_LOCAL_BLOCKS = 2
WINDOW = 512
CMP_HIDDEN = 128
NSA_KV_KINDS = 6
MEM_HEADS = 4
PAGE_SIZE = 128
RMS_EPS = 1e-6
FORCE_SCORE = 1e3
NEG_INF = -1e30
ATTN_SCALE = HEAD_DIM ** -0.5

LANES = 128
P_QKV = 0
P_Z = P_QKV + DN_CONV_DIM
P_NQ = P_Z + DN_WIDTH
P_NKV = P_NQ + NSA_WIDTH
P_SMALL = P_NKV + NSA_KV_KINDS * NSA_KV_HEADS * HEAD_DIM
P_COLS = P_SMALL + LANES
KV_COLS = NSA_KV_KINDS * NSA_KV_HEADS * HEAD_DIM
VMEM_LIMIT = 56 * 1024 * 1024
DN_SOLVE_PASSES = 3


def _cp(sem, vmem=VMEM_LIMIT):
    return pltpu.CompilerParams(dimension_semantics=sem, vmem_limit_bytes=vmem)


def _split2(a):
    hi = a.astype(BF16)
    return hi, (a - hi.astype(F32)).astype(BF16)


def _dot(a, b, passes=1):
    if passes == 1:
        return jnp.dot(a.astype(BF16), b.astype(BF16), preferred_element_type=F32)
    ah, al = _split2(a)
    bh, bl = _split2(b)
    return (jnp.dot(ah, bh, preferred_element_type=F32) + jnp.dot(ah, bl, preferred_element_type=F32)
            + jnp.dot(al, bh, preferred_element_type=F32))


def _dot_nt(a, b):
    return lax.dot_general(a.astype(BF16), b.astype(BF16), (((1,), (1,)), ((), ())),
                           preferred_element_type=F32)


def _split3(a):
    hi = a.astype(BF16)
    r1 = a - hi.astype(F32)
    mid = r1.astype(BF16)
    lo = (r1 - mid.astype(F32)).astype(BF16)
    return hi, mid, lo


def _dot_exact_lhs01(a01, b):
    a = a01.astype(BF16)
    hi, mid, lo = _split3(b)
    return (jnp.dot(a, hi, preferred_element_type=F32) + jnp.dot(a, mid, preferred_element_type=F32)
            + jnp.dot(a, lo, preferred_element_type=F32))


def _dot_exact_rhs01(a, b01):
    b = b01.astype(BF16)
    hi, mid, lo = _split3(a)
    return (jnp.dot(hi, b, preferred_element_type=F32) + jnp.dot(mid, b, preferred_element_type=F32)
            + jnp.dot(lo, b, preferred_element_type=F32))


def _rms(x, gain):
    ms = jnp.mean(x * x, axis=-1, keepdims=True)
    return x * lax.rsqrt(ms + RMS_EPS) * gain


def _softplus(x):
    return jnp.maximum(x, 0.0) + jnp.log1p(jnp.exp(-jnp.abs(x)))


def _silu(x):
    return x * jax.nn.sigmoid(x)


def _rms_mm_kernel(x_ref, g_ref, w_ref, o_ref, h_ref):
    @pl.when(pl.program_id(1) == 0)
    def _():
        h_ref[...] = _rms(x_ref[...], g_ref[...]).astype(BF16)

    o_ref[...] = jnp.dot(h_ref[...], w_ref[...], preferred_element_type=F32)


def rms_matmul(x, gain, w_bf16, tm, tn):
    n, d = x.shape
    m = w_bf16.shape[1]
    assert n % tm == 0 and m % tn == 0
    return pl.pallas_call(
        _rms_mm_kernel,
        grid=(n // tm, m // tn),
        in_specs=[pl.BlockSpec((tm, d), lambda i, j: (i, 0)),
                  pl.BlockSpec((1, d), lambda i, j: (0, 0)),
                  pl.BlockSpec((d, tn), lambda i, j: (0, j))],
        out_specs=pl.BlockSpec((tm, tn), lambda i, j: (i, j)),
        out_shape=jax.ShapeDtypeStruct((n, m), F32),
        scratch_shapes=[pltpu.VMEM((tm, d), BF16)],
        compiler_params=_cp(("parallel", "arbitrary")),
        name="rms_matmul",
    )(x, gain.reshape(1, d), w_bf16)


def _mm_res_kernel(*refs, n_in):
    res_ref = refs[0]
    a_refs = refs[1:1 + n_in]
    w_refs = refs[1 + n_in:1 + 2 * n_in]
    o_ref = refs[1 + 2 * n_in]
    acc = res_ref[...]
    for a_ref, w_ref in zip(a_refs, w_refs):
        acc = acc + jnp.dot(a_ref[...].astype(BF16), w_ref[...], preferred_element_type=F32)
    o_ref[...] = acc


def matmul_residual(res, a_list, w_list, tm):
    n, d = res.shape
    assert n % tm == 0
    n_in = len(a_list)
    in_specs = [pl.BlockSpec((tm, d), lambda i: (i, 0))]
    in_specs += [pl.BlockSpec((tm, a.shape[1]), lambda i: (i, 0)) for a in a_list]
    in_specs += [pl.BlockSpec(w.shape, lambda i: (0, 0)) for w in w_list]
    return pl.pallas_call(
        functools.partial(_mm_res_kernel, n_in=n_in),
        grid=(n // tm,),
        in_specs=in_specs,
        out_specs=pl.BlockSpec((tm, d), lambda i: (i, 0)),
        out_shape=jax.ShapeDtypeStruct((n, d), F32),
        compiler_params=_cp(("parallel",)),
        name="matmul_residual",
    )(res, *a_list, *w_list)


def _ffn_kernel(x_ref, g_ref, wu_ref, wd_ref, gf_ref, o_ref, hn_ref, acc_ref, *, final_norm):
    j = pl.program_id(1)

    @pl.when(j == 0)
    def _():
        x = x_ref[...]
        hn_ref[...] = _rms(x, g_ref[...]).astype(BF16)
        acc_ref[...] = x

    u = jnp.dot(hn_ref[...], wu_ref[...], preferred_element_type=F32)
    u = jnp.square(jnp.maximum(u, 0.0)).astype(BF16)
    acc_ref[...] += jnp.dot(u, wd_ref[...], preferred_element_type=F32)

    @pl.when(j == pl.num_programs(1) - 1)
    def _():
        y = acc_ref[...]
        if final_norm:
            y = _rms(y, gf_ref[...])
        o_ref[...] = y


def ffn(x, gain, wu_bf16, wd_bf16, gain_final, final_norm, tm, tf):
    n, d = x.shape
    f = wu_bf16.shape[1]
    assert n % tm == 0 and f % tf == 0
    return pl.pallas_call(
        functools.partial(_ffn_kernel, final_norm=final_norm),
        grid=(n // tm, f // tf),
        in_specs=[pl.BlockSpec((tm, d), lambda i, j: (i, 0)),
                  pl.BlockSpec((1, d), lambda i, j: (0, 0)),
                  pl.BlockSpec((d, tf), lambda i, j: (0, j)),
                  pl.BlockSpec((tf, d), lambda i, j: (j, 0)),
                  pl.BlockSpec((1, d), lambda i, j: (0, 0))],
        out_specs=pl.BlockSpec((tm, d), lambda i, j: (i, 0)),
        out_shape=jax.ShapeDtypeStruct((n, d), F32),
        scratch_shapes=[pltpu.VMEM((tm, d), BF16), pltpu.VMEM((tm, d), F32)],
        compiler_params=_cp(("parallel", "arbitrary")),
        name="ffn",
    )(x, gain.reshape(1, d), wu_bf16, wd_bf16, gain_final.reshape(1, d))


def _mem_attn_kernel(q_ref, kv_ref, o_ref, *, heads, hd):
    scale = hd ** -0.5
    for h in range(heads):
        q = q_ref[0, :, h * hd:(h + 1) * hd]
        k = kv_ref[0, :, 0, h, :]
        v = kv_ref[0, :, 1, h, :]
        s = _dot_nt(q, k) * scale
        p = jnp.exp(s - jnp.max(s, axis=-1, keepdims=True))
        p = p / jnp.sum(p, axis=-1, keepdims=True)
        o_ref[0, :, h * hd:(h + 1) * hd] = _dot(p, v)


def mem_attention(q, kv, tq):
    b, t, d = q.shape
    m = kv.shape[1]
    hd = d // MEM_HEADS
    assert t % tq == 0 and kv.shape[2:] == (2, MEM_HEADS, hd)
    return pl.pallas_call(
        functools.partial(_mem_attn_kernel, heads=MEM_HEADS, hd=hd),
        grid=(b, t // tq),
        in_specs=[pl.BlockSpec((1, tq, d), lambda i, j: (i, j, 0)),
                  pl.BlockSpec((1, m, 2, MEM_HEADS, hd), lambda i, j: (i, 0, 0, 0, 0))],
        out_specs=pl.BlockSpec((1, tq, d), lambda i, j: (i, j, 0)),
        out_shape=jax.ShapeDtypeStruct((b, t, d), F32),
        compiler_params=_cp(("parallel", "parallel")),
        name="mem_attention",
    )(q, kv)


def _compress_kernel(x_ref, w1_ref, pe_ref, b1_ref, w2_ref, o_ref, acc_ref, *, row_cols, tt):
    j = pl.program_id(1)

    @pl.when(j == 0)
    def _():
        acc_ref[...] = jnp.zeros_like(acc_ref)

    hid2 = 2 * CMP_HIDDEN
    for kind in range(2):
        part = None
        for t in range(tt):
            c0 = t * row_cols + kind * LANES
            xs = x_ref[:, c0:c0 + LANES] + pe_ref[kind, t:t + 1, :]
            d = jnp.dot(xs.astype(BF16), w1_ref[kind, t], preferred_element_type=F32)
            part = d if part is None else part + d
        acc_ref[:, kind * hid2:(kind + 1) * hid2] += part

    @pl.when(j == pl.num_programs(1) - 1)
    def _():
        h = jnp.maximum(acc_ref[...] + b1_ref[...], 0.0)
        for kind in range(2):
            o_ref[:, kind * LANES:(kind + 1) * LANES] = jnp.dot(
                h[:, kind * hid2:(kind + 1) * hid2].astype(BF16), w2_ref[kind], preferred_element_type=F32)


def compress_blocks(x2, row_cols, cw, bt, tt=8):
    nb = x2.shape[0]
    assert nb % bt == 0 and CMP_BLOCK % tt == 0
    w1bd, pe2, b1bd, w2bd = cw
    return pl.pallas_call(
        functools.partial(_compress_kernel, row_cols=row_cols, tt=tt),
        grid=(nb // bt, CMP_BLOCK // tt),
        in_specs=[pl.BlockSpec((bt, tt * row_cols), lambda i, j: (i, j)),
                  pl.BlockSpec((2, tt, LANES, 2 * CMP_HIDDEN), lambda i, j: (0, j, 0, 0)),
                  pl.BlockSpec((2, tt, LANES), lambda i, j: (0, j, 0)),
                  pl.BlockSpec((1, 4 * CMP_HIDDEN), lambda i, j: (0, 0)),
                  pl.BlockSpec((2, 2 * CMP_HIDDEN, LANES), lambda i, j: (0, 0, 0))],
        out_specs=pl.BlockSpec((bt, 2 * LANES), lambda i, j: (i, 0)),
        out_shape=jax.ShapeDtypeStruct((nb, 2 * LANES), F32),
        scratch_shapes=[pltpu.VMEM((bt, 4 * CMP_HIDDEN), F32)],
        compiler_params=_cp(("parallel", "arbitrary")),
        name="compress_blocks",
    )(x2, w1bd, pe2, b1bd, w2bd)


def _compress_weights(cmp_pe, cmp_w1, cmp_b1, cmp_w2):
    w1r = cmp_w1.reshape(2, CMP_BLOCK, HEAD_DIM, CMP_HIDDEN)
    z = jnp.zeros_like(w1r)
    w1bd = jnp.concatenate([jnp.concatenate([w1r, z], -1), jnp.concatenate([z, w1r], -1)], axis=2).astype(BF16)
    pe2 = jnp.concatenate([cmp_pe, cmp_pe], -1)
    b1bd = jnp.concatenate([cmp_b1[0], cmp_b1[0], cmp_b1[1], cmp_b1[1]]).reshape(1, 4 * CMP_HIDDEN)
    z2 = jnp.zeros_like(cmp_w2)
    w2bd = jnp.concatenate([jnp.concatenate([cmp_w2, z2], -1), jnp.concatenate([z2, cmp_w2], -1)], axis=1).astype(BF16)
    return w1bd, pe2, b1bd, w2bd


def _topk_mask(score, k, ids=None):
    if ids is None:
        ids = lax.broadcasted_iota(jnp.int32, score.shape, score.ndim - 1).astype(F32)
    sel = jnp.zeros(score.shape, F32)
    for _ in range(k):
        m = jnp.max(score, axis=-1, keepdims=True)
        idx = jnp.min(jnp.where(score == m, ids, 1e9), axis=-1, keepdims=True)
        hit = ids == idx
        sel = jnp.where(hit, 1.0, sel)
        score = jnp.where(hit, -2.0, score)
    return sel


def _masked_softmax(s, mask):
    s = jnp.where(mask, s, NEG_INF)
    p = jnp.where(mask, jnp.exp(s - jnp.max(s, axis=-1, keepdims=True)), 0.0)
    return p / jnp.maximum(jnp.sum(p, axis=-1, keepdims=True), 1e-30)


def _cmp_topk_kernel(q_ref, kc_ref, vc_ref, ocmp_ref, sel_ref, *, tq, nblk, topn):
    qi = pl.program_id(2)
    qpos = qi * tq + lax.broadcasted_iota(jnp.int32, (tq, nblk), 0)
    blk = lax.broadcasted_iota(jnp.int32, (tq, nblk), 1)
    vis = (blk + 1) * CMP_BLOCK - 1 <= qpos
    kc = kc_ref[0, 0]
    vc = vc_ref[0, 0]
    imp = jnp.zeros((tq, nblk), F32)
    for r in range(NSA_GROUP):
        s = _dot_nt(q_ref[0, 0, r], kc) * ATTN_SCALE
        p = _masked_softmax(s, vis)
        ocmp_ref[0, 0, r] = _dot(p, vc)
        imp = imp + p
    cur = lax.shift_right_logical(qpos, 6)
    valid = blk <= cur
    forced = valid & ((blk == 0) | (cur - blk < N_LOCAL_BLOCKS))
    score = jnp.where(valid, imp + jnp.where(forced, FORCE_SCORE, 0.0), -1.0)
    sel_ref[0, 0] = _topk_mask(score, topn).astype(BF16)


def nsa_cmp_topk(q4, kc, vc, tq):
    b, g, r, sq, dh = q4.shape
    nblk = kc.shape[2]
    assert sq % tq == 0 and SEL_BLOCK == 64
    return pl.pallas_call(
        functools.partial(_cmp_topk_kernel, tq=tq, nblk=nblk, topn=min(TOP_N, nblk)),
        grid=(b, g, sq // tq),
        in_specs=[pl.BlockSpec((1, 1, r, tq, dh), lambda i, j, k: (i, j, 0, k, 0)),
                  pl.BlockSpec((1, 1, nblk, dh), lambda i, j, k: (i, j, 0, 0)),
                  pl.BlockSpec((1, 1, nblk, dh), lambda i, j, k: (i, j, 0, 0))],
        out_specs=[pl.BlockSpec((1, 1, r, tq, dh), lambda i, j, k: (i, j, 0, k, 0)),
                   pl.BlockSpec((1, 1, tq, nblk), lambda i, j, k: (i, j, k, 0))],
        out_shape=[jax.ShapeDtypeStruct((b, g, r, sq, dh), F32),
                   jax.ShapeDtypeStruct((b, g, sq, nblk), BF16)],
        compiler_params=_cp(("parallel", "parallel", "parallel")),
        name="nsa_cmp_topk",
    )(q4, kc, vc)


def _nsa_attn_kernel(q_ref, ks_ref, vs_ref, kw_ref, vw_ref, sel_ref, ocmp_ref, gate_ref, o_ref,
                     m_sc, l_sc, acc_sc, *, tq, tk, nsel):
    qi = pl.program_id(2)
    r = NSA_GROUP
    dh = HEAD_DIM
    q = (q_ref[0, 0] * ATTN_SCALE).astype(BF16).reshape(r * tq, dh)
    qpos = qi * tq + lax.broadcasted_iota(jnp.int32, (tq, 1), 0)

    def reset():
        m_sc[...] = jnp.full(m_sc.shape, NEG_INF, F32)
        l_sc[...] = jnp.zeros(l_sc.shape, F32)
        acc_sc[...] = jnp.zeros(acc_sc.shape, F32)

    def step(k, v, mask):
        s = _dot_nt(q, k).reshape(r, tq, k.shape[0]) + jnp.where(mask, 0.0, NEG_INF)[None]
        m_prev = m_sc[...].reshape(r, tq, 1)
        m_new = jnp.maximum(m_prev, jnp.max(s, axis=-1, keepdims=True))
        alpha = jnp.exp(m_prev - m_new)
        p = jnp.exp(s - m_new)
        l_new = alpha * l_sc[...].reshape(r, tq, 1) + jnp.sum(p, axis=-1, keepdims=True)
        pv = jnp.dot(p.reshape(r * tq, k.shape[0]).astype(BF16), v, preferred_element_type=F32)
        acc_sc[...] = alpha.reshape(r * tq, 1) * acc_sc[...] + pv
        m_sc[...] = m_new.reshape(r * tq, 1)
        l_sc[...] = l_new.reshape(r * tq, 1)

    def result():
        return (acc_sc[...] / jnp.maximum(l_sc[...], 1e-30)).reshape(r, tq, dh)

    reset()
    sel = sel_ref[0, 0]
    n_kt = ((qi + 1) * tq + tk - 1) // tk

    def slc_body(kt, carry):
        k0 = pl.multiple_of(kt * tk, tk)
        kpos = k0 + lax.broadcasted_iota(jnp.int32, (1, tk), 1)
        eblk = lax.broadcasted_iota(jnp.int32, (nsel, tk), 0)
        ecol = k0 + lax.broadcasted_iota(jnp.int32, (nsel, tk), 1)
        expand = jnp.where(lax.shift_right_logical(ecol, 6) == eblk, 1.0, 0.0).astype(BF16)
        chosen = jnp.dot(sel, expand, preferred_element_type=F32) > 0.5
        step(ks_ref[0, 0, pl.ds(k0, tk), :], vs_ref[0, 0, pl.ds(k0, tk), :], chosen & (kpos <= qpos))
        return carry

    lax.fori_loop(0, n_kt, slc_body, 0)
    o_slc = result()

    reset()
    kt_hi = qi + 1
    kt_lo = jnp.maximum(qi - WINDOW // tq, 0)

    def win_body(kt, carry):
        k0 = pl.multiple_of(kt * tq, tq)
        kpos = k0 + lax.broadcasted_iota(jnp.int32, (1, tq), 1)
        dpos = qpos - kpos
        step(kw_ref[0, 0, pl.ds(k0, tq), :], vw_ref[0, 0, pl.ds(k0, tq), :], (dpos >= 0) & (dpos < WINDOW))
        return carry

    lax.fori_loop(kt_lo, kt_hi, win_body, 0)
    o_win = result()

    gates = jax.nn.sigmoid(gate_ref[0, 0])
    for h in range(r):
        o_ref[0, 0, h] = (gates[:, 3 * h:3 * h + 1] * ocmp_ref[0, 0, h]
                          + gates[:, 3 * h + 1:3 * h + 2] * o_slc[h]
                          + gates[:, 3 * h + 2:3 * h + 3] * o_win[h])


def nsa_attention(q4, ks, vs, kw, vw, sel, ocmp, gate_g, tq, tk):
    b, g, r, sq, dh = q4.shape
    t = ks.shape[2]
    nsel = sel.shape[3]
    assert sq == t and sq % tq == 0 and t % tk == 0 and tk % tq == 0 and WINDOW % tq == 0
    kv_spec = pl.BlockSpec((1, 1, t, dh), lambda i, j, k: (i, j, 0, 0))
    q_spec = pl.BlockSpec((1, 1, r, tq, dh), lambda i, j, k: (i, j, 0, k, 0))
    return pl.pallas_call(
        functools.partial(_nsa_attn_kernel, tq=tq, tk=tk, nsel=nsel),
        grid=(b, g, sq // tq),
        in_specs=[q_spec, kv_spec, kv_spec, kv_spec, kv_spec,
                  pl.BlockSpec((1, 1, tq, nsel), lambda i, j, k: (i, j, k, 0)),
                  q_spec,
                  pl.BlockSpec((1, 1, tq, LANES), lambda i, j, k: (i, j, k, 0))],
        out_specs=q_spec,
        out_shape=jax.ShapeDtypeStruct((b, g, r, sq, dh), F32),
        scratch_shapes=[pltpu.VMEM((r * tq, 1), F32), pltpu.VMEM((r * tq, 1), F32),
                        pltpu.VMEM((r * tq, dh), F32)],
        compiler_params=_cp(("parallel", "parallel", "parallel")),
        name="nsa_attention",
    )(q4, ks, vs, kw, vw, sel, ocmp, gate_g)


def _deltanet_kernel(qkv_ref, z_ref, sm_ref, cw_ref, alog_ref, dtb_ref, gn_ref, o_ref, s_out_ref,
                     xbuf, s_sc):
    c = pl.program_id(1)
    ch = DN_CHUNK
    n_pairs = DN_HEADS // 2
    two = 2 * ch

    @pl.when(c == 0)
    def _():
        xbuf[0:8, :] = jnp.zeros((8, DN_CONV_DIM), F32)
        s_sc[...] = jnp.zeros_like(s_sc)

    xbuf[8:8 + ch, :] = qkv_ref[0]
    conv = None
    for w in range(CONV_WIDTH):
        term = xbuf[5 + w:5 + w + ch, :] * cw_ref[w:w + 1, :]
        conv = term if conv is None else conv + term
    xbuf[0:8, :] = xbuf[ch:ch + 8, :]
    act = _silu(conv)

    sm = sm_ref[0]
    beta_all = jax.nn.sigmoid(sm)
    g_all = -jnp.exp(alog_ref[...]) * _softplus(sm + dtb_ref[...])
    ti = lax.broadcasted_iota(jnp.int32, (ch, ch), 0)
    tj = lax.broadcasted_iota(jnp.int32, (ch, ch), 1)
    gcum_all = _dot_exact_lhs01(jnp.where(ti >= tj, 1.0, 0.0), g_all)

    lane = lax.broadcasted_iota(jnp.int32, (ch, LANES), 1)
    lo = lane < HEAD_DIM
    row2 = lax.broadcasted_iota(jnp.int32, (two, two), 0)
    col2 = lax.broadcasted_iota(jnp.int32, (two, two), 1)
    same = (row2 >= ch) == (col2 >= ch)
    incl = same & (row2 >= col2)
    strict = same & (row2 > col2)
    top = lax.broadcasted_iota(jnp.int32, (two, 1), 0) < ch

    def seg_sum(x):
        s_lo = jnp.sum(jnp.where(lo, x, 0.0), axis=-1, keepdims=True)
        s_hi = jnp.sum(jnp.where(lo, 0.0, x), axis=-1, keepdims=True)
        return jnp.where(lo, s_lo, s_hi)

    def stack(x):
        return jnp.concatenate([jnp.where(lo, x, 0.0), jnp.where(lo, 0.0, x)], axis=0)

    def col2x(a, b):
        return jnp.concatenate([jnp.broadcast_to(a, (ch, LANES)), jnp.broadcast_to(b, (ch, LANES))], axis=0)

    for p in range(n_pairs):
        c0 = p * LANES
        qp = act[:, c0:c0 + LANES]
        kp = act[:, DN_WIDTH + c0:DN_WIDTH + c0 + LANES]
        vp = act[:, 2 * DN_WIDTH + c0:2 * DN_WIDTH + c0 + LANES]
        qp = qp * lax.rsqrt(seg_sum(qp * qp) + 1e-6) * (HEAD_DIM ** -0.5)
        kp = kp * lax.rsqrt(seg_sum(kp * kp) + 1e-6)
        q2, k2, v2 = stack(qp), stack(kp), stack(vp)
        h0, h1 = 2 * p, 2 * p + 1
        beta2 = col2x(beta_all[:, h0:h0 + 1], beta_all[:, h1:h1 + 1])
        gc2 = col2x(gcum_all[:, DN_HEADS + h0:DN_HEADS + h0 + 1], gcum_all[:, DN_HEADS + h1:DN_HEADS + h1 + 1])
        gl2 = jnp.where(top, gcum_all[ch - 1:ch, DN_HEADS + h0:DN_HEADS + h0 + 1],
                        gcum_all[ch - 1:ch, DN_HEADS + h1:DN_HEADS + h1 + 1])
        decay = jnp.exp(jnp.where(incl, gc2 - gc2.T, NEG_INF))
        kb2 = k2 * beta2
        a_mat = jnp.where(strict, _dot_nt(kb2, k2) * decay, 0.0)
        aqk = jnp.where(incl, _dot_nt(q2, k2) * decay, 0.0)
        s_old = s_sc[p]
        egc = jnp.exp(gc2)
        x = beta2 * (v2 - egc * _dot(k2, s_old))
        pw = -a_mat
        n_lvl = ch.bit_length() - 1
        for lvl in range(n_lvl):
            x = x + _dot(pw, x, DN_SOLVE_PASSES)
            if lvl + 1 < n_lvl:
                pw = _dot(pw, pw, DN_SOLVE_PASSES)
        o2 = _dot(q2 * egc, s_old) + _dot(aqk, x)
        kdec = k2 * jnp.exp(gl2 - gc2)
        s_sc[p] = s_old * jnp.exp(gl2) + _dot(kdec.T, x)
        o_pair = o2[0:ch] + o2[ch:two]
        zp = z_ref[0, :, c0:c0 + LANES]
        inv = lax.rsqrt(seg_sum(o_pair * o_pair) * (1.0 / HEAD_DIM) + RMS_EPS)
        o_ref[0, :, c0:c0 + LANES] = o_pair * inv * gn_ref[...] * _silu(zp)

    @pl.when(c == pl.num_programs(1) - 1)
    def _():
        s_out_ref[0] = s_sc[...]


def deltanet_prompt(proj3, conv_w, a_log, dt_bias, norm_gain):
    b, t, _ = proj3.shape
    ch = DN_CHUNK
    assert t % ch == 0
    pad = jnp.zeros((LANES - 2 * DN_HEADS,), F32)
    alog_row = jnp.concatenate([jnp.zeros((DN_HEADS,), F32), a_log, pad]).reshape(1, LANES)
    dtb_row = jnp.concatenate([jnp.zeros((DN_HEADS,), F32), dt_bias, pad]).reshape(1, LANES)
    gn_row = jnp.concatenate([norm_gain, norm_gain]).reshape(1, LANES)
    n_pairs = DN_HEADS // 2
    return pl.pallas_call(
        _deltanet_kernel,
        grid=(b, t // ch),
        in_specs=[pl.BlockSpec((1, ch, DN_CONV_DIM), lambda i, j: (i, j, P_QKV // DN_CONV_DIM)),
                  pl.BlockSpec((1, ch, DN_WIDTH), lambda i, j: (i, j, P_Z // DN_WIDTH)),
                  pl.BlockSpec((1, ch, LANES), lambda i, j: (i, j, P_SMALL // LANES)),
                  pl.BlockSpec((CONV_WIDTH, DN_CONV_DIM), lambda i, j: (0, 0)),
                  pl.BlockSpec((1, LANES), lambda i, j: (0, 0)),
                  pl.BlockSpec((1, LANES), lambda i, j: (0, 0)),
                  pl.BlockSpec((1, LANES), lambda i, j: (0, 0))],
        out_specs=[pl.BlockSpec((1, ch, DN_WIDTH), lambda i, j: (i, j, 0)),
                   pl.BlockSpec((1, n_pairs, 2 * ch, LANES), lambda i, j: (i, 0, 0, 0))],
        out_shape=[jax.ShapeDtypeStruct((b, t, DN_WIDTH), F32),
                   jax.ShapeDtypeStruct((b, n_pairs, 2 * ch, LANES), F32)],
        scratch_shapes=[pltpu.VMEM((ch + 8, DN_CONV_DIM), F32), pltpu.VMEM((n_pairs, 2 * ch, LANES), F32)],
        compiler_params=_cp(("parallel", "arbitrary")),
        name="deltanet_prompt",
    )(proj3, proj3, proj3, conv_w, alog_row, dtb_row, gn_row)


def _pairs_to_heads(s_pairs):
    d = HEAD_DIM
    return jnp.stack([s_pairs[:, :, :d, :d], s_pairs[:, :, d:, d:]], axis=2).reshape(
        s_pairs.shape[0], DN_HEADS, d, d)


def _dn_step_prep_kernel(qkv_ref, cs_ref, sm_ref, cw_ref, alog_ref, dtb_ref, ones_ref,
                         q_ref, k_ref, v_ref, sc_ref):
    conv = qkv_ref[...] * cw_ref[CONV_WIDTH - 1:CONV_WIDTH, :]
    for w in range(CONV_WIDTH - 1):
        conv = conv + cs_ref[w] * cw_ref[w:w + 1, :]
    act = _silu(conv)
    q = act[:, 0:DN_WIDTH]
    k = act[:, DN_WIDTH:2 * DN_WIDTH]

    def seg_sum(x):
        return _dot_exact_rhs01(x, ones_ref[...])

    q_ref[...] = q * lax.rsqrt(seg_sum(q * q) + 1e-6) * (HEAD_DIM ** -0.5)
    k_ref[...] = k * lax.rsqrt(seg_sum(k * k) + 1e-6)
    v_ref[...] = act[:, 2 * DN_WIDTH:]
    sm = sm_ref[...]
    g = -jnp.exp(alog_ref[...]) * _softplus(sm + dtb_ref[...])
    lane = lax.broadcasted_iota(jnp.int32, sm.shape, 1)
    sc_ref[...] = jnp.where(lane < DN_HEADS, jax.nn.sigmoid(sm), jnp.exp(g))


def _dn_step_kernel(kt_ref, qt_ref, v_ref, sc_ref, z_ref, gn_ref, s_ref, o_ref, s_out_ref, *, nseq):
    hds, d = DN_HEADS, HEAD_DIM
    for j in range(nseq):
        s_old = s_ref[j]
        kcol = kt_ref[0, :, j:j + 1]
        qcol = qt_ref[0, :, j:j + 1]
        beta = sc_ref[j, :, 0:1]
        eg = sc_ref[j, :, 1:2]
        v = v_ref[j]
        ks = jnp.sum((kcol * s_old).reshape(hds, d, d), axis=1)
        qs = jnp.sum((qcol * s_old).reshape(hds, d, d), axis=1)
        qk = jnp.sum((qcol * kcol).reshape(hds, d, 1), axis=1)
        v_new = beta * (v - eg * ks)
        o = eg * qs + qk * v_new
        inv = lax.rsqrt(jnp.mean(o * o, axis=-1, keepdims=True) + RMS_EPS)
        o_ref[j] = o * inv * gn_ref[...] * _silu(z_ref[j])
        s3 = s_old.reshape(hds, d, d) * eg[:, :, None] + kcol.reshape(hds, d, 1) * v_new[:, None, :]
        s_out_ref[j] = s3.reshape(hds * d, d)


def deltanet_sample(proj_s, conv_state, rec_state, conv_w, a_log, dt_bias, norm_gain, nseq=8):
    n = proj_s.shape[0]
    assert n % nseq == 0
    hds, d = DN_HEADS, HEAD_DIM
    pad = jnp.zeros((LANES - 2 * hds,), F32)
    alog_row = jnp.concatenate([jnp.zeros((hds,), F32), a_log, pad]).reshape(1, LANES)
    dtb_row = jnp.concatenate([jnp.zeros((hds,), F32), dt_bias, pad]).reshape(1, LANES)
    head_of = jnp.arange(DN_WIDTH) // d
    ones_bd = (head_of[:, None] == head_of[None, :]).astype(BF16)
    cs = jnp.transpose(conv_state, (1, 0, 2))
    full = lambda shape: pl.BlockSpec(shape, lambda i: (0,) * len(shape))
    q, k, v, sc = pl.pallas_call(
        _dn_step_prep_kernel,
        grid=(1,),
        in_specs=[pl.BlockSpec((n, DN_CONV_DIM), lambda i: (0, P_QKV // DN_CONV_DIM)),
                  full((CONV_WIDTH - 1, n, DN_CONV_DIM)),
                  pl.BlockSpec((n, LANES), lambda i: (0, P_SMALL // LANES)),
                  full((CONV_WIDTH, DN_CONV_DIM)), full((1, LANES)), full((1, LANES)),
                  full((DN_WIDTH, DN_WIDTH))],
        out_specs=[full((n, DN_WIDTH)), full((n, DN_WIDTH)), full((n, DN_WIDTH)), full((n, LANES))],
        out_shape=[jax.ShapeDtypeStruct((n, DN_WIDTH), F32)] * 3 + [jax.ShapeDtypeStruct((n, LANES), F32)],
        compiler_params=_cp(("arbitrary",)),
        name="dn_step_prep",
    )(proj_s, cs, proj_s, conv_w, alog_row, dtb_row, ones_bd)
    ng = n // nseq
    kt = jnp.transpose(k.reshape(ng, nseq, DN_WIDTH), (0, 2, 1))
    qt = jnp.transpose(q.reshape(ng, nseq, DN_WIDTH), (0, 2, 1))
    sc3 = jnp.stack([sc[:, :hds], sc[:, hds:2 * hds]], axis=-1)
    z = proj_s[:, P_Z:P_Z + DN_WIDTH].reshape(n, hds, d)
    o, s_new = pl.pallas_call(
        functools.partial(_dn_step_kernel, nseq=nseq),
        grid=(ng,),
        in_specs=[pl.BlockSpec((1, DN_WIDTH, nseq), lambda i: (i, 0, 0)),
                  pl.BlockSpec((1, DN_WIDTH, nseq), lambda i: (i, 0, 0)),
                  pl.BlockSpec((nseq, hds, d), lambda i: (i, 0, 0)),
                  pl.BlockSpec((nseq, hds, 2), lambda i: (i, 0, 0)),
                  pl.BlockSpec((nseq, hds, d), lambda i: (i, 0, 0)),
                  pl.BlockSpec((1, d), lambda i: (0, 0)),
                  pl.BlockSpec((nseq, hds * d, d), lambda i: (i, 0, 0))],
        out_specs=[pl.BlockSpec((nseq, hds, d), lambda i: (i, 0, 0)),
                   pl.BlockSpec((nseq, hds * d, d), lambda i: (i, 0, 0))],
        out_shape=[jax.ShapeDtypeStruct((n, hds, d), F32), jax.ShapeDtypeStruct((n, hds * d, d), F32)],
        compiler_params=_cp(("parallel",)),
        name="dn_step",
    )(kt, qt, v.reshape(n, hds, d), sc3, z, norm_gain.reshape(1, d), rec_state.reshape(n, hds * d, d))
    return o.reshape(n, DN_WIDTH), s_new.reshape(n, hds, d, d)


def _nsa_decode_kernel(pt_ref, q_ref, new_ref, gate_ref, win_ref, *refs, n_pages):
    kc_refs = refs[:n_pages]
    pg_refs = refs[n_pages:2 * n_pages]
    o_ref = refs[2 * n_pages]
    kc_sc = refs[2 * n_pages + 1]
    del pt_ref
    dh, r, g2, nh = HEAD_DIM, NSA_GROUP, NSA_KV_HEADS, NSA_HEADS
    past = n_pages * PAGE_SIZE
    nb = past // CMP_BLOCK
    nsel = nb + 1
    wlen = win_ref.shape[4]
    for j in range(n_pages):
        kc_sc[j:j + 1, :] = kc_refs[j][0]
    kcv = kc_sc[...]
    new = new_ref[0]
    q8 = q_ref[0] * ATTN_SCALE
    head = lax.broadcasted_iota(jnp.int32, (nh, 1), 0)
    lane = lax.broadcasted_iota(jnp.int32, (nh, LANES), 1)
    blk = jnp.where(lane < n_pages, 2 * lane, jnp.where(lane < nb, 2 * (lane - n_pages) + 1, lane))
    blk_f = blk.astype(F32)
    tok_lo = lane < CMP_BLOCK
    wp = lax.broadcasted_iota(jnp.int32, (1, wlen + LANES), 1)
    dpos = wlen - wp
    wmask = (dpos >= 0) & (dpos < WINDOW) & (past - wlen + wp >= 0)
    new_only = lane == 0
    outs = []
    for g in range(g2):
        in_g = (head >= g * r) & (head < (g + 1) * r)
        kc = jnp.concatenate([kcv[:, g * LANES:g * LANES + dh], kcv[:, g * LANES + dh:(g + 1) * LANES]], axis=0)
        vc = jnp.concatenate([kcv[:, (g2 + g) * LANES:(g2 + g) * LANES + dh],
                              kcv[:, (g2 + g) * LANES + dh:(g2 + g + 1) * LANES]], axis=0)
        s = _dot_nt(q8, kc)
        p = jnp.exp(s - jnp.max(s, axis=-1, keepdims=True))
        p = p / jnp.maximum(jnp.sum(p, axis=-1, keepdims=True), 1e-30)
        o_cmp = _dot(p, vc)
        imp = jnp.sum(jnp.where(in_g, p, 0.0), axis=0, keepdims=True)
        imp = jnp.broadcast_to(jnp.concatenate([imp, jnp.zeros((1, LANES - nb), F32)], axis=1), (nh, LANES))
        valid = lane < nsel
        forced = valid & ((blk == 0) | (nb - blk < N_LOCAL_BLOCKS))
        score = jnp.where(valid, imp + jnp.where(forced, FORCE_SCORE, 0.0), -1.0)
        sel = _topk_mask(score, min(TOP_N, nsel), blk_f)

        def picked(i):
            return jnp.sum(jnp.where(lane == i, sel, 0.0), axis=-1, keepdims=True)

        parts, masks = [], []
        for j in range(n_pages):
            parts.append(_dot(q8, pg_refs[j][0, 0, g]))
            masks.append(jnp.where(tok_lo, picked(j), picked(n_pages + j)))
        s_new = jnp.sum(q8 * new[:, (4 + g) * dh:(5 + g) * dh], axis=-1, keepdims=True)
        parts.append(jnp.broadcast_to(s_new, (nh, LANES)))
        masks.append(jnp.where(new_only, picked(nb), 0.0))
        pm = _masked_softmax(jnp.concatenate(parts, axis=1), jnp.concatenate(masks, axis=1) > 0.5)
        o_slc = pm[:, past:past + 1] * new[:, (6 + g) * dh:(7 + g) * dh]
        for j in range(n_pages):
            o_slc = o_slc + _dot_nt(pm[:, j * PAGE_SIZE:(j + 1) * PAGE_SIZE], pg_refs[j][0, 1, g])
        sw = _dot(q8, win_ref[0, 0, g])
        sw_new = jnp.sum(q8 * new[:, (8 + g) * dh:(9 + g) * dh], axis=-1, keepdims=True)
        pw = _masked_softmax(jnp.concatenate([sw, jnp.broadcast_to(sw_new, (nh, LANES))], axis=1), wmask)
        o_win = pw[:, wlen:wlen + 1] * new[:, (10 + g) * dh:(11 + g) * dh] + _dot_nt(pw[:, 0:wlen], win_ref[0, 1, g])
        outs.append((in_g, o_cmp, o_slc, o_win))
    gates = jnp.broadcast_to(jax.nn.sigmoid(gate_ref[0]), (nh, LANES))

    def gate(branch):
        return jnp.sum(jnp.where(lane == 3 * head + branch, gates, 0.0), axis=-1, keepdims=True)

    total = jnp.zeros((nh, dh), F32)
    for in_g, o_cmp, o_slc, o_win in outs:
        total = jnp.where(in_g, gate(0) * o_cmp + gate(1) * o_slc + gate(2) * o_win, total)
    o_ref[0] = total


def nsa_decode(page_table, q3, new_row, gate_row, win_t, kc_phys, cache_t):
    n, n_pages = page_table.shape
    dh = HEAD_DIM
    g2 = NSA_KV_HEADS
    wlen = win_t.shape[4]
    assert 2 * n_pages + 1 <= LANES and PAGE_SIZE == 2 * CMP_BLOCK and PAGE_SIZE == LANES

    def kc_map(j):
        return lambda i, pt: (pt[i, j], 0, 0)

    def slc_map(j):
        return lambda i, pt: (pt[i, j], 1, 0, 0, 0)

    in_specs = [pl.BlockSpec((1, NSA_HEADS, dh), lambda i, pt: (i, 0, 0)),
                pl.BlockSpec((1, 1, KV_COLS), lambda i, pt: (i, 0, 0)),
                pl.BlockSpec((1, 1, LANES), lambda i, pt: (i, 0, 0)),
                pl.BlockSpec((1, 2, g2, dh, wlen), lambda i, pt: (i, 0, 0, 0, 0))]
    in_specs += [pl.BlockSpec((1, 1, 4 * LANES), kc_map(j)) for j in range(n_pages)]
    in_specs += [pl.BlockSpec((1, 2, g2, dh, PAGE_SIZE), slc_map(j)) for j in range(n_pages)]
    grid_spec = pltpu.PrefetchScalarGridSpec(
        num_scalar_prefetch=1, grid=(n,), in_specs=in_specs,
        out_specs=pl.BlockSpec((1, NSA_HEADS, dh), lambda i, pt: (i, 0, 0)),
        scratch_shapes=[pltpu.VMEM((n_pages, 4 * LANES), F32)])
    return pl.pallas_call(
        functools.partial(_nsa_decode_kernel, n_pages=n_pages),
        grid_spec=grid_spec,
        out_shape=jax.ShapeDtypeStruct((n, NSA_HEADS, dh), F32),
        compiler_params=_cp(("arbitrary",)),
        name="nsa_decode",
    )(page_table, q3, new_row, gate_row, win_t, *([kc_phys] * n_pages), *([cache_t] * n_pages))


def _compress_pages_kernel(x_ref, wd_ref, ped_ref, b1_ref, w2_ref, o_ref, acc_ref, *, dd):
    j = pl.program_id(1)
    g2 = NSA_KV_HEADS

    @pl.when(j == 0)
    def _():
        acc_ref[...] = jnp.zeros_like(acc_ref)

    for kind in range(2):
        for g in range(g2):
            part = None
            for dl in range(dd):
                xs = x_ref[:, kind, g, dl, :] + ped_ref[kind, dl:dl + 1, :]
                d = jnp.dot(xs.astype(BF16), wd_ref[kind, dl], preferred_element_type=F32)
                part = d if part is None else part + d
            acc_ref[kind * g2 + g] += part

    @pl.when(j == pl.num_programs(1) - 1)
    def _():
        for kind in range(2):
            for g in range(g2):
                h = jnp.maximum(acc_ref[kind * g2 + g] + b1_ref[kind:kind + 1, :], 0.0)
                o_ref[:, 0, (kind * g2 + g) * LANES:(kind * g2 + g + 1) * LANES] = jnp.dot(
                    h.astype(BF16), w2_ref[kind], preferred_element_type=F32)


def compress_pages(cache_t, cwp, bp, dd=8):
    n_phys = cache_t.shape[0]
    g2, dh = NSA_KV_HEADS, HEAD_DIM
    assert n_phys % bp == 0 and dh % dd == 0
    wd, ped, b1h, w2h = cwp
    return pl.pallas_call(
        functools.partial(_compress_pages_kernel, dd=dd),
        grid=(n_phys // bp, dh // dd),
        in_specs=[pl.BlockSpec((bp, 2, g2, dd, PAGE_SIZE), lambda i, j: (i, 0, 0, j, 0)),
                  pl.BlockSpec((2, dd, PAGE_SIZE, 2 * CMP_HIDDEN), lambda i, j: (0, j, 0, 0)),
                  pl.BlockSpec((2, dd, PAGE_SIZE), lambda i, j: (0, j, 0)),
                  pl.BlockSpec((2, 2 * CMP_HIDDEN), lambda i, j: (0, 0)),
                  pl.BlockSpec((2, 2 * CMP_HIDDEN, LANES), lambda i, j: (0, 0, 0))],
        out_specs=pl.BlockSpec((bp, 1, 2 * g2 * LANES), lambda i, j: (i, 0, 0)),
        out_shape=jax.ShapeDtypeStruct((n_phys, 1, 2 * g2 * LANES), F32),
        scratch_shapes=[pltpu.VMEM((2 * g2, bp, 2 * CMP_HIDDEN), F32)],
        compiler_params=_cp(("parallel", "arbitrary")),
        name="compress_pages",
    )(cache_t, wd, ped, b1h, w2h)


def _compress_page_weights(cmp_pe, cmp_w1, cmp_b1, cmp_w2):
    w1t = jnp.transpose(cmp_w1.reshape(2, CMP_BLOCK, HEAD_DIM, CMP_HIDDEN), (0, 2, 1, 3))
    z = jnp.zeros_like(w1t)
    wd = jnp.concatenate([jnp.concatenate([w1t, z], -1), jnp.concatenate([z, w1t], -1)], axis=2).astype(BF16)
    pet = jnp.transpose(cmp_pe, (0, 2, 1))
    ped = jnp.concatenate([pet, pet], -1)
    b1h = jnp.concatenate([cmp_b1, cmp_b1], -1)
    z2 = jnp.zeros_like(cmp_w2)
    w2h = jnp.concatenate([jnp.concatenate([cmp_w2, z2], -1), jnp.concatenate([z2, cmp_w2], -1)], axis=1).astype(BF16)
    return wd, ped, b1h, w2h


def _permute_w_in(w):
    d = w.shape[0]
    off_b = DN_CONV_DIM + DN_WIDTH
    off_q = off_b + 2 * DN_HEADS
    off_g = off_q + NSA_WIDTH + KV_COLS
    n_g = 3 * NSA_HEADS
    pad = jnp.zeros((d, LANES - 2 * DN_HEADS - n_g), w.dtype)
    return jnp.concatenate([w[:, :off_b], w[:, off_q:off_g], w[:, off_b:off_q], w[:, off_g:off_g + n_g], pad], axis=1)


def _row_tile(n, cap):
    t = min(n, cap)
    while n % t:
        t //= 2
    return t


def _trunk_tail(x2, mixer_dn, mixer_nsa, mem_kv3, bshape, lw, final_norm):
    n, d = x2.shape
    b, t = bshape
    tm = _row_tile(n, 512)
    x2 = matmul_residual(x2, [mixer_dn, mixer_nsa], [lw["w_out_dn"], lw["w_out_nsa"]], tm)
    qm = rms_matmul(x2, lw["ln_mem"], lw["w_mem_q"], tm, 512)
    q3 = qm.reshape(b, t, d)
    sub = 8
    if t < sub:
        q3 = jnp.broadcast_to(q3[:, :1], (b, sub, d))
    att = mem_attention(q3, mem_kv3, _row_tile(q3.shape[1], 512))[:, :t]
    x2 = matmul_residual(x2, [att.reshape(n, d)], [lw["w_mem_o"]], tm)
    return ffn(x2, lw["ln_ffn"], lw["w_up"], lw["w_down"], lw["ln_final"], final_norm, _row_tile(n, 1024), 512)


def _gate_groups(small):
    per = 3 * NSA_GROUP
    g0 = 2 * DN_HEADS
    parts = []
    for g in range(NSA_KV_HEADS):
        gl = small[..., g0 + g * per:g0 + (g + 1) * per]
        parts.append(jnp.concatenate([gl, jnp.zeros(gl.shape[:-1] + (LANES - per,), gl.dtype)], axis=-1))
    return jnp.stack(parts, axis=-2)


def _prompt_layer(xp, mem_prompt, lw, cw, final_norm):
    b, s, d = xp.shape
    n = b * s
    x2 = xp.reshape(n, d)
    proj = rms_matmul(x2, lw["ln_mix"], lw["w_in"], _row_tile(n, 512), 1152)
    proj3 = proj.reshape(b, s, P_COLS)
    dn_out, s_pairs = deltanet_prompt(proj3, lw["dn_conv_w"], lw["dn_a_log"], lw["dn_dt_bias"], lw["dn_norm"])
    p_conv = proj3[:, s - (CONV_WIDTH - 1):, P_QKV:P_QKV + DN_CONV_DIM]
    p_rec = _pairs_to_heads(s_pairs)
    kv = proj3[:, :, P_NKV:P_NKV + KV_COLS]
    kv6 = kv.reshape(b, s, NSA_KV_KINDS, NSA_KV_HEADS, HEAD_DIM)
    nb = s // CMP_BLOCK
    kcv = compress_blocks(kv[:, :nb * CMP_BLOCK].reshape(b * nb, CMP_BLOCK * KV_COLS), KV_COLS, cw,
                          _row_tile(b * nb, 256))
    kcv = kcv.reshape(b, nb, 2, NSA_KV_HEADS, HEAD_DIM)
    kc = jnp.transpose(kcv[:, :, 0], (0, 2, 1, 3))
    vc = jnp.transpose(kcv[:, :, 1], (0, 2, 1, 3))
    q4 = jnp.transpose(proj3[:, :, P_NQ:P_NQ + NSA_WIDTH].reshape(b, s, NSA_KV_HEADS, NSA_GROUP, HEAD_DIM),
                       (0, 2, 3, 1, 4))
    ocmp, sel = nsa_cmp_topk(q4, kc, vc, _row_tile(s, 512))
    kvt = jnp.transpose(kv6, (2, 0, 3, 1, 4)).astype(BF16)
    gate_g = jnp.transpose(_gate_groups(proj3[:, :, P_SMALL:P_SMALL + LANES]), (0, 2, 1, 3))
    o4 = nsa_attention(q4, kvt[2], kvt[3], kvt[4], kvt[5], sel, ocmp, gate_g, 128, 512)
    nsa_out = jnp.transpose(o4, (0, 3, 1, 2, 4)).reshape(n, NSA_WIDTH)
    m = mem_prompt.shape[1]
    mem_kv = rms_matmul(mem_prompt.reshape(b * m, d), lw["ln_memkv"], lw["w_mem_kv"], _row_tile(b * m, 512), 512)
    mem_kv5 = mem_kv.reshape(b, m, 2, MEM_HEADS, d // MEM_HEADS)
    y = _trunk_tail(x2, dn_out.reshape(n, DN_WIDTH), nsa_out, mem_kv5, (b, s), lw, final_norm)
    wk = min(WINDOW, s)
    return y.reshape(b, s, d), kv6[:, :, :4], kv6[:, s - wk:, 4:], mem_kv5, p_conv, p_rec


def _sample_layer(xs, cache_nsa, cache_win, cache_mem, conv_state, rec_state, page_table, lw, cwp, final_norm):
    db, ds, d = xs.shape
    assert ds == 1
    n = db
    x2 = xs.reshape(n, d)
    proj = rms_matmul(x2, lw["ln_mix"], lw["w_in"], _row_tile(n, 512), 1152)
    dn_out, s_rec = deltanet_sample(proj, conv_state, rec_state, lw["dn_conv_w"], lw["dn_a_log"], lw["dn_dt_bias"],
                                    lw["dn_norm"])
    s_conv = jnp.concatenate([conv_state[:, 1:], proj[:, None, P_QKV:P_QKV + DN_CONV_DIM]], axis=1)
    n_phys = cache_nsa.shape[0]
    cache_t = jnp.transpose(cache_nsa, (0, 2, 3, 4, 1))
    win_t = jnp.transpose(cache_win, (0, 2, 3, 4, 1))
    kc_phys = compress_pages(cache_t, cwp, _row_tile(n_phys, 256))
    kv_new = proj[:, P_NKV:P_NKV + KV_COLS]
    o8 = nsa_decode(page_table, proj[:, P_NQ:P_NQ + NSA_WIDTH].reshape(n, NSA_HEADS, HEAD_DIM),
                    kv_new.reshape(n, 1, KV_COLS),
                    _flat_gates(proj[:, P_SMALL:P_SMALL + LANES]).reshape(n, 1, LANES),
                    win_t, kc_phys, cache_t)
    kv6 = kv_new.reshape(n, 1, NSA_KV_KINDS, NSA_KV_HEADS, HEAD_DIM)
    s_win = jnp.concatenate([cache_win, kv6[:, :, 4:]], axis=1)[:, 1:]
    y = _trunk_tail(x2, dn_out, o8.reshape(n, NSA_WIDTH), cache_mem, (n, 1), lw, final_norm)
    return y.reshape(db, ds, d), kv6[:, :, :4], s_win, s_conv, s_rec


def _flat_gates(small):
    g0 = 2 * DN_HEADS
    n_g = 3 * NSA_HEADS
    gl = small[..., g0:g0 + n_g]
    return jnp.concatenate([gl, jnp.zeros(gl.shape[:-1] + (LANES - n_g,), gl.dtype)], axis=-1)


def kernel(x_prompt, x_sample, mem_prompt, cache_nsa_kv, cache_win_kv, cache_mem_kv, state_dn_conv, state_dn_rec, page_table, ln_mix, w_in, dn_conv_w, dn_a_log, dn_dt_bias, dn_norm, cmp_pe, cmp_w1, cmp_b1, cmp_w2, w_out, ln_mem, ln_memkv, w_mem_q, w_mem_kv, w_mem_o, ln_ffn, w_up, w_down, ln_final):
    depth = w_in.shape[0]
    xp, xs = x_prompt, x_sample
    outs_p = [[] for _ in range(5)]
    outs_s = [[] for _ in range(4)]
    for l in range(depth):
        lw = {
            "ln_mix": ln_mix[l], "w_in": _permute_w_in(w_in[l]).astype(BF16),
            "dn_conv_w": dn_conv_w[l], "dn_a_log": dn_a_log[l], "dn_dt_bias": dn_dt_bias[l], "dn_norm": dn_norm[l],
            "w_out_dn": w_out[l][:DN_WIDTH].astype(BF16), "w_out_nsa": w_out[l][DN_WIDTH:].astype(BF16),
            "ln_mem": ln_mem[l], "ln_memkv": ln_memkv[l], "w_mem_q": w_mem_q[l].astype(BF16),
            "w_mem_kv": w_mem_kv[l].astype(BF16), "w_mem_o": w_mem_o[l].astype(BF16),
            "ln_ffn": ln_ffn[l], "w_up": w_up[l].astype(BF16), "w_down": w_down[l].astype(BF16),
            "ln_final": ln_final,
        }
        cw = _compress_weights(cmp_pe[l], cmp_w1[l], cmp_b1[l], cmp_w2[l])
        last = l == depth - 1
        xp, p_nsa, p_win, p_mem, p_conv, p_rec = _prompt_layer(xp, mem_prompt, lw, cw, last)
        for acc, val in zip(outs_p, (p_nsa, p_win, p_mem, p_conv, p_rec)):
            acc.append(val)
        xs, s_nsa, s_win, s_conv, s_rec = _sample_layer(
            xs, cache_nsa_kv[l], cache_win_kv[l], cache_mem_kv[l], state_dn_conv[l], state_dn_rec[l],
            page_table, lw, _compress_page_weights(cmp_pe[l], cmp_w1[l], cmp_b1[l], cmp_w2[l]), last)
        for acc, val in zip(outs_s, (s_nsa, s_win, s_conv, s_rec)):
            acc.append(val)
    return (xp, xs) + tuple(jnp.stack(a) for a in outs_p) + tuple(jnp.stack(a) for a in outs_s)
```

```python
import functools

import jax
import jax.numpy as jnp
from jax import lax
from jax.experimental import pallas as pl
from jax.experimental.pallas import tpu as pltpu

F32 = jnp.float32
BF16 = jnp.bfloat16

HEAD_DIM = 64
DN_HEADS = 8
NSA_HEADS = 8
NSA_KV_HEADS = 2
NSA_GROUP = NSA_HEADS // NSA_KV_HEADS
DN_WIDTH = DN_HEADS * HEAD_DIM
NSA_WIDTH = NSA_HEADS * HEAD_DIM
CONV_WIDTH = 4
DN_CONV_DIM = 3 * DN_WIDTH
DN_CHUNK = 64
CMP_BLOCK = 64
SEL_BLOCK = 64
TOP_N = 16
N_LOCAL_BLOCKS = 2
WINDOW = 512
CMP_HIDDEN = 128
NSA_KV_KINDS = 6
MEM_HEADS = 4
PAGE_SIZE = 128
RMS_EPS = 1e-6
FORCE_SCORE = 1e3
NEG_INF = -1e30
ATTN_SCALE = HEAD_DIM ** -0.5

LANES = 128
P_QKV = 0
P_Z = P_QKV + DN_CONV_DIM
P_NQ = P_Z + DN_WIDTH
P_NKV = P_NQ + NSA_WIDTH
P_SMALL = P_NKV + NSA_KV_KINDS * NSA_KV_HEADS * HEAD_DIM
P_COLS = P_SMALL + LANES
KV_COLS = NSA_KV_KINDS * NSA_KV_HEADS * HEAD_DIM
VMEM_LIMIT = 56 * 1024 * 1024
DN_SOLVE_PASSES = 3


def _cp(sem, vmem=VMEM_LIMIT):
    return pltpu.CompilerParams(dimension_semantics=sem, vmem_limit_bytes=vmem)


def _split2(a):
    hi = a.astype(BF16)
    return hi, (a - hi.astype(F32)).astype(BF16)


def _dot(a, b, passes=1):
    if passes == 1:
        return jnp.dot(a.astype(BF16), b.astype(BF16), preferred_element_type=F32)
    ah, al = _split2(a)
    bh, bl = _split2(b)
    return (jnp.dot(ah, bh, preferred_element_type=F32) + jnp.dot(ah, bl, preferred_element_type=F32)
            + jnp.dot(al, bh, preferred_element_type=F32))


def _dot_nt(a, b):
    return lax.dot_general(a.astype(BF16), b.astype(BF16), (((1,), (1,)), ((), ())),
                           preferred_element_type=F32)


def _split3(a):
    hi = a.astype(BF16)
    r1 = a - hi.astype(F32)
    mid = r1.astype(BF16)
    lo = (r1 - mid.astype(F32)).astype(BF16)
    return hi, mid, lo


def _dot_exact_lhs01(a01, b):
    a = a01.astype(BF16)
    hi, mid, lo = _split3(b)
    return (jnp.dot(a, hi, preferred_element_type=F32) + jnp.dot(a, mid, preferred_element_type=F32)
            + jnp.dot(a, lo, preferred_element_type=F32))


def _dot_exact_rhs01(a, b01):
    b = b01.astype(BF16)
    hi, mid, lo = _split3(a)
    return (jnp.dot(hi, b, preferred_element_type=F32) + jnp.dot(mid, b, preferred_element_type=F32)
            + jnp.dot(lo, b, preferred_element_type=F32))


def _rms(x, gain):
    ms = jnp.mean(x * x, axis=-1, keepdims=True)
    return x * lax.rsqrt(ms + RMS_EPS) * gain


def _softplus(x):
    return jnp.maximum(x, 0.0) + jnp.log1p(jnp.exp(-jnp.abs(x)))


def _silu(x):
    return x * jax.nn.sigmoid(x)


def _rms_mm_kernel(x_ref, g_ref, w_ref, o_ref, h_ref):
    @pl.when(pl.program_id(1) == 0)
    def _():
        h_ref[...] = _rms(x_ref[...], g_ref[...]).astype(BF16)

    o_ref[...] = jnp.dot(h_ref[...], w_ref[...], preferred_element_type=F32)


def rms_matmul(x, gain, w_bf16, tm, tn):
    n, d = x.shape
    m = w_bf16.shape[1]
    assert n % tm == 0 and m % tn == 0
    return pl.pallas_call(
        _rms_mm_kernel,
        grid=(n // tm, m // tn),
        in_specs=[pl.BlockSpec((tm, d), lambda i, j: (i, 0)),
                  pl.BlockSpec((1, d), lambda i, j: (0, 0)),
                  pl.BlockSpec((d, tn), lambda i, j: (0, j))],
        out_specs=pl.BlockSpec((tm, tn), lambda i, j: (i, j)),
        out_shape=jax.ShapeDtypeStruct((n, m), F32),
        scratch_shapes=[pltpu.VMEM((tm, d), BF16)],
        compiler_params=_cp(("parallel", "arbitrary")),
        name="rms_matmul",
    )(x, gain.reshape(1, d), w_bf16)


def _mm_res_kernel(*refs, n_in):
    res_ref = refs[0]
    a_refs = refs[1:1 + n_in]
    w_refs = refs[1 + n_in:1 + 2 * n_in]
    o_ref = refs[1 + 2 * n_in]
    acc = res_ref[...]
    for a_ref, w_ref in zip(a_refs, w_refs):
        acc = acc + jnp.dot(a_ref[...].astype(BF16), w_ref[...], preferred_element_type=F32)
    o_ref[...] = acc


def matmul_residual(res, a_list, w_list, tm):
    n, d = res.shape
    assert n % tm == 0
    n_in = len(a_list)
    in_specs = [pl.BlockSpec((tm, d), lambda i: (i, 0))]
    in_specs += [pl.BlockSpec((tm, a.shape[1]), lambda i: (i, 0)) for a in a_list]
    in_specs += [pl.BlockSpec(w.shape, lambda i: (0, 0)) for w in w_list]
    return pl.pallas_call(
        functools.partial(_mm_res_kernel, n_in=n_in),
        grid=(n // tm,),
        in_specs=in_specs,
        out_specs=pl.BlockSpec((tm, d), lambda i: (i, 0)),
        out_shape=jax.ShapeDtypeStruct((n, d), F32),
        compiler_params=_cp(("parallel",)),
        name="matmul_residual",
    )(res, *a_list, *w_list)


def _ffn_kernel(x_ref, g_ref, wu_ref, wd_ref, gf_ref, o_ref, hn_ref, acc_ref, *, final_norm):
    j = pl.program_id(1)

    @pl.when(j == 0)
    def _():
        x = x_ref[...]
        hn_ref[...] = _rms(x, g_ref[...]).astype(BF16)
        acc_ref[...] = x

    u = jnp.dot(hn_ref[...], wu_ref[...], preferred_element_type=F32)
    u = jnp.square(jnp.maximum(u, 0.0)).astype(BF16)
    acc_ref[...] += jnp.dot(u, wd_ref[...], preferred_element_type=F32)

    @pl.when(j == pl.num_programs(1) - 1)
    def _():
        y = acc_ref[...]
        if final_norm:
            y = _rms(y, gf_ref[...])
        o_ref[...] = y


def ffn(x, gain, wu_bf16, wd_bf16, gain_final, final_norm, tm, tf):
    n, d = x.shape
    f = wu_bf16.shape[1]
    assert n % tm == 0 and f % tf == 0
    return pl.pallas_call(
        functools.partial(_ffn_kernel, final_norm=final_norm),
        grid=(n // tm, f // tf),
        in_specs=[pl.BlockSpec((tm, d), lambda i, j: (i, 0)),
                  pl.BlockSpec((1, d), lambda i, j: (0, 0)),
                  pl.BlockSpec((d, tf), lambda i, j: (0, j)),
                  pl.BlockSpec((tf, d), lambda i, j: (j, 0)),
                  pl.BlockSpec((1, d), lambda i, j: (0, 0))],
        out_specs=pl.BlockSpec((tm, d), lambda i, j: (i, 0)),
        out_shape=jax.ShapeDtypeStruct((n, d), F32),
        scratch_shapes=[pltpu.VMEM((tm, d), BF16), pltpu.VMEM((tm, d), F32)],
        compiler_params=_cp(("parallel", "arbitrary")),
        name="ffn",
    )(x, gain.reshape(1, d), wu_bf16, wd_bf16, gain_final.reshape(1, d))


def _mem_attn_kernel(q_ref, kv_ref, o_ref, *, heads, hd):
    scale = hd ** -0.5
    kt = jnp.swapaxes(kv_ref[0, :, 0], 0, 1)
    vt = jnp.swapaxes(kv_ref[0, :, 1], 0, 1)
    for h in range(heads):
        q = q_ref[0, :, h * hd:(h + 1) * hd]
        k = kt[h]
        v = vt[h]
        s = _dot_nt(q, k) * scale
        p = jnp.exp(s - jnp.max(s, axis=-1, keepdims=True))
        p = p / jnp.sum(p, axis=-1, keepdims=True)
        o_ref[0, :, h * hd:(h + 1) * hd] = _dot(p, v)


def mem_attention(q, kv, tq):
    b, t, d = q.shape
    m = kv.shape[1]
    hd = d // MEM_HEADS
    assert t % tq == 0 and kv.shape[2:] == (2, MEM_HEADS, hd)
    return pl.pallas_call(
        functools.partial(_mem_attn_kernel, heads=MEM_HEADS, hd=hd),
        grid=(b, t // tq),
        in_specs=[pl.BlockSpec((1, tq, d), lambda i, j: (i, j, 0)),
                  pl.BlockSpec((1, m, 2, MEM_HEADS, hd), lambda i, j: (i, 0, 0, 0, 0))],
        out_specs=pl.BlockSpec((1, tq, d), lambda i, j: (i, j, 0)),
        out_shape=jax.ShapeDtypeStruct((b, t, d), F32),
        compiler_params=_cp(("parallel", "parallel")),
        name="mem_attention",
    )(q, kv)


def _compress_kernel(x_ref, w1_ref, pe_ref, b1_ref, w2_ref, o_ref, acc_ref, *, row_cols, tt):
    j = pl.program_id(1)

    @pl.when(j == 0)
    def _():
        acc_ref[...] = jnp.zeros_like(acc_ref)

    hid2 = 2 * CMP_HIDDEN
    for kind in range(2):
        part = None
        for t in range(tt):
            c0 = t * row_cols + kind * LANES
            xs = x_ref[:, c0:c0 + LANES] + pe_ref[kind, t:t + 1, :]
            d = jnp.dot(xs.astype(BF16), w1_ref[kind, t], preferred_element_type=F32)
            part = d if part is None else part + d
        acc_ref[:, kind * hid2:(kind + 1) * hid2] += part

    @pl.when(j == pl.num_programs(1) - 1)
    def _():
        h = jnp.maximum(acc_ref[...] + b1_ref[...], 0.0)
        for kind in range(2):
            o_ref[:, kind * LANES:(kind + 1) * LANES] = jnp.dot(
                h[:, kind * hid2:(kind + 1) * hid2].astype(BF16), w2_ref[kind], preferred_element_type=F32)


def compress_blocks(x2, row_cols, cw, bt, tt=8):
    nb = x2.shape[0]
    assert nb % bt == 0 and CMP_BLOCK % tt == 0
    w1bd, pe2, b1bd, w2bd = cw
    return pl.pallas_call(
        functools.partial(_compress_kernel, row_cols=row_cols, tt=tt),
        grid=(nb // bt, CMP_BLOCK // tt),
        in_specs=[pl.BlockSpec((bt, tt * row_cols), lambda i, j: (i, j)),
                  pl.BlockSpec((2, tt, LANES, 2 * CMP_HIDDEN), lambda i, j: (0, j, 0, 0)),
                  pl.BlockSpec((2, tt, LANES), lambda i, j: (0, j, 0)),
                  pl.BlockSpec((1, 4 * CMP_HIDDEN), lambda i, j: (0, 0)),
                  pl.BlockSpec((2, 2 * CMP_HIDDEN, LANES), lambda i, j: (0, 0, 0))],
        out_specs=pl.BlockSpec((bt, 2 * LANES), lambda i, j: (i, 0)),
        out_shape=jax.ShapeDtypeStruct((nb, 2 * LANES), F32),
        scratch_shapes=[pltpu.VMEM((bt, 4 * CMP_HIDDEN), F32)],
        compiler_params=_cp(("parallel", "arbitrary")),
        name="compress_blocks",
    )(x2, w1bd, pe2, b1bd, w2bd)


def _compress_weights(cmp_pe, cmp_w1, cmp_b1, cmp_w2):
    w1r = cmp_w1.reshape(2, CMP_BLOCK, HEAD_DIM, CMP_HIDDEN)
    z = jnp.zeros_like(w1r)
    w1bd = jnp.concatenate([jnp.concatenate([w1r, z], -1), jnp.concatenate([z, w1r], -1)], axis=2).astype(BF16)
    pe2 = jnp.concatenate([cmp_pe, cmp_pe], -1)
    b1bd = jnp.concatenate([cmp_b1[0], cmp_b1[0], cmp_b1[1], cmp_b1[1]]).reshape(1, 4 * CMP_HIDDEN)
    z2 = jnp.zeros_like(cmp_w2)
    w2bd = jnp.concatenate([jnp.concatenate([cmp_w2, z2], -1), jnp.concatenate([z2, cmp_w2], -1)], axis=1).astype(BF16)
    return w1bd, pe2, b1bd, w2bd


def _topk_mask(score, k, ids=None):
    if ids is None:
        ids = lax.broadcasted_iota(jnp.int32, score.shape, score.ndim - 1).astype(F32)
    sel = jnp.zeros(score.shape, F32)
    for _ in range(k):
        m = jnp.max(score, axis=-1, keepdims=True)
        idx = jnp.min(jnp.where(score == m, ids, 1e9), axis=-1, keepdims=True)
        hit = ids == idx
        sel = jnp.where(hit, 1.0, sel)
        score = jnp.where(hit, -2.0, score)
    return sel


def _masked_softmax(s, mask):
    s = jnp.where(mask, s, NEG_INF)
    p = jnp.where(mask, jnp.exp(s - jnp.max(s, axis=-1, keepdims=True)), 0.0)
    return p / jnp.maximum(jnp.sum(p, axis=-1, keepdims=True), 1e-30)


def _cmp_topk_kernel(q_ref, kc_ref, vc_ref, ocmp_ref, sel_ref, *, tq, nblk, topn):
    qi = pl.program_id(2)
    qpos = qi * tq + lax.broadcasted_iota(jnp.int32, (tq, nblk), 0)
    blk = lax.broadcasted_iota(jnp.int32, (tq, nblk), 1)
    vis = (blk + 1) * CMP_BLOCK - 1 <= qpos
    kc = kc_ref[0, 0]
    vc = vc_ref[0, 0]
    imp = jnp.zeros((tq, nblk), F32)
    for r in range(NSA_GROUP):
        s = _dot_nt(q_ref[0, 0, r], kc) * ATTN_SCALE
        p = _masked_softmax(s, vis)
        ocmp_ref[0, 0, r] = _dot(p, vc)
        imp = imp + p
    cur = lax.shift_right_logical(qpos, 6)
    valid = blk <= cur
    forced = valid & ((blk == 0) | (cur - blk < N_LOCAL_BLOCKS))
    score = jnp.where(valid, imp + jnp.where(forced, FORCE_SCORE, 0.0), -1.0)
    sel_ref[0, 0] = _topk_mask(score, topn).astype(BF16)


def nsa_cmp_topk(q4, kc, vc, tq):
    b, g, r, sq, dh = q4.shape
    nblk = kc.shape[2]
    assert sq % tq == 0 and SEL_BLOCK == 64
    return pl.pallas_call(
        functools.partial(_cmp_topk_kernel, tq=tq, nblk=nblk, topn=min(TOP_N, nblk)),
        grid=(b, g, sq // tq),
        in_specs=[pl.BlockSpec((1, 1, r, tq, dh), lambda i, j, k: (i, j, 0, k, 0)),
                  pl.BlockSpec((1, 1, nblk, dh), lambda i, j, k: (i, j, 0, 0)),
                  pl.BlockSpec((1, 1, nblk, dh), lambda i, j, k: (i, j, 0, 0))],
        out_specs=[pl.BlockSpec((1, 1, r, tq, dh), lambda i, j, k: (i, j, 0, k, 0)),
                   pl.BlockSpec((1, 1, tq, nblk), lambda i, j, k: (i, j, k, 0))],
        out_shape=[jax.ShapeDtypeStruct((b, g, r, sq, dh), F32),
                   jax.ShapeDtypeStruct((b, g, sq, nblk), BF16)],
        compiler_params=_cp(("parallel", "parallel", "parallel")),
        name="nsa_cmp_topk",
    )(q4, kc, vc)


def _nsa_attn_kernel(q_ref, ks_ref, vst_ref, kw_ref, vwt_ref, selt_ref, ocmp_ref, gate_ref, o_ref,
                     m_sc, l_sc, acc_sc, *, tq, tk, nsel):
    qi = pl.program_id(2)
    r = NSA_GROUP
    dh = HEAD_DIM
    q = (q_ref[0, 0] * ATTN_SCALE).astype(BF16).reshape(r * tq, dh)
    qpos = qi * tq + lax.broadcasted_iota(jnp.int32, (1, tq), 1)

    def reset():
        m_sc[...] = jnp.full(m_sc.shape, NEG_INF, F32)
        l_sc[...] = jnp.zeros(l_sc.shape, F32)
        acc_sc[...] = jnp.zeros(acc_sc.shape, F32)

    def step(k, vt, mask):
        bias = jnp.where(mask, 0.0, NEG_INF)
        s = _dot_nt(k, q) + jnp.concatenate([bias] * r, axis=1)
        m_prev = m_sc[...]
        m_new = jnp.maximum(m_prev, jnp.max(s, axis=0, keepdims=True))
        alpha = jnp.exp(m_prev - m_new)
        p = jnp.exp(s - m_new)
        l_sc[...] = alpha * l_sc[...] + jnp.sum(p, axis=0, keepdims=True)
        acc_sc[...] = alpha * acc_sc[...] + jnp.dot(vt, p.astype(BF16), preferred_element_type=F32)
        m_sc[...] = m_new

    def result():
        return acc_sc[...] / jnp.maximum(l_sc[...], 1e-30)

    reset()
    selt = selt_ref[0, 0]
    n_kt = ((qi + 1) * tq + tk - 1) // tk

    def slc_body(kt, carry):
        k0 = pl.multiple_of(kt * tk, tk)
        kpos = k0 + lax.broadcasted_iota(jnp.int32, (tk, 1), 0)
        erow = k0 + lax.broadcasted_iota(jnp.int32, (tk, nsel), 0)
        eblk = lax.broadcasted_iota(jnp.int32, (tk, nsel), 1)
        expand = jnp.where(lax.shift_right_logical(erow, 6) == eblk, 1.0, 0.0).astype(BF16)
        chosen = jnp.dot(expand, selt, preferred_element_type=F32) > 0.5
        step(ks_ref[0, 0, pl.ds(k0, tk), :], vst_ref[0, 0, :, pl.ds(k0, tk)], chosen & (kpos <= qpos))
        return carry

    lax.fori_loop(0, n_kt, slc_body, 0)
    o_slc = result()

    reset()
    wk = WINDOW + tq
    w0 = pl.multiple_of(jnp.maximum(qi - WINDOW // tq, 0) * tq, tq)
    dpos = qpos - (w0 + lax.broadcasted_iota(jnp.int32, (wk, 1), 0))
    step(kw_ref[0, 0, pl.ds(w0, wk), :], vwt_ref[0, 0, :, pl.ds(w0, wk)], (dpos >= 0) & (dpos < WINDOW))
    o_win = result()

    gates = jax.nn.sigmoid(gate_ref[0, 0])
    for h in range(r):
        o_ref[0, 0, h] = (gates[:, 3 * h:3 * h + 1] * ocmp_ref[0, 0, h]
                          + gates[:, 3 * h + 1:3 * h + 2] * o_slc[:, h * tq:(h + 1) * tq].T
                          + gates[:, 3 * h + 2:3 * h + 3] * o_win[:, h * tq:(h + 1) * tq].T)


def nsa_attention(q4, ks, vst, kw, vwt, selt, ocmp, gate_g, tq, tk):
    b, g, r, sq, dh = q4.shape
    t = ks.shape[2]
    nsel = selt.shape[2]
    assert sq == t and sq % tq == 0 and t % tk == 0 and tk % tq == 0 and WINDOW % tq == 0 and tq % LANES == 0
    assert t >= WINDOW + tq
    k_spec = pl.BlockSpec((1, 1, t, dh), lambda i, j, k: (i, j, 0, 0))
    vt_spec = pl.BlockSpec((1, 1, dh, t), lambda i, j, k: (i, j, 0, 0))
    q_spec = pl.BlockSpec((1, 1, r, tq, dh), lambda i, j, k: (i, j, 0, k, 0))
    return pl.pallas_call(
        functools.partial(_nsa_attn_kernel, tq=tq, tk=tk, nsel=nsel),
        grid=(b, g, sq // tq),
        in_specs=[q_spec, k_spec, vt_spec, k_spec, vt_spec,
                  pl.BlockSpec((1, 1, nsel, tq), lambda i, j, k: (i, j, 0, k)),
                  q_spec,
                  pl.BlockSpec((1, 1, tq, LANES), lambda i, j, k: (i, j, k, 0))],
        out_specs=q_spec,
        out_shape=jax.ShapeDtypeStruct((b, g, r, sq, dh), F32),
        scratch_shapes=[pltpu.VMEM((1, r * tq), F32), pltpu.VMEM((1, r * tq), F32),
                        pltpu.VMEM((dh, r * tq), F32)],
        compiler_params=_cp(("parallel", "parallel", "parallel")),
        name="nsa_attention",
    )(q4, ks, vst, kw, vwt, selt, ocmp, gate_g)


def _bdot(a, b, passes=1):
    dims = (((2,), (1,)), ((0,), (0,)))
    if passes == 1:
        return lax.dot_general(a.astype(BF16), b.astype(BF16), dims, preferred_element_type=F32)
    ah, al = _split2(a)
    bh, bl = _split2(b)
    return (lax.dot_general(ah, bh, dims, preferred_element_type=F32)
            + lax.dot_general(ah, bl, dims, preferred_element_type=F32)
            + lax.dot_general(al, bh, dims, preferred_element_type=F32))


def _bdot_nt(a, b):
    return lax.dot_general(a.astype(BF16), b.astype(BF16), (((2,), (2,)), ((0,), (0,))),
                           preferred_element_type=F32)


def _deltanet_kernel(qkv_ref, z_ref, sm_ref, cw_ref, alog_ref, dtb_ref, gn_ref, o_ref, s_out_ref,
                     xbuf, s_sc):
    c = pl.program_id(0)
    ch = DN_CHUNK
    n_pairs = DN_HEADS // 2
    two = 2 * ch
    n_batch = qkv_ref.shape[0]

    @pl.when(c == 0)
    def _():
        xbuf[:, 0:8, :] = jnp.zeros((n_batch, 8, DN_CONV_DIM), F32)
        s_sc[...] = jnp.zeros_like(s_sc)

    ti = lax.broadcasted_iota(jnp.int32, (ch, ch), 0)
    tj = lax.broadcasted_iota(jnp.int32, (ch, ch), 1)
    tri = jnp.where(ti >= tj, 1.0, 0.0)
    lane = lax.broadcasted_iota(jnp.int32, (ch, LANES), 1)
    lo = lane < HEAD_DIM
    row2 = lax.broadcasted_iota(jnp.int32, (two, two), 0)
    col2 = lax.broadcasted_iota(jnp.int32, (two, two), 1)
    same = (row2 >= ch) == (col2 >= ch)
    incl = (same & (row2 >= col2))[None]
    strict = (same & (row2 > col2))[None]
    top = lax.broadcasted_iota(jnp.int32, (two, 1), 0) < ch

    def seg_sum(x):
        s_lo = jnp.sum(jnp.where(lo, x, 0.0), axis=-1, keepdims=True)
        s_hi = jnp.sum(jnp.where(lo, 0.0, x), axis=-1, keepdims=True)
        return jnp.where(lo, s_lo, s_hi)

    def stack2(x):
        return jnp.concatenate([jnp.where(lo, x, 0.0), jnp.where(lo, 0.0, x)], axis=0)

    def col2x(a, b):
        return jnp.concatenate([jnp.broadcast_to(a, (ch, LANES)), jnp.broadcast_to(b, (ch, LANES))], axis=0)

    q_l, k_l, v_l, beta_l, gc_l, gl_l = [], [], [], [], [], []
    for bi in range(n_batch):
        xbuf[bi, 8:8 + ch, :] = qkv_ref[bi]
        conv = None
        for w in range(CONV_WIDTH):
            term = xbuf[bi, 5 + w:5 + w + ch, :] * cw_ref[w:w + 1, :]
            conv = term if conv is None else conv + term
        xbuf[bi, 0:8, :] = xbuf[bi, ch:ch + 8, :]
        act = _silu(conv)
        sm = sm_ref[bi]
        beta_all = jax.nn.sigmoid(sm)
        g_all = -jnp.exp(alog_ref[...]) * _softplus(sm + dtb_ref[...])
        gcum_all = _dot_exact_lhs01(tri, g_all)
        for p in range(n_pairs):
            c0 = p * LANES
            qp = act[:, c0:c0 + LANES]
            kp = act[:, DN_WIDTH + c0:DN_WIDTH + c0 + LANES]
            vp = act[:, 2 * DN_WIDTH + c0:2 * DN_WIDTH + c0 + LANES]
            qp = qp * lax.rsqrt(seg_sum(qp * qp) + 1e-6) * (HEAD_DIM ** -0.5)
            kp = kp * lax.rsqrt(seg_sum(kp * kp) + 1e-6)
            h0, h1 = DN_HEADS + 2 * p, DN_HEADS + 2 * p + 1
            q_l.append(stack2(qp))
            k_l.append(stack2(kp))
            v_l.append(stack2(vp))
            beta_l.append(col2x(beta_all[:, 2 * p:2 * p + 1], beta_all[:, 2 * p + 1:2 * p + 2]))
            gc_l.append(col2x(gcum_all[:, h0:h0 + 1], gcum_all[:, h1:h1 + 1]))
            gl_l.append(jnp.broadcast_to(jnp.where(top, gcum_all[ch - 1:ch, h0:h0 + 1], gcum_all[ch - 1:ch, h1:h1 + 1]),
                                         (two, LANES)))
    q2, k2, v2 = jnp.stack(q_l), jnp.stack(k_l), jnp.stack(v_l)
    beta2, gc2, gl2 = jnp.stack(beta_l), jnp.stack(gc_l), jnp.stack(gl_l)
    decay = jnp.exp(jnp.where(incl, gc2 - jnp.swapaxes(gc2, 1, 2), NEG_INF))
    kb2 = k2 * beta2
    a_mat = jnp.where(strict, _bdot_nt(kb2, k2) * decay, 0.0)
    aqk = jnp.where(incl, _bdot_nt(q2, k2) * decay, 0.0)
    s_old = s_sc[...]
    egc = jnp.exp(gc2)
    x = beta2 * (v2 - egc * _bdot(k2, s_old))
    pw = -a_mat
    n_lvl = ch.bit_length() - 1
    for lvl in range(n_lvl):
        x = x + _bdot(pw, x, DN_SOLVE_PASSES)
        if lvl + 1 < n_lvl:
            pw = _bdot(pw, pw, DN_SOLVE_PASSES)
    o2 = _bdot(q2 * egc, s_old) + _bdot(aqk, x)
    kdec = k2 * jnp.exp(gl2 - gc2)
    s_sc[...] = s_old * jnp.exp(gl2) + _bdot(jnp.swapaxes(kdec, 1, 2), x)
    for bi in range(n_batch):
        for p in range(n_pairs):
            c0 = p * LANES
            o_n = o2[bi * n_pairs + p]
            o_pair = o_n[0:ch] + o_n[ch:two]
            inv = lax.rsqrt(seg_sum(o_pair * o_pair) * (1.0 / HEAD_DIM) + RMS_EPS)
            o_ref[bi, :, c0:c0 + LANES] = o_pair * inv * gn_ref[...] * _silu(z_ref[bi, :, c0:c0 + LANES])

    @pl.when(c == pl.num_programs(0) - 1)
    def _():
        s_out_ref[...] = s_sc[...]


def deltanet_prompt(proj3, conv_w, a_log, dt_bias, norm_gain):
    b, t, _ = proj3.shape
    ch = DN_CHUNK
    assert t % ch == 0
    pad = jnp.zeros((LANES - 2 * DN_HEADS,), F32)
    alog_row = jnp.concatenate([jnp.zeros((DN_HEADS,), F32), a_log, pad]).reshape(1, LANES)
    dtb_row = jnp.concatenate([jnp.zeros((DN_HEADS,), F32), dt_bias, pad]).reshape(1, LANES)
    gn_row = jnp.concatenate([norm_gain, norm_gain]).reshape(1, LANES)
    n_pairs = DN_HEADS // 2
    o, s_fin = pl.pallas_call(
        _deltanet_kernel,
        grid=(t // ch,),
        in_specs=[pl.BlockSpec((b, ch, DN_CONV_DIM), lambda j: (0, j, P_QKV // DN_CONV_DIM)),
                  pl.BlockSpec((b, ch, DN_WIDTH), lambda j: (0, j, P_Z // DN_WIDTH)),
                  pl.BlockSpec((b, ch, LANES), lambda j: (0, j, P_SMALL // LANES)),
                  pl.BlockSpec((CONV_WIDTH, DN_CONV_DIM), lambda j: (0, 0)),
                  pl.BlockSpec((1, LANES), lambda j: (0, 0)),
                  pl.BlockSpec((1, LANES), lambda j: (0, 0)),
                  pl.BlockSpec((1, LANES), lambda j: (0, 0))],
        out_specs=[pl.BlockSpec((b, ch, DN_WIDTH), lambda j: (0, j, 0)),
                   pl.BlockSpec((b * n_pairs, 2 * ch, LANES), lambda j: (0, 0, 0))],
        out_shape=[jax.ShapeDtypeStruct((b, t, DN_WIDTH), F32),
                   jax.ShapeDtypeStruct((b * n_pairs, 2 * ch, LANES), F32)],
        scratch_shapes=[pltpu.VMEM((b, ch + 8, DN_CONV_DIM), F32), pltpu.VMEM((b * n_pairs, 2 * ch, LANES), F32)],
        compiler_params=_cp(("arbitrary",)),
        name="deltanet_prompt",
    )(proj3, proj3, proj3, conv_w, alog_row, dtb_row, gn_row)
    return o, s_fin.reshape(b, n_pairs, 2 * ch, LANES)


def _pairs_to_heads(s_pairs):
    d = HEAD_DIM
    return jnp.stack([s_pairs[:, :, :d, :d], s_pairs[:, :, d:, d:]], axis=2).reshape(
        s_pairs.shape[0], DN_HEADS, d, d)


def _dn_step_prep_kernel(qkv_ref, cs_ref, sm_ref, cw_ref, alog_ref, dtb_ref, ones_ref,
                         q_ref, k_ref, v_ref, sc_ref):
    conv = qkv_ref[...] * cw_ref[CONV_WIDTH - 1:CONV_WIDTH, :]
    for w in range(CONV_WIDTH - 1):
        conv = conv + cs_ref[w] * cw_ref[w:w + 1, :]
    act = _silu(conv)
    q = act[:, 0:DN_WIDTH]
    k = act[:, DN_WIDTH:2 * DN_WIDTH]

    def seg_sum(x):
        return _dot_exact_rhs01(x, ones_ref[...])

    q_ref[...] = q * lax.rsqrt(seg_sum(q * q) + 1e-6) * (HEAD_DIM ** -0.5)
    k_ref[...] = k * lax.rsqrt(seg_sum(k * k) + 1e-6)
    v_ref[...] = act[:, 2 * DN_WIDTH:]
    sm = sm_ref[...]
    g = -jnp.exp(alog_ref[...]) * _softplus(sm + dtb_ref[...])
    lane = lax.broadcasted_iota(jnp.int32, sm.shape, 1)
    sc_ref[...] = jnp.where(lane < DN_HEADS, jax.nn.sigmoid(sm), jnp.exp(g))


def _dn_step_kernel(kt_ref, qt_ref, v_ref, sc_ref, z_ref, gn_ref, s_ref, o_ref, s_out_ref, *, nseq):
    hds, d = DN_HEADS, HEAD_DIM
    for j in range(nseq):
        s_old = s_ref[j]
        kcol = kt_ref[0, :, j:j + 1]
        qcol = qt_ref[0, :, j:j + 1]
        beta = sc_ref[j, :, 0:1]
        eg = sc_ref[j, :, 1:2]
        v = v_ref[j]
        ks = jnp.sum((kcol * s_old).reshape(hds, d, d), axis=1)
        qs = jnp.sum((qcol * s_old).reshape(hds, d, d), axis=1)
        qk = jnp.sum((qcol * kcol).reshape(hds, d, 1), axis=1)
        v_new = beta * (v - eg * ks)
        o = eg * qs + qk * v_new
        inv = lax.rsqrt(jnp.mean(o * o, axis=-1, keepdims=True) + RMS_EPS)
        o_ref[j] = o * inv * gn_ref[...] * _silu(z_ref[j])
        s3 = s_old.reshape(hds, d, d) * eg[:, :, None] + kcol.reshape(hds, d, 1) * v_new[:, None, :]
        s_out_ref[j] = s3.reshape(hds * d, d)


def deltanet_sample(proj_s, conv_state, rec_state, conv_w, a_log, dt_bias, norm_gain, nseq=8):
    n = proj_s.shape[0]
    assert n % nseq == 0
    hds, d = DN_HEADS, HEAD_DIM
    pad = jnp.zeros((LANES - 2 * hds,), F32)
    alog_row = jnp.concatenate([jnp.zeros((hds,), F32), a_log, pad]).reshape(1, LANES)
    dtb_row = jnp.concatenate([jnp.zeros((hds,), F32), dt_bias, pad]).reshape(1, LANES)
    head_of = jnp.arange(DN_WIDTH) // d
    ones_bd = (head_of[:, None] == head_of[None, :]).astype(BF16)
    cs = jnp.transpose(conv_state, (1, 0, 2))
    full = lambda shape: pl.BlockSpec(shape, lambda i: (0,) * len(shape))
    q, k, v, sc = pl.pallas_call(
        _dn_step_prep_kernel,
        grid=(1,),
        in_specs=[pl.BlockSpec((n, DN_CONV_DIM), lambda i: (0, P_QKV // DN_CONV_DIM)),
                  full((CONV_WIDTH - 1, n, DN_CONV_DIM)),
                  pl.BlockSpec((n, LANES), lambda i: (0, P_SMALL // LANES)),
                  full((CONV_WIDTH, DN_CONV_DIM)), full((1, LANES)), full((1, LANES)),
                  full((DN_WIDTH, DN_WIDTH))],
        out_specs=[full((n, DN_WIDTH)), full((n, DN_WIDTH)), full((n, DN_WIDTH)), full((n, LANES))],
        out_shape=[jax.ShapeDtypeStruct((n, DN_WIDTH), F32)] * 3 + [jax.ShapeDtypeStruct((n, LANES), F32)],
        compiler_params=_cp(("arbitrary",)),
        name="dn_step_prep",
    )(proj_s, cs, proj_s, conv_w, alog_row, dtb_row, ones_bd)
    ng = n // nseq
    kt = jnp.transpose(k.reshape(ng, nseq, DN_WIDTH), (0, 2, 1))
    qt = jnp.transpose(q.reshape(ng, nseq, DN_WIDTH), (0, 2, 1))
    sc3 = jnp.stack([sc[:, :hds], sc[:, hds:2 * hds]], axis=-1)
    z = proj_s[:, P_Z:P_Z + DN_WIDTH].reshape(n, hds, d)
    o, s_new = pl.pallas_call(
        functools.partial(_dn_step_kernel, nseq=nseq),
        grid=(ng,),
        in_specs=[pl.BlockSpec((1, DN_WIDTH, nseq), lambda i: (i, 0, 0)),
                  pl.BlockSpec((1, DN_WIDTH, nseq), lambda i: (i, 0, 0)),
                  pl.BlockSpec((nseq, hds, d), lambda i: (i, 0, 0)),
                  pl.BlockSpec((nseq, hds, 2), lambda i: (i, 0, 0)),
                  pl.BlockSpec((nseq, hds, d), lambda i: (i, 0, 0)),
                  pl.BlockSpec((1, d), lambda i: (0, 0)),
                  pl.BlockSpec((nseq, hds * d, d), lambda i: (i, 0, 0))],
        out_specs=[pl.BlockSpec((nseq, hds, d), lambda i: (i, 0, 0)),
                   pl.BlockSpec((nseq, hds * d, d), lambda i: (i, 0, 0))],
        out_shape=[jax.ShapeDtypeStruct((n, hds, d), F32), jax.ShapeDtypeStruct((n, hds * d, d), F32)],
        compiler_params=_cp(("parallel",)),
        name="dn_step",
    )(kt, qt, v.reshape(n, hds, d), sc3, z, norm_gain.reshape(1, d), rec_state.reshape(n, hds * d, d))
    return o.reshape(n, DN_WIDTH), s_new.reshape(n, hds, d, d)


def _nsa_decode_kernel(pt_ref, q_ref, new_ref, gate_ref, win_ref, exp_ref, *refs, n_pages):
    kc_refs = refs[:n_pages]
    pg_refs = refs[n_pages:2 * n_pages]
    o_ref = refs[2 * n_pages]
    kc_sc = refs[2 * n_pages + 1]
    del pt_ref
    dh, r, g2, nh = HEAD_DIM, NSA_GROUP, NSA_KV_HEADS, NSA_HEADS
    past = n_pages * PAGE_SIZE
    nb = past // CMP_BLOCK
    nsel = nb + 1
    wlen = win_ref.shape[4]
    for j in range(n_pages):
        kc_sc[j:j + 1, :] = kc_refs[j][0]
    kcv = kc_sc[...]
    new = new_ref[0]
    q8 = q_ref[0] * ATTN_SCALE
    head = lax.broadcasted_iota(jnp.int32, (nh, 1), 0)
    lane = lax.broadcasted_iota(jnp.int32, (nh, LANES), 1)
    blk = jnp.where(lane < n_pages, 2 * lane, jnp.where(lane < nb, 2 * (lane - n_pages) + 1, lane))
    blk_f = blk.astype(F32)
    wp = lax.broadcasted_iota(jnp.int32, (1, wlen + LANES), 1)
    dpos = wlen - wp
    wmask = (dpos >= 0) & (dpos < WINDOW) & (past - wlen + wp >= 0)
    bid_row = blk_f[0:1]
    bid_col = jnp.broadcast_to(bid_row, (LANES, LANES)).T
    outs = []
    for g in range(g2):
        in_g = (head >= g * r) & (head < (g + 1) * r)
        kc = jnp.concatenate([kcv[:, g * LANES:g * LANES + dh], kcv[:, g * LANES + dh:(g + 1) * LANES]], axis=0)
        vc = jnp.concatenate([kcv[:, (g2 + g) * LANES:(g2 + g) * LANES + dh],
                              kcv[:, (g2 + g) * LANES + dh:(g2 + g + 1) * LANES]], axis=0)
        s = _dot_nt(q8, kc)
        p = jnp.exp(s - jnp.max(s, axis=-1, keepdims=True))
        p = p / jnp.maximum(jnp.sum(p, axis=-1, keepdims=True), 1e-30)
        o_cmp = _dot(p, vc)
        imp = jnp.sum(jnp.where(in_g, p, 0.0), axis=0, keepdims=True)
        imp = jnp.concatenate([imp, jnp.zeros((1, LANES - nb), F32)], axis=1)
        valid = lane[0:1] < nsel
        forced = valid & ((blk[0:1] == 0) | (nb - blk[0:1] < N_LOCAL_BLOCKS))
        score = jnp.where(valid, imp + jnp.where(forced, FORCE_SCORE, 0.0), -1.0)
        sc_row = jnp.broadcast_to(score, (LANES, LANES))
        sc_col = sc_row.T
        beats = (sc_col > sc_row) | ((sc_col == sc_row) & (bid_col < bid_row))
        rank = jnp.sum(jnp.where(beats, 1.0, 0.0), axis=0, keepdims=True)
        sel = jnp.where(valid & (rank < min(TOP_N, nsel)), 1.0, 0.0)
        kt_all = jnp.concatenate([pg_refs[j][0, 0, g] for j in range(n_pages)], axis=1)
        vt_all = jnp.concatenate([pg_refs[j][0, 1, g] for j in range(n_pages)], axis=1)
        s_new = jnp.sum(q8 * new[:, (4 + g) * dh:(5 + g) * dh], axis=-1, keepdims=True)
        s_all = jnp.concatenate([_dot(q8, kt_all), jnp.broadcast_to(s_new, (nh, LANES))], axis=1)
        chosen = jnp.dot(jnp.broadcast_to(sel, (nh, LANES)).astype(BF16), exp_ref[...], preferred_element_type=F32) > 0.5
        pm = _masked_softmax(s_all, chosen)
        o_slc = pm[:, past:past + 1] * new[:, (6 + g) * dh:(7 + g) * dh] + _dot_nt(pm[:, 0:past], vt_all)
        sw = _dot(q8, win_ref[0, 0, g])
        sw_new = jnp.sum(q8 * new[:, (8 + g) * dh:(9 + g) * dh], axis=-1, keepdims=True)
        pw = _masked_softmax(jnp.concatenate([sw, jnp.broadcast_to(sw_new, (nh, LANES))], axis=1), wmask)
        o_win = pw[:, wlen:wlen + 1] * new[:, (10 + g) * dh:(11 + g) * dh] + _dot_nt(pw[:, 0:wlen], win_ref[0, 1, g])
        outs.append((in_g, o_cmp, o_slc, o_win))
    gates = jnp.broadcast_to(jax.nn.sigmoid(gate_ref[0]), (nh, LANES))

    def gate(branch):
        return jnp.sum(jnp.where(lane == 3 * head + branch, gates, 0.0), axis=-1, keepdims=True)

    total = jnp.zeros((nh, dh), F32)
    for in_g, o_cmp, o_slc, o_win in outs:
        total = jnp.where(in_g, gate(0) * o_cmp + gate(1) * o_slc + gate(2) * o_win, total)
    o_ref[0] = total


def nsa_decode(page_table, q3, new_row, gate_row, win_t, kc_phys, cache_t):
    n, n_pages = page_table.shape
    dh = HEAD_DIM
    g2 = NSA_KV_HEADS
    wlen = win_t.shape[4]
    assert 2 * n_pages + 1 <= LANES and PAGE_SIZE == 2 * CMP_BLOCK and PAGE_SIZE == LANES

    past = n_pages * PAGE_SIZE
    nb = past // CMP_BLOCK
    erow = lax.broadcasted_iota(jnp.int32, (LANES, past + LANES), 0)
    ecol = lax.broadcasted_iota(jnp.int32, (LANES, past + LANES), 1)
    page, second = ecol // PAGE_SIZE, (ecol % PAGE_SIZE) >= CMP_BLOCK
    expand = (((erow < n_pages) & (page == erow) & ~second & (ecol < past))
              | ((erow >= n_pages) & (erow < nb) & (page == erow - n_pages) & second & (ecol < past))
              | ((erow == nb) & (ecol == past))).astype(BF16)

    def kc_map(j):
        return lambda i, pt: (pt[i, j], 0, 0)

    def slc_map(j):
        return lambda i, pt: (pt[i, j], 1, 0, 0, 0)

    in_specs = [pl.BlockSpec((1, NSA_HEADS, dh), lambda i, pt: (i, 0, 0)),
                pl.BlockSpec((1, 1, KV_COLS), lambda i, pt: (i, 0, 0)),
                pl.BlockSpec((1, 1, LANES), lambda i, pt: (i, 0, 0)),
                pl.BlockSpec((1, 2, g2, dh, wlen), lambda i, pt: (i, 0, 0, 0, 0)),
                pl.BlockSpec((LANES, past + LANES), lambda i, pt: (0, 0))]
    in_specs += [pl.BlockSpec((1, 1, 4 * LANES), kc_map(j)) for j in range(n_pages)]
    in_specs += [pl.BlockSpec((1, 2, g2, dh, PAGE_SIZE), slc_map(j)) for j in range(n_pages)]
    grid_spec = pltpu.PrefetchScalarGridSpec(
        num_scalar_prefetch=1, grid=(n,), in_specs=in_specs,
        out_specs=pl.BlockSpec((1, NSA_HEADS, dh), lambda i, pt: (i, 0, 0)),
        scratch_shapes=[pltpu.VMEM((n_pages, 4 * LANES), F32)])
    return pl.pallas_call(
        functools.partial(_nsa_decode_kernel, n_pages=n_pages),
        grid_spec=grid_spec,
        out_shape=jax.ShapeDtypeStruct((n, NSA_HEADS, dh), F32),
        compiler_params=_cp(("arbitrary",)),
        name="nsa_decode",
    )(page_table, q3, new_row, gate_row, win_t, expand, *([kc_phys] * n_pages), *([cache_t] * n_pages))


def _compress_pages_kernel(x_ref, wd_ref, ped_ref, b1_ref, w2_ref, o_ref, acc_ref, *, dd):
    j = pl.program_id(1)
    g2 = NSA_KV_HEADS

    @pl.when(j == 0)
    def _():
        acc_ref[...] = jnp.zeros_like(acc_ref)

    for kind in range(2):
        for g in range(g2):
            part = None
            xt = jnp.swapaxes(x_ref[:, kind, g], 0, 1)
            for dl in range(dd):
                xs = xt[dl] + ped_ref[kind, dl:dl + 1, :]
                d = jnp.dot(xs.astype(BF16), wd_ref[kind, dl], preferred_element_type=F32)
                part = d if part is None else part + d
            acc_ref[kind * g2 + g] += part

    @pl.when(j == pl.num_programs(1) - 1)
    def _():
        for kind in range(2):
            for g in range(g2):
                h = jnp.maximum(acc_ref[kind * g2 + g] + b1_ref[kind:kind + 1, :], 0.0)
                o_ref[:, 0, (kind * g2 + g) * LANES:(kind * g2 + g + 1) * LANES] = jnp.dot(
                    h.astype(BF16), w2_ref[kind], preferred_element_type=F32)


def compress_pages(cache_t, cwp, bp, dd=8):
    n_phys = cache_t.shape[0]
    g2, dh = NSA_KV_HEADS, HEAD_DIM
    assert n_phys % bp == 0 and dh % dd == 0
    wd, ped, b1h, w2h = cwp
    return pl.pallas_call(
        functools.partial(_compress_pages_kernel, dd=dd),
        grid=(n_phys // bp, dh // dd),
        in_specs=[pl.BlockSpec((bp, 2, g2, dd, PAGE_SIZE), lambda i, j: (i, 0, 0, j, 0)),
                  pl.BlockSpec((2, dd, PAGE_SIZE, 2 * CMP_HIDDEN), lambda i, j: (0, j, 0, 0)),
                  pl.BlockSpec((2, dd, PAGE_SIZE), lambda i, j: (0, j, 0)),
                  pl.BlockSpec((2, 2 * CMP_HIDDEN), lambda i, j: (0, 0)),
                  pl.BlockSpec((2, 2 * CMP_HIDDEN, LANES), lambda i, j: (0, 0, 0))],
        out_specs=pl.BlockSpec((bp, 1, 2 * g2 * LANES), lambda i, j: (i, 0, 0)),
        out_shape=jax.ShapeDtypeStruct((n_phys, 1, 2 * g2 * LANES), F32),
        scratch_shapes=[pltpu.VMEM((2 * g2, bp, 2 * CMP_HIDDEN), F32)],
        compiler_params=_cp(("parallel", "arbitrary")),
        name="compress_pages",
    )(cache_t, wd, ped, b1h, w2h)


def _compress_page_weights(cmp_pe, cmp_w1, cmp_b1, cmp_w2):
    w1t = jnp.transpose(cmp_w1.reshape(2, CMP_BLOCK, HEAD_DIM, CMP_HIDDEN), (0, 2, 1, 3))
    z = jnp.zeros_like(w1t)
    wd = jnp.concatenate([jnp.concatenate([w1t, z], -1), jnp.concatenate([z, w1t], -1)], axis=2).astype(BF16)
    pet = jnp.transpose(cmp_pe, (0, 2, 1))
    ped = jnp.concatenate([pet, pet], -1)
    b1h = jnp.concatenate([cmp_b1, cmp_b1], -1)
    z2 = jnp.zeros_like(cmp_w2)
    w2h = jnp.concatenate([jnp.concatenate([cmp_w2, z2], -1), jnp.concatenate([z2, cmp_w2], -1)], axis=1).astype(BF16)
    return wd, ped, b1h, w2h


def _permute_w_in(w):
    d = w.shape[0]
    off_b = DN_CONV_DIM + DN_WIDTH
    off_q = off_b + 2 * DN_HEADS
    off_g = off_q + NSA_WIDTH + KV_COLS
    n_g = 3 * NSA_HEADS
    pad = jnp.zeros((d, LANES - 2 * DN_HEADS - n_g), w.dtype)
    return jnp.concatenate([w[:, :off_b], w[:, off_q:off_g], w[:, off_b:off_q], w[:, off_g:off_g + n_g], pad], axis=1)


def _row_tile(n, cap):
    t = min(n, cap)
    while n % t:
        t //= 2
    return t


def _trunk_tail(x2, mixer_dn, mixer_nsa, mem_kv3, bshape, lw, final_norm):
    n, d = x2.shape
    b, t = bshape
    tm = _row_tile(n, 512)
    x2 = matmul_residual(x2, [mixer_dn, mixer_nsa], [lw["w_out_dn"], lw["w_out_nsa"]], tm)
    qm = rms_matmul(x2, lw["ln_mem"], lw["w_mem_q"], tm, 512)
    q3 = qm.reshape(b, t, d)
    sub = 8
    if t < sub:
        q3 = jnp.broadcast_to(q3[:, :1], (b, sub, d))
    att = mem_attention(q3, mem_kv3, _row_tile(q3.shape[1], 512))[:, :t]
    x2 = matmul_residual(x2, [att.reshape(n, d)], [lw["w_mem_o"]], tm)
    return ffn(x2, lw["ln_ffn"], lw["w_up"], lw["w_down"], lw["ln_final"], final_norm, _row_tile(n, 1024), 512)


def _gate_groups(small):
    per = 3 * NSA_GROUP
    g0 = 2 * DN_HEADS
    parts = []
    for g in range(NSA_KV_HEADS):
        gl = small[..., g0 + g * per:g0 + (g + 1) * per]
        parts.append(jnp.concatenate([gl, jnp.zeros(gl.shape[:-1] + (LANES - per,), gl.dtype)], axis=-1))
    return jnp.stack(parts, axis=-2)


def _prompt_layer(xp, mem_prompt, lw, cw, final_norm):
    b, s, d = xp.shape
    n = b * s
    x2 = xp.reshape(n, d)
    proj = rms_matmul(x2, lw["ln_mix"], lw["w_in"], _row_tile(n, 512), 1152)
    proj3 = proj.reshape(b, s, P_COLS)
    dn_out, s_pairs = deltanet_prompt(proj3, lw["dn_conv_w"], lw["dn_a_log"], lw["dn_dt_bias"], lw["dn_norm"])
    p_conv = proj3[:, s - (CONV_WIDTH - 1):, P_QKV:P_QKV + DN_CONV_DIM]
    p_rec = _pairs_to_heads(s_pairs)
    kv = proj3[:, :, P_NKV:P_NKV + KV_COLS]
    kv6 = kv.reshape(b, s, NSA_KV_KINDS, NSA_KV_HEADS, HEAD_DIM)
    nb = s // CMP_BLOCK
    kcv = compress_blocks(kv[:, :nb * CMP_BLOCK].reshape(b * nb, CMP_BLOCK * KV_COLS), KV_COLS, cw,
                          _row_tile(b * nb, 256))
    kcv = kcv.reshape(b, nb, 2, NSA_KV_HEADS, HEAD_DIM)
    kc = jnp.transpose(kcv[:, :, 0], (0, 2, 1, 3))
    vc = jnp.transpose(kcv[:, :, 1], (0, 2, 1, 3))
    q4 = jnp.transpose(proj3[:, :, P_NQ:P_NQ + NSA_WIDTH].reshape(b, s, NSA_KV_HEADS, NSA_GROUP, HEAD_DIM),
                       (0, 2, 3, 1, 4))
    ocmp, sel = nsa_cmp_topk(q4, kc, vc, _row_tile(s, 512))
    kvb = kv6.astype(BF16)
    kvt = jnp.transpose(kvb, (2, 0, 3, 1, 4))
    kvtt = jnp.transpose(kvb, (2, 0, 3, 4, 1))
    gate_g = jnp.transpose(_gate_groups(proj3[:, :, P_SMALL:P_SMALL + LANES]), (0, 2, 1, 3))
    o4 = nsa_attention(q4, kvt[2], kvtt[3], kvt[4], kvtt[5], jnp.swapaxes(sel, 2, 3), ocmp, gate_g, 128, 512)
    nsa_out = jnp.transpose(o4, (0, 3, 1, 2, 4)).reshape(n, NSA_WIDTH)
    m = mem_prompt.shape[1]
    mem_kv = rms_matmul(mem_prompt.reshape(b * m, d), lw["ln_memkv"], lw["w_mem_kv"], _row_tile(b * m, 512), 512)
    mem_kv5 = mem_kv.reshape(b, m, 2, MEM_HEADS, d // MEM_HEADS)
    y = _trunk_tail(x2, dn_out.reshape(n, DN_WIDTH), nsa_out, mem_kv5, (b, s), lw, final_norm)
    wk = min(WINDOW, s)
    return y.reshape(b, s, d), kv6[:, :, :4], kv6[:, s - wk:, 4:], mem_kv5, p_conv, p_rec


def _sample_layer(xs, cache_nsa, cache_win, cache_mem, conv_state, rec_state, page_table, lw, cwp, final_norm):
    db, ds, d = xs.shape
    assert ds == 1
    n = db
    x2 = xs.reshape(n, d)
    proj = rms_matmul(x2, lw["ln_mix"], lw["w_in"], _row_tile(n, 512), 1152)
    dn_out, s_rec = deltanet_sample(proj, conv_state, rec_state, lw["dn_conv_w"], lw["dn_a_log"], lw["dn_dt_bias"],
                                    lw["dn_norm"])
    s_conv = jnp.concatenate([conv_state[:, 1:], proj[:, None, P_QKV:P_QKV + DN_CONV_DIM]], axis=1)
    n_phys = cache_nsa.shape[0]
    cache_t = jnp.transpose(cache_nsa, (0, 2, 3, 4, 1))
    win_t = jnp.transpose(cache_win, (0, 2, 3, 4, 1))
    kc_phys = compress_pages(cache_t, cwp, _row_tile(n_phys, 256))
    kv_new = proj[:, P_NKV:P_NKV + KV_COLS]
    o8 = nsa_decode(page_table, proj[:, P_NQ:P_NQ + NSA_WIDTH].reshape(n, NSA_HEADS, HEAD_DIM),
                    kv_new.reshape(n, 1, KV_COLS),
                    _flat_gates(proj[:, P_SMALL:P_SMALL + LANES]).reshape(n, 1, LANES),
                    win_t, kc_phys, cache_t)
    kv6 = kv_new.reshape(n, 1, NSA_KV_KINDS, NSA_KV_HEADS, HEAD_DIM)
    s_win = jnp.concatenate([cache_win, kv6[:, :, 4:]], axis=1)[:, 1:]
    y = _trunk_tail(x2, dn_out, o8.reshape(n, NSA_WIDTH), cache_mem, (n, 1), lw, final_norm)
    return y.reshape(db, ds, d), kv6[:, :, :4], s_win, s_conv, s_rec


def _flat_gates(small):
    g0 = 2 * DN_HEADS
    n_g = 3 * NSA_HEADS
    gl = small[..., g0:g0 + n_g]
    return jnp.concatenate([gl, jnp.zeros(gl.shape[:-1] + (LANES - n_g,), gl.dtype)], axis=-1)


def kernel(x_prompt, x_sample, mem_prompt, cache_nsa_kv, cache_win_kv, cache_mem_kv, state_dn_conv, state_dn_rec, page_table, ln_mix, w_in, dn_conv_w, dn_a_log, dn_dt_bias, dn_norm, cmp_pe, cmp_w1, cmp_b1, cmp_w2, w_out, ln_mem, ln_memkv, w_mem_q, w_mem_kv, w_mem_o, ln_ffn, w_up, w_down, ln_final):
    depth = w_in.shape[0]
    xp, xs = x_prompt, x_sample
    outs_p = [[] for _ in range(5)]
    outs_s = [[] for _ in range(4)]
    for l in range(depth):
        lw = {
            "ln_mix": ln_mix[l], "w_in": _permute_w_in(w_in[l]).astype(BF16),
            "dn_conv_w": dn_conv_w[l], "dn_a_log": dn_a_log[l], "dn_dt_bias": dn_dt_bias[l], "dn_norm": dn_norm[l],
            "w_out_dn": w_out[l][:DN_WIDTH].astype(BF16), "w_out_nsa": w_out[l][DN_WIDTH:].astype(BF16),
            "ln_mem": ln_mem[l], "ln_memkv": ln_memkv[l], "w_mem_q": w_mem_q[l].astype(BF16),
            "w_mem_kv": w_mem_kv[l].astype(BF16), "w_mem_o": w_mem_o[l].astype(BF16),
            "ln_ffn": ln_ffn[l], "w_up": w_up[l].astype(BF16), "w_down": w_down[l].astype(BF16),
            "ln_final": ln_final,
        }
        cw = _compress_weights(cmp_pe[l], cmp_w1[l], cmp_b1[l], cmp_w2[l])
        last = l == depth - 1
        xp, p_nsa, p_win, p_mem, p_conv, p_rec = _prompt_layer(xp, mem_prompt, lw, cw, last)
        for acc, val in zip(outs_p, (p_nsa, p_win, p_mem, p_conv, p_rec)):
            acc.append(val)
        xs, s_nsa, s_win, s_conv, s_rec = _sample_layer(
            xs, cache_nsa_kv[l], cache_win_kv[l], cache_mem_kv[l], state_dn_conv[l], state_dn_rec[l],
            page_table, lw, _compress_page_weights(cmp_pe[l], cmp_w1[l], cmp_b1[l], cmp_w2[l]), last)
        for acc, val in zip(outs_s, (s_nsa, s_win, s_conv, s_rec)):
            acc.append(val)
    return (xp, xs) + tuple(jnp.stack(a) for a in outs_p) + tuple(jnp.stack(a) for a in outs_s)
```

```python
import functools

import jax
import jax.numpy as jnp
from jax import lax
from jax.experimental import pallas as pl
from jax.experimental.pallas import tpu as pltpu

F32 = jnp.float32
BF16 = jnp.bfloat16

HEAD_DIM = 64
DN_HEADS = 8
NSA_HEADS = 8
NSA_KV_HEADS = 2
NSA_GROUP = NSA_HEADS // NSA_KV_HEADS
DN_WIDTH = DN_HEADS * HEAD_DIM
NSA_WIDTH = NSA_HEADS * HEAD_DIM
CONV_WIDTH = 4
DN_CONV_DIM = 3 * DN_WIDTH
DN_CHUNK = 64
CMP_BLOCK = 64
SEL_BLOCK = 64
TOP_N = 16
N_LOCAL_BLOCKS = 2
WINDOW = 512
CMP_HIDDEN = 128
NSA_KV_KINDS = 6
MEM_HEADS = 4
PAGE_SIZE = 128
RMS_EPS = 1e-6
FORCE_SCORE = 1e3
NEG_INF = -1e30
ATTN_SCALE = HEAD_DIM ** -0.5

LANES = 128
P_QKV = 0
P_Z = P_QKV + DN_CONV_DIM
P_NQ = P_Z + DN_WIDTH
P_CMP = P_NQ + NSA_WIDTH
CMP_COLS = 2 * NSA_KV_HEADS * HEAD_DIM
P_SMALL = P_CMP + CMP_COLS
P_COLS = P_SMALL + 2 * LANES
KV_COLS = NSA_KV_KINDS * NSA_KV_HEADS * HEAD_DIM
VMEM_LIMIT = 56 * 1024 * 1024
DN_APPLY_PASSES = (3, 3, 3, 1, 1, 1)
DN_SQUARE_PASSES = (3, 3, 1, 1, 1)


def _cp(sem, vmem=VMEM_LIMIT):
    return pltpu.CompilerParams(dimension_semantics=sem, vmem_limit_bytes=vmem)


def _split2(a):
    hi = a.astype(BF16)
    return hi, (a - hi.astype(F32)).astype(BF16)


def _dot(a, b, passes=1):
    if passes == 1:
        return jnp.dot(a.astype(BF16), b.astype(BF16), preferred_element_type=F32)
    ah, al = _split2(a)
    bh, bl = _split2(b)
    return (jnp.dot(ah, bh, preferred_element_type=F32) + jnp.dot(ah, bl, preferred_element_type=F32)
            + jnp.dot(al, bh, preferred_element_type=F32))


def _dot_nt(a, b):
    return lax.dot_general(a.astype(BF16), b.astype(BF16), (((1,), (1,)), ((), ())),
                           preferred_element_type=F32)


def _split3(a):
    hi = a.astype(BF16)
    r1 = a - hi.astype(F32)
    mid = r1.astype(BF16)
    lo = (r1 - mid.astype(F32)).astype(BF16)
    return hi, mid, lo


def _dot_exact_lhs01(a01, b):
    a = a01.astype(BF16)
    hi, mid, lo = _split3(b)
    return (jnp.dot(a, hi, preferred_element_type=F32) + jnp.dot(a, mid, preferred_element_type=F32)
            + jnp.dot(a, lo, preferred_element_type=F32))


def _dot_exact_rhs01(a, b01):
    b = b01.astype(BF16)
    hi, mid, lo = _split3(a)
    return (jnp.dot(hi, b, preferred_element_type=F32) + jnp.dot(mid, b, preferred_element_type=F32)
            + jnp.dot(lo, b, preferred_element_type=F32))


def _rms(x, gain):
    ms = jnp.mean(x * x, axis=-1, keepdims=True)
    return x * lax.rsqrt(ms + RMS_EPS) * gain


def _softplus(x):
    return jnp.maximum(x, 0.0) + jnp.log1p(jnp.exp(-jnp.abs(x)))


def _silu(x):
    return x * jax.nn.sigmoid(x)


def _rms_mm_kernel(x_ref, g_ref, w_ref, o_ref, h_ref):
    @pl.when(pl.program_id(1) == 0)
    def _():
        h_ref[...] = _rms(x_ref[...], g_ref[...]).astype(BF16)

    o_ref[...] = jnp.dot(h_ref[...], w_ref[...], preferred_element_type=F32)


def rms_matmul(x, gain, w_bf16, tm, tn):
    n, d = x.shape
    m = w_bf16.shape[1]
    assert n % tm == 0 and m % tn == 0
    return pl.pallas_call(
        _rms_mm_kernel,
        grid=(n // tm, m // tn),
        in_specs=[pl.BlockSpec((tm, d), lambda i, j: (i, 0)),
                  pl.BlockSpec((1, d), lambda i, j: (0, 0)),
                  pl.BlockSpec((d, tn), lambda i, j: (0, j))],
        out_specs=pl.BlockSpec((tm, tn), lambda i, j: (i, j)),
        out_shape=jax.ShapeDtypeStruct((n, m), F32),
        scratch_shapes=[pltpu.VMEM((tm, d), BF16)],
        compiler_params=_cp(("parallel", "arbitrary")),
        name="rms_matmul",
    )(x, gain.reshape(1, d), w_bf16)


def _proj_kernel(x_ref, g_ref, w_ref, wkv_ref, o_ref, okv_ref, h_ref):
    @pl.when(pl.program_id(1) == 0)
    def _():
        h = _rms(x_ref[...], g_ref[...]).astype(BF16)
        h_ref[...] = h
        okv_ref[0] = lax.dot_general(wkv_ref[...], h, (((1,), (1,)), ((), ())), preferred_element_type=F32)

    o_ref[...] = jnp.dot(h_ref[...], w_ref[...], preferred_element_type=F32)


def input_projection(x3, gain, w_main, w_kvt, tm, tn):
    b, s, d = x3.shape
    n = b * s
    m = w_main.shape[1]
    kvc = w_kvt.shape[0]
    assert s % tm == 0 and m % tn == 0
    spt = s // tm
    return pl.pallas_call(
        _proj_kernel,
        grid=(n // tm, m // tn),
        in_specs=[pl.BlockSpec((tm, d), lambda i, j: (i, 0)),
                  pl.BlockSpec((1, d), lambda i, j: (0, 0)),
                  pl.BlockSpec((d, tn), lambda i, j: (0, j)),
                  pl.BlockSpec((kvc, d), lambda i, j: (0, 0))],
        out_specs=[pl.BlockSpec((tm, tn), lambda i, j: (i, j)),
                   pl.BlockSpec((1, kvc, tm), lambda i, j: (i // spt, 0, i % spt))],
        out_shape=[jax.ShapeDtypeStruct((n, m), F32), jax.ShapeDtypeStruct((b, kvc, s), F32)],
        scratch_shapes=[pltpu.VMEM((tm, d), BF16)],
        compiler_params=_cp(("parallel", "arbitrary")),
        name="input_projection",
    )(x3.reshape(n, d), gain.reshape(1, d), w_main, w_kvt)


def _mm_res_kernel(*refs, n_in):
    res_ref = refs[0]
    a_refs = refs[1:1 + n_in]
    w_refs = refs[1 + n_in:1 + 2 * n_in]
    o_ref = refs[1 + 2 * n_in]
    acc = res_ref[...]
    for a_ref, w_ref in zip(a_refs, w_refs):
        acc = acc + jnp.dot(a_ref[...].astype(BF16), w_ref[...], preferred_element_type=F32)
    o_ref[...] = acc


def matmul_residual(res, a_list, w_list, tm):
    n, d = res.shape
    assert n % tm == 0
    n_in = len(a_list)
    in_specs = [pl.BlockSpec((tm, d), lambda i: (i, 0))]
    in_specs += [pl.BlockSpec((tm, a.shape[1]), lambda i: (i, 0)) for a in a_list]
    in_specs += [pl.BlockSpec(w.shape, lambda i: (0, 0)) for w in w_list]
    return pl.pallas_call(
        functools.partial(_mm_res_kernel, n_in=n_in),
        grid=(n // tm,),
        in_specs=in_specs,
        out_specs=pl.BlockSpec((tm, d), lambda i: (i, 0)),
        out_shape=jax.ShapeDtypeStruct((n, d), F32),
        compiler_params=_cp(("parallel",)),
        name="matmul_residual",
    )(res, *a_list, *w_list)


def _ffn_kernel(x_ref, g_ref, wu_ref, wd_ref, gf_ref, o_ref, hn_ref, acc_ref, *, final_norm):
    j = pl.program_id(1)

    @pl.when(j == 0)
    def _():
        x = x_ref[...]
        hn_ref[...] = _rms(x, g_ref[...]).astype(BF16)
        acc_ref[...] = x

    u = jnp.dot(hn_ref[...], wu_ref[...], preferred_element_type=F32)
    u = jnp.square(jnp.maximum(u, 0.0)).astype(BF16)
    acc_ref[...] += jnp.dot(u, wd_ref[...], preferred_element_type=F32)

    @pl.when(j == pl.num_programs(1) - 1)
    def _():
        y = acc_ref[...]
        if final_norm:
            y = _rms(y, gf_ref[...])
        o_ref[...] = y


def ffn(x, gain, wu_bf16, wd_bf16, gain_final, final_norm, tm, tf):
    n, d = x.shape
    f = wu_bf16.shape[1]
    assert n % tm == 0 and f % tf == 0
    return pl.pallas_call(
        functools.partial(_ffn_kernel, final_norm=final_norm),
        grid=(n // tm, f // tf),
        in_specs=[pl.BlockSpec((tm, d), lambda i, j: (i, 0)),
                  pl.BlockSpec((1, d), lambda i, j: (0, 0)),
                  pl.BlockSpec((d, tf), lambda i, j: (0, j)),
                  pl.BlockSpec((tf, d), lambda i, j: (j, 0)),
                  pl.BlockSpec((1, d), lambda i, j: (0, 0))],
        out_specs=pl.BlockSpec((tm, d), lambda i, j: (i, 0)),
        out_shape=jax.ShapeDtypeStruct((n, d), F32),
        scratch_shapes=[pltpu.VMEM((tm, d), BF16), pltpu.VMEM((tm, d), F32)],
        compiler_params=_cp(("parallel", "arbitrary")),
        name="ffn",
    )(x, gain.reshape(1, d), wu_bf16, wd_bf16, gain_final.reshape(1, d))


def _mem_attn_kernel(q_ref, kv_ref, o_ref, *, heads, hd):
    scale = hd ** -0.5
    kt = jnp.swapaxes(kv_ref[0, :, 0], 0, 1)
    vt = jnp.swapaxes(kv_ref[0, :, 1], 0, 1)
    for h in range(heads):
        q = q_ref[0, :, h * hd:(h + 1) * hd]
        k = kt[h]
        v = vt[h]
        s = _dot_nt(q, k) * scale
        p = jnp.exp(s - jnp.max(s, axis=-1, keepdims=True))
        p = p / jnp.sum(p, axis=-1, keepdims=True)
        o_ref[0, :, h * hd:(h + 1) * hd] = _dot(p, v)


def mem_attention(q, kv, tq):
    b, t, d = q.shape
    m = kv.shape[1]
    hd = d // MEM_HEADS
    assert t % tq == 0 and kv.shape[2:] == (2, MEM_HEADS, hd)
    return pl.pallas_call(
        functools.partial(_mem_attn_kernel, heads=MEM_HEADS, hd=hd),
        grid=(b, t // tq),
        in_specs=[pl.BlockSpec((1, tq, d), lambda i, j: (i, j, 0)),
                  pl.BlockSpec((1, m, 2, MEM_HEADS, hd), lambda i, j: (i, 0, 0, 0, 0))],
        out_specs=pl.BlockSpec((1, tq, d), lambda i, j: (i, j, 0)),
        out_shape=jax.ShapeDtypeStruct((b, t, d), F32),
        compiler_params=_cp(("parallel", "parallel")),
        name="mem_attention",
    )(q, kv)


def _mem_block_kernel(x_ref, a1_ref, a2_ref, w1_ref, w2_ref, g_ref, wq_ref, kv_ref, wo_ref, o_ref, *, heads, hd):
    scale = hd ** -0.5
    x = (x_ref[0] + jnp.dot(a1_ref[0].astype(BF16), w1_ref[...], preferred_element_type=F32)
         + jnp.dot(a2_ref[0].astype(BF16), w2_ref[...], preferred_element_type=F32))
    q = jnp.dot(_rms(x, g_ref[...]).astype(BF16), wq_ref[...], preferred_element_type=F32)
    kt = jnp.swapaxes(kv_ref[0, :, 0], 0, 1)
    vt = jnp.swapaxes(kv_ref[0, :, 1], 0, 1)
    outs = []
    for h in range(heads):
        s = _dot_nt(q[:, h * hd:(h + 1) * hd], kt[h]) * scale
        p = jnp.exp(s - jnp.max(s, axis=-1, keepdims=True))
        p = p / jnp.sum(p, axis=-1, keepdims=True)
        outs.append(_dot(p, vt[h]))
    att = jnp.concatenate(outs, axis=1).astype(BF16)
    o_ref[0] = x + jnp.dot(att, wo_ref[...], preferred_element_type=F32)


def mem_block(x3, a1, a2, w1, w2, gain, wq, kv, wo, tq):
    b, t, d = x3.shape
    m = kv.shape[1]
    hd = d // MEM_HEADS
    assert t % tq == 0 and kv.shape[2:] == (2, MEM_HEADS, hd)
    return pl.pallas_call(
        functools.partial(_mem_block_kernel, heads=MEM_HEADS, hd=hd),
        grid=(b, t // tq),
        in_specs=[pl.BlockSpec((1, tq, d), lambda i, j: (i, j, 0)),
                  pl.BlockSpec((1, tq, a1.shape[2]), lambda i, j: (i, j, 0)),
                  pl.BlockSpec((1, tq, a2.shape[2]), lambda i, j: (i, j, 0)),
                  pl.BlockSpec(w1.shape, lambda i, j: (0, 0)),
                  pl.BlockSpec(w2.shape, lambda i, j: (0, 0)),
                  pl.BlockSpec((1, d), lambda i, j: (0, 0)),
                  pl.BlockSpec((d, d), lambda i, j: (0, 0)),
                  pl.BlockSpec((1, m, 2, MEM_HEADS, hd), lambda i, j: (i, 0, 0, 0, 0)),
                  pl.BlockSpec((d, d), lambda i, j: (0, 0))],
        out_specs=pl.BlockSpec((1, tq, d), lambda i, j: (i, j, 0)),
        out_shape=jax.ShapeDtypeStruct((b, t, d), F32),
        compiler_params=_cp(("parallel", "parallel")),
        name="mem_block",
    )(x3, a1, a2, w1, w2, gain.reshape(1, d), wq, kv, wo)


def _compress_kernel(x_ref, w1_ref, pe_ref, b1_ref, w2_ref, o_ref, acc_ref, *, row_cols, tt):
    j = pl.program_id(1)

    @pl.when(j == 0)
    def _():
        acc_ref[...] = jnp.zeros_like(acc_ref)

    hid2 = 2 * CMP_HIDDEN
    for kind in range(2):
        part = None
        for t in range(tt):
            c0 = t * row_cols + kind * LANES
            xs = x_ref[:, c0:c0 + LANES] + pe_ref[kind, t:t + 1, :]
            d = jnp.dot(xs.astype(BF16), w1_ref[kind, t], preferred_element_type=F32)
            part = d if part is None else part + d
        acc_ref[:, kind * hid2:(kind + 1) * hid2] += part

    @pl.when(j == pl.num_programs(1) - 1)
    def _():
        h = jnp.maximum(acc_ref[...] + b1_ref[...], 0.0)
        for kind in range(2):
            o_ref[:, kind * LANES:(kind + 1) * LANES] = jnp.dot(
                h[:, kind * hid2:(kind + 1) * hid2].astype(BF16), w2_ref[kind], preferred_element_type=F32)


def compress_blocks(x2, row_cols, cw, bt, tt=8):
    nb = x2.shape[0]
    assert nb % bt == 0 and CMP_BLOCK % tt == 0
    w1bd, pe2, b1bd, w2bd = cw
    return pl.pallas_call(
        functools.partial(_compress_kernel, row_cols=row_cols, tt=tt),
        grid=(nb // bt, CMP_BLOCK // tt),
        in_specs=[pl.BlockSpec((bt, tt * row_cols), lambda i, j: (i, j)),
                  pl.BlockSpec((2, tt, LANES, 2 * CMP_HIDDEN), lambda i, j: (0, j, 0, 0)),
                  pl.BlockSpec((2, tt, LANES), lambda i, j: (0, j, 0)),
                  pl.BlockSpec((1, 4 * CMP_HIDDEN), lambda i, j: (0, 0)),
                  pl.BlockSpec((2, 2 * CMP_HIDDEN, LANES), lambda i, j: (0, 0, 0))],
        out_specs=pl.BlockSpec((bt, 2 * LANES), lambda i, j: (i, 0)),
        out_shape=jax.ShapeDtypeStruct((nb, 2 * LANES), F32),
        scratch_shapes=[pltpu.VMEM((bt, 4 * CMP_HIDDEN), F32)],
        compiler_params=_cp(("parallel", "arbitrary")),
        name="compress_blocks",
    )(x2, w1bd, pe2, b1bd, w2bd)


def _compress_weights(cmp_pe, cmp_w1, cmp_b1, cmp_w2):
    w1r = cmp_w1.reshape(2, CMP_BLOCK, HEAD_DIM, CMP_HIDDEN)
    z = jnp.zeros_like(w1r)
    w1bd = jnp.concatenate([jnp.concatenate([w1r, z], -1), jnp.concatenate([z, w1r], -1)], axis=2).astype(BF16)
    pe2 = jnp.concatenate([cmp_pe, cmp_pe], -1)
    b1bd = jnp.concatenate([cmp_b1[0], cmp_b1[0], cmp_b1[1], cmp_b1[1]]).reshape(1, 4 * CMP_HIDDEN)
    z2 = jnp.zeros_like(cmp_w2)
    w2bd = jnp.concatenate([jnp.concatenate([cmp_w2, z2], -1), jnp.concatenate([z2, cmp_w2], -1)], axis=1).astype(BF16)
    return w1bd, pe2, b1bd, w2bd


def _masked_softmax(s, mask):
    s = jnp.where(mask, s, NEG_INF)
    p = jnp.where(mask, jnp.exp(s - jnp.max(s, axis=-1, keepdims=True)), 0.0)
    return p / jnp.maximum(jnp.sum(p, axis=-1, keepdims=True), 1e-30)


def _heads_as_rows(q_ref, tq):
    qb = q_ref[0] * ATTN_SCALE
    return jnp.concatenate([qb[:, h * HEAD_DIM:(h + 1) * HEAD_DIM] for h in range(NSA_GROUP)], axis=0).astype(BF16)


def _cmp_topk_kernel(q_ref, kc_ref, vct_ref, ocmp_ref, sel_ref, *, tq, nblk, topn):
    qi = pl.program_id(2)
    r, dh = NSA_GROUP, HEAD_DIM
    qpos = qi * tq + lax.broadcasted_iota(jnp.int32, (nblk, tq), 1)
    blk = lax.broadcasted_iota(jnp.int32, (nblk, tq), 0)
    vis = (blk + 1) * CMP_BLOCK - 1 <= qpos
    q = _heads_as_rows(q_ref, tq)
    kc = kc_ref[0, 0]
    vct = vct_ref[0, 0]
    imp = jnp.zeros((nblk, tq), F32)
    for h in range(r):
        s = jnp.where(vis, _dot_nt(kc, q[h * tq:(h + 1) * tq]), NEG_INF)
        p = jnp.where(vis, jnp.exp(s - jnp.max(s, axis=0, keepdims=True)), 0.0)
        p = p / jnp.maximum(jnp.sum(p, axis=0, keepdims=True), 1e-30)
        ocmp_ref[0, 0, h * dh:(h + 1) * dh, :] = _dot(vct, p)
        imp = imp + p
    cur = lax.shift_right_logical(qpos, 6)
    valid = blk <= cur
    forced = valid & ((blk == 0) | (cur - blk < N_LOCAL_BLOCKS))
    score = jnp.where(valid, imp + jnp.where(forced, FORCE_SCORE, 0.0), -1.0)
    rank = jnp.zeros((nblk, tq), F32)
    for i in range(nblk):
        si = score[i:i + 1, :]
        rank = rank + jnp.where((si > score) | ((si == score) & (blk > i)), 1.0, 0.0)
    sel_ref[0, 0] = jnp.where(rank < topn, 1.0, 0.0).astype(BF16)


def nsa_cmp_topk(proj3, kc, vct, tq):
    b, sq, _ = proj3.shape
    g = kc.shape[1]
    nblk = kc.shape[2]
    dh = HEAD_DIM
    gw = NSA_GROUP * dh
    assert sq % tq == 0 and SEL_BLOCK == 64 and P_NQ % gw == 0
    return pl.pallas_call(
        functools.partial(_cmp_topk_kernel, tq=tq, nblk=nblk, topn=min(TOP_N, nblk)),
        grid=(b, g, sq // tq),
        in_specs=[pl.BlockSpec((1, tq, gw), lambda i, j, k: (i, k, P_NQ // gw + j)),
                  pl.BlockSpec((1, 1, nblk, dh), lambda i, j, k: (i, j, 0, 0)),
                  pl.BlockSpec((1, 1, dh, nblk), lambda i, j, k: (i, j, 0, 0))],
        out_specs=[pl.BlockSpec((1, 1, gw, tq), lambda i, j, k: (i, j, 0, k)),
                   pl.BlockSpec((1, 1, nblk, tq), lambda i, j, k: (i, j, 0, k))],
        out_shape=[jax.ShapeDtypeStruct((b, g, gw, sq), F32),
                   jax.ShapeDtypeStruct((b, g, nblk, sq), BF16)],
        compiler_params=_cp(("parallel", "parallel", "parallel")),
        name="nsa_cmp_topk",
    )(proj3, kc, vct)


def _nsa_attn_kernel(q_ref, kst_ref, vst_ref, kwt_ref, vwt_ref, selt_ref, ocmp_ref, sm_ref, o_ref,
                     m_sc, l_sc, acc_sc, *, tq, tk, nsel):
    grp = pl.program_id(1)
    qi = pl.program_id(2)
    r = NSA_GROUP
    dh = HEAD_DIM
    q = _heads_as_rows(q_ref, tq)
    qpos = qi * tq + lax.broadcasted_iota(jnp.int32, (1, tq), 1)

    def reset():
        m_sc[...] = jnp.full(m_sc.shape, NEG_INF, F32)
        l_sc[...] = jnp.zeros(l_sc.shape, F32)
        acc_sc[...] = jnp.zeros(acc_sc.shape, F32)

    def step(kt, vt, mask):
        bias = jnp.where(mask, 0.0, NEG_INF)
        s = _dot_nt(kt.T, q) + jnp.concatenate([bias] * r, axis=1)
        m_prev = m_sc[...]
        m_new = jnp.maximum(m_prev, jnp.max(s, axis=0, keepdims=True))
        alpha = jnp.exp(m_prev - m_new)
        p = jnp.exp(s - m_new)
        l_sc[...] = alpha * l_sc[...] + jnp.sum(p, axis=0, keepdims=True)
        acc_sc[...] = alpha * acc_sc[...] + _dot(vt, p)
        m_sc[...] = m_new

    def result():
        return acc_sc[...] / jnp.maximum(l_sc[...], 1e-30)

    reset()
    selt = selt_ref[0, 0]
    n_kt = ((qi + 1) * tq + tk - 1) // tk

    def slc_body(kt, carry):
        k0 = pl.multiple_of(kt * tk, tk)
        kpos = k0 + lax.broadcasted_iota(jnp.int32, (tk, 1), 0)
        erow = k0 + lax.broadcasted_iota(jnp.int32, (tk, nsel), 0)
        eblk = lax.broadcasted_iota(jnp.int32, (tk, nsel), 1)
        expand = jnp.where(lax.shift_right_logical(erow, 6) == eblk, 1.0, 0.0).astype(BF16)
        chosen = jnp.dot(expand, selt, preferred_element_type=F32) > 0.5
        step(kst_ref[0, :, pl.ds(k0, tk)], vst_ref[0, :, pl.ds(k0, tk)], chosen & (kpos <= qpos))
        return carry

    lax.fori_loop(0, n_kt, slc_body, 0)
    o_slc = result()

    reset()
    wk = WINDOW + tq
    w0 = pl.multiple_of(jnp.maximum(qi - WINDOW // tq, 0) * tq, tq)
    dpos = qpos - (w0 + lax.broadcasted_iota(jnp.int32, (wk, 1), 0))
    step(kwt_ref[0, :, pl.ds(w0, wk)], vwt_ref[0, :, pl.ds(w0, wk)], (dpos >= 0) & (dpos < WINDOW))
    o_win = result()

    gates_t = jax.nn.sigmoid(sm_ref[0]).T
    g0 = 2 * DN_HEADS
    per = 3 * r
    gt = jnp.where(grp == 0, gates_t[g0:g0 + per], gates_t[g0 + per:g0 + 2 * per])
    outs = []
    for h in range(r):
        outs.append(gt[3 * h:3 * h + 1] * ocmp_ref[0, 0, h * dh:(h + 1) * dh, :]
                    + gt[3 * h + 1:3 * h + 2] * o_slc[:, h * tq:(h + 1) * tq]
                    + gt[3 * h + 2:3 * h + 3] * o_win[:, h * tq:(h + 1) * tq])
    o_ref[0] = jnp.concatenate(outs, axis=0).T


def nsa_attention(proj3, kvt, selt, ocmpt, tq, tk):
    b, sq, _ = proj3.shape
    t = kvt.shape[2]
    g, nsel = selt.shape[1], selt.shape[2]
    r, dh = NSA_GROUP, HEAD_DIM
    gw = r * dh
    assert sq == t and sq % tq == 0 and t % tk == 0 and tk % tq == 0 and WINDOW % tq == 0 and tq % LANES == 0
    assert t >= WINDOW + tq and g == NSA_KV_HEADS and g == 2

    def kv_spec(kind):
        return pl.BlockSpec((1, dh, t), lambda i, j, k: (i, kind * g + j, 0))

    return pl.pallas_call(
        functools.partial(_nsa_attn_kernel, tq=tq, tk=tk, nsel=nsel),
        grid=(b, g, sq // tq),
        in_specs=[pl.BlockSpec((1, tq, gw), lambda i, j, k: (i, k, P_NQ // gw + j)),
                  kv_spec(2), kv_spec(3), kv_spec(4), kv_spec(5),
                  pl.BlockSpec((1, 1, nsel, tq), lambda i, j, k: (i, j, 0, k)),
                  pl.BlockSpec((1, 1, gw, tq), lambda i, j, k: (i, j, 0, k)),
                  pl.BlockSpec((1, tq, LANES), lambda i, j, k: (i, k, P_SMALL // LANES))],
        out_specs=pl.BlockSpec((1, tq, gw), lambda i, j, k: (i, k, j)),
        out_shape=jax.ShapeDtypeStruct((b, sq, g * gw), F32),
        scratch_shapes=[pltpu.VMEM((1, r * tq), F32), pltpu.VMEM((1, r * tq), F32),
                        pltpu.VMEM((dh, r * tq), F32)],
        compiler_params=_cp(("parallel", "parallel", "parallel")),
        name="nsa_attention",
    )(proj3, kvt, kvt, kvt, kvt, selt, ocmpt, proj3)


def _bdot(a, b, passes=1):
    dims = (((2,), (1,)), ((0,), (0,)))
    if passes == 1:
        return lax.dot_general(a.astype(BF16), b.astype(BF16), dims, preferred_element_type=F32)
    ah, al = _split2(a)
    bh, bl = _split2(b)
    return (lax.dot_general(ah, bh, dims, preferred_element_type=F32)
            + lax.dot_general(ah, bl, dims, preferred_element_type=F32)
            + lax.dot_general(al, bh, dims, preferred_element_type=F32))


def _bdot_nt(a, b):
    return lax.dot_general(a.astype(BF16), b.astype(BF16), (((2,), (2,)), ((0,), (0,))),
                           preferred_element_type=F32)


def _deltanet_kernel(qkv_ref, z_ref, sm_ref, cw_ref, alog_ref, dtb_ref, gn_ref, o_ref, s_out_ref,
                     xbuf, s_sc):
    c = pl.program_id(0)
    ch = DN_CHUNK
    n_pairs = DN_HEADS // 2
    two = 2 * ch
    n_batch = qkv_ref.shape[0]

    @pl.when(c == 0)
    def _():
        xbuf[:, 0:8, :] = jnp.zeros((n_batch, 8, DN_CONV_DIM), F32)
        s_sc[...] = jnp.zeros_like(s_sc)

    ti = lax.broadcasted_iota(jnp.int32, (ch, ch), 0)
    tj = lax.broadcasted_iota(jnp.int32, (ch, ch), 1)
    tri = jnp.where(ti >= tj, 1.0, 0.0)
    lane = lax.broadcasted_iota(jnp.int32, (ch, LANES), 1)
    lo = lane < HEAD_DIM
    row2 = lax.broadcasted_iota(jnp.int32, (two, two), 0)
    col2 = lax.broadcasted_iota(jnp.int32, (two, two), 1)
    same = (row2 >= ch) == (col2 >= ch)
    incl = (same & (row2 >= col2))[None]
    strict = (same & (row2 > col2))[None]
    top = lax.broadcasted_iota(jnp.int32, (two, 1), 0) < ch

    def seg_sum(x):
        s_lo = jnp.sum(jnp.where(lo, x, 0.0), axis=-1, keepdims=True)
        s_hi = jnp.sum(jnp.where(lo, 0.0, x), axis=-1, keepdims=True)
        return jnp.where(lo, s_lo, s_hi)

    def stack2(x):
        return jnp.concatenate([jnp.where(lo, x, 0.0), jnp.where(lo, 0.0, x)], axis=0)

    def col2x(a, b):
        return jnp.concatenate([jnp.broadcast_to(a, (ch, LANES)), jnp.broadcast_to(b, (ch, LANES))], axis=0)

    q_l, k_l, v_l, beta_l, gc_l, gl_l = [], [], [], [], [], []
    for bi in range(n_batch):
        xbuf[bi, 8:8 + ch, :] = qkv_ref[bi]
        conv = None
        for w in range(CONV_WIDTH):
            term = xbuf[bi, 5 + w:5 + w + ch, :] * cw_ref[w:w + 1, :]
            conv = term if conv is None else conv + term
        xbuf[bi, 0:8, :] = xbuf[bi, ch:ch + 8, :]
        act = _silu(conv)
        sm = sm_ref[bi]
        beta_all = jax.nn.sigmoid(sm)
        g_all = -jnp.exp(alog_ref[...]) * _softplus(sm + dtb_ref[...])
        gcum_all = _dot_exact_lhs01(tri, g_all)
        for p in range(n_pairs):
            c0 = p * LANES
            qp = act[:, c0:c0 + LANES]
            kp = act[:, DN_WIDTH + c0:DN_WIDTH + c0 + LANES]
            vp = act[:, 2 * DN_WIDTH + c0:2 * DN_WIDTH + c0 + LANES]
            qp = qp * lax.rsqrt(seg_sum(qp * qp) + 1e-6) * (HEAD_DIM ** -0.5)
            kp = kp * lax.rsqrt(seg_sum(kp * kp) + 1e-6)
            h0, h1 = DN_HEADS + 2 * p, DN_HEADS + 2 * p + 1
            q_l.append(stack2(qp))
            k_l.append(stack2(kp))
            v_l.append(stack2(vp))
            beta_l.append(col2x(beta_all[:, 2 * p:2 * p + 1], beta_all[:, 2 * p + 1:2 * p + 2]))
            gc_l.append(col2x(gcum_all[:, h0:h0 + 1], gcum_all[:, h1:h1 + 1]))
            gl_l.append(jnp.broadcast_to(jnp.where(top, gcum_all[ch - 1:ch, h0:h0 + 1], gcum_all[ch - 1:ch, h1:h1 + 1]),
                                         (two, LANES)))
    q2, k2, v2 = jnp.stack(q_l), jnp.stack(k_l), jnp.stack(v_l)
    beta2, gc2, gl2 = jnp.stack(beta_l), jnp.stack(gc_l), jnp.stack(gl_l)
    decay = jnp.exp(jnp.where(incl, gc2 - jnp.swapaxes(gc2, 1, 2), NEG_INF))
    kb2 = k2 * beta2
    a_mat = jnp.where(strict, _bdot_nt(kb2, k2) * decay, 0.0)
    aqk = jnp.where(incl, _bdot_nt(q2, k2) * decay, 0.0)
    s_old = s_sc[...]
    egc = jnp.exp(gc2)
    x = beta2 * (v2 - egc * _bdot(k2, s_old))
    pw = -a_mat
    n_lvl = ch.bit_length() - 1
    for lvl in range(n_lvl):
        x = x + _bdot(pw, x, DN_APPLY_PASSES[lvl])
        if lvl + 1 < n_lvl:
            pw = _bdot(pw, pw, DN_SQUARE_PASSES[lvl])
    o2 = _bdot(q2 * egc, s_old) + _bdot(aqk, x)
    kdec = k2 * jnp.exp(gl2 - gc2)
    s_sc[...] = s_old * jnp.exp(gl2) + _bdot(jnp.swapaxes(kdec, 1, 2), x)
    for bi in range(n_batch):
        for p in range(n_pairs):
            c0 = p * LANES
            o_n = o2[bi * n_pairs + p]
            o_pair = o_n[0:ch] + o_n[ch:two]
            inv = lax.rsqrt(seg_sum(o_pair * o_pair) * (1.0 / HEAD_DIM) + RMS_EPS)
            o_ref[bi, :, c0:c0 + LANES] = o_pair * inv * gn_ref[...] * _silu(z_ref[bi, :, c0:c0 + LANES])

    @pl.when(c == pl.num_programs(0) - 1)
    def _():
        s_out_ref[...] = s_sc[...]


def deltanet_prompt(proj3, conv_w, a_log, dt_bias, norm_gain):
    b, t, _ = proj3.shape
    ch = DN_CHUNK
    assert t % ch == 0
    pad = jnp.zeros((LANES - 2 * DN_HEADS,), F32)
    alog_row = jnp.concatenate([jnp.zeros((DN_HEADS,), F32), a_log, pad]).reshape(1, LANES)
    dtb_row = jnp.concatenate([jnp.zeros((DN_HEADS,), F32), dt_bias, pad]).reshape(1, LANES)
    gn_row = jnp.concatenate([norm_gain, norm_gain]).reshape(1, LANES)
    n_pairs = DN_HEADS // 2
    o, s_fin = pl.pallas_call(
        _deltanet_kernel,
        grid=(t // ch,),
        in_specs=[pl.BlockSpec((b, ch, DN_CONV_DIM), lambda j: (0, j, P_QKV // DN_CONV_DIM)),
                  pl.BlockSpec((b, ch, DN_WIDTH), lambda j: (0, j, P_Z // DN_WIDTH)),
                  pl.BlockSpec((b, ch, LANES), lambda j: (0, j, P_SMALL // LANES)),
                  pl.BlockSpec((CONV_WIDTH, DN_CONV_DIM), lambda j: (0, 0)),
                  pl.BlockSpec((1, LANES), lambda j: (0, 0)),
                  pl.BlockSpec((1, LANES), lambda j: (0, 0)),
                  pl.BlockSpec((1, LANES), lambda j: (0, 0))],
        out_specs=[pl.BlockSpec((b, ch, DN_WIDTH), lambda j: (0, j, 0)),
                   pl.BlockSpec((b * n_pairs, 2 * ch, LANES), lambda j: (0, 0, 0))],
        out_shape=[jax.ShapeDtypeStruct((b, t, DN_WIDTH), F32),
                   jax.ShapeDtypeStruct((b * n_pairs, 2 * ch, LANES), F32)],
        scratch_shapes=[pltpu.VMEM((b, ch + 8, DN_CONV_DIM), F32), pltpu.VMEM((b * n_pairs, 2 * ch, LANES), F32)],
        compiler_params=_cp(("arbitrary",)),
        name="deltanet_prompt",
    )(proj3, proj3, proj3, conv_w, alog_row, dtb_row, gn_row)
    return o, s_fin.reshape(b, n_pairs, 2 * ch, LANES)


def _pairs_to_heads(s_pairs):
    d = HEAD_DIM
    return jnp.stack([s_pairs[:, :, :d, :d], s_pairs[:, :, d:, d:]], axis=2).reshape(
        s_pairs.shape[0], DN_HEADS, d, d)


def _dn_step_prep_kernel(qkv_ref, cs_ref, sm_ref, cw_ref, alog_ref, dtb_ref, ones_ref,
                         q_ref, k_ref, v_ref, sc_ref):
    conv = qkv_ref[...] * cw_ref[CONV_WIDTH - 1:CONV_WIDTH, :]
    for w in range(CONV_WIDTH - 1):
        conv = conv + cs_ref[w] * cw_ref[w:w + 1, :]
    act = _silu(conv)
    q = act[:, 0:DN_WIDTH]
    k = act[:, DN_WIDTH:2 * DN_WIDTH]

    def seg_sum(x):
        return _dot_exact_rhs01(x, ones_ref[...])

    q_ref[...] = q * lax.rsqrt(seg_sum(q * q) + 1e-6) * (HEAD_DIM ** -0.5)
    k_ref[...] = k * lax.rsqrt(seg_sum(k * k) + 1e-6)
    v_ref[...] = act[:, 2 * DN_WIDTH:]
    sm = sm_ref[...]
    g = -jnp.exp(alog_ref[...]) * _softplus(sm + dtb_ref[...])
    lane = lax.broadcasted_iota(jnp.int32, sm.shape, 1)
    sc_ref[...] = jnp.where(lane < DN_HEADS, jax.nn.sigmoid(sm), jnp.exp(g))


def _dn_step_kernel(kt_ref, qt_ref, v_ref, sc_ref, z_ref, gn_ref, s_ref, o_ref, s_out_ref, *, nseq):
    hds, d = DN_HEADS, HEAD_DIM
    for j in range(nseq):
        s_old = s_ref[j]
        kcol = kt_ref[0, :, j:j + 1]
        qcol = qt_ref[0, :, j:j + 1]
        beta = sc_ref[j, :, 0:1]
        eg = sc_ref[j, :, 1:2]
        v = v_ref[j]
        ks = jnp.sum((kcol * s_old).reshape(hds, d, d), axis=1)
        qs = jnp.sum((qcol * s_old).reshape(hds, d, d), axis=1)
        qk = jnp.sum((qcol * kcol).reshape(hds, d, 1), axis=1)
        v_new = beta * (v - eg * ks)
        o = eg * qs + qk * v_new
        inv = lax.rsqrt(jnp.mean(o * o, axis=-1, keepdims=True) + RMS_EPS)
        o_ref[j] = o * inv * gn_ref[...] * _silu(z_ref[j])
        s3 = s_old.reshape(hds, d, d) * eg[:, :, None] + kcol.reshape(hds, d, 1) * v_new[:, None, :]
        s_out_ref[j] = s3.reshape(hds * d, d)


def deltanet_sample(proj_s, conv_state, rec_state, conv_w, a_log, dt_bias, norm_gain, nseq=8):
    n = proj_s.shape[0]
    assert n % nseq == 0
    hds, d = DN_HEADS, HEAD_DIM
    pad = jnp.zeros((LANES - 2 * hds,), F32)
    alog_row = jnp.concatenate([jnp.zeros((hds,), F32), a_log, pad]).reshape(1, LANES)
    dtb_row = jnp.concatenate([jnp.zeros((hds,), F32), dt_bias, pad]).reshape(1, LANES)
    head_of = jnp.arange(DN_WIDTH) // d
    ones_bd = (head_of[:, None] == head_of[None, :]).astype(BF16)
    cs = jnp.transpose(conv_state, (1, 0, 2))
    full = lambda shape: pl.BlockSpec(shape, lambda i: (0,) * len(shape))
    q, k, v, sc = pl.pallas_call(
        _dn_step_prep_kernel,
        grid=(1,),
        in_specs=[pl.BlockSpec((n, DN_CONV_DIM), lambda i: (0, P_QKV // DN_CONV_DIM)),
                  full((CONV_WIDTH - 1, n, DN_CONV_DIM)),
                  pl.BlockSpec((n, LANES), lambda i: (0, P_SMALL // LANES)),
                  full((CONV_WIDTH, DN_CONV_DIM)), full((1, LANES)), full((1, LANES)),
                  full((DN_WIDTH, DN_WIDTH))],
        out_specs=[full((n, DN_WIDTH)), full((n, DN_WIDTH)), full((n, DN_WIDTH)), full((n, LANES))],
        out_shape=[jax.ShapeDtypeStruct((n, DN_WIDTH), F32)] * 3 + [jax.ShapeDtypeStruct((n, LANES), F32)],
        compiler_params=_cp(("arbitrary",)),
        name="dn_step_prep",
    )(proj_s, cs, proj_s, conv_w, alog_row, dtb_row, ones_bd)
    ng = n // nseq
    kt = jnp.transpose(k.reshape(ng, nseq, DN_WIDTH), (0, 2, 1))
    qt = jnp.transpose(q.reshape(ng, nseq, DN_WIDTH), (0, 2, 1))
    sc3 = jnp.stack([sc[:, :hds], sc[:, hds:2 * hds]], axis=-1)
    z = proj_s[:, P_Z:P_Z + DN_WIDTH].reshape(n, hds, d)
    o, s_new = pl.pallas_call(
        functools.partial(_dn_step_kernel, nseq=nseq),
        grid=(ng,),
        in_specs=[pl.BlockSpec((1, DN_WIDTH, nseq), lambda i: (i, 0, 0)),
                  pl.BlockSpec((1, DN_WIDTH, nseq), lambda i: (i, 0, 0)),
                  pl.BlockSpec((nseq, hds, d), lambda i: (i, 0, 0)),
                  pl.BlockSpec((nseq, hds, 2), lambda i: (i, 0, 0)),
                  pl.BlockSpec((nseq, hds, d), lambda i: (i, 0, 0)),
                  pl.BlockSpec((1, d), lambda i: (0, 0)),
                  pl.BlockSpec((nseq, hds * d, d), lambda i: (i, 0, 0))],
        out_specs=[pl.BlockSpec((nseq, hds, d), lambda i: (i, 0, 0)),
                   pl.BlockSpec((nseq, hds * d, d), lambda i: (i, 0, 0))],
        out_shape=[jax.ShapeDtypeStruct((n, hds, d), F32), jax.ShapeDtypeStruct((n, hds * d, d), F32)],
        compiler_params=_cp(("parallel",)),
        name="dn_step",
    )(kt, qt, v.reshape(n, hds, d), sc3, z, norm_gain.reshape(1, d), rec_state.reshape(n, hds * d, d))
    return o.reshape(n, DN_WIDTH), s_new.reshape(n, hds, d, d)


def _nsa_decode_one(u, q_ref, new_ref, gate_ref, win_ref, exp_ref, kc_refs, pg_refs, o_ref, kc_sc, n_pages):
    dh, r, g2, nh = HEAD_DIM, NSA_GROUP, NSA_KV_HEADS, NSA_HEADS
    past = n_pages * PAGE_SIZE
    nb = past // CMP_BLOCK
    nsel = nb + 1
    wlen = win_ref.shape[4]
    for j in range(n_pages):
        kc_sc[u, j:j + 1, :] = kc_refs[j][0]
    kcv = kc_sc[u]
    new = new_ref[u]
    q8 = q_ref[u] * ATTN_SCALE
    head = lax.broadcasted_iota(jnp.int32, (nh, 1), 0)
    lane = lax.broadcasted_iota(jnp.int32, (nh, LANES), 1)
    g0 = head < r
    blk = jnp.where(lane < n_pages, 2 * lane, jnp.where(lane < nb, 2 * (lane - n_pages) + 1, lane))
    wp = lax.broadcasted_iota(jnp.int32, (1, wlen + LANES), 1)
    dpos = wlen - wp
    wmask = (dpos >= 0) & (dpos < WINDOW) & (past - wlen + wp >= 0)

    def both(x):
        return jnp.where(g0, x[:, 0:dh], x[:, dh:2 * dh])

    def new_part(kind):
        return both(jnp.broadcast_to(new[:, 2 * kind * dh:2 * (kind + 1) * dh], (nh, 2 * dh)))

    q_bd = jnp.concatenate([jnp.where(g0, q8, 0.0), jnp.where(g0, 0.0, q8)], axis=1)

    def cmp_rows(base):
        even = jnp.concatenate([kcv[:, base:base + dh], kcv[:, base + LANES:base + LANES + dh]], axis=1)
        odd = jnp.concatenate([kcv[:, base + dh:base + LANES], kcv[:, base + LANES + dh:base + 2 * LANES]], axis=1)
        return jnp.concatenate([even, odd], axis=0)

    s = _dot_nt(q_bd, cmp_rows(0))
    p = jnp.exp(s - jnp.max(s, axis=-1, keepdims=True))
    p = p / jnp.maximum(jnp.sum(p, axis=-1, keepdims=True), 1e-30)
    o_cmp = both(_dot(p, cmp_rows(g2 * LANES)))
    bid_row = blk[0:1].astype(F32)
    bid_col = jnp.broadcast_to(bid_row, (LANES, LANES)).T
    valid = lane[0:1] < nsel
    forced = valid & ((blk[0:1] == 0) | (nb - blk[0:1] < N_LOCAL_BLOCKS))
    sels = []
    for g in range(g2):
        in_g = (head >= g * r) & (head < (g + 1) * r)
        imp = jnp.sum(jnp.where(in_g, p, 0.0), axis=0, keepdims=True)
        imp = jnp.concatenate([imp, jnp.zeros((1, LANES - nb), F32)], axis=1)
        score = jnp.where(valid, imp + jnp.where(forced, FORCE_SCORE, 0.0), -1.0)
        sc_row = jnp.broadcast_to(score, (LANES, LANES))
        sc_col = sc_row.T
        beats = (sc_col > sc_row) | ((sc_col == sc_row) & (bid_col < bid_row))
        rank = jnp.sum(jnp.where(beats, 1.0, 0.0), axis=0, keepdims=True)
        sels.append(jnp.where(valid & (rank < min(TOP_N, nsel)), 1.0, 0.0))
    sel8 = jnp.where(g0, sels[0], sels[1]).astype(BF16)
    kt_all = jnp.concatenate([pg_refs[j][0, 0].reshape(2 * dh, PAGE_SIZE) for j in range(n_pages)], axis=1)
    vt_all = jnp.concatenate([pg_refs[j][0, 1].reshape(2 * dh, PAGE_SIZE) for j in range(n_pages)], axis=1)
    s_new = jnp.sum(q8 * new_part(2), axis=-1, keepdims=True)
    s_all = jnp.concatenate([_dot(q_bd, kt_all), jnp.broadcast_to(s_new, (nh, LANES))], axis=1)
    chosen = jnp.dot(sel8, exp_ref[...], preferred_element_type=F32) > 0.5
    pm = _masked_softmax(s_all, chosen)
    o_slc = pm[:, past:past + 1] * new_part(3) + both(_dot_nt(pm[:, 0:past], vt_all))
    sw = _dot(q_bd, win_ref[u, 0].reshape(2 * dh, wlen))
    sw_new = jnp.sum(q8 * new_part(4), axis=-1, keepdims=True)
    pw = _masked_softmax(jnp.concatenate([sw, jnp.broadcast_to(sw_new, (nh, LANES))], axis=1), wmask)
    o_win = pw[:, wlen:wlen + 1] * new_part(5) + both(_dot_nt(pw[:, 0:wlen], win_ref[u, 1].reshape(2 * dh, wlen)))
    gates = jnp.broadcast_to(jax.nn.sigmoid(gate_ref[u]), (nh, LANES))

    def gate(branch):
        return jnp.sum(jnp.where(lane == 3 * head + branch, gates, 0.0), axis=-1, keepdims=True)

    o_ref[u] = gate(0) * o_cmp + gate(1) * o_slc + gate(2) * o_win


def _nsa_decode_kernel(pt_ref, q_ref, new_ref, gate_ref, win_ref, exp_ref, *refs, n_pages, seqs):
    del pt_ref
    o_ref, kc_sc = refs[2 * seqs * n_pages], refs[2 * seqs * n_pages + 1]
    for u in range(seqs):
        kc_refs = refs[u * n_pages:(u + 1) * n_pages]
        pg_refs = refs[(seqs + u) * n_pages:(seqs + u + 1) * n_pages]
        _nsa_decode_one(u, q_ref, new_ref, gate_ref, win_ref, exp_ref, kc_refs, pg_refs, o_ref, kc_sc, n_pages)


def nsa_decode(page_table, q3, new_row, gate_row, win_t, kc_phys, cache_t, seqs=2):
    n, n_pages = page_table.shape
    dh = HEAD_DIM
    g2 = NSA_KV_HEADS
    wlen = win_t.shape[4]
    assert 2 * n_pages + 1 <= LANES and PAGE_SIZE == 2 * CMP_BLOCK and PAGE_SIZE == LANES and g2 == 2

    past = n_pages * PAGE_SIZE
    nb = past // CMP_BLOCK
    erow = lax.broadcasted_iota(jnp.int32, (LANES, past + LANES), 0)
    ecol = lax.broadcasted_iota(jnp.int32, (LANES, past + LANES), 1)
    page, second = ecol // PAGE_SIZE, (ecol % PAGE_SIZE) >= CMP_BLOCK
    expand = (((erow < n_pages) & (page == erow) & ~second & (ecol < past))
              | ((erow >= n_pages) & (erow < nb) & (page == erow - n_pages) & second & (ecol < past))
              | ((erow == nb) & (ecol == past))).astype(BF16)

    assert n % seqs == 0

    def kc_map(u, j):
        return lambda i, pt: (pt[seqs * i + u, j], 0, 0)

    def slc_map(u, j):
        return lambda i, pt: (pt[seqs * i + u, j], 1, 0, 0, 0)

    in_specs = [pl.BlockSpec((seqs, NSA_HEADS, dh), lambda i, pt: (i, 0, 0)),
                pl.BlockSpec((seqs, 1, KV_COLS), lambda i, pt: (i, 0, 0)),
                pl.BlockSpec((seqs, 1, LANES), lambda i, pt: (i, 0, 0)),
                pl.BlockSpec((seqs, 2, g2, dh, wlen), lambda i, pt: (i, 0, 0, 0, 0)),
                pl.BlockSpec((LANES, past + LANES), lambda i, pt: (0, 0))]
    in_specs += [pl.BlockSpec((1, 1, 4 * LANES), kc_map(u, j)) for u in range(seqs) for j in range(n_pages)]
    in_specs += [pl.BlockSpec((1, 2, g2, dh, PAGE_SIZE), slc_map(u, j)) for u in range(seqs) for j in range(n_pages)]
    grid_spec = pltpu.PrefetchScalarGridSpec(
        num_scalar_prefetch=1, grid=(n // seqs,), in_specs=in_specs,
        out_specs=pl.BlockSpec((seqs, NSA_HEADS, dh), lambda i, pt: (i, 0, 0)),
        scratch_shapes=[pltpu.VMEM((seqs, n_pages, 4 * LANES), F32)])
    return pl.pallas_call(
        functools.partial(_nsa_decode_kernel, n_pages=n_pages, seqs=seqs),
        grid_spec=grid_spec,
        out_shape=jax.ShapeDtypeStruct((n, NSA_HEADS, dh), F32),
        compiler_params=_cp(("arbitrary",)),
        name="nsa_decode",
    )(page_table, q3, new_row, gate_row, win_t, expand, *([kc_phys] * (seqs * n_pages)), *([cache_t] * (seqs * n_pages)))


def _compress_pages_kernel(x_ref, wd_ref, ped_ref, b1_ref, w2_ref, o_ref, acc_ref, *, dd):
    j = pl.program_id(1)
    g2 = NSA_KV_HEADS

    @pl.when(j == 0)
    def _():
        acc_ref[...] = jnp.zeros_like(acc_ref)

    for kind in range(2):
        for g in range(g2):
            part = None
            xt = jnp.swapaxes(x_ref[:, kind, g], 0, 1)
            for dl in range(dd):
                xs = xt[dl] + ped_ref[kind, dl:dl + 1, :]
                d = jnp.dot(xs.astype(BF16), wd_ref[kind, dl], preferred_element_type=F32)
                part = d if part is None else part + d
            acc_ref[kind * g2 + g] += part

    @pl.when(j == pl.num_programs(1) - 1)
    def _():
        for kind in range(2):
            for g in range(g2):
                h = jnp.maximum(acc_ref[kind * g2 + g] + b1_ref[kind:kind + 1, :], 0.0)
                o_ref[:, 0, (kind * g2 + g) * LANES:(kind * g2 + g + 1) * LANES] = jnp.dot(
                    h.astype(BF16), w2_ref[kind], preferred_element_type=F32)


def compress_pages(cache_t, cwp, bp, dd=8):
    n_phys = cache_t.shape[0]
    g2, dh = NSA_KV_HEADS, HEAD_DIM
    assert n_phys % bp == 0 and dh % dd == 0
    wd, ped, b1h, w2h = cwp
    return pl.pallas_call(
        functools.partial(_compress_pages_kernel, dd=dd),
        grid=(n_phys // bp, dh // dd),
        in_specs=[pl.BlockSpec((bp, 2, g2, dd, PAGE_SIZE), lambda i, j: (i, 0, 0, j, 0)),
                  pl.BlockSpec((2, dd, PAGE_SIZE, 2 * CMP_HIDDEN), lambda i, j: (0, j, 0, 0)),
                  pl.BlockSpec((2, dd, PAGE_SIZE), lambda i, j: (0, j, 0)),
                  pl.BlockSpec((2, 2 * CMP_HIDDEN), lambda i, j: (0, 0)),
                  pl.BlockSpec((2, 2 * CMP_HIDDEN, LANES), lambda i, j: (0, 0, 0))],
        out_specs=pl.BlockSpec((bp, 1, 2 * g2 * LANES), lambda i, j: (i, 0, 0)),
        out_shape=jax.ShapeDtypeStruct((n_phys, 1, 2 * g2 * LANES), F32),
        scratch_shapes=[pltpu.VMEM((2 * g2, bp, 2 * CMP_HIDDEN), F32)],
        compiler_params=_cp(("parallel", "arbitrary")),
        name="compress_pages",
    )(cache_t, wd, ped, b1h, w2h)


def _compress_page_weights(cmp_pe, cmp_w1, cmp_b1, cmp_w2):
    w1t = jnp.transpose(cmp_w1.reshape(2, CMP_BLOCK, HEAD_DIM, CMP_HIDDEN), (0, 2, 1, 3))
    z = jnp.zeros_like(w1t)
    wd = jnp.concatenate([jnp.concatenate([w1t, z], -1), jnp.concatenate([z, w1t], -1)], axis=2).astype(BF16)
    pet = jnp.transpose(cmp_pe, (0, 2, 1))
    ped = jnp.concatenate([pet, pet], -1)
    b1h = jnp.concatenate([cmp_b1, cmp_b1], -1)
    z2 = jnp.zeros_like(cmp_w2)
    w2h = jnp.concatenate([jnp.concatenate([cmp_w2, z2], -1), jnp.concatenate([z2, cmp_w2], -1)], axis=1).astype(BF16)
    return wd, ped, b1h, w2h


def _split_w_in(w):
    d = w.shape[0]
    off_b = DN_CONV_DIM + DN_WIDTH
    off_q = off_b + 2 * DN_HEADS
    off_kv = off_q + NSA_WIDTH
    off_g = off_kv + KV_COLS
    n_g = 3 * NSA_HEADS
    pad = jnp.zeros((d, P_COLS - P_SMALL - 2 * DN_HEADS - n_g), w.dtype)
    main = jnp.concatenate([w[:, :off_b], w[:, off_q:off_kv], w[:, off_kv:off_kv + CMP_COLS], w[:, off_b:off_q],
                            w[:, off_g:off_g + n_g], pad], axis=1)
    return main.astype(BF16), w[:, off_kv:off_g].T.astype(BF16)


def _row_tile(n, cap):
    t = min(n, cap)
    while n % t:
        t //= 2
    return t


def _trunk_tail(x2, mixer_dn, mixer_nsa, mem_kv3, bshape, lw, final_norm):
    n, d = x2.shape
    b, t = bshape
    sub = 8
    if t >= sub:
        x3 = mem_block(x2.reshape(b, t, d), mixer_dn.reshape(b, t, -1), mixer_nsa.reshape(b, t, -1),
                       lw["w_out_dn"], lw["w_out_nsa"], lw["ln_mem"], lw["w_mem_q"], mem_kv3, lw["w_mem_o"],
                       _row_tile(t, 512))
        x2 = x3.reshape(n, d)
    else:
        assert t == 1
        tm = _row_tile(n, 512)
        x2 = matmul_residual(x2, [mixer_dn, mixer_nsa], [lw["w_out_dn"], lw["w_out_nsa"]], tm)
        qm = rms_matmul(x2, lw["ln_mem"], lw["w_mem_q"], tm, 512)
        att = mem_attention(jnp.broadcast_to(qm[:, None], (b, sub, d)), mem_kv3, sub)[:, 0]
        x2 = matmul_residual(x2, [att], [lw["w_mem_o"]], tm)
    return ffn(x2, lw["ln_ffn"], lw["w_up"], lw["w_down"], lw["ln_final"], final_norm, _row_tile(n, 1024), 512)


def _kv_rows(kvt, lo, hi):
    b, _, s = kvt.shape
    g, dh = NSA_KV_HEADS, HEAD_DIM
    return jnp.transpose(kvt[:, lo * g * dh:hi * g * dh].reshape(b, hi - lo, g, dh, s), (0, 4, 1, 2, 3))


def _prompt_layer(xp, mem_prompt, lw, cw, final_norm):
    b, s, d = xp.shape
    n = b * s
    x2 = xp.reshape(n, d)
    proj, kvt = input_projection(xp, lw["ln_mix"], lw["w_in"], lw["w_kvt"], _row_tile(s, 512), 1024)
    proj3 = proj.reshape(b, s, P_COLS)
    dn_out, s_pairs = deltanet_prompt(proj3, lw["dn_conv_w"], lw["dn_a_log"], lw["dn_dt_bias"], lw["dn_norm"])
    p_conv = proj3[:, s - (CONV_WIDTH - 1):, P_QKV:P_QKV + DN_CONV_DIM]
    p_rec = _pairs_to_heads(s_pairs)
    nb = s // CMP_BLOCK
    cmp_rows = proj3[:, :nb * CMP_BLOCK, P_CMP:P_CMP + CMP_COLS]
    kcv = compress_blocks(cmp_rows.reshape(b * nb, CMP_BLOCK * CMP_COLS), CMP_COLS, cw, _row_tile(b * nb, 256))
    kcv = kcv.reshape(b, nb, 2, NSA_KV_HEADS, HEAD_DIM)
    kc = jnp.transpose(kcv[:, :, 0], (0, 2, 1, 3))
    vct = jnp.transpose(kcv[:, :, 1], (0, 2, 3, 1))
    ocmpt, selt = nsa_cmp_topk(proj3, kc, vct, _row_tile(s, 512))
    nsa_out = nsa_attention(proj3, kvt, selt, ocmpt, 256, 512).reshape(n, NSA_WIDTH)
    m = mem_prompt.shape[1]
    mem_kv = rms_matmul(mem_prompt.reshape(b * m, d), lw["ln_memkv"], lw["w_mem_kv"], _row_tile(b * m, 512), 512)
    mem_kv5 = mem_kv.reshape(b, m, 2, MEM_HEADS, d // MEM_HEADS)
    y = _trunk_tail(x2, dn_out.reshape(n, DN_WIDTH), nsa_out, mem_kv5, (b, s), lw, final_norm)
    wk = min(WINDOW, s)
    return y.reshape(b, s, d), _kv_rows(kvt, 0, 4), _kv_rows(kvt[:, :, s - wk:], 4, 6), mem_kv5, p_conv, p_rec


def _sample_layer(xs, cache_nsa, cache_win, cache_mem, conv_state, rec_state, page_table, lw, cwp, final_norm):
    db, ds, d = xs.shape
    assert ds == 1
    n = db
    x2 = xs.reshape(n, d)
    proj, kvt = input_projection(x2[None], lw["ln_mix"], lw["w_in"], lw["w_kvt"], n, 1024)
    dn_out, s_rec = deltanet_sample(proj, conv_state, rec_state, lw["dn_conv_w"], lw["dn_a_log"], lw["dn_dt_bias"],
                                    lw["dn_norm"])
    s_conv = jnp.concatenate([conv_state[:, 1:], proj[:, None, P_QKV:P_QKV + DN_CONV_DIM]], axis=1)
    n_phys = cache_nsa.shape[0]
    cache_t = jnp.transpose(cache_nsa, (0, 2, 3, 4, 1))
    win_t = jnp.transpose(cache_win, (0, 2, 3, 4, 1))
    kc_phys = compress_pages(cache_t, cwp, _row_tile(n_phys, 256))
    kv_new = kvt[0].T
    o8 = nsa_decode(page_table, proj[:, P_NQ:P_NQ + NSA_WIDTH].reshape(n, NSA_HEADS, HEAD_DIM),
                    kv_new.reshape(n, 1, KV_COLS),
                    _flat_gates(proj[:, P_SMALL:P_SMALL + LANES]).reshape(n, 1, LANES),
                    win_t, kc_phys, cache_t)
    kv6 = kv_new.reshape(n, 1, NSA_KV_KINDS, NSA_KV_HEADS, HEAD_DIM)
    s_nsa = jnp.transpose(kvt.reshape(NSA_KV_KINDS, NSA_KV_HEADS, HEAD_DIM, n)[:4], (3, 0, 1, 2))[:, None]
    s_win = jnp.concatenate([cache_win, kv6[:, :, 4:]], axis=1)[:, 1:]
    y = _trunk_tail(x2, dn_out, o8.reshape(n, NSA_WIDTH), cache_mem, (n, 1), lw, final_norm)
    return y.reshape(db, ds, d), s_nsa, s_win, s_conv, s_rec


def _flat_gates(small):
    g0 = 2 * DN_HEADS
    n_g = 3 * NSA_HEADS
    gl = small[..., g0:g0 + n_g]
    return jnp.concatenate([gl, jnp.zeros(gl.shape[:-1] + (LANES - n_g,), gl.dtype)], axis=-1)


def kernel(x_prompt, x_sample, mem_prompt, cache_nsa_kv, cache_win_kv, cache_mem_kv, state_dn_conv, state_dn_rec, page_table, ln_mix, w_in, dn_conv_w, dn_a_log, dn_dt_bias, dn_norm, cmp_pe, cmp_w1, cmp_b1, cmp_w2, w_out, ln_mem, ln_memkv, w_mem_q, w_mem_kv, w_mem_o, ln_ffn, w_up, w_down, ln_final):
    depth = w_in.shape[0]
    xp, xs = x_prompt, x_sample
    outs_p = [[] for _ in range(5)]
    outs_s = [[] for _ in range(4)]
    for l in range(depth):
        lw = {
            "ln_mix": ln_mix[l],
            "dn_conv_w": dn_conv_w[l], "dn_a_log": dn_a_log[l], "dn_dt_bias": dn_dt_bias[l], "dn_norm": dn_norm[l],
            "w_out_dn": w_out[l][:DN_WIDTH].astype(BF16), "w_out_nsa": w_out[l][DN_WIDTH:].astype(BF16),
            "ln_mem": ln_mem[l], "ln_memkv": ln_memkv[l], "w_mem_q": w_mem_q[l].astype(BF16),
            "w_mem_kv": w_mem_kv[l].astype(BF16), "w_mem_o": w_mem_o[l].astype(BF16),
            "ln_ffn": ln_ffn[l], "w_up": w_up[l].astype(BF16), "w_down": w_down[l].astype(BF16),
            "ln_final": ln_final,
        }
        lw["w_in"], lw["w_kvt"] = _split_w_in(w_in[l])
        cw = _compress_weights(cmp_pe[l], cmp_w1[l], cmp_b1[l], cmp_w2[l])
        last = l == depth - 1
        xp, p_nsa, p_win, p_mem, p_conv, p_rec = _prompt_layer(xp, mem_prompt, lw, cw, last)
        for acc, val in zip(outs_p, (p_nsa, p_win, p_mem, p_conv, p_rec)):
            acc.append(val)
        xs, s_nsa, s_win, s_conv, s_rec = _sample_layer(
            xs, cache_nsa_kv[l], cache_win_kv[l], cache_mem_kv[l], state_dn_conv[l], state_dn_rec[l],
            page_table, lw, _compress_page_weights(cmp_pe[l], cmp_w1[l], cmp_b1[l], cmp_w2[l]), last)
        for acc, val in zip(outs_s, (s_nsa, s_win, s_conv, s_rec)):
            acc.append(val)
    return (xp, xs) + tuple(jnp.stack(a) for a in outs_p) + tuple(jnp.stack(a) for a in outs_s)
```

```python
import functools

import jax
import jax.numpy as jnp
from jax import lax
from jax.experimental import pallas as pl
from jax.experimental.pallas import tpu as pltpu

F32 = jnp.float32
BF16 = jnp.bfloat16

HEAD_DIM = 64
DN_HEADS = 8
NSA_HEADS = 8
NSA_KV_HEADS = 2
NSA_GROUP = NSA_HEADS // NSA_KV_HEADS
DN_WIDTH = DN_HEADS * HEAD_DIM
NSA_WIDTH = NSA_HEADS * HEAD_DIM
CONV_WIDTH = 4
DN_CONV_DIM = 3 * DN_WIDTH
DN_CHUNK = 64
CMP_BLOCK = 64
SEL_BLOCK = 64
TOP_N = 16
N_LOCAL_BLOCKS = 2
WINDOW = 512
CMP_HIDDEN = 128
NSA_KV_KINDS = 6
MEM_HEADS = 4
PAGE_SIZE = 128
RMS_EPS = 1e-6
FORCE_SCORE = 1e3
NEG_INF = -1e30
ATTN_SCALE = HEAD_DIM ** -0.5

LANES = 128
P_QKV = 0
P_Z = P_QKV + DN_CONV_DIM
P_NQ = P_Z + DN_WIDTH
P_CMP = P_NQ + NSA_WIDTH
CMP_COLS = 2 * NSA_KV_HEADS * HEAD_DIM
P_SMALL = P_CMP + CMP_COLS
P_COLS = P_SMALL + 2 * LANES
KV_COLS = NSA_KV_KINDS * NSA_KV_HEADS * HEAD_DIM
VMEM_LIMIT = 56 * 1024 * 1024
DN_APPLY_PASSES = (3, 3, 3, 1, 1, 1)
DN_SQUARE_PASSES = (3, 3, 1, 1, 1)


def _cp(sem, vmem=VMEM_LIMIT):
    return pltpu.CompilerParams(dimension_semantics=sem, vmem_limit_bytes=vmem)


def _split2(a):
    hi = a.astype(BF16)
    return hi, (a - hi.astype(F32)).astype(BF16)


def _dot(a, b, passes=1):
    if passes == 1:
        return jnp.dot(a.astype(BF16), b.astype(BF16), preferred_element_type=F32)
    ah, al = _split2(a)
    bh, bl = _split2(b)
    return (jnp.dot(ah, bh, preferred_element_type=F32) + jnp.dot(ah, bl, preferred_element_type=F32)
            + jnp.dot(al, bh, preferred_element_type=F32))


def _dot_nt(a, b):
    return lax.dot_general(a.astype(BF16), b.astype(BF16), (((1,), (1,)), ((), ())),
                           preferred_element_type=F32)


def _split3(a):
    hi = a.astype(BF16)
    r1 = a - hi.astype(F32)
    mid = r1.astype(BF16)
    lo = (r1 - mid.astype(F32)).astype(BF16)
    return hi, mid, lo


def _dot_exact_lhs01(a01, b):
    a = a01.astype(BF16)
    hi, mid, lo = _split3(b)
    return (jnp.dot(a, hi, preferred_element_type=F32) + jnp.dot(a, mid, preferred_element_type=F32)
            + jnp.dot(a, lo, preferred_element_type=F32))


def _dot_exact_rhs01(a, b01):
    b = b01.astype(BF16)
    hi, mid, lo = _split3(a)
    return (jnp.dot(hi, b, preferred_element_type=F32) + jnp.dot(mid, b, preferred_element_type=F32)
            + jnp.dot(lo, b, preferred_element_type=F32))


def _rms(x, gain):
    ms = jnp.mean(x * x, axis=-1, keepdims=True)
    return x * lax.rsqrt(ms + RMS_EPS) * gain


def _softplus(x):
    return jnp.maximum(x, 0.0) + jnp.log1p(jnp.exp(-jnp.abs(x)))


def _silu(x):
    return x * jax.nn.sigmoid(x)


def _rms_mm_kernel(x_ref, g_ref, w_ref, o_ref, h_ref):
    @pl.when(pl.program_id(1) == 0)
    def _():
        h_ref[...] = _rms(x_ref[...], g_ref[...]).astype(BF16)

    o_ref[...] = jnp.dot(h_ref[...], w_ref[...], preferred_element_type=F32)


def rms_matmul(x, gain, w_bf16, tm, tn):
    n, d = x.shape
    m = w_bf16.shape[1]
    assert n % tm == 0 and m % tn == 0
    return pl.pallas_call(
        _rms_mm_kernel,
        grid=(n // tm, m // tn),
        in_specs=[pl.BlockSpec((tm, d), lambda i, j: (i, 0)),
                  pl.BlockSpec((1, d), lambda i, j: (0, 0)),
                  pl.BlockSpec((d, tn), lambda i, j: (0, j))],
        out_specs=pl.BlockSpec((tm, tn), lambda i, j: (i, j)),
        out_shape=jax.ShapeDtypeStruct((n, m), F32),
        scratch_shapes=[pltpu.VMEM((tm, d), BF16)],
        compiler_params=_cp(("parallel", "arbitrary")),
        name="rms_matmul",
    )(x, gain.reshape(1, d), w_bf16)


def _proj_kernel(x_ref, g_ref, w_ref, wkv_ref, o_ref, okv_ref, h_ref):
    @pl.when(pl.program_id(1) == 0)
    def _():
        h = _rms(x_ref[...], g_ref[...]).astype(BF16)
        h_ref[...] = h
        okv_ref[0] = lax.dot_general(wkv_ref[...], h, (((1,), (1,)), ((), ())), preferred_element_type=F32)

    o_ref[...] = jnp.dot(h_ref[...], w_ref[...], preferred_element_type=F32)


def input_projection(x3, gain, w_main, w_kvt, tm, tn):
    b, s, d = x3.shape
    n = b * s
    m = w_main.shape[1]
    kvc = w_kvt.shape[0]
    assert s % tm == 0 and m % tn == 0
    spt = s // tm
    return pl.pallas_call(
        _proj_kernel,
        grid=(n // tm, m // tn),
        in_specs=[pl.BlockSpec((tm, d), lambda i, j: (i, 0)),
                  pl.BlockSpec((1, d), lambda i, j: (0, 0)),
                  pl.BlockSpec((d, tn), lambda i, j: (0, j)),
                  pl.BlockSpec((kvc, d), lambda i, j: (0, 0))],
        out_specs=[pl.BlockSpec((tm, tn), lambda i, j: (i, j)),
                   pl.BlockSpec((1, kvc, tm), lambda i, j: (i // spt, 0, i % spt))],
        out_shape=[jax.ShapeDtypeStruct((n, m), F32), jax.ShapeDtypeStruct((b, kvc, s), F32)],
        scratch_shapes=[pltpu.VMEM((tm, d), BF16)],
        compiler_params=_cp(("parallel", "arbitrary")),
        name="input_projection",
    )(x3.reshape(n, d), gain.reshape(1, d), w_main, w_kvt)


def _mm_res_kernel(*refs, n_in):
    res_ref = refs[0]
    a_refs = refs[1:1 + n_in]
    w_refs = refs[1 + n_in:1 + 2 * n_in]
    o_ref = refs[1 + 2 * n_in]
    acc = res_ref[...]
    for a_ref, w_ref in zip(a_refs, w_refs):
        acc = acc + jnp.dot(a_ref[...].astype(BF16), w_ref[...], preferred_element_type=F32)
    o_ref[...] = acc


def matmul_residual(res, a_list, w_list, tm):
    n, d = res.shape
    assert n % tm == 0
    n_in = len(a_list)
    in_specs = [pl.BlockSpec((tm, d), lambda i: (i, 0))]
    in_specs += [pl.BlockSpec((tm, a.shape[1]), lambda i: (i, 0)) for a in a_list]
    in_specs += [pl.BlockSpec(w.shape, lambda i: (0, 0)) for w in w_list]
    return pl.pallas_call(
        functools.partial(_mm_res_kernel, n_in=n_in),
        grid=(n // tm,),
        in_specs=in_specs,
        out_specs=pl.BlockSpec((tm, d), lambda i: (i, 0)),
        out_shape=jax.ShapeDtypeStruct((n, d), F32),
        compiler_params=_cp(("parallel",)),
        name="matmul_residual",
    )(res, *a_list, *w_list)


def _ffn_kernel(x_ref, g_ref, wu_ref, wd_ref, gf_ref, o_ref, hn_ref, acc_ref, *, final_norm):
    j = pl.program_id(1)

    @pl.when(j == 0)
    def _():
        x = x_ref[...]
        hn_ref[...] = _rms(x, g_ref[...]).astype(BF16)
        acc_ref[...] = x

    u = jnp.dot(hn_ref[...], wu_ref[...], preferred_element_type=F32)
    u = jnp.square(jnp.maximum(u, 0.0)).astype(BF16)
    acc_ref[...] += jnp.dot(u, wd_ref[...], preferred_element_type=F32)

    @pl.when(j == pl.num_programs(1) - 1)
    def _():
        y = acc_ref[...]
        if final_norm:
            y = _rms(y, gf_ref[...])
        o_ref[...] = y


def ffn(x, gain, wu_bf16, wd_bf16, gain_final, final_norm, tm, tf):
    n, d = x.shape
    f = wu_bf16.shape[1]
    assert n % tm == 0 and f % tf == 0
    return pl.pallas_call(
        functools.partial(_ffn_kernel, final_norm=final_norm),
        grid=(n // tm, f // tf),
        in_specs=[pl.BlockSpec((tm, d), lambda i, j: (i, 0)),
                  pl.BlockSpec((1, d), lambda i, j: (0, 0)),
                  pl.BlockSpec((d, tf), lambda i, j: (0, j)),
                  pl.BlockSpec((tf, d), lambda i, j: (j, 0)),
                  pl.BlockSpec((1, d), lambda i, j: (0, 0))],
        out_specs=pl.BlockSpec((tm, d), lambda i, j: (i, 0)),
        out_shape=jax.ShapeDtypeStruct((n, d), F32),
        scratch_shapes=[pltpu.VMEM((tm, d), BF16), pltpu.VMEM((tm, d), F32)],
        compiler_params=_cp(("parallel", "arbitrary")),
        name="ffn",
    )(x, gain.reshape(1, d), wu_bf16, wd_bf16, gain_final.reshape(1, d))


def _mem_attn_row_kernel(q_ref, kv_ref, o_ref, *, hd):
    scale = hd ** -0.5
    q = q_ref[0]
    s = jnp.sum(kv_ref[0, :, 0] * q[None], axis=-1) * scale
    p = jnp.exp(s - jnp.max(s, axis=0, keepdims=True))
    p = p / jnp.sum(p, axis=0, keepdims=True)
    o_ref[0] = jnp.sum(p[:, :, None] * kv_ref[0, :, 1], axis=0)


def mem_attention_row(q, kv):
    b, h, hd = q.shape
    m = kv.shape[1]
    assert kv.shape[2:] == (2, h, hd)
    return pl.pallas_call(
        functools.partial(_mem_attn_row_kernel, hd=hd),
        grid=(b,),
        in_specs=[pl.BlockSpec((1, h, hd), lambda i: (i, 0, 0)),
                  pl.BlockSpec((1, m, 2, h, hd), lambda i: (i, 0, 0, 0, 0))],
        out_specs=pl.BlockSpec((1, h, hd), lambda i: (i, 0, 0)),
        out_shape=jax.ShapeDtypeStruct((b, h, hd), F32),
        compiler_params=_cp(("parallel",)),
        name="mem_attention_row",
    )(q, kv)


def _mem_block_kernel(x_ref, a1_ref, a2_ref, w1_ref, w2_ref, g_ref, wq_ref, kv_ref, wo_ref, o_ref, *, heads, hd):
    scale = hd ** -0.5
    x = (x_ref[0] + jnp.dot(a1_ref[0].astype(BF16), w1_ref[...], preferred_element_type=F32)
         + jnp.dot(a2_ref[0].astype(BF16), w2_ref[...], preferred_element_type=F32))
    q = jnp.dot(_rms(x, g_ref[...]).astype(BF16), wq_ref[...], preferred_element_type=F32)
    kt = jnp.swapaxes(kv_ref[0, :, 0], 0, 1)
    vt = jnp.swapaxes(kv_ref[0, :, 1], 0, 1)
    outs = []
    for h in range(heads):
        s = _dot_nt(q[:, h * hd:(h + 1) * hd], kt[h]) * scale
        p = jnp.exp(s - jnp.max(s, axis=-1, keepdims=True))
        p = p / jnp.sum(p, axis=-1, keepdims=True)
        outs.append(_dot(p, vt[h]))
    att = jnp.concatenate(outs, axis=1).astype(BF16)
    o_ref[0] = x + jnp.dot(att, wo_ref[...], preferred_element_type=F32)


def mem_block(x3, a1, a2, w1, w2, gain, wq, kv, wo, tq):
    b, t, d = x3.shape
    m = kv.shape[1]
    hd = d // MEM_HEADS
    assert t % tq == 0 and kv.shape[2:] == (2, MEM_HEADS, hd)
    return pl.pallas_call(
        functools.partial(_mem_block_kernel, heads=MEM_HEADS, hd=hd),
        grid=(b, t // tq),
        in_specs=[pl.BlockSpec((1, tq, d), lambda i, j: (i, j, 0)),
                  pl.BlockSpec((1, tq, a1.shape[2]), lambda i, j: (i, j, 0)),
                  pl.BlockSpec((1, tq, a2.shape[2]), lambda i, j: (i, j, 0)),
                  pl.BlockSpec(w1.shape, lambda i, j: (0, 0)),
                  pl.BlockSpec(w2.shape, lambda i, j: (0, 0)),
                  pl.BlockSpec((1, d), lambda i, j: (0, 0)),
                  pl.BlockSpec((d, d), lambda i, j: (0, 0)),
                  pl.BlockSpec((1, m, 2, MEM_HEADS, hd), lambda i, j: (i, 0, 0, 0, 0)),
                  pl.BlockSpec((d, d), lambda i, j: (0, 0))],
        out_specs=pl.BlockSpec((1, tq, d), lambda i, j: (i, j, 0)),
        out_shape=jax.ShapeDtypeStruct((b, t, d), F32),
        compiler_params=_cp(("parallel", "parallel")),
        name="mem_block",
    )(x3, a1, a2, w1, w2, gain.reshape(1, d), wq, kv, wo)


def _compress_kernel(x_ref, w1_ref, pe_ref, b1_ref, w2_ref, o_ref, acc_ref, *, row_cols, tt):
    j = pl.program_id(1)

    @pl.when(j == 0)
    def _():
        acc_ref[...] = jnp.zeros_like(acc_ref)

    hid2 = 2 * CMP_HIDDEN
    for kind in range(2):
        part = None
        for t in range(tt):
            c0 = t * row_cols + kind * LANES
            xs = x_ref[:, c0:c0 + LANES] + pe_ref[kind, t:t + 1, :]
            d = jnp.dot(xs.astype(BF16), w1_ref[kind, t], preferred_element_type=F32)
            part = d if part is None else part + d
        acc_ref[:, kind * hid2:(kind + 1) * hid2] += part

    @pl.when(j == pl.num_programs(1) - 1)
    def _():
        h = jnp.maximum(acc_ref[...] + b1_ref[...], 0.0)
        for kind in range(2):
            o_ref[:, kind * LANES:(kind + 1) * LANES] = jnp.dot(
                h[:, kind * hid2:(kind + 1) * hid2].astype(BF16), w2_ref[kind], preferred_element_type=F32)


def compress_blocks(x2, row_cols, cw, bt, tt=8):
    nb = x2.shape[0]
    assert nb % bt == 0 and CMP_BLOCK % tt == 0
    w1bd, pe2, b1bd, w2bd = cw
    return pl.pallas_call(
        functools.partial(_compress_kernel, row_cols=row_cols, tt=tt),
        grid=(nb // bt, CMP_BLOCK // tt),
        in_specs=[pl.BlockSpec((bt, tt * row_cols), lambda i, j: (i, j)),
                  pl.BlockSpec((2, tt, LANES, 2 * CMP_HIDDEN), lambda i, j: (0, j, 0, 0)),
                  pl.BlockSpec((2, tt, LANES), lambda i, j: (0, j, 0)),
                  pl.BlockSpec((1, 4 * CMP_HIDDEN), lambda i, j: (0, 0)),
                  pl.BlockSpec((2, 2 * CMP_HIDDEN, LANES), lambda i, j: (0, 0, 0))],
        out_specs=pl.BlockSpec((bt, 2 * LANES), lambda i, j: (i, 0)),
        out_shape=jax.ShapeDtypeStruct((nb, 2 * LANES), F32),
        scratch_shapes=[pltpu.VMEM((bt, 4 * CMP_HIDDEN), F32)],
        compiler_params=_cp(("parallel", "arbitrary")),
        name="compress_blocks",
    )(x2, w1bd, pe2, b1bd, w2bd)


def _compress_weights(cmp_pe, cmp_w1, cmp_b1, cmp_w2):
    w1r = cmp_w1.reshape(2, CMP_BLOCK, HEAD_DIM, CMP_HIDDEN)
    z = jnp.zeros_like(w1r)
    w1bd = jnp.concatenate([jnp.concatenate([w1r, z], -1), jnp.concatenate([z, w1r], -1)], axis=2).astype(BF16)
    pe2 = jnp.concatenate([cmp_pe, cmp_pe], -1)
    b1bd = jnp.concatenate([cmp_b1[0], cmp_b1[0], cmp_b1[1], cmp_b1[1]]).reshape(1, 4 * CMP_HIDDEN)
    z2 = jnp.zeros_like(cmp_w2)
    w2bd = jnp.concatenate([jnp.concatenate([cmp_w2, z2], -1), jnp.concatenate([z2, cmp_w2], -1)], axis=1).astype(BF16)
    return w1bd, pe2, b1bd, w2bd


def _masked_softmax(s, mask):
    s = jnp.where(mask, s, NEG_INF)
    p = jnp.where(mask, jnp.exp(s - jnp.max(s, axis=-1, keepdims=True)), 0.0)
    return p / jnp.maximum(jnp.sum(p, axis=-1, keepdims=True), 1e-30)


def _heads_as_rows(q_ref, tq):
    qb = q_ref[0] * ATTN_SCALE
    return jnp.concatenate([qb[:, h * HEAD_DIM:(h + 1) * HEAD_DIM] for h in range(NSA_GROUP)], axis=0).astype(BF16)


def _cmp_topk_kernel(q_ref, kc_ref, vct_ref, ocmp_ref, sel_ref, *, tq, nblk, topn):
    qi = pl.program_id(2)
    r, dh = NSA_GROUP, HEAD_DIM
    qpos = qi * tq + lax.broadcasted_iota(jnp.int32, (nblk, tq), 1)
    blk = lax.broadcasted_iota(jnp.int32, (nblk, tq), 0)
    vis = (blk + 1) * CMP_BLOCK - 1 <= qpos
    q = _heads_as_rows(q_ref, tq)
    kc = kc_ref[0, 0]
    vct = vct_ref[0, 0]
    imp = jnp.zeros((nblk, tq), F32)
    for h in range(r):
        s = jnp.where(vis, _dot_nt(kc, q[h * tq:(h + 1) * tq]), NEG_INF)
        p = jnp.where(vis, jnp.exp(s - jnp.max(s, axis=0, keepdims=True)), 0.0)
        p = p / jnp.maximum(jnp.sum(p, axis=0, keepdims=True), 1e-30)
        ocmp_ref[0, 0, h * dh:(h + 1) * dh, :] = _dot(vct, p)
        imp = imp + p
    cur = lax.shift_right_logical(qpos, 6)
    valid = blk <= cur
    forced = valid & ((blk == 0) | (cur - blk < N_LOCAL_BLOCKS))
    score = jnp.where(valid, imp + jnp.where(forced, FORCE_SCORE, 0.0), -1.0)
    rank = jnp.zeros((nblk, tq), F32)
    for i in range(nblk):
        si = score[i:i + 1, :]
        rank = rank + jnp.where((si > score) | ((si == score) & (blk > i)), 1.0, 0.0)
    sel_ref[0, 0] = jnp.where(rank < topn, 1.0, 0.0).astype(BF16)


def nsa_cmp_topk(proj3, kc, vct, tq):
    b, sq, _ = proj3.shape
    g = kc.shape[1]
    nblk = kc.shape[2]
    dh = HEAD_DIM
    gw = NSA_GROUP * dh
    assert sq % tq == 0 and SEL_BLOCK == 64 and P_NQ % gw == 0
    return pl.pallas_call(
        functools.partial(_cmp_topk_kernel, tq=tq, nblk=nblk, topn=min(TOP_N, nblk)),
        grid=(b, g, sq // tq),
        in_specs=[pl.BlockSpec((1, tq, gw), lambda i, j, k: (i, k, P_NQ // gw + j)),
                  pl.BlockSpec((1, 1, nblk, dh), lambda i, j, k: (i, j, 0, 0)),
                  pl.BlockSpec((1, 1, dh, nblk), lambda i, j, k: (i, j, 0, 0))],
        out_specs=[pl.BlockSpec((1, 1, gw, tq), lambda i, j, k: (i, j, 0, k)),
                   pl.BlockSpec((1, 1, nblk, tq), lambda i, j, k: (i, j, 0, k))],
        out_shape=[jax.ShapeDtypeStruct((b, g, gw, sq), F32),
                   jax.ShapeDtypeStruct((b, g, nblk, sq), BF16)],
        compiler_params=_cp(("parallel", "parallel", "parallel")),
        name="nsa_cmp_topk",
    )(proj3, kc, vct)


def _nsa_attn_kernel(q_ref, kst_ref, vst_ref, kwt_ref, vwt_ref, selt_ref, ocmp_ref, sm_ref, o_ref,
                     m_sc, acc_sc, *, tq, tk, nsel):
    grp = pl.program_id(1)
    qi = pl.program_id(2)
    r = NSA_GROUP
    dh = HEAD_DIM
    q = _heads_as_rows(q_ref, tq)
    qpos = qi * tq + lax.broadcasted_iota(jnp.int32, (1, tq), 1)
    ones_row = jnp.where(lax.broadcasted_iota(jnp.int32, (8, 1), 0) == 0, 1.0, 0.0)

    def reset():
        m_sc[...] = jnp.full(m_sc.shape, NEG_INF, F32)
        acc_sc[...] = jnp.zeros(acc_sc.shape, F32)

    def with_ones(vt):
        return jnp.concatenate([vt, jnp.broadcast_to(ones_row, (8, vt.shape[1]))], axis=0).astype(BF16)

    def step(kt, vt, mask):
        bias = jnp.where(mask, 0.0, NEG_INF)
        s = _dot_nt(kt.T, q) + jnp.concatenate([bias] * r, axis=1)
        m_prev = m_sc[...]
        m_new = jnp.maximum(m_prev, jnp.max(s, axis=0, keepdims=True))
        p = jnp.exp(s - m_new).astype(BF16)
        acc_sc[...] = jnp.exp(m_prev - m_new) * acc_sc[...] + jnp.dot(with_ones(vt), p, preferred_element_type=F32)
        m_sc[...] = m_new

    def result():
        acc = acc_sc[...]
        return acc[0:dh] / jnp.maximum(acc[dh:dh + 1], 1e-30)

    reset()
    selt = selt_ref[0, 0]
    n_kt = ((qi + 1) * tq + tk - 1) // tk

    def slc_body(kt, carry):
        k0 = pl.multiple_of(kt * tk, tk)
        kpos = k0 + lax.broadcasted_iota(jnp.int32, (tk, 1), 0)
        erow = k0 + lax.broadcasted_iota(jnp.int32, (tk, nsel), 0)
        eblk = lax.broadcasted_iota(jnp.int32, (tk, nsel), 1)
        expand = jnp.where(lax.shift_right_logical(erow, 6) == eblk, 1.0, 0.0).astype(BF16)
        chosen = jnp.dot(expand, selt, preferred_element_type=F32) > 0.5
        step(kst_ref[0, :, pl.ds(k0, tk)], vst_ref[0, :, pl.ds(k0, tk)], chosen & (kpos <= qpos))
        return carry

    lax.fori_loop(0, n_kt, slc_body, 0)
    o_slc = result()

    reset()
    wk = WINDOW + tq
    w0 = pl.multiple_of(jnp.maximum(qi - WINDOW // tq, 0) * tq, tq)
    dpos = qpos - (w0 + lax.broadcasted_iota(jnp.int32, (wk, 1), 0))
    step(kwt_ref[0, :, pl.ds(w0, wk)], vwt_ref[0, :, pl.ds(w0, wk)], (dpos >= 0) & (dpos < WINDOW))
    o_win = result()

    gates_t = jax.nn.sigmoid(sm_ref[0]).T
    g0 = 2 * DN_HEADS
    per = 3 * r
    gt = jnp.where(grp == 0, gates_t[g0:g0 + per], gates_t[g0 + per:g0 + 2 * per])
    outs = []
    for h in range(r):
        outs.append(gt[3 * h:3 * h + 1] * ocmp_ref[0, 0, h * dh:(h + 1) * dh, :]
                    + gt[3 * h + 1:3 * h + 2] * o_slc[:, h * tq:(h + 1) * tq]
                    + gt[3 * h + 2:3 * h + 3] * o_win[:, h * tq:(h + 1) * tq])
    o_ref[0] = jnp.concatenate(outs, axis=0).T


def nsa_attention(proj3, kvt, selt, ocmpt, tq, tk):
    b, sq, _ = proj3.shape
    t = kvt.shape[2]
    g, nsel = selt.shape[1], selt.shape[2]
    r, dh = NSA_GROUP, HEAD_DIM
    gw = r * dh
    assert sq == t and sq % tq == 0 and t % tk == 0 and tk % tq == 0 and WINDOW % tq == 0 and tq % LANES == 0
    assert t >= WINDOW + tq and g == NSA_KV_HEADS and g == 2

    def kv_spec(kind):
        return pl.BlockSpec((1, dh, t), lambda i, j, k: (i, kind * g + j, 0))

    return pl.pallas_call(
        functools.partial(_nsa_attn_kernel, tq=tq, tk=tk, nsel=nsel),
        grid=(b, g, sq // tq),
        in_specs=[pl.BlockSpec((1, tq, gw), lambda i, j, k: (i, k, P_NQ // gw + j)),
                  kv_spec(2), kv_spec(3), kv_spec(4), kv_spec(5),
                  pl.BlockSpec((1, 1, nsel, tq), lambda i, j, k: (i, j, 0, k)),
                  pl.BlockSpec((1, 1, gw, tq), lambda i, j, k: (i, j, 0, k)),
                  pl.BlockSpec((1, tq, LANES), lambda i, j, k: (i, k, P_SMALL // LANES))],
        out_specs=pl.BlockSpec((1, tq, gw), lambda i, j, k: (i, k, j)),
        out_shape=jax.ShapeDtypeStruct((b, sq, g * gw), F32),
        scratch_shapes=[pltpu.VMEM((1, r * tq), F32), pltpu.VMEM((dh + 8, r * tq), F32)],
        compiler_params=_cp(("parallel", "parallel", "parallel")),
        name="nsa_attention",
    )(proj3, kvt, kvt, kvt, kvt, selt, ocmpt, proj3)


def _bdot(a, b, passes=1):
    dims = (((2,), (1,)), ((0,), (0,)))
    if passes == 1:
        return lax.dot_general(a.astype(BF16), b.astype(BF16), dims, preferred_element_type=F32)
    ah, al = _split2(a)
    bh, bl = _split2(b)
    return (lax.dot_general(ah, bh, dims, preferred_element_type=F32)
            + lax.dot_general(ah, bl, dims, preferred_element_type=F32)
            + lax.dot_general(al, bh, dims, preferred_element_type=F32))


def _bdot_nt(a, b):
    return lax.dot_general(a.astype(BF16), b.astype(BF16), (((2,), (2,)), ((0,), (0,))),
                           preferred_element_type=F32)


def _deltanet_kernel(qkv_ref, z_ref, sm_ref, cw_ref, alog_ref, dtb_ref, gn_ref, o_ref, s_out_ref,
                     xbuf, s_sc):
    c = pl.program_id(0)
    ch = DN_CHUNK
    n_pairs = DN_HEADS // 2
    two = 2 * ch
    n_batch = qkv_ref.shape[0]

    @pl.when(c == 0)
    def _():
        xbuf[:, 0:8, :] = jnp.zeros((n_batch, 8, DN_CONV_DIM), F32)
        s_sc[...] = jnp.zeros_like(s_sc)

    ti = lax.broadcasted_iota(jnp.int32, (ch, ch), 0)
    tj = lax.broadcasted_iota(jnp.int32, (ch, ch), 1)
    tri = jnp.where(ti >= tj, 1.0, 0.0)
    lane = lax.broadcasted_iota(jnp.int32, (ch, LANES), 1)
    lo = lane < HEAD_DIM
    row2 = lax.broadcasted_iota(jnp.int32, (two, two), 0)
    col2 = lax.broadcasted_iota(jnp.int32, (two, two), 1)
    same = (row2 >= ch) == (col2 >= ch)
    incl = (same & (row2 >= col2))[None]
    strict = (same & (row2 > col2))[None]
    top = lax.broadcasted_iota(jnp.int32, (two, 1), 0) < ch

    def seg_sum(x):
        s_lo = jnp.sum(jnp.where(lo, x, 0.0), axis=-1, keepdims=True)
        s_hi = jnp.sum(jnp.where(lo, 0.0, x), axis=-1, keepdims=True)
        return jnp.where(lo, s_lo, s_hi)

    def stack2(x):
        return jnp.concatenate([jnp.where(lo, x, 0.0), jnp.where(lo, 0.0, x)], axis=0)

    def col2x(a, b):
        return jnp.concatenate([jnp.broadcast_to(a, (ch, LANES)), jnp.broadcast_to(b, (ch, LANES))], axis=0)

    q_l, k_l, v_l, beta_l, gc_l, gl_l = [], [], [], [], [], []
    for bi in range(n_batch):
        xbuf[bi, 8:8 + ch, :] = qkv_ref[bi]
        conv = None
        for w in range(CONV_WIDTH):
            term = xbuf[bi, 5 + w:5 + w + ch, :] * cw_ref[w:w + 1, :]
            conv = term if conv is None else conv + term
        xbuf[bi, 0:8, :] = xbuf[bi, ch:ch + 8, :]
        act = _silu(conv)
        sm = sm_ref[bi]
        beta_all = jax.nn.sigmoid(sm)
        g_all = -jnp.exp(alog_ref[...]) * _softplus(sm + dtb_ref[...])
        gcum_all = _dot_exact_lhs01(tri, g_all)
        for p in range(n_pairs):
            c0 = p * LANES
            qp = act[:, c0:c0 + LANES]
            kp = act[:, DN_WIDTH + c0:DN_WIDTH + c0 + LANES]
            vp = act[:, 2 * DN_WIDTH + c0:2 * DN_WIDTH + c0 + LANES]
            qp = qp * lax.rsqrt(seg_sum(qp * qp) + 1e-6) * (HEAD_DIM ** -0.5)
            kp = kp * lax.rsqrt(seg_sum(kp * kp) + 1e-6)
            h0, h1 = DN_HEADS + 2 * p, DN_HEADS + 2 * p + 1
            q_l.append(stack2(qp))
            k_l.append(stack2(kp))
            v_l.append(stack2(vp))
            beta_l.append(col2x(beta_all[:, 2 * p:2 * p + 1], beta_all[:, 2 * p + 1:2 * p + 2]))
            gc_l.append(col2x(gcum_all[:, h0:h0 + 1], gcum_all[:, h1:h1 + 1]))
            gl_l.append(jnp.broadcast_to(jnp.where(top, gcum_all[ch - 1:ch, h0:h0 + 1], gcum_all[ch - 1:ch, h1:h1 + 1]),
                                         (two, LANES)))
    q2, k2, v2 = jnp.stack(q_l), jnp.stack(k_l), jnp.stack(v_l)
    beta2, gc2, gl2 = jnp.stack(beta_l), jnp.stack(gc_l), jnp.stack(gl_l)
    decay = jnp.exp(jnp.where(incl, gc2 - jnp.swapaxes(gc2, 1, 2), NEG_INF))
    kb2 = k2 * beta2
    a_mat = jnp.where(strict, _bdot_nt(kb2, k2) * decay, 0.0)
    aqk = jnp.where(incl, _bdot_nt(q2, k2) * decay, 0.0)
    s_old = s_sc[...]
    egc = jnp.exp(gc2)
    x = beta2 * (v2 - egc * _bdot(k2, s_old))
    pw = -a_mat
    n_lvl = ch.bit_length() - 1
    for lvl in range(n_lvl):
        x = x + _bdot(pw, x, DN_APPLY_PASSES[lvl])
        if lvl + 1 < n_lvl:
            pw = _bdot(pw, pw, DN_SQUARE_PASSES[lvl])
    o2 = _bdot(q2 * egc, s_old) + _bdot(aqk, x)
    kdec = k2 * jnp.exp(gl2 - gc2)
    s_sc[...] = s_old * jnp.exp(gl2) + _bdot(jnp.swapaxes(kdec, 1, 2), x)
    for bi in range(n_batch):
        for p in range(n_pairs):
            c0 = p * LANES
            o_n = o2[bi * n_pairs + p]
            o_pair = o_n[0:ch] + o_n[ch:two]
            inv = lax.rsqrt(seg_sum(o_pair * o_pair) * (1.0 / HEAD_DIM) + RMS_EPS)
            o_ref[bi, :, c0:c0 + LANES] = o_pair * inv * gn_ref[...] * _silu(z_ref[bi, :, c0:c0 + LANES])

    @pl.when(c == pl.num_programs(0) - 1)
    def _():
        s_out_ref[...] = s_sc[...]


def deltanet_prompt(proj3, conv_w, a_log, dt_bias, norm_gain):
    b, t, _ = proj3.shape
    ch = DN_CHUNK
    assert t % ch == 0
    pad = jnp.zeros((LANES - 2 * DN_HEADS,), F32)
    alog_row = jnp.concatenate([jnp.zeros((DN_HEADS,), F32), a_log, pad]).reshape(1, LANES)
    dtb_row = jnp.concatenate([jnp.zeros((DN_HEADS,), F32), dt_bias, pad]).reshape(1, LANES)
    gn_row = jnp.concatenate([norm_gain, norm_gain]).reshape(1, LANES)
    n_pairs = DN_HEADS // 2
    o, s_fin = pl.pallas_call(
        _deltanet_kernel,
        grid=(t // ch,),
        in_specs=[pl.BlockSpec((b, ch, DN_CONV_DIM), lambda j: (0, j, P_QKV // DN_CONV_DIM)),
                  pl.BlockSpec((b, ch, DN_WIDTH), lambda j: (0, j, P_Z // DN_WIDTH)),
                  pl.BlockSpec((b, ch, LANES), lambda j: (0, j, P_SMALL // LANES)),
                  pl.BlockSpec((CONV_WIDTH, DN_CONV_DIM), lambda j: (0, 0)),
                  pl.BlockSpec((1, LANES), lambda j: (0, 0)),
                  pl.BlockSpec((1, LANES), lambda j: (0, 0)),
                  pl.BlockSpec((1, LANES), lambda j: (0, 0))],
        out_specs=[pl.BlockSpec((b, ch, DN_WIDTH), lambda j: (0, j, 0)),
                   pl.BlockSpec((b * n_pairs, 2 * ch, LANES), lambda j: (0, 0, 0))],
        out_shape=[jax.ShapeDtypeStruct((b, t, DN_WIDTH), F32),
                   jax.ShapeDtypeStruct((b * n_pairs, 2 * ch, LANES), F32)],
        scratch_shapes=[pltpu.VMEM((b, ch + 8, DN_CONV_DIM), F32), pltpu.VMEM((b * n_pairs, 2 * ch, LANES), F32)],
        compiler_params=_cp(("arbitrary",)),
        name="deltanet_prompt",
    )(proj3, proj3, proj3, conv_w, alog_row, dtb_row, gn_row)
    return o, s_fin.reshape(b, n_pairs, 2 * ch, LANES)


def _pairs_to_heads(s_pairs):
    d = HEAD_DIM
    return jnp.stack([s_pairs[:, :, :d, :d], s_pairs[:, :, d:, d:]], axis=2).reshape(
        s_pairs.shape[0], DN_HEADS, d, d)


def _dn_step_prep_kernel(qkv_ref, cs_ref, sm_ref, cw_ref, alog_ref, dtb_ref, ones_ref,
                         q_ref, k_ref, v_ref, sc_ref):
    conv = qkv_ref[...] * cw_ref[CONV_WIDTH - 1:CONV_WIDTH, :]
    for w in range(CONV_WIDTH - 1):
        conv = conv + cs_ref[w] * cw_ref[w:w + 1, :]
    act = _silu(conv)
    q = act[:, 0:DN_WIDTH]
    k = act[:, DN_WIDTH:2 * DN_WIDTH]

    def seg_sum(x):
        return _dot_exact_rhs01(x, ones_ref[...])

    q_ref[...] = q * lax.rsqrt(seg_sum(q * q) + 1e-6) * (HEAD_DIM ** -0.5)
    k_ref[...] = k * lax.rsqrt(seg_sum(k * k) + 1e-6)
    v_ref[...] = act[:, 2 * DN_WIDTH:]
    sm = sm_ref[...]
    g = -jnp.exp(alog_ref[...]) * _softplus(sm + dtb_ref[...])
    lane = lax.broadcasted_iota(jnp.int32, sm.shape, 1)
    sc_ref[...] = jnp.where(lane < DN_HEADS, jax.nn.sigmoid(sm), jnp.exp(g))


def _dn_step_kernel(kt_ref, qt_ref, v_ref, sc_ref, z_ref, gn_ref, s_ref, o_ref, s_out_ref, *, nseq):
    hds, d = DN_HEADS, HEAD_DIM
    for j in range(nseq):
        s_old = s_ref[j]
        kcol = kt_ref[0, :, j:j + 1]
        qcol = qt_ref[0, :, j:j + 1]
        beta = sc_ref[j, :, 0:1]
        eg = sc_ref[j, :, 1:2]
        v = v_ref[j]
        ks = jnp.sum((kcol * s_old).reshape(hds, d, d), axis=1)
        qs = jnp.sum((qcol * s_old).reshape(hds, d, d), axis=1)
        qk = jnp.sum((qcol * kcol).reshape(hds, d, 1), axis=1)
        v_new = beta * (v - eg * ks)
        o = eg * qs + qk * v_new
        inv = lax.rsqrt(jnp.mean(o * o, axis=-1, keepdims=True) + RMS_EPS)
        o_ref[j] = o * inv * gn_ref[...] * _silu(z_ref[j])
        s3 = s_old.reshape(hds, d, d) * eg[:, :, None] + kcol.reshape(hds, d, 1) * v_new[:, None, :]
        s_out_ref[j] = s3.reshape(hds * d, d)


def deltanet_sample(proj_s, conv_state, rec_state, conv_w, a_log, dt_bias, norm_gain, nseq=8):
    n = proj_s.shape[0]
    assert n % nseq == 0
    hds, d = DN_HEADS, HEAD_DIM
    pad = jnp.zeros((LANES - 2 * hds,), F32)
    alog_row = jnp.concatenate([jnp.zeros((hds,), F32), a_log, pad]).reshape(1, LANES)
    dtb_row = jnp.concatenate([jnp.zeros((hds,), F32), dt_bias, pad]).reshape(1, LANES)
    head_of = jnp.arange(DN_WIDTH) // d
    ones_bd = (head_of[:, None] == head_of[None, :]).astype(BF16)
    cs = jnp.transpose(conv_state, (1, 0, 2))
    full = lambda shape: pl.BlockSpec(shape, lambda i: (0,) * len(shape))
    q, k, v, sc = pl.pallas_call(
        _dn_step_prep_kernel,
        grid=(1,),
        in_specs=[pl.BlockSpec((n, DN_CONV_DIM), lambda i: (0, P_QKV // DN_CONV_DIM)),
                  full((CONV_WIDTH - 1, n, DN_CONV_DIM)),
                  pl.BlockSpec((n, LANES), lambda i: (0, P_SMALL // LANES)),
                  full((CONV_WIDTH, DN_CONV_DIM)), full((1, LANES)), full((1, LANES)),
                  full((DN_WIDTH, DN_WIDTH))],
        out_specs=[full((n, DN_WIDTH)), full((n, DN_WIDTH)), full((n, DN_WIDTH)), full((n, LANES))],
        out_shape=[jax.ShapeDtypeStruct((n, DN_WIDTH), F32)] * 3 + [jax.ShapeDtypeStruct((n, LANES), F32)],
        compiler_params=_cp(("arbitrary",)),
        name="dn_step_prep",
    )(proj_s, cs, proj_s, conv_w, alog_row, dtb_row, ones_bd)
    ng = n // nseq
    kt = jnp.transpose(k.reshape(ng, nseq, DN_WIDTH), (0, 2, 1))
    qt = jnp.transpose(q.reshape(ng, nseq, DN_WIDTH), (0, 2, 1))
    sc3 = jnp.stack([sc[:, :hds], sc[:, hds:2 * hds]], axis=-1)
    z = proj_s[:, P_Z:P_Z + DN_WIDTH].reshape(n, hds, d)
    o, s_new = pl.pallas_call(
        functools.partial(_dn_step_kernel, nseq=nseq),
        grid=(ng,),
        in_specs=[pl.BlockSpec((1, DN_WIDTH, nseq), lambda i: (i, 0, 0)),
                  pl.BlockSpec((1, DN_WIDTH, nseq), lambda i: (i, 0, 0)),
                  pl.BlockSpec((nseq, hds, d), lambda i: (i, 0, 0)),
                  pl.BlockSpec((nseq, hds, 2), lambda i: (i, 0, 0)),
                  pl.BlockSpec((nseq, hds, d), lambda i: (i, 0, 0)),
                  pl.BlockSpec((1, d), lambda i: (0, 0)),
                  pl.BlockSpec((nseq, hds * d, d), lambda i: (i, 0, 0))],
        out_specs=[pl.BlockSpec((nseq, hds, d), lambda i: (i, 0, 0)),
                   pl.BlockSpec((nseq, hds * d, d), lambda i: (i, 0, 0))],
        out_shape=[jax.ShapeDtypeStruct((n, hds, d), F32), jax.ShapeDtypeStruct((n, hds * d, d), F32)],
        compiler_params=_cp(("parallel",)),
        name="dn_step",
    )(kt, qt, v.reshape(n, hds, d), sc3, z, norm_gain.reshape(1, d), rec_state.reshape(n, hds * d, d))
    return o.reshape(n, DN_WIDTH), s_new.reshape(n, hds, d, d)


def _nsa_decode_one(u, q_ref, new_ref, gate_ref, win_ref, exp_ref, kc_refs, pg_refs, o_ref, kc_sc, n_pages):
    dh, r, g2, nh = HEAD_DIM, NSA_GROUP, NSA_KV_HEADS, NSA_HEADS
    past = n_pages * PAGE_SIZE
    nb = past // CMP_BLOCK
    nsel = nb + 1
    wlen = win_ref.shape[4]
    for j in range(n_pages):
        kc_sc[u, j:j + 1, :] = kc_refs[j][0]
    kcv = kc_sc[u]
    new = new_ref[u]
    q8 = q_ref[u] * ATTN_SCALE
    head = lax.broadcasted_iota(jnp.int32, (nh, 1), 0)
    lane = lax.broadcasted_iota(jnp.int32, (nh, LANES), 1)
    g0 = head < r
    blk = jnp.where(lane < n_pages, 2 * lane, jnp.where(lane < nb, 2 * (lane - n_pages) + 1, lane))
    wp = lax.broadcasted_iota(jnp.int32, (1, wlen + LANES), 1)
    dpos = wlen - wp
    wmask = (dpos >= 0) & (dpos < WINDOW) & (past - wlen + wp >= 0)

    def both(x):
        return jnp.where(g0, x[:, 0:dh], x[:, dh:2 * dh])

    def new_part(kind):
        return both(jnp.broadcast_to(new[:, 2 * kind * dh:2 * (kind + 1) * dh], (nh, 2 * dh)))

    q_bd = jnp.concatenate([jnp.where(g0, q8, 0.0), jnp.where(g0, 0.0, q8)], axis=1)

    kt_all = jnp.concatenate([pg_refs[j][0, 0].reshape(2 * dh, PAGE_SIZE) for j in range(n_pages)], axis=1)
    s_new = jnp.sum(q8 * new_part(2), axis=-1, keepdims=True)
    s_all = jnp.concatenate([_dot(q_bd, kt_all), jnp.broadcast_to(s_new, (nh, LANES))], axis=1)
    sw = _dot(q_bd, win_ref[u, 0].reshape(2 * dh, wlen))
    sw_new = jnp.sum(q8 * new_part(4), axis=-1, keepdims=True)

    def cmp_rows(base):
        even = jnp.concatenate([kcv[:, base:base + dh], kcv[:, base + LANES:base + LANES + dh]], axis=1)
        odd = jnp.concatenate([kcv[:, base + dh:base + LANES], kcv[:, base + LANES + dh:base + 2 * LANES]], axis=1)
        return jnp.concatenate([even, odd], axis=0)

    s = _dot_nt(q_bd, cmp_rows(0))
    p = jnp.exp(s - jnp.max(s, axis=-1, keepdims=True))
    p = p / jnp.maximum(jnp.sum(p, axis=-1, keepdims=True), 1e-30)
    o_cmp = both(_dot(p, cmp_rows(g2 * LANES)))
    bid_row = blk[0:1].astype(F32)
    bid_col = jnp.broadcast_to(bid_row, (LANES, LANES)).T
    valid = lane[0:1] < nsel
    forced = valid & ((blk[0:1] == 0) | (nb - blk[0:1] < N_LOCAL_BLOCKS))
    sels = []
    for g in range(g2):
        in_g = (head >= g * r) & (head < (g + 1) * r)
        imp = jnp.sum(jnp.where(in_g, p, 0.0), axis=0, keepdims=True)
        imp = jnp.concatenate([imp, jnp.zeros((1, LANES - nb), F32)], axis=1)
        score = jnp.where(valid, imp + jnp.where(forced, FORCE_SCORE, 0.0), -1.0)
        sc_row = jnp.broadcast_to(score, (LANES, LANES))
        sc_col = sc_row.T
        beats = (sc_col > sc_row) | ((sc_col == sc_row) & (bid_col < bid_row))
        rank = jnp.sum(jnp.where(beats, 1.0, 0.0), axis=0, keepdims=True)
        sels.append(jnp.where(valid & (rank < min(TOP_N, nsel)), 1.0, 0.0))
    sel8 = jnp.where(g0, sels[0], sels[1]).astype(BF16)
    vt_all = jnp.concatenate([pg_refs[j][0, 1].reshape(2 * dh, PAGE_SIZE) for j in range(n_pages)], axis=1)
    chosen =jnp.dot(sel8, exp_ref[...], preferred_element_type=F32) > 0.5
    pm = _masked_softmax(s_all, chosen)
    o_slc = pm[:, past:past + 1] * new_part(3) + both(_dot_nt(pm[:, 0:past], vt_all))
    pw =_masked_softmax(jnp.concatenate([sw, jnp.broadcast_to(sw_new, (nh, LANES))], axis=1), wmask)
    o_win = pw[:, wlen:wlen + 1] * new_part(5) + both(_dot_nt(pw[:, 0:wlen], win_ref[u, 1].reshape(2 * dh, wlen)))
    gates = jnp.broadcast_to(jax.nn.sigmoid(gate_ref[u]), (nh, LANES))

    def gate(branch):
        return jnp.sum(jnp.where(lane == 3 * head + branch, gates, 0.0), axis=-1, keepdims=True)

    o_ref[u] = gate(0) * o_cmp + gate(1) * o_slc + gate(2) * o_win


def _nsa_decode_kernel(pt_ref, q_ref, new_ref, gate_ref, win_ref, exp_ref, *refs, n_pages, seqs):
    del pt_ref
    o_ref, kc_sc = refs[2 * seqs * n_pages], refs[2 * seqs * n_pages + 1]
    for u in range(seqs):
        kc_refs = refs[u * n_pages:(u + 1) * n_pages]
        pg_refs = refs[(seqs + u) * n_pages:(seqs + u + 1) * n_pages]
        _nsa_decode_one(u, q_ref, new_ref, gate_ref, win_ref, exp_ref, kc_refs, pg_refs, o_ref, kc_sc, n_pages)


def nsa_decode(page_table, q3, new_row, gate_row, win_t, kc_phys, cache_t, seqs=2):
    n, n_pages = page_table.shape
    dh = HEAD_DIM
    g2 = NSA_KV_HEADS
    wlen = win_t.shape[4]
    assert 2 * n_pages + 1 <= LANES and PAGE_SIZE == 2 * CMP_BLOCK and PAGE_SIZE == LANES and g2 == 2

    past = n_pages * PAGE_SIZE
    nb = past // CMP_BLOCK
    erow = lax.broadcasted_iota(jnp.int32, (LANES, past + LANES), 0)
    ecol = lax.broadcasted_iota(jnp.int32, (LANES, past + LANES), 1)
    page, second = ecol // PAGE_SIZE, (ecol % PAGE_SIZE) >= CMP_BLOCK
    expand = (((erow < n_pages) & (page == erow) & ~second & (ecol < past))
              | ((erow >= n_pages) & (erow < nb) & (page == erow - n_pages) & second & (ecol < past))
              | ((erow == nb) & (ecol == past))).astype(BF16)

    assert n % seqs == 0

    def kc_map(u, j):
        return lambda i, pt: (pt[seqs * i + u, j], 0, 0)

    def slc_map(u, j):
        return lambda i, pt: (pt[seqs * i + u, j], 1, 0, 0, 0)

    in_specs = [pl.BlockSpec((seqs, NSA_HEADS, dh), lambda i, pt: (i, 0, 0)),
                pl.BlockSpec((seqs, 1, KV_COLS), lambda i, pt: (i, 0, 0)),
                pl.BlockSpec((seqs, 1, LANES), lambda i, pt: (i, 0, 0)),
                pl.BlockSpec((seqs, 2, g2, dh, wlen), lambda i, pt: (i, 0, 0, 0, 0)),
                pl.BlockSpec((LANES, past + LANES), lambda i, pt: (0, 0))]
    in_specs += [pl.BlockSpec((1, 1, 4 * LANES), kc_map(u, j)) for u in range(seqs) for j in range(n_pages)]
    in_specs += [pl.BlockSpec((1, 2, g2, dh, PAGE_SIZE), slc_map(u, j)) for u in range(seqs) for j in range(n_pages)]
    grid_spec = pltpu.PrefetchScalarGridSpec(
        num_scalar_prefetch=1, grid=(n // seqs,), in_specs=in_specs,
        out_specs=pl.BlockSpec((seqs, NSA_HEADS, dh), lambda i, pt: (i, 0, 0)),
        scratch_shapes=[pltpu.VMEM((seqs, n_pages, 4 * LANES), F32)])
    return pl.pallas_call(
        functools.partial(_nsa_decode_kernel, n_pages=n_pages, seqs=seqs),
        grid_spec=grid_spec,
        out_shape=jax.ShapeDtypeStruct((n, NSA_HEADS, dh), F32),
        compiler_params=_cp(("arbitrary",)),
        name="nsa_decode",
    )(page_table, q3, new_row, gate_row, win_t, expand, *([kc_phys] * (seqs * n_pages)), *([cache_t] * (seqs * n_pages)))


def _compress_pages_kernel(x_ref, wd_ref, ped_ref, b1_ref, w2_ref, o_ref, acc_ref, *, dd):
    j = pl.program_id(1)
    g2 = NSA_KV_HEADS

    @pl.when(j == 0)
    def _():
        acc_ref[...] = jnp.zeros_like(acc_ref)

    for kind in range(2):
        for g in range(g2):
            part = None
            xt = jnp.swapaxes(x_ref[:, kind, g], 0, 1)
            for dp in range(dd // 2):
                xs = jnp.concatenate([xt[2 * dp] + ped_ref[kind, 2 * dp:2 * dp + 1, :],
                                      xt[2 * dp + 1] + ped_ref[kind, 2 * dp + 1:2 * dp + 2, :]], axis=1)
                d = jnp.dot(xs.astype(BF16), wd_ref[kind, dp], preferred_element_type=F32)
                part = d if part is None else part + d
            acc_ref[kind * g2 + g] += part

    @pl.when(j == pl.num_programs(1) - 1)
    def _():
        for kind in range(2):
            for g in range(g2):
                h = jnp.maximum(acc_ref[kind * g2 + g] + b1_ref[kind:kind + 1, :], 0.0)
                o_ref[:, 0, (kind * g2 + g) * LANES:(kind * g2 + g + 1) * LANES] = jnp.dot(
                    h.astype(BF16), w2_ref[kind], preferred_element_type=F32)


def compress_pages(cache_t, cwp, bp, dd=8):
    n_phys = cache_t.shape[0]
    g2, dh = NSA_KV_HEADS, HEAD_DIM
    assert n_phys % bp == 0 and dh % dd == 0 and dd % 2 == 0
    wd, ped, b1h, w2h = cwp
    return pl.pallas_call(
        functools.partial(_compress_pages_kernel, dd=dd),
        grid=(n_phys // bp, dh // dd),
        in_specs=[pl.BlockSpec((bp, 2, g2, dd, PAGE_SIZE), lambda i, j: (i, 0, 0, j, 0)),
                  pl.BlockSpec((2, dd // 2, 2 * PAGE_SIZE, 2 * CMP_HIDDEN), lambda i, j: (0, j, 0, 0)),
                  pl.BlockSpec((2, dd, PAGE_SIZE), lambda i, j: (0, j, 0)),
                  pl.BlockSpec((2, 2 * CMP_HIDDEN), lambda i, j: (0, 0)),
                  pl.BlockSpec((2, 2 * CMP_HIDDEN, LANES), lambda i, j: (0, 0, 0))],
        out_specs=pl.BlockSpec((bp, 1, 2 * g2 * LANES), lambda i, j: (i, 0, 0)),
        out_shape=jax.ShapeDtypeStruct((n_phys, 1, 2 * g2 * LANES), F32),
        scratch_shapes=[pltpu.VMEM((2 * g2, bp, 2 * CMP_HIDDEN), F32)],
        compiler_params=_cp(("parallel", "arbitrary")),
        name="compress_pages",
    )(cache_t, wd, ped, b1h, w2h)


def _compress_page_weights(cmp_pe, cmp_w1, cmp_b1, cmp_w2):
    w1t = jnp.transpose(cmp_w1.reshape(2, CMP_BLOCK, HEAD_DIM, CMP_HIDDEN), (0, 2, 1, 3))
    z = jnp.zeros_like(w1t)
    wd = jnp.concatenate([jnp.concatenate([w1t, z], -1), jnp.concatenate([z, w1t], -1)], axis=2).astype(BF16)
    wd = wd.reshape(2, HEAD_DIM // 2, 2 * PAGE_SIZE, 2 * CMP_HIDDEN)
    pet =jnp.transpose(cmp_pe, (0, 2, 1))
    ped = jnp.concatenate([pet, pet], -1)
    b1h = jnp.concatenate([cmp_b1, cmp_b1], -1)
    z2 = jnp.zeros_like(cmp_w2)
    w2h = jnp.concatenate([jnp.concatenate([cmp_w2, z2], -1), jnp.concatenate([z2, cmp_w2], -1)], axis=1).astype(BF16)
    return wd, ped, b1h, w2h


def _split_w_in(w):
    d = w.shape[0]
    off_b = DN_CONV_DIM + DN_WIDTH
    off_q = off_b + 2 * DN_HEADS
    off_kv = off_q + NSA_WIDTH
    off_g = off_kv + KV_COLS
    n_g = 3 * NSA_HEADS
    pad = jnp.zeros((d, P_COLS - P_SMALL - 2 * DN_HEADS - n_g), w.dtype)
    main = jnp.concatenate([w[:, :off_b], w[:, off_q:off_kv], w[:, off_kv:off_kv + CMP_COLS], w[:, off_b:off_q],
                            w[:, off_g:off_g + n_g], pad], axis=1)
    return main.astype(BF16), w[:, off_kv:off_g].T.astype(BF16)


def _row_tile(n, cap):
    t = min(n, cap)
    while n % t:
        t //= 2
    return t


def _trunk_tail(x2, mixer_dn, mixer_nsa, mem_kv3, bshape, lw, final_norm):
    n, d = x2.shape
    b, t = bshape
    sub = 8
    if t >= sub:
        x3 = mem_block(x2.reshape(b, t, d), mixer_dn.reshape(b, t, -1), mixer_nsa.reshape(b, t, -1),
                       lw["w_out_dn"], lw["w_out_nsa"], lw["ln_mem"], lw["w_mem_q"], mem_kv3, lw["w_mem_o"],
                       _row_tile(t, 512))
        x2 = x3.reshape(n, d)
    else:
        assert t == 1
        tm = _row_tile(n, 512)
        x2 = matmul_residual(x2, [mixer_dn, mixer_nsa], [lw["w_out_dn"], lw["w_out_nsa"]], tm)
        qm = rms_matmul(x2, lw["ln_mem"], lw["w_mem_q"], tm, 512)
        att = mem_attention_row(qm.reshape(n, MEM_HEADS, d // MEM_HEADS), mem_kv3).reshape(n, d)
        x2 = matmul_residual(x2, [att], [lw["w_mem_o"]], tm)
    return ffn(x2, lw["ln_ffn"], lw["w_up"], lw["w_down"], lw["ln_final"], final_norm, _row_tile(n, 1024), 512)


def _kv_rows(kvt, lo, hi):
    b, _, s = kvt.shape
    g, dh = NSA_KV_HEADS, HEAD_DIM
    return jnp.transpose(kvt[:, lo * g * dh:hi * g * dh].reshape(b, hi - lo, g, dh, s), (0, 4, 1, 2, 3))


def _prompt_layer(xp, mem_prompt, lw, cw, final_norm):
    b, s, d = xp.shape
    n = b * s
    x2 = xp.reshape(n, d)
    proj, kvt = input_projection(xp, lw["ln_mix"], lw["w_in"], lw["w_kvt"], _row_tile(s, 512), 1024)
    proj3 = proj.reshape(b, s, P_COLS)
    dn_out, s_pairs = deltanet_prompt(proj3, lw["dn_conv_w"], lw["dn_a_log"], lw["dn_dt_bias"], lw["dn_norm"])
    p_conv = proj3[:, s - (CONV_WIDTH - 1):, P_QKV:P_QKV + DN_CONV_DIM]
    p_rec = _pairs_to_heads(s_pairs)
    nb = s // CMP_BLOCK
    cmp_rows = proj3[:, :nb * CMP_BLOCK, P_CMP:P_CMP + CMP_COLS]
    kcv = compress_blocks(cmp_rows.reshape(b * nb, CMP_BLOCK * CMP_COLS), CMP_COLS, cw, _row_tile(b * nb, 256))
    kcv = kcv.reshape(b, nb, 2, NSA_KV_HEADS, HEAD_DIM)
    kc = jnp.transpose(kcv[:, :, 0], (0, 2, 1, 3))
    vct = jnp.transpose(kcv[:, :, 1], (0, 2, 3, 1))
    ocmpt, selt = nsa_cmp_topk(proj3, kc, vct, _row_tile(s, 512))
    nsa_out = nsa_attention(proj3, kvt, selt, ocmpt, 256, 512).reshape(n, NSA_WIDTH)
    m = mem_prompt.shape[1]
    mem_kv = rms_matmul(mem_prompt.reshape(b * m, d), lw["ln_memkv"], lw["w_mem_kv"], _row_tile(b * m, 512), 512)
    mem_kv5 = mem_kv.reshape(b, m, 2, MEM_HEADS, d // MEM_HEADS)
    y = _trunk_tail(x2, dn_out.reshape(n, DN_WIDTH), nsa_out, mem_kv5, (b, s), lw, final_norm)
    wk = min(WINDOW, s)
    return y.reshape(b, s, d), _kv_rows(kvt, 0, 4), _kv_rows(kvt[:, :, s - wk:], 4, 6), mem_kv5, p_conv, p_rec


def _sample_layer(xs, cache_nsa, cache_win, cache_mem, conv_state, rec_state, page_table, lw, cwp, final_norm):
    db, ds, d = xs.shape
    assert ds == 1
    n = db
    x2 = xs.reshape(n, d)
    proj, kvt = input_projection(x2[None], lw["ln_mix"], lw["w_in"], lw["w_kvt"], n, 1024)
    dn_out, s_rec = deltanet_sample(proj, conv_state, rec_state, lw["dn_conv_w"], lw["dn_a_log"], lw["dn_dt_bias"],
                                    lw["dn_norm"])
    s_conv = jnp.concatenate([conv_state[:, 1:], proj[:, None, P_QKV:P_QKV + DN_CONV_DIM]], axis=1)
    n_phys = cache_nsa.shape[0]
    cache_t = jnp.transpose(cache_nsa, (0, 2, 3, 4, 1))
    win_t = jnp.transpose(cache_win, (0, 2, 3, 4, 1))
    kc_phys = compress_pages(cache_t, cwp, _row_tile(n_phys, 256))
    kv_new = kvt[0].T
    o8 = nsa_decode(page_table, proj[:, P_NQ:P_NQ + NSA_WIDTH].reshape(n, NSA_HEADS, HEAD_DIM),
                    kv_new.reshape(n, 1, KV_COLS),
                    _flat_gates(proj[:, P_SMALL:P_SMALL + LANES]).reshape(n, 1, LANES),
                    win_t, kc_phys, cache_t)
    kv6 = kv_new.reshape(n, 1, NSA_KV_KINDS, NSA_KV_HEADS, HEAD_DIM)
    s_nsa = jnp.transpose(kvt.reshape(NSA_KV_KINDS, NSA_KV_HEADS, HEAD_DIM, n)[:4], (3, 0, 1, 2))[:, None]
    s_win = jnp.concatenate([cache_win, kv6[:, :, 4:]], axis=1)[:, 1:]
    y = _trunk_tail(x2, dn_out, o8.reshape(n, NSA_WIDTH), cache_mem, (n, 1), lw, final_norm)
    return y.reshape(db, ds, d), s_nsa, s_win, s_conv, s_rec


def _flat_gates(small):
    g0 = 2 * DN_HEADS
    n_g = 3 * NSA_HEADS
    gl = small[..., g0:g0 + n_g]
    return jnp.concatenate([gl, jnp.zeros(gl.shape[:-1] + (LANES - n_g,), gl.dtype)], axis=-1)


def kernel(x_prompt, x_sample, mem_prompt, cache_nsa_kv, cache_win_kv, cache_mem_kv, state_dn_conv, state_dn_rec, page_table, ln_mix, w_in, dn_conv_w, dn_a_log, dn_dt_bias, dn_norm, cmp_pe, cmp_w1, cmp_b1, cmp_w2, w_out, ln_mem, ln_memkv, w_mem_q, w_mem_kv, w_mem_o, ln_ffn, w_up, w_down, ln_final):
    depth = w_in.shape[0]
    xp, xs = x_prompt, x_sample
    outs_p = [[] for _ in range(5)]
    outs_s = [[] for _ in range(4)]
    for l in range(depth):
        lw = {
            "ln_mix": ln_mix[l],
            "dn_conv_w": dn_conv_w[l], "dn_a_log": dn_a_log[l], "dn_dt_bias": dn_dt_bias[l], "dn_norm": dn_norm[l],
            "w_out_dn": w_out[l][:DN_WIDTH].astype(BF16), "w_out_nsa": w_out[l][DN_WIDTH:].astype(BF16),
            "ln_mem": ln_mem[l], "ln_memkv": ln_memkv[l], "w_mem_q": w_mem_q[l].astype(BF16),
            "w_mem_kv": w_mem_kv[l].astype(BF16), "w_mem_o": w_mem_o[l].astype(BF16),
            "ln_ffn": ln_ffn[l], "w_up": w_up[l].astype(BF16), "w_down": w_down[l].astype(BF16),
            "ln_final": ln_final,
        }
        lw["w_in"], lw["w_kvt"] = _split_w_in(w_in[l])
        cw = _compress_weights(cmp_pe[l], cmp_w1[l], cmp_b1[l], cmp_w2[l])
        last = l == depth - 1
        xp, p_nsa, p_win, p_mem, p_conv, p_rec = _prompt_layer(xp, mem_prompt, lw, cw, last)
        for acc, val in zip(outs_p, (p_nsa, p_win, p_mem, p_conv, p_rec)):
            acc.append(val)
        xs, s_nsa, s_win, s_conv, s_rec = _sample_layer(
            xs, cache_nsa_kv[l], cache_win_kv[l], cache_mem_kv[l], state_dn_conv[l], state_dn_rec[l],
            page_table, lw, _compress_page_weights(cmp_pe[l], cmp_w1[l], cmp_b1[l], cmp_w2[l]), last)
        for acc, val in zip(outs_s, (s_nsa, s_win, s_conv, s_rec)):
            acc.append(val)
    return (xp, xs) + tuple(jnp.stack(a) for a in outs_p) + tuple(jnp.stack(a) for a in outs_s)
```

```python
import functools

import jax
import jax.numpy as jnp
from jax import lax
from jax.experimental import pallas as pl
from jax.experimental.pallas import tpu as pltpu

F32 = jnp.float32
BF16 = jnp.bfloat16

HEAD_DIM = 64
DN_HEADS = 8
NSA_HEADS = 8
NSA_KV_HEADS = 2
NSA_GROUP = NSA_HEADS // NSA_KV_HEADS
DN_WIDTH = DN_HEADS * HEAD_DIM
NSA_WIDTH = NSA_HEADS * HEAD_DIM
CONV_WIDTH = 4
DN_CONV_DIM = 3 * DN_WIDTH
DN_CHUNK = 64
CMP_BLOCK = 64
SEL_BLOCK = 64
TOP_N = 16
N_LOCAL_BLOCKS = 2
WINDOW = 512
CMP_HIDDEN = 128
NSA_KV_KINDS = 6
MEM_HEADS = 4
PAGE_SIZE = 128
RMS_EPS = 1e-6
FORCE_SCORE = 1e3
NEG_INF = -1e30
ATTN_SCALE = HEAD_DIM ** -0.5
LOG2_E = 1.4426950408889634

LANES = 128
P_QKV = 0
P_Z = P_QKV + DN_CONV_DIM
P_NQ = P_Z + DN_WIDTH
P_CMP = P_NQ + NSA_WIDTH
CMP_COLS = 2 * NSA_KV_HEADS * HEAD_DIM
P_SMALL = P_CMP + CMP_COLS
P_COLS = P_SMALL + 2 * LANES
KV_COLS = NSA_KV_KINDS * NSA_KV_HEADS * HEAD_DIM
VMEM_LIMIT = 56 * 1024 * 1024
DN_APPLY_PASSES = (3, 3, 3, 1, 1, 1)
DN_SQUARE_PASSES = (3, 3, 1, 1, 1)


def _cp(sem, vmem=VMEM_LIMIT):
    return pltpu.CompilerParams(dimension_semantics=sem, vmem_limit_bytes=vmem)


def _split2(a):
    hi = a.astype(BF16)
    return hi, (a - hi.astype(F32)).astype(BF16)


def _dot(a, b, passes=1):
    if passes == 1:
        return jnp.dot(a.astype(BF16), b.astype(BF16), preferred_element_type=F32)
    ah, al = _split2(a)
    bh, bl = _split2(b)
    return (jnp.dot(ah, bh, preferred_element_type=F32) + jnp.dot(ah, bl, preferred_element_type=F32)
            + jnp.dot(al, bh, preferred_element_type=F32))


def _dot_nt(a, b):
    return lax.dot_general(a.astype(BF16), b.astype(BF16), (((1,), (1,)), ((), ())),
                           preferred_element_type=F32)


def _split3(a):
    hi = a.astype(BF16)
    r1 = a - hi.astype(F32)
    mid = r1.astype(BF16)
    lo = (r1 - mid.astype(F32)).astype(BF16)
    return hi, mid, lo


def _dot_exact_lhs01(a01, b):
    a = a01.astype(BF16)
    hi, mid, lo = _split3(b)
    return (jnp.dot(a, hi, preferred_element_type=F32) + jnp.dot(a, mid, preferred_element_type=F32)
            + jnp.dot(a, lo, preferred_element_type=F32))


def _dot_exact_rhs01(a, b01):
    b = b01.astype(BF16)
    hi, mid, lo = _split3(a)
    return (jnp.dot(hi, b, preferred_element_type=F32) + jnp.dot(mid, b, preferred_element_type=F32)
            + jnp.dot(lo, b, preferred_element_type=F32))


def _rms(x, gain):
    ms = jnp.mean(x * x, axis=-1, keepdims=True)
    return x * lax.rsqrt(ms + RMS_EPS) * gain


def _softplus(x):
    return jnp.maximum(x, 0.0) + jnp.log1p(jnp.exp(-jnp.abs(x)))


def _silu(x):
    return x * jax.nn.sigmoid(x)


def _rms_mm_kernel(x_ref, g_ref, w_ref, o_ref, h_ref):
    @pl.when(pl.program_id(1) == 0)
    def _():
        h_ref[...] = _rms(x_ref[...], g_ref[...]).astype(BF16)

    o_ref[...] = jnp.dot(h_ref[...], w_ref[...], preferred_element_type=F32)


def rms_matmul(x, gain, w_bf16, tm, tn):
    n, d = x.shape
    m = w_bf16.shape[1]
    assert n % tm == 0 and m % tn == 0
    return pl.pallas_call(
        _rms_mm_kernel,
        grid=(n // tm, m // tn),
        in_specs=[pl.BlockSpec((tm, d), lambda i, j: (i, 0)),
                  pl.BlockSpec((1, d), lambda i, j: (0, 0)),
                  pl.BlockSpec((d, tn), lambda i, j: (0, j))],
        out_specs=pl.BlockSpec((tm, tn), lambda i, j: (i, j)),
        out_shape=jax.ShapeDtypeStruct((n, m), F32),
        scratch_shapes=[pltpu.VMEM((tm, d), BF16)],
        compiler_params=_cp(("parallel", "arbitrary")),
        name="rms_matmul",
    )(x, gain.reshape(1, d), w_bf16)


def _proj_kernel(x_ref, g_ref, w_ref, wkv_ref, o_ref, okv_ref, h_ref):
    @pl.when(pl.program_id(1) == 0)
    def _():
        h = _rms(x_ref[...], g_ref[...]).astype(BF16)
        h_ref[...] = h
        okv_ref[0] = lax.dot_general(wkv_ref[...], h, (((1,), (1,)), ((), ())), preferred_element_type=F32)

    o_ref[...] = jnp.dot(h_ref[...], w_ref[...], preferred_element_type=F32)


def input_projection(x3, gain, w_main, w_kvt, tm, tn):
    b, s, d = x3.shape
    n = b * s
    m = w_main.shape[1]
    kvc = w_kvt.shape[0]
    assert s % tm == 0 and m % tn == 0
    spt = s // tm
    return pl.pallas_call(
        _proj_kernel,
        grid=(n // tm, m // tn),
        in_specs=[pl.BlockSpec((tm, d), lambda i, j: (i, 0)),
                  pl.BlockSpec((1, d), lambda i, j: (0, 0)),
                  pl.BlockSpec((d, tn), lambda i, j: (0, j)),
                  pl.BlockSpec((kvc, d), lambda i, j: (0, 0))],
        out_specs=[pl.BlockSpec((tm, tn), lambda i, j: (i, j)),
                   pl.BlockSpec((1, kvc, tm), lambda i, j: (i // spt, 0, i % spt))],
        out_shape=[jax.ShapeDtypeStruct((n, m), F32), jax.ShapeDtypeStruct((b, kvc, s), F32)],
        scratch_shapes=[pltpu.VMEM((tm, d), BF16)],
        compiler_params=_cp(("parallel", "arbitrary")),
        name="input_projection",
    )(x3.reshape(n, d), gain.reshape(1, d), w_main, w_kvt)


def _mm_res_kernel(*refs, n_in):
    res_ref = refs[0]
    a_refs = refs[1:1 + n_in]
    w_refs = refs[1 + n_in:1 + 2 * n_in]
    o_ref = refs[1 + 2 * n_in]
    acc = res_ref[...]
    for a_ref, w_ref in zip(a_refs, w_refs):
        acc = acc + jnp.dot(a_ref[...].astype(BF16), w_ref[...], preferred_element_type=F32)
    o_ref[...] = acc


def matmul_residual(res, a_list, w_list, tm):
    n, d = res.shape
    assert n % tm == 0
    n_in = len(a_list)
    in_specs = [pl.BlockSpec((tm, d), lambda i: (i, 0))]
    in_specs += [pl.BlockSpec((tm, a.shape[1]), lambda i: (i, 0)) for a in a_list]
    in_specs += [pl.BlockSpec(w.shape, lambda i: (0, 0)) for w in w_list]
    return pl.pallas_call(
        functools.partial(_mm_res_kernel, n_in=n_in),
        grid=(n // tm,),
        in_specs=in_specs,
        out_specs=pl.BlockSpec((tm, d), lambda i: (i, 0)),
        out_shape=jax.ShapeDtypeStruct((n, d), F32),
        compiler_params=_cp(("parallel",)),
        name="matmul_residual",
    )(res, *a_list, *w_list)


def _ffn_kernel(x_ref, g_ref, wu_ref, wd_ref, gf_ref, o_ref, hn_ref, acc_ref, *, final_norm):
    j = pl.program_id(1)

    @pl.when(j == 0)
    def _():
        x = x_ref[...]
        hn_ref[...] = _rms(x, g_ref[...]).astype(BF16)
        acc_ref[...] = x

    u = jnp.dot(hn_ref[...], wu_ref[...], preferred_element_type=F32)
    u = jnp.square(jnp.maximum(u, 0.0)).astype(BF16)
    acc_ref[...] += jnp.dot(u, wd_ref[...], preferred_element_type=F32)

    @pl.when(j == pl.num_programs(1) - 1)
    def _():
        y = acc_ref[...]
        if final_norm:
            y = _rms(y, gf_ref[...])
        o_ref[...] = y


def ffn(x, gain, wu_bf16, wd_bf16, gain_final, final_norm, tm, tf):
    n, d = x.shape
    f = wu_bf16.shape[1]
    assert n % tm == 0 and f % tf == 0
    return pl.pallas_call(
        functools.partial(_ffn_kernel, final_norm=final_norm),
        grid=(n // tm, f // tf),
        in_specs=[pl.BlockSpec((tm, d), lambda i, j: (i, 0)),
                  pl.BlockSpec((1, d), lambda i, j: (0, 0)),
                  pl.BlockSpec((d, tf), lambda i, j: (0, j)),
                  pl.BlockSpec((tf, d), lambda i, j: (j, 0)),
                  pl.BlockSpec((1, d), lambda i, j: (0, 0))],
        out_specs=pl.BlockSpec((tm, d), lambda i, j: (i, 0)),
        out_shape=jax.ShapeDtypeStruct((n, d), F32),
        scratch_shapes=[pltpu.VMEM((tm, d), BF16), pltpu.VMEM((tm, d), F32)],
        compiler_params=_cp(("parallel", "arbitrary")),
        name="ffn",
    )(x, gain.reshape(1, d), wu_bf16, wd_bf16, gain_final.reshape(1, d))


def _mem_attn_row_kernel(q_ref, kv_ref, o_ref, *, hd):
    scale = hd ** -0.5
    q = q_ref[0]
    s = jnp.sum(kv_ref[0, :, 0] * q[None], axis=-1) * scale
    p = jnp.exp(s - jnp.max(s, axis=0, keepdims=True))
    p = p / jnp.sum(p, axis=0, keepdims=True)
    o_ref[0] = jnp.sum(p[:, :, None] * kv_ref[0, :, 1], axis=0)


def mem_attention_row(q, kv):
    b, h, hd = q.shape
    m = kv.shape[1]
    assert kv.shape[2:] == (2, h, hd)
    return pl.pallas_call(
        functools.partial(_mem_attn_row_kernel, hd=hd),
        grid=(b,),
        in_specs=[pl.BlockSpec((1, h, hd), lambda i: (i, 0, 0)),
                  pl.BlockSpec((1, m, 2, h, hd), lambda i: (i, 0, 0, 0, 0))],
        out_specs=pl.BlockSpec((1, h, hd), lambda i: (i, 0, 0)),
        out_shape=jax.ShapeDtypeStruct((b, h, hd), F32),
        compiler_params=_cp(("parallel",)),
        name="mem_attention_row",
    )(q, kv)


def _mem_block_kernel(x_ref, a1_ref, a2_ref, w1_ref, w2_ref, g_ref, wq_ref, kv_ref, wo_ref, o_ref, *, heads, hd):
    scale = hd ** -0.5
    x = (x_ref[0] + jnp.dot(a1_ref[0].astype(BF16), w1_ref[...], preferred_element_type=F32)
         + jnp.dot(a2_ref[0].astype(BF16), w2_ref[...], preferred_element_type=F32))
    q = jnp.dot(_rms(x, g_ref[...]).astype(BF16), wq_ref[...], preferred_element_type=F32)
    kt = jnp.swapaxes(kv_ref[0, :, 0], 0, 1)
    vt = jnp.swapaxes(kv_ref[0, :, 1], 0, 1)
    outs = []
    for h in range(heads):
        s = _dot_nt(q[:, h * hd:(h + 1) * hd], kt[h]) * scale
        p = jnp.exp(s - jnp.max(s, axis=-1, keepdims=True))
        p = p / jnp.sum(p, axis=-1, keepdims=True)
        outs.append(_dot(p, vt[h]))
    att = jnp.concatenate(outs, axis=1).astype(BF16)
    o_ref[0] = x + jnp.dot(att, wo_ref[...], preferred_element_type=F32)


def mem_block(x3, a1, a2, w1, w2, gain, wq, kv, wo, tq):
    b, t, d = x3.shape
    m = kv.shape[1]
    hd = d // MEM_HEADS
    assert t % tq == 0 and kv.shape[2:] == (2, MEM_HEADS, hd)
    return pl.pallas_call(
        functools.partial(_mem_block_kernel, heads=MEM_HEADS, hd=hd),
        grid=(b, t // tq),
        in_specs=[pl.BlockSpec((1, tq, d), lambda i, j: (i, j, 0)),
                  pl.BlockSpec((1, tq, a1.shape[2]), lambda i, j: (i, j, 0)),
                  pl.BlockSpec((1, tq, a2.shape[2]), lambda i, j: (i, j, 0)),
                  pl.BlockSpec(w1.shape, lambda i, j: (0, 0)),
                  pl.BlockSpec(w2.shape, lambda i, j: (0, 0)),
                  pl.BlockSpec((1, d), lambda i, j: (0, 0)),
                  pl.BlockSpec((d, d), lambda i, j: (0, 0)),
                  pl.BlockSpec((1, m, 2, MEM_HEADS, hd), lambda i, j: (i, 0, 0, 0, 0)),
                  pl.BlockSpec((d, d), lambda i, j: (0, 0))],
        out_specs=pl.BlockSpec((1, tq, d), lambda i, j: (i, j, 0)),
        out_shape=jax.ShapeDtypeStruct((b, t, d), F32),
        compiler_params=_cp(("parallel", "parallel")),
        name="mem_block",
    )(x3, a1, a2, w1, w2, gain.reshape(1, d), wq, kv, wo)


def _compress_kernel(x_ref, w1_ref, pe_ref, b1_ref, w2_ref, o_ref, acc_ref, *, row_cols, tt):
    j = pl.program_id(1)

    @pl.when(j == 0)
    def _():
        acc_ref[...] = jnp.zeros_like(acc_ref)

    hid2 = 2 * CMP_HIDDEN
    for kind in range(2):
        part = None
        for t in range(tt):
            c0 = t * row_cols + kind * LANES
            xs = x_ref[:, c0:c0 + LANES] + pe_ref[kind, t:t + 1, :]
            d = jnp.dot(xs.astype(BF16), w1_ref[kind, t], preferred_element_type=F32)
            part = d if part is None else part + d
        acc_ref[:, kind * hid2:(kind + 1) * hid2] += part

    @pl.when(j == pl.num_programs(1) - 1)
    def _():
        h = jnp.maximum(acc_ref[...] + b1_ref[...], 0.0)
        for kind in range(2):
            o_ref[:, kind * LANES:(kind + 1) * LANES] = jnp.dot(
                h[:, kind * hid2:(kind + 1) * hid2].astype(BF16), w2_ref[kind], preferred_element_type=F32)


def compress_blocks(x2, row_cols, cw, bt, tt=8):
    nb = x2.shape[0]
    assert nb % bt == 0 and CMP_BLOCK % tt == 0
    w1bd, pe2, b1bd, w2bd = cw
    return pl.pallas_call(
        functools.partial(_compress_kernel, row_cols=row_cols, tt=tt),
        grid=(nb // bt, CMP_BLOCK // tt),
        in_specs=[pl.BlockSpec((bt, tt * row_cols), lambda i, j: (i, j)),
                  pl.BlockSpec((2, tt, LANES, 2 * CMP_HIDDEN), lambda i, j: (0, j, 0, 0)),
                  pl.BlockSpec((2, tt, LANES), lambda i, j: (0, j, 0)),
                  pl.BlockSpec((1, 4 * CMP_HIDDEN), lambda i, j: (0, 0)),
                  pl.BlockSpec((2, 2 * CMP_HIDDEN, LANES), lambda i, j: (0, 0, 0))],
        out_specs=pl.BlockSpec((bt, 2 * LANES), lambda i, j: (i, 0)),
        out_shape=jax.ShapeDtypeStruct((nb, 2 * LANES), F32),
        scratch_shapes=[pltpu.VMEM((bt, 4 * CMP_HIDDEN), F32)],
        compiler_params=_cp(("parallel", "arbitrary")),
        name="compress_blocks",
    )(x2, w1bd, pe2, b1bd, w2bd)


def _compress_weights(cmp_pe, cmp_w1, cmp_b1, cmp_w2):
    w1r = cmp_w1.reshape(2, CMP_BLOCK, HEAD_DIM, CMP_HIDDEN)
    z = jnp.zeros_like(w1r)
    w1bd = jnp.concatenate([jnp.concatenate([w1r, z], -1), jnp.concatenate([z, w1r], -1)], axis=2).astype(BF16)
    pe2 = jnp.concatenate([cmp_pe, cmp_pe], -1)
    b1bd = jnp.concatenate([cmp_b1[0], cmp_b1[0], cmp_b1[1], cmp_b1[1]]).reshape(1, 4 * CMP_HIDDEN)
    z2 = jnp.zeros_like(cmp_w2)
    w2bd = jnp.concatenate([jnp.concatenate([cmp_w2, z2], -1), jnp.concatenate([z2, cmp_w2], -1)], axis=1).astype(BF16)
    return w1bd, pe2, b1bd, w2bd


def _masked_softmax(s, mask):
    s = jnp.where(mask, s, NEG_INF)
    p = jnp.where(mask, jnp.exp(s - jnp.max(s, axis=-1, keepdims=True)), 0.0)
    return p / jnp.maximum(jnp.sum(p, axis=-1, keepdims=True), 1e-30)


def _heads_as_rows(q_ref, tq, scale=ATTN_SCALE):
    qb = q_ref[0] * scale
    return jnp.concatenate([qb[:, h * HEAD_DIM:(h + 1) * HEAD_DIM] for h in range(NSA_GROUP)], axis=0).astype(BF16)


def _cmp_topk_kernel(q_ref, kc_ref, vct_ref, ocmp_ref, sel_ref, *, tq, nblk, topn):
    qi = pl.program_id(2)
    r, dh = NSA_GROUP, HEAD_DIM
    qpos = qi * tq + lax.broadcasted_iota(jnp.int32, (nblk, tq), 1)
    blk = lax.broadcasted_iota(jnp.int32, (nblk, tq), 0)
    vis = (blk + 1) * CMP_BLOCK - 1 <= qpos
    q = _heads_as_rows(q_ref, tq)
    kc = kc_ref[0, 0]
    vct = vct_ref[0, 0]
    imp = jnp.zeros((nblk, tq), F32)
    for h in range(r):
        s = jnp.where(vis, _dot_nt(kc, q[h * tq:(h + 1) * tq]), NEG_INF)
        p = jnp.where(vis, jnp.exp(s - jnp.max(s, axis=0, keepdims=True)), 0.0)
        p = p / jnp.maximum(jnp.sum(p, axis=0, keepdims=True), 1e-30)
        ocmp_ref[0, 0, h * dh:(h + 1) * dh, :] = _dot(vct, p)
        imp = imp + p
    cur = lax.shift_right_logical(qpos, 6)
    valid = blk <= cur
    forced = valid & ((blk == 0) | (cur - blk < N_LOCAL_BLOCKS))
    score = jnp.where(valid, imp + jnp.where(forced, FORCE_SCORE, 0.0), -1.0)
    rank = jnp.zeros((nblk, tq), F32)
    for i in range(nblk):
        si = score[i:i + 1, :]
        rank = rank + jnp.where((si > score) | ((si == score) & (blk > i)), 1.0, 0.0)
    sel_ref[0, 0] = jnp.where(rank < topn, 1.0, 0.0).astype(BF16)


def nsa_cmp_topk(proj3, kc, vct, tq):
    b, sq, _ = proj3.shape
    g = kc.shape[1]
    nblk = kc.shape[2]
    dh = HEAD_DIM
    gw = NSA_GROUP * dh
    assert sq % tq == 0 and SEL_BLOCK == 64 and P_NQ % gw == 0
    return pl.pallas_call(
        functools.partial(_cmp_topk_kernel, tq=tq, nblk=nblk, topn=min(TOP_N, nblk)),
        grid=(b, g, sq // tq),
        in_specs=[pl.BlockSpec((1, tq, gw), lambda i, j, k: (i, k, P_NQ // gw + j)),
                  pl.BlockSpec((1, 1, nblk, dh), lambda i, j, k: (i, j, 0, 0)),
                  pl.BlockSpec((1, 1, dh, nblk), lambda i, j, k: (i, j, 0, 0))],
        out_specs=[pl.BlockSpec((1, 1, gw, tq), lambda i, j, k: (i, j, 0, k)),
                   pl.BlockSpec((1, 1, nblk, tq), lambda i, j, k: (i, j, 0, k))],
        out_shape=[jax.ShapeDtypeStruct((b, g, gw, sq), F32),
                   jax.ShapeDtypeStruct((b, g, nblk, sq), BF16)],
        compiler_params=_cp(("parallel", "parallel", "parallel")),
        name="nsa_cmp_topk",
    )(proj3, kc, vct)


def _nsa_attn_kernel(q_ref, kst_ref, vst_ref, kwt_ref, vwt_ref, selt_ref, ocmp_ref, sm_ref, o_ref,
                     m_sc, acc_sc, *, tq, tk, nsel):
    grp = pl.program_id(1)
    qi = pl.program_id(2)
    r = NSA_GROUP
    dh = HEAD_DIM
    q = _heads_as_rows(q_ref, tq, ATTN_SCALE * LOG2_E)
    qpos = qi * tq + lax.broadcasted_iota(jnp.int32, (1, tq), 1)
    ones_row = jnp.where(lax.broadcasted_iota(jnp.int32, (8, 1), 0) == 0, 1.0, 0.0)

    def reset():
        m_sc[...] = jnp.full(m_sc.shape, NEG_INF, F32)
        acc_sc[...] = jnp.zeros(acc_sc.shape, F32)

    def with_ones(vt):
        return jnp.concatenate([vt, jnp.broadcast_to(ones_row, (8, vt.shape[1]))], axis=0).astype(BF16)

    def scores(kt, mask):
        bias = jnp.where(mask, 0.0, NEG_INF)
        return _dot_nt(kt.T, q) + jnp.concatenate([bias] * r, axis=1)

    def fold(s, vt):
        m_prev = m_sc[...]
        m_new = jnp.maximum(m_prev, jnp.max(s, axis=0, keepdims=True))
        p = jnp.exp2(s - m_new).astype(BF16)
        acc_sc[...] = jnp.exp2(m_prev - m_new) * acc_sc[...] + jnp.dot(with_ones(vt), p, preferred_element_type=F32)
        m_sc[...] = m_new

    def step(kt, vt, mask):
        fold(scores(kt, mask), vt)

    def result():
        acc = acc_sc[...]
        return acc[0:dh] / jnp.maximum(acc[dh:dh + 1], 1e-30)

    wk = WINDOW + tq
    w0 = pl.multiple_of(jnp.maximum(qi - WINDOW // tq, 0) * tq, tq)
    dpos = qpos - (w0 + lax.broadcasted_iota(jnp.int32, (wk, 1), 0))
    s_win = scores(kwt_ref[0, :, pl.ds(w0, wk)], (dpos >= 0) & (dpos < WINDOW))

    reset()
    selt = selt_ref[0, 0]
    n_kt = ((qi + 1) * tq + tk - 1) // tk

    def slc_mask(k0):
        kpos = k0 + lax.broadcasted_iota(jnp.int32, (tk, 1), 0)
        erow = k0 + lax.broadcasted_iota(jnp.int32, (tk, nsel), 0)
        eblk = lax.broadcasted_iota(jnp.int32, (tk, nsel), 1)
        expand = jnp.where(lax.shift_right_logical(erow, 6) == eblk, 1.0, 0.0).astype(BF16)
        chosen = jnp.dot(expand, selt, preferred_element_type=F32) > 0.5
        return chosen & (kpos <= qpos)

    def slc_pair(c, carry):
        ka = pl.multiple_of(2 * c * tk, tk)
        kb = pl.multiple_of((2 * c + 1) * tk, tk)
        sa = scores(kst_ref[0, :, pl.ds(ka, tk)], slc_mask(ka))
        sb = scores(kst_ref[0, :, pl.ds(kb, tk)], slc_mask(kb))
        fold(sa, vst_ref[0, :, pl.ds(ka, tk)])
        fold(sb, vst_ref[0, :, pl.ds(kb, tk)])
        return carry

    def slc_single(kt, carry):
        k0 = pl.multiple_of(kt * tk, tk)
        step(kst_ref[0, :, pl.ds(k0, tk)], vst_ref[0, :, pl.ds(k0, tk)], slc_mask(k0))
        return carry

    lax.fori_loop(0, n_kt // 2, slc_pair, 0)
    lax.fori_loop(2 * (n_kt // 2), n_kt, slc_single, 0)
    o_slc = result()

    reset()
    fold(s_win, vwt_ref[0, :, pl.ds(w0, wk)])
    o_win = result()

    gates_t = jax.nn.sigmoid(sm_ref[0]).T
    g0 = 2 * DN_HEADS
    per = 3 * r
    gt = jnp.where(grp == 0, gates_t[g0:g0 + per], gates_t[g0 + per:g0 + 2 * per])
    outs = []
    for h in range(r):
        outs.append(gt[3 * h:3 * h + 1] * ocmp_ref[0, 0, h * dh:(h + 1) * dh, :]
                    + gt[3 * h + 1:3 * h + 2] * o_slc[:, h * tq:(h + 1) * tq]
                    + gt[3 * h + 2:3 * h + 3] * o_win[:, h * tq:(h + 1) * tq])
    o_ref[0] = jnp.concatenate(outs, axis=0).T


def nsa_attention(proj3, kvt, selt, ocmpt, tq, tk):
    b, sq, _ = proj3.shape
    t = kvt.shape[2]
    g, nsel = selt.shape[1], selt.shape[2]
    r, dh = NSA_GROUP, HEAD_DIM
    gw = r * dh
    assert sq == t and sq % tq == 0 and t % tk == 0 and tk % tq == 0 and WINDOW % tq == 0 and tq % LANES == 0
    assert t >= WINDOW + tq and g == NSA_KV_HEADS and g == 2

    def kv_spec(kind):
        return pl.BlockSpec((1, dh, t), lambda i, j, k: (i, kind * g + j, 0))

    return pl.pallas_call(
        functools.partial(_nsa_attn_kernel, tq=tq, tk=tk, nsel=nsel),
        grid=(b, g, sq // tq),
        in_specs=[pl.BlockSpec((1, tq, gw), lambda i, j, k: (i, k, P_NQ // gw + j)),
                  kv_spec(2), kv_spec(3), kv_spec(4), kv_spec(5),
                  pl.BlockSpec((1, 1, nsel, tq), lambda i, j, k: (i, j, 0, k)),
                  pl.BlockSpec((1, 1, gw, tq), lambda i, j, k: (i, j, 0, k)),
                  pl.BlockSpec((1, tq, LANES), lambda i, j, k: (i, k, P_SMALL // LANES))],
        out_specs=pl.BlockSpec((1, tq, gw), lambda i, j, k: (i, k, j)),
        out_shape=jax.ShapeDtypeStruct((b, sq, g * gw), F32),
        scratch_shapes=[pltpu.VMEM((1, r * tq), F32), pltpu.VMEM((dh + 8, r * tq), F32)],
        compiler_params=_cp(("parallel", "parallel", "parallel")),
        name="nsa_attention",
    )(proj3, kvt, kvt, kvt, kvt, selt, ocmpt, proj3)


def _bdot(a, b, passes=1):
    dims = (((2,), (1,)), ((0,), (0,)))
    if passes == 1:
        return lax.dot_general(a.astype(BF16), b.astype(BF16), dims, preferred_element_type=F32)
    ah, al = _split2(a)
    bh, bl = _split2(b)
    return (lax.dot_general(ah, bh, dims, preferred_element_type=F32)
            + lax.dot_general(ah, bl, dims, preferred_element_type=F32)
            + lax.dot_general(al, bh, dims, preferred_element_type=F32))


def _bdot_nt(a, b):
    return lax.dot_general(a.astype(BF16), b.astype(BF16), (((2,), (2,)), ((0,), (0,))),
                           preferred_element_type=F32)


def _deltanet_kernel(qkv_ref, z_ref, sm_ref, cw_ref, alog_ref, dtb_ref, gn_ref, o_ref, s_out_ref,
                     xbuf, s_sc):
    c = pl.program_id(0)
    ch = DN_CHUNK
    n_pairs = DN_HEADS // 2
    two = 2 * ch
    n_batch = qkv_ref.shape[0]

    @pl.when(c == 0)
    def _():
        xbuf[:, 0:8, :] = jnp.zeros((n_batch, 8, DN_CONV_DIM), F32)
        s_sc[...] = jnp.zeros_like(s_sc)

    ti = lax.broadcasted_iota(jnp.int32, (ch, ch), 0)
    tj = lax.broadcasted_iota(jnp.int32, (ch, ch), 1)
    tri = jnp.where(ti >= tj, 1.0, 0.0)
    lane = lax.broadcasted_iota(jnp.int32, (ch, LANES), 1)
    lo = lane < HEAD_DIM
    row2 = lax.broadcasted_iota(jnp.int32, (two, two), 0)
    col2 = lax.broadcasted_iota(jnp.int32, (two, two), 1)
    same = (row2 >= ch) == (col2 >= ch)
    incl = (same & (row2 >= col2))[None]
    strict = (same & (row2 > col2))[None]
    top = lax.broadcasted_iota(jnp.int32, (two, 1), 0) < ch

    def seg_sum(x):
        s_lo = jnp.sum(jnp.where(lo, x, 0.0), axis=-1, keepdims=True)
        s_hi = jnp.sum(jnp.where(lo, 0.0, x), axis=-1, keepdims=True)
        return jnp.where(lo, s_lo, s_hi)

    def stack2(x):
        return jnp.concatenate([jnp.where(lo, x, 0.0), jnp.where(lo, 0.0, x)], axis=0)

    def col2x(a, b):
        return jnp.concatenate([jnp.broadcast_to(a, (ch, LANES)), jnp.broadcast_to(b, (ch, LANES))], axis=0)

    q_l, k_l, v_l, beta_l, gc_l, gl_l = [], [], [], [], [], []
    for bi in range(n_batch):
        xbuf[bi, 8:8 + ch, :] = qkv_ref[bi]
        conv = None
        for w in range(CONV_WIDTH):
            term = xbuf[bi, 5 + w:5 + w + ch, :] * cw_ref[w:w + 1, :]
            conv = term if conv is None else conv + term
        xbuf[bi, 0:8, :] = xbuf[bi, ch:ch + 8, :]
        act = _silu(conv)
        sm = sm_ref[bi]
        beta_all = jax.nn.sigmoid(sm)
        g_all = -jnp.exp(alog_ref[...]) * _softplus(sm + dtb_ref[...])
        gcum_all = _dot_exact_lhs01(tri, g_all)
        for p in range(n_pairs):
            c0 = p * LANES
            qp = act[:, c0:c0 + LANES]
            kp = act[:, DN_WIDTH + c0:DN_WIDTH + c0 + LANES]
            vp = act[:, 2 * DN_WIDTH + c0:2 * DN_WIDTH + c0 + LANES]
            qp = qp * lax.rsqrt(seg_sum(qp * qp) + 1e-6) * (HEAD_DIM ** -0.5)
            kp = kp * lax.rsqrt(seg_sum(kp * kp) + 1e-6)
            h0, h1 = DN_HEADS + 2 * p, DN_HEADS + 2 * p + 1
            q_l.append(stack2(qp))
            k_l.append(stack2(kp))
            v_l.append(stack2(vp))
            beta_l.append(col2x(beta_all[:, 2 * p:2 * p + 1], beta_all[:, 2 * p + 1:2 * p + 2]))
            gc_l.append(col2x(gcum_all[:, h0:h0 + 1], gcum_all[:, h1:h1 + 1]))
            gl_l.append(jnp.broadcast_to(jnp.where(top, gcum_all[ch - 1:ch, h0:h0 + 1], gcum_all[ch - 1:ch, h1:h1 + 1]),
                                         (two, LANES)))
    q2, k2, v2 = jnp.stack(q_l), jnp.stack(k_l), jnp.stack(v_l)
    beta2, gc2, gl2 = jnp.stack(beta_l), jnp.stack(gc_l), jnp.stack(gl_l)
    decay = jnp.exp(jnp.where(incl, gc2 - jnp.swapaxes(gc2, 1, 2), NEG_INF))
    kb2 = k2 * beta2
    a_mat = jnp.where(strict, _bdot_nt(kb2, k2) * decay, 0.0)
    aqk = jnp.where(incl, _bdot_nt(q2, k2) * decay, 0.0)
    s_old = s_sc[...]
    egc = jnp.exp(gc2)
    x = beta2 * (v2 - egc * _bdot(k2, s_old))
    pw = -a_mat
    n_lvl = ch.bit_length() - 1
    for lvl in range(n_lvl):
        x = x + _bdot(pw, x, DN_APPLY_PASSES[lvl])
        if lvl + 1 < n_lvl:
            pw = _bdot(pw, pw, DN_SQUARE_PASSES[lvl])
    o2 = _bdot(q2 * egc, s_old) + _bdot(aqk, x)
    kdec = k2 * jnp.exp(gl2 - gc2)
    s_sc[...] = s_old * jnp.exp(gl2) + _bdot(jnp.swapaxes(kdec, 1, 2), x)
    for bi in range(n_batch):
        for p in range(n_pairs):
            c0 = p * LANES
            o_n = o2[bi * n_pairs + p]
            o_pair = o_n[0:ch] + o_n[ch:two]
            inv = lax.rsqrt(seg_sum(o_pair * o_pair) * (1.0 / HEAD_DIM) + RMS_EPS)
            o_ref[bi, :, c0:c0 + LANES] = o_pair * inv * gn_ref[...] * _silu(z_ref[bi, :, c0:c0 + LANES])

    @pl.when(c == pl.num_programs(0) - 1)
    def _():
        s_out_ref[...] = s_sc[...]


def deltanet_prompt(proj3, conv_w, a_log, dt_bias, norm_gain):
    b, t, _ = proj3.shape
    ch = DN_CHUNK
    assert t % ch == 0
    pad = jnp.zeros((LANES - 2 * DN_HEADS,), F32)
    alog_row = jnp.concatenate([jnp.zeros((DN_HEADS,), F32), a_log, pad]).reshape(1, LANES)
    dtb_row = jnp.concatenate([jnp.zeros((DN_HEADS,), F32), dt_bias, pad]).reshape(1, LANES)
    gn_row = jnp.concatenate([norm_gain, norm_gain]).reshape(1, LANES)
    n_pairs = DN_HEADS // 2
    o, s_fin = pl.pallas_call(
        _deltanet_kernel,
        grid=(t // ch,),
        in_specs=[pl.BlockSpec((b, ch, DN_CONV_DIM), lambda j: (0, j, P_QKV // DN_CONV_DIM)),
                  pl.BlockSpec((b, ch, DN_WIDTH), lambda j: (0, j, P_Z // DN_WIDTH)),
                  pl.BlockSpec((b, ch, LANES), lambda j: (0, j, P_SMALL // LANES)),
                  pl.BlockSpec((CONV_WIDTH, DN_CONV_DIM), lambda j: (0, 0)),
                  pl.BlockSpec((1, LANES), lambda j: (0, 0)),
                  pl.BlockSpec((1, LANES), lambda j: (0, 0)),
                  pl.BlockSpec((1, LANES), lambda j: (0, 0))],
        out_specs=[pl.BlockSpec((b, ch, DN_WIDTH), lambda j: (0, j, 0)),
                   pl.BlockSpec((b * n_pairs, 2 * ch, LANES), lambda j: (0, 0, 0))],
        out_shape=[jax.ShapeDtypeStruct((b, t, DN_WIDTH), F32),
                   jax.ShapeDtypeStruct((b * n_pairs, 2 * ch, LANES), F32)],
        scratch_shapes=[pltpu.VMEM((b, ch + 8, DN_CONV_DIM), F32), pltpu.VMEM((b * n_pairs, 2 * ch, LANES), F32)],
        compiler_params=_cp(("arbitrary",)),
        name="deltanet_prompt",
    )(proj3, proj3, proj3, conv_w, alog_row, dtb_row, gn_row)
    return o, s_fin.reshape(b, n_pairs, 2 * ch, LANES)


def _pairs_to_heads(s_pairs):
    d = HEAD_DIM
    return jnp.stack([s_pairs[:, :, :d, :d], s_pairs[:, :, d:, d:]], axis=2).reshape(
        s_pairs.shape[0], DN_HEADS, d, d)


def _dn_step_prep_kernel(qkv_ref, cs_ref, sm_ref, cw_ref, alog_ref, dtb_ref, ones_ref,
                         q_ref, k_ref, v_ref, sc_ref):
    conv = qkv_ref[...] * cw_ref[CONV_WIDTH - 1:CONV_WIDTH, :]
    for w in range(CONV_WIDTH - 1):
        conv = conv + cs_ref[w] * cw_ref[w:w + 1, :]
    act = _silu(conv)
    q = act[:, 0:DN_WIDTH]
    k = act[:, DN_WIDTH:2 * DN_WIDTH]

    def seg_sum(x):
        return _dot_exact_rhs01(x, ones_ref[...])

    q_ref[...] = q * lax.rsqrt(seg_sum(q * q) + 1e-6) * (HEAD_DIM ** -0.5)
    k_ref[...] = k * lax.rsqrt(seg_sum(k * k) + 1e-6)
    v_ref[...] = act[:, 2 * DN_WIDTH:]
    sm = sm_ref[...]
    g = -jnp.exp(alog_ref[...]) * _softplus(sm + dtb_ref[...])
    lane = lax.broadcasted_iota(jnp.int32, sm.shape, 1)
    sc_ref[...] = jnp.where(lane < DN_HEADS, jax.nn.sigmoid(sm), jnp.exp(g))


def _dn_step_kernel(k_ref, q_ref, v_ref, be_ref, eg_ref, z_ref, gn_ref, s_ref, o_ref, s_out_ref):
    s_old = s_ref[0]
    k, q, v = k_ref[0], q_ref[0], v_ref[0]
    beta, eg = be_ref[0], eg_ref[0]
    ks = jnp.sum(k[:, None, :] * s_old, axis=0)
    qs = jnp.sum(q[:, None, :] * s_old, axis=0)
    qk = jnp.sum(q * k, axis=0, keepdims=True)
    v_new = beta * (v - eg * ks)
    o = eg * qs + qk * v_new
    inv = lax.rsqrt(jnp.mean(o * o, axis=0, keepdims=True) + RMS_EPS)
    o_ref[0] = o * inv * gn_ref[...] * _silu(z_ref[0])
    s_out_ref[0] = s_old * eg[None] + k[:, None, :] * v_new[None]


def deltanet_sample(proj_s, conv_state, rec_state, conv_w, a_log, dt_bias, norm_gain):
    n = proj_s.shape[0]
    hds, d = DN_HEADS, HEAD_DIM
    pad = jnp.zeros((LANES - 2 * hds,), F32)
    alog_row = jnp.concatenate([jnp.zeros((hds,), F32), a_log, pad]).reshape(1, LANES)
    dtb_row = jnp.concatenate([jnp.zeros((hds,), F32), dt_bias, pad]).reshape(1, LANES)
    head_of = jnp.arange(DN_WIDTH) // d
    ones_bd = (head_of[:, None] == head_of[None, :]).astype(BF16)
    cs = jnp.transpose(conv_state, (1, 0, 2))
    full = lambda shape: pl.BlockSpec(shape, lambda i: (0,) * len(shape))
    q, k, v, sc = pl.pallas_call(
        _dn_step_prep_kernel,
        grid=(1,),
        in_specs=[pl.BlockSpec((n, DN_CONV_DIM), lambda i: (0, P_QKV // DN_CONV_DIM)),
                  full((CONV_WIDTH - 1, n, DN_CONV_DIM)),
                  pl.BlockSpec((n, LANES), lambda i: (0, P_SMALL // LANES)),
                  full((CONV_WIDTH, DN_CONV_DIM)), full((1, LANES)), full((1, LANES)),
                  full((DN_WIDTH, DN_WIDTH))],
        out_specs=[full((n, DN_WIDTH)), full((n, DN_WIDTH)), full((n, DN_WIDTH)), full((n, LANES))],
        out_shape=[jax.ShapeDtypeStruct((n, DN_WIDTH), F32)] * 3 + [jax.ShapeDtypeStruct((n, LANES), F32)],
        compiler_params=_cp(("arbitrary",)),
        name="dn_step_prep",
    )(proj_s, cs, proj_s, conv_w, alog_row, dtb_row, ones_bd)
    t3 = lambda a: a.T.reshape(hds, d, n)
    sct = sc[:, :2 * hds].T.reshape(2, hds, 1, n)
    head_vec = pl.BlockSpec((1, d, n), lambda i: (i, 0, 0))
    head_scl = pl.BlockSpec((1, 1, n), lambda i: (i, 0, 0))
    state = pl.BlockSpec((1, d, d, n), lambda i: (i, 0, 0, 0))
    o, s_new = pl.pallas_call(
        _dn_step_kernel,
        grid=(hds,),
        in_specs=[head_vec, head_vec, head_vec, head_scl, head_scl, head_vec,
                  pl.BlockSpec((d, 1), lambda i: (0, 0)), state],
        out_specs=[head_vec, state],
        out_shape=[jax.ShapeDtypeStruct((hds, d, n), F32), jax.ShapeDtypeStruct((hds, d, d, n), F32)],
        compiler_params=_cp(("parallel",)),
        name="dn_step",
    )(t3(k), t3(q), t3(v), sct[0], sct[1], t3(proj_s[:, P_Z:P_Z + DN_WIDTH]), norm_gain.reshape(d, 1),
      jnp.transpose(rec_state, (1, 2, 3, 0)))
    return o.reshape(DN_WIDTH, n).T, jnp.transpose(s_new, (3, 0, 1, 2))


def _nsa_decode_kernel(pt_ref, q_ref, new_ref, gate_ref, win_ref, exp_ref, *refs, n_pages, seqs):
    del pt_ref
    o_ref, kc_sc = refs[2 * seqs * n_pages], refs[2 * seqs * n_pages + 1]
    dh, r, g2, nh = HEAD_DIM, NSA_GROUP, NSA_KV_HEADS, NSA_HEADS
    past = n_pages * PAGE_SIZE
    nb = past // CMP_BLOCK
    nsel = nb + 1
    wlen = win_ref.shape[4]
    for u in range(seqs):
        for j in range(n_pages):
            kc_sc[u, j:j + 1, :] = refs[u * n_pages + j][0]
    kcv = kc_sc[...]
    new = new_ref[...]
    q8 = q_ref[...] * ATTN_SCALE
    head = lax.broadcasted_iota(jnp.int32, (1, nh, 1), 1)
    lane = lax.broadcasted_iota(jnp.int32, (1, 1, LANES), 2)
    g0 = head < r
    blk = jnp.where(lane < n_pages, 2 * lane, jnp.where(lane < nb, 2 * (lane - n_pages) + 1, lane))
    wp = lax.broadcasted_iota(jnp.int32, (1, 1, wlen + LANES), 2)
    dpos = wlen - wp
    wmask = (dpos >= 0) & (dpos < WINDOW) & (past - wlen + wp >= 0)

    def both(x):
        return jnp.where(g0, x[:, :, 0:dh], x[:, :, dh:2 * dh])

    def new_part(kind):
        return both(jnp.broadcast_to(new[:, :, 2 * kind * dh:2 * (kind + 1) * dh], (seqs, nh, 2 * dh)))

    def pages(kind):
        return jnp.stack([jnp.concatenate([refs[(seqs + u) * n_pages + j][0, kind].reshape(2 * dh, PAGE_SIZE)
                                           for j in range(n_pages)], axis=1) for u in range(seqs)])

    q_bd = jnp.concatenate([jnp.where(g0, q8, 0.0), jnp.where(g0, 0.0, q8)], axis=2)
    s_new = jnp.sum(q8 * new_part(2), axis=-1, keepdims=True)
    s_all = jnp.concatenate([_bdot(q_bd, pages(0)), jnp.broadcast_to(s_new, (seqs, nh, LANES))], axis=2)
    sw = _bdot(q_bd, win_ref[:, 0].reshape(seqs, 2 * dh, wlen))
    sw_new = jnp.sum(q8 * new_part(4), axis=-1, keepdims=True)

    def cmp_rows(base):
        even = jnp.concatenate([kcv[:, :, base:base + dh], kcv[:, :, base + LANES:base + LANES + dh]], axis=2)
        odd = jnp.concatenate([kcv[:, :, base + dh:base + LANES], kcv[:, :, base + LANES + dh:base + 2 * LANES]], axis=2)
        return jnp.concatenate([even, odd], axis=1)

    s = _bdot_nt(q_bd, cmp_rows(0))
    p = jnp.exp(s - jnp.max(s, axis=-1, keepdims=True))
    p = p / jnp.maximum(jnp.sum(p, axis=-1, keepdims=True), 1e-30)
    o_cmp = both(_bdot(p, cmp_rows(g2 * LANES)))
    bid_row = jnp.broadcast_to(blk.astype(F32), (1, LANES, LANES))
    bid_col = jnp.swapaxes(bid_row, 1, 2)
    valid = lane < nsel
    forced = valid & ((blk == 0) | (nb - blk < N_LOCAL_BLOCKS))
    sels = []
    for g in range(g2):
        in_g = (head >= g * r) & (head < (g + 1) * r)
        imp = jnp.sum(jnp.where(in_g, p, 0.0), axis=1, keepdims=True)
        imp = jnp.concatenate([imp, jnp.zeros((seqs, 1, LANES - nb), F32)], axis=2)
        score = jnp.where(valid, imp + jnp.where(forced, FORCE_SCORE, 0.0), -1.0)
        sc_row = jnp.broadcast_to(score, (seqs, LANES, LANES))
        sc_col = jnp.swapaxes(sc_row, 1, 2)
        beats = (sc_col > sc_row) | ((sc_col == sc_row) & (bid_col < bid_row))
        rank = jnp.sum(jnp.where(beats, 1.0, 0.0), axis=1, keepdims=True)
        sels.append(jnp.where(valid & (rank < min(TOP_N, nsel)), 1.0, 0.0))
    sel8 = jnp.where(g0, sels[0], sels[1]).astype(BF16)
    chosen = jnp.dot(sel8.reshape(seqs * nh, LANES), exp_ref[...], preferred_element_type=F32) > 0.5
    pm = _masked_softmax(s_all, chosen.reshape(seqs, nh, past + LANES))
    o_slc = pm[:, :, past:past + 1] * new_part(3) + both(_bdot_nt(pm[:, :, 0:past], pages(1)))
    pw = _masked_softmax(jnp.concatenate([sw, jnp.broadcast_to(sw_new, (seqs, nh, LANES))], axis=2), wmask)
    o_win = pw[:, :, wlen:wlen + 1] * new_part(5) + both(
        _bdot_nt(pw[:, :, 0:wlen], win_ref[:, 1].reshape(seqs, 2 * dh, wlen)))
    gates = jnp.broadcast_to(jax.nn.sigmoid(gate_ref[...]), (seqs, nh, LANES))

    def gate(branch):
        return jnp.sum(jnp.where(lane == 3 * head + branch, gates, 0.0), axis=-1, keepdims=True)

    o_ref[...] = gate(0) * o_cmp + gate(1) * o_slc + gate(2) * o_win


def nsa_decode(page_table, q3, new_row, gate_row, win_t, kc_phys, cache_t, seqs=8):
    n, n_pages = page_table.shape
    dh = HEAD_DIM
    g2 = NSA_KV_HEADS
    wlen = win_t.shape[4]
    assert 2 * n_pages + 1 <= LANES and PAGE_SIZE == 2 * CMP_BLOCK and PAGE_SIZE == LANES and g2 == 2

    past = n_pages * PAGE_SIZE
    nb = past // CMP_BLOCK
    erow = lax.broadcasted_iota(jnp.int32, (LANES, past + LANES), 0)
    ecol = lax.broadcasted_iota(jnp.int32, (LANES, past + LANES), 1)
    page, second = ecol // PAGE_SIZE, (ecol % PAGE_SIZE) >= CMP_BLOCK
    expand = (((erow < n_pages) & (page == erow) & ~second & (ecol < past))
              | ((erow >= n_pages) & (erow < nb) & (page == erow - n_pages) & second & (ecol < past))
              | ((erow == nb) & (ecol == past))).astype(BF16)

    assert n % seqs == 0

    def kc_map(u, j):
        return lambda i, pt: (pt[seqs * i + u, j], 0, 0)

    def slc_map(u, j):
        return lambda i, pt: (pt[seqs * i + u, j], 1, 0, 0, 0)

    in_specs = [pl.BlockSpec((seqs, NSA_HEADS, dh), lambda i, pt: (i, 0, 0)),
                pl.BlockSpec((seqs, 1, KV_COLS), lambda i, pt: (i, 0, 0)),
                pl.BlockSpec((seqs, 1, LANES), lambda i, pt: (i, 0, 0)),
                pl.BlockSpec((seqs, 2, g2, dh, wlen), lambda i, pt: (i, 0, 0, 0, 0)),
                pl.BlockSpec((LANES, past + LANES), lambda i, pt: (0, 0))]
    in_specs += [pl.BlockSpec((1, 1, 4 * LANES), kc_map(u, j)) for u in range(seqs) for j in range(n_pages)]
    in_specs += [pl.BlockSpec((1, 2, g2, dh, PAGE_SIZE), slc_map(u, j)) for u in range(seqs) for j in range(n_pages)]
    grid_spec = pltpu.PrefetchScalarGridSpec(
        num_scalar_prefetch=1, grid=(n // seqs,), in_specs=in_specs,
        out_specs=pl.BlockSpec((seqs, NSA_HEADS, dh), lambda i, pt: (i, 0, 0)),
        scratch_shapes=[pltpu.VMEM((seqs, n_pages, 4 * LANES), F32)])
    return pl.pallas_call(
        functools.partial(_nsa_decode_kernel, n_pages=n_pages, seqs=seqs),
        grid_spec=grid_spec,
        out_shape=jax.ShapeDtypeStruct((n, NSA_HEADS, dh), F32),
        compiler_params=_cp(("arbitrary",)),
        name="nsa_decode",
    )(page_table, q3, new_row, gate_row, win_t, expand, *([kc_phys] * (seqs * n_pages)), *([cache_t] * (seqs * n_pages)))


def _compress_pages_kernel(x_ref, wd_ref, ped_ref, b1_ref, w2_ref, o_ref, acc_ref, *, dd):
    j = pl.program_id(1)
    g2 = NSA_KV_HEADS

    @pl.when(j == 0)
    def _():
        acc_ref[...] = jnp.zeros_like(acc_ref)

    for kind in range(2):
        for g in range(g2):
            part = None
            xt = jnp.swapaxes(x_ref[:, kind, g], 0, 1)
            for dp in range(dd // 2):
                xs = jnp.concatenate([xt[2 * dp] + ped_ref[kind, 2 * dp:2 * dp + 1, :],
                                      xt[2 * dp + 1] + ped_ref[kind, 2 * dp + 1:2 * dp + 2, :]], axis=1)
                d = jnp.dot(xs.astype(BF16), wd_ref[kind, dp], preferred_element_type=F32)
                part = d if part is None else part + d
            acc_ref[kind * g2 + g] += part

    @pl.when(j == pl.num_programs(1) - 1)
    def _():
        for kind in range(2):
            for g in range(g2):
                h = jnp.maximum(acc_ref[kind * g2 + g] + b1_ref[kind:kind + 1, :], 0.0)
                o_ref[:, 0, (kind * g2 + g) * LANES:(kind * g2 + g + 1) * LANES] = jnp.dot(
                    h.astype(BF16), w2_ref[kind], preferred_element_type=F32)


def compress_pages(cache_t, cwp, bp, dd=8):
    n_phys = cache_t.shape[0]
    g2, dh = NSA_KV_HEADS, HEAD_DIM
    assert n_phys % bp == 0 and dh % dd == 0 and dd % 2 == 0
    wd, ped, b1h, w2h = cwp
    return pl.pallas_call(
        functools.partial(_compress_pages_kernel, dd=dd),
        grid=(n_phys // bp, dh // dd),
        in_specs=[pl.BlockSpec((bp, 2, g2, dd, PAGE_SIZE), lambda i, j: (i, 0, 0, j, 0)),
                  pl.BlockSpec((2, dd // 2, 2 * PAGE_SIZE, 2 * CMP_HIDDEN), lambda i, j: (0, j, 0, 0)),
                  pl.BlockSpec((2, dd, PAGE_SIZE), lambda i, j: (0, j, 0)),
                  pl.BlockSpec((2, 2 * CMP_HIDDEN), lambda i, j: (0, 0)),
                  pl.BlockSpec((2, 2 * CMP_HIDDEN, LANES), lambda i, j: (0, 0, 0))],
        out_specs=pl.BlockSpec((bp, 1, 2 * g2 * LANES), lambda i, j: (i, 0, 0)),
        out_shape=jax.ShapeDtypeStruct((n_phys, 1, 2 * g2 * LANES), F32),
        scratch_shapes=[pltpu.VMEM((2 * g2, bp, 2 * CMP_HIDDEN), F32)],
        compiler_params=_cp(("parallel", "arbitrary")),
        name="compress_pages",
    )(cache_t, wd, ped, b1h, w2h)


def _compress_page_weights(cmp_pe, cmp_w1, cmp_b1, cmp_w2):
    w1t = jnp.transpose(cmp_w1.reshape(2, CMP_BLOCK, HEAD_DIM, CMP_HIDDEN), (0, 2, 1, 3))
    z = jnp.zeros_like(w1t)
    wd = jnp.concatenate([jnp.concatenate([w1t, z], -1), jnp.concatenate([z, w1t], -1)], axis=2).astype(BF16)
    wd = wd.reshape(2, HEAD_DIM // 2, 2 * PAGE_SIZE, 2 * CMP_HIDDEN)
    pet = jnp.transpose(cmp_pe, (0, 2, 1))
    ped = jnp.concatenate([pet, pet], -1)
    b1h = jnp.concatenate([cmp_b1, cmp_b1], -1)
    z2 = jnp.zeros_like(cmp_w2)
    w2h = jnp.concatenate([jnp.concatenate([cmp_w2, z2], -1), jnp.concatenate([z2, cmp_w2], -1)], axis=1).astype(BF16)
    return wd, ped, b1h, w2h


def _split_w_in(w):
    d = w.shape[0]
    off_b = DN_CONV_DIM + DN_WIDTH
    off_q = off_b + 2 * DN_HEADS
    off_kv = off_q + NSA_WIDTH
    off_g = off_kv + KV_COLS
    n_g = 3 * NSA_HEADS
    pad = jnp.zeros((d, P_COLS - P_SMALL - 2 * DN_HEADS - n_g), w.dtype)
    main = jnp.concatenate([w[:, :off_b], w[:, off_q:off_kv], w[:, off_kv:off_kv + CMP_COLS], w[:, off_b:off_q],
                            w[:, off_g:off_g + n_g], pad], axis=1)
    return main.astype(BF16), w[:, off_kv:off_g].T.astype(BF16)


def _row_tile(n, cap):
    t = min(n, cap)
    while n % t:
        t //= 2
    return t


def _trunk_tail(x2, mixer_dn, mixer_nsa, mem_kv3, bshape, lw, final_norm):
    n, d = x2.shape
    b, t = bshape
    sub = 8
    if t >= sub:
        x3 = mem_block(x2.reshape(b, t, d), mixer_dn.reshape(b, t, -1), mixer_nsa.reshape(b, t, -1),
                       lw["w_out_dn"], lw["w_out_nsa"], lw["ln_mem"], lw["w_mem_q"], mem_kv3, lw["w_mem_o"],
                       _row_tile(t, 512))
        x2 = x3.reshape(n, d)
    else:
        assert t == 1
        tm = _row_tile(n, 512)
        x2 = matmul_residual(x2, [mixer_dn, mixer_nsa], [lw["w_out_dn"], lw["w_out_nsa"]], tm)
        qm = rms_matmul(x2, lw["ln_mem"], lw["w_mem_q"], tm, 512)
        att = mem_attention_row(qm.reshape(n, MEM_HEADS, d // MEM_HEADS), mem_kv3).reshape(n, d)
        x2 = matmul_residual(x2, [att], [lw["w_mem_o"]], tm)
    return ffn(x2, lw["ln_ffn"], lw["w_up"], lw["w_down"], lw["ln_final"], final_norm, _row_tile(n, 1024), 512)


def _kv_rows(kvt, lo, hi):
    b, _, s = kvt.shape
    g, dh = NSA_KV_HEADS, HEAD_DIM
    return jnp.transpose(kvt[:, lo * g * dh:hi * g * dh].reshape(b, hi - lo, g, dh, s), (0, 4, 1, 2, 3))


def _prompt_layer(xp, mem_prompt, lw, cw, final_norm):
    b, s, d = xp.shape
    n = b * s
    x2 = xp.reshape(n, d)
    proj, kvt = input_projection(xp, lw["ln_mix"], lw["w_in"], lw["w_kvt"], _row_tile(s, 1024), 1024)
    proj3 = proj.reshape(b, s, P_COLS)
    dn_out, s_pairs = deltanet_prompt(proj3, lw["dn_conv_w"], lw["dn_a_log"], lw["dn_dt_bias"], lw["dn_norm"])
    p_conv = proj3[:, s - (CONV_WIDTH - 1):, P_QKV:P_QKV + DN_CONV_DIM]
    p_rec = _pairs_to_heads(s_pairs)
    nb = s // CMP_BLOCK
    cmp_rows = proj3[:, :nb * CMP_BLOCK, P_CMP:P_CMP + CMP_COLS]
    kcv = compress_blocks(cmp_rows.reshape(b * nb, CMP_BLOCK * CMP_COLS), CMP_COLS, cw, _row_tile(b * nb, 256))
    kcv = kcv.reshape(b, nb, 2, NSA_KV_HEADS, HEAD_DIM)
    kc = jnp.transpose(kcv[:, :, 0], (0, 2, 1, 3))
    vct = jnp.transpose(kcv[:, :, 1], (0, 2, 3, 1))
    ocmpt, selt = nsa_cmp_topk(proj3, kc, vct, _row_tile(s, 512))
    nsa_out = nsa_attention(proj3, kvt, selt, ocmpt, 256, 512).reshape(n, NSA_WIDTH)
    m = mem_prompt.shape[1]
    mem_kv = rms_matmul(mem_prompt.reshape(b * m, d), lw["ln_memkv"], lw["w_mem_kv"], _row_tile(b * m, 512), 512)
    mem_kv5 = mem_kv.reshape(b, m, 2, MEM_HEADS, d // MEM_HEADS)
    y = _trunk_tail(x2, dn_out.reshape(n, DN_WIDTH), nsa_out, mem_kv5, (b, s), lw, final_norm)
    wk = min(WINDOW, s)
    return y.reshape(b, s, d), _kv_rows(kvt, 0, 4), _kv_rows(kvt[:, :, s - wk:], 4, 6), mem_kv5, p_conv, p_rec


def _sample_layer(xs, cache_nsa, cache_win, cache_mem, conv_state, rec_state, page_table, lw, cwp, final_norm):
    db, ds, d = xs.shape
    assert ds == 1
    n = db
    x2 = xs.reshape(n, d)
    proj, kvt = input_projection(x2[None], lw["ln_mix"], lw["w_in"], lw["w_kvt"], n, 1024)
    dn_out, s_rec = deltanet_sample(proj, conv_state, rec_state, lw["dn_conv_w"], lw["dn_a_log"], lw["dn_dt_bias"],
                                    lw["dn_norm"])
    s_conv = jnp.concatenate([conv_state[:, 1:], proj[:, None, P_QKV:P_QKV + DN_CONV_DIM]], axis=1)
    n_phys = cache_nsa.shape[0]
    cache_t = jnp.transpose(cache_nsa, (0, 2, 3, 4, 1))
    win_t = jnp.transpose(cache_win, (0, 2, 3, 4, 1))
    kc_phys = compress_pages(cache_t, cwp, _row_tile(n_phys, 256))
    kv_new = kvt[0].T
    o8 = nsa_decode(page_table, proj[:, P_NQ:P_NQ + NSA_WIDTH].reshape(n, NSA_HEADS, HEAD_DIM),
                    kv_new.reshape(n, 1, KV_COLS),
                    _flat_gates(proj[:, P_SMALL:P_SMALL + LANES]).reshape(n, 1, LANES),
                    win_t, kc_phys, cache_t)
    kv6 = kv_new.reshape(n, 1, NSA_KV_KINDS, NSA_KV_HEADS, HEAD_DIM)
    s_nsa = jnp.transpose(kvt.reshape(NSA_KV_KINDS, NSA_KV_HEADS, HEAD_DIM, n)[:4], (3, 0, 1, 2))[:, None]
    s_win = jnp.concatenate([cache_win, kv6[:, :, 4:]], axis=1)[:, 1:]
    y = _trunk_tail(x2, dn_out, o8.reshape(n, NSA_WIDTH), cache_mem, (n, 1), lw, final_norm)
    return y.reshape(db, ds, d), s_nsa, s_win, s_conv, s_rec


def _flat_gates(small):
    g0 = 2 * DN_HEADS
    n_g = 3 * NSA_HEADS
    gl = small[..., g0:g0 + n_g]
    return jnp.concatenate([gl, jnp.zeros(gl.shape[:-1] + (LANES - n_g,), gl.dtype)], axis=-1)


def kernel(x_prompt, x_sample, mem_prompt, cache_nsa_kv, cache_win_kv, cache_mem_kv, state_dn_conv, state_dn_rec, page_table, ln_mix, w_in, dn_conv_w, dn_a_log, dn_dt_bias, dn_norm, cmp_pe, cmp_w1, cmp_b1, cmp_w2, w_out, ln_mem, ln_memkv, w_mem_q, w_mem_kv, w_mem_o, ln_ffn, w_up, w_down, ln_final):
    depth = w_in.shape[0]
    xp, xs = x_prompt, x_sample
    outs_p = [[] for _ in range(5)]
    outs_s = [[] for _ in range(4)]
    for l in range(depth):
        lw = {
            "ln_mix": ln_mix[l],
            "dn_conv_w": dn_conv_w[l], "dn_a_log": dn_a_log[l], "dn_dt_bias": dn_dt_bias[l], "dn_norm": dn_norm[l],
            "w_out_dn": w_out[l][:DN_WIDTH].astype(BF16), "w_out_nsa": w_out[l][DN_WIDTH:].astype(BF16),
            "ln_mem": ln_mem[l], "ln_memkv": ln_memkv[l], "w_mem_q": w_mem_q[l].astype(BF16),
            "w_mem_kv": w_mem_kv[l].astype(BF16), "w_mem_o": w_mem_o[l].astype(BF16),
            "ln_ffn": ln_ffn[l], "w_up": w_up[l].astype(BF16), "w_down": w_down[l].astype(BF16),
            "ln_final": ln_final,
        }
        lw["w_in"], lw["w_kvt"] = _split_w_in(w_in[l])
        cw = _compress_weights(cmp_pe[l], cmp_w1[l], cmp_b1[l], cmp_w2[l])
        last = l == depth - 1
        xp, p_nsa, p_win, p_mem, p_conv, p_rec = _prompt_layer(xp, mem_prompt, lw, cw, last)
        for acc, val in zip(outs_p, (p_nsa, p_win, p_mem, p_conv, p_rec)):
            acc.append(val)
        xs, s_nsa, s_win, s_conv, s_rec = _sample_layer(
            xs, cache_nsa_kv[l], cache_win_kv[l], cache_mem_kv[l], state_dn_conv[l], state_dn_rec[l],
            page_table, lw, _compress_page_weights(cmp_pe[l], cmp_w1[l], cmp_b1[l], cmp_w2[l]), last)
        for acc, val in zip(outs_s, (s_nsa, s_win, s_conv, s_rec)):
            acc.append(val)
    return (xp, xs) + tuple(jnp.stack(a) for a in outs_p) + tuple(jnp.stack(a) for a in outs_s)
```

```python
import functools

import jax
import jax.numpy as jnp
from jax import lax
from jax.experimental import pallas as pl
from jax.experimental.pallas import tpu as pltpu

F32 = jnp.float32
BF16 = jnp.bfloat16

HEAD_DIM = 64
DN_HEADS = 8
NSA_HEADS = 8
NSA_KV_HEADS = 2
NSA_GROUP = NSA_HEADS // NSA_KV_HEADS
DN_WIDTH = DN_HEADS * HEAD_DIM
NSA_WIDTH = NSA_HEADS * HEAD_DIM
CONV_WIDTH = 4
DN_CONV_DIM = 3 * DN_WIDTH
DN_CHUNK = 64
CMP_BLOCK = 64
SEL_BLOCK = 64
TOP_N = 16
N_LOCAL_BLOCKS = 2
WINDOW = 512
CMP_HIDDEN = 128
NSA_KV_KINDS = 6
MEM_HEADS = 4
PAGE_SIZE = 128
RMS_EPS = 1e-6
FORCE_SCORE = 1e3
NEG_INF = -1e30
ATTN_SCALE = HEAD_DIM ** -0.5
LOG2_E = 1.4426950408889634

LANES = 128
SUBLANES = 8
P_QKV = 0
P_Z = P_QKV + DN_CONV_DIM
P_NQ = P_Z + DN_WIDTH
P_CMP = P_NQ + NSA_WIDTH
CMP_COLS = 2 * NSA_KV_HEADS * HEAD_DIM
P_SMALL = P_CMP + CMP_COLS
P_COLS = P_SMALL + 2 * LANES
KV_COLS = NSA_KV_KINDS * NSA_KV_HEADS * HEAD_DIM
VMEM_LIMIT = 56 * 1024 * 1024
DN_APPLY_PASSES = (3, 3, 3, 1, 1, 1)
DN_SQUARE_PASSES = (3, 3, 1, 1, 1)


def _cp(sem, vmem=VMEM_LIMIT):
    return pltpu.CompilerParams(dimension_semantics=sem, vmem_limit_bytes=vmem)


def _split2(a):
    hi = a.astype(BF16)
    return hi, (a - hi.astype(F32)).astype(BF16)


def _dot(a, b, passes=1):
    if passes == 1:
        return jnp.dot(a.astype(BF16), b.astype(BF16), preferred_element_type=F32)
    ah, al = _split2(a)
    bh, bl = _split2(b)
    return (jnp.dot(ah, bh, preferred_element_type=F32) + jnp.dot(ah, bl, preferred_element_type=F32)
            + jnp.dot(al, bh, preferred_element_type=F32))


def _dot_nt(a, b):
    return lax.dot_general(a.astype(BF16), b.astype(BF16), (((1,), (1,)), ((), ())),
                           preferred_element_type=F32)


def _split3(a):
    hi = a.astype(BF16)
    r1 = a - hi.astype(F32)
    mid = r1.astype(BF16)
    lo = (r1 - mid.astype(F32)).astype(BF16)
    return hi, mid, lo


def _dot_exact_lhs01(a01, b):
    a = a01.astype(BF16)
    hi, mid, lo = _split3(b)
    return (jnp.dot(a, hi, preferred_element_type=F32) + jnp.dot(a, mid, preferred_element_type=F32)
            + jnp.dot(a, lo, preferred_element_type=F32))


def _dot_exact_rhs01(a, b01):
    b = b01.astype(BF16)
    hi, mid, lo = _split3(a)
    return (jnp.dot(hi, b, preferred_element_type=F32) + jnp.dot(mid, b, preferred_element_type=F32)
            + jnp.dot(lo, b, preferred_element_type=F32))


def _rms(x, gain):
    ms = jnp.mean(x * x, axis=-1, keepdims=True)
    return x * lax.rsqrt(ms + RMS_EPS) * gain


def _softplus(x):
    return jnp.maximum(x, 0.0) + jnp.log1p(jnp.exp(-jnp.abs(x)))


def _silu(x):
    return x * jax.nn.sigmoid(x)


def _rms_mm_kernel(x_ref, g_ref, w_ref, o_ref, h_ref):
    @pl.when(pl.program_id(1) == 0)
    def _():
        h_ref[...] = _rms(x_ref[...], g_ref[...]).astype(BF16)

    o_ref[...] = jnp.dot(h_ref[...], w_ref[...], preferred_element_type=F32)


def rms_matmul(x, gain, w_bf16, tm, tn):
    n, d = x.shape
    m = w_bf16.shape[1]
    assert n % tm == 0 and m % tn == 0
    return pl.pallas_call(
        _rms_mm_kernel,
        grid=(n // tm, m // tn),
        in_specs=[pl.BlockSpec((tm, d), lambda i, j: (i, 0)),
                  pl.BlockSpec((1, d), lambda i, j: (0, 0)),
                  pl.BlockSpec((d, tn), lambda i, j: (0, j))],
        out_specs=pl.BlockSpec((tm, tn), lambda i, j: (i, j)),
        out_shape=jax.ShapeDtypeStruct((n, m), F32),
        scratch_shapes=[pltpu.VMEM((tm, d), BF16)],
        compiler_params=_cp(("parallel", "arbitrary")),
        name="rms_matmul",
    )(x, gain.reshape(1, d), w_bf16)


def _proj_kernel(x_ref, g_ref, w_ref, wkv_ref, o_ref, okv_ref, h_ref):
    @pl.when(pl.program_id(1) == 0)
    def _():
        h = _rms(x_ref[...], g_ref[...]).astype(BF16)
        h_ref[...] = h
        okv_ref[0] = lax.dot_general(wkv_ref[...], h, (((1,), (1,)), ((), ())), preferred_element_type=F32)

    o_ref[...] = jnp.dot(h_ref[...], w_ref[...], preferred_element_type=F32)


def input_projection(x3, gain, w_main, w_kvt, tm, tn):
    b, s, d = x3.shape
    n = b * s
    m = w_main.shape[1]
    kvc = w_kvt.shape[0]
    assert s % tm == 0 and m % tn == 0
    spt = s // tm
    return pl.pallas_call(
        _proj_kernel,
        grid=(n // tm, m // tn),
        in_specs=[pl.BlockSpec((tm, d), lambda i, j: (i, 0)),
                  pl.BlockSpec((1, d), lambda i, j: (0, 0)),
                  pl.BlockSpec((d, tn), lambda i, j: (0, j)),
                  pl.BlockSpec((kvc, d), lambda i, j: (0, 0))],
        out_specs=[pl.BlockSpec((tm, tn), lambda i, j: (i, j)),
                   pl.BlockSpec((1, kvc, tm), lambda i, j: (i // spt, 0, i % spt))],
        out_shape=[jax.ShapeDtypeStruct((n, m), F32), jax.ShapeDtypeStruct((b, kvc, s), F32)],
        scratch_shapes=[pltpu.VMEM((tm, d), BF16)],
        compiler_params=_cp(("parallel", "arbitrary")),
        name="input_projection",
    )(x3.reshape(n, d), gain.reshape(1, d), w_main, w_kvt)


def _mm_res_kernel(*refs, n_in):
    res_ref = refs[0]
    a_refs = refs[1:1 + n_in]
    w_refs = refs[1 + n_in:1 + 2 * n_in]
    o_ref = refs[1 + 2 * n_in]
    acc = res_ref[...]
    for a_ref, w_ref in zip(a_refs, w_refs):
        acc = acc + jnp.dot(a_ref[...].astype(BF16), w_ref[...], preferred_element_type=F32)
    o_ref[...] = acc


def matmul_residual(res, a_list, w_list, tm):
    n, d = res.shape
    assert n % tm == 0
    n_in = len(a_list)
    in_specs = [pl.BlockSpec((tm, d), lambda i: (i, 0))]
    in_specs += [pl.BlockSpec((tm, a.shape[1]), lambda i: (i, 0)) for a in a_list]
    in_specs += [pl.BlockSpec(w.shape, lambda i: (0, 0)) for w in w_list]
    return pl.pallas_call(
        functools.partial(_mm_res_kernel, n_in=n_in),
        grid=(n // tm,),
        in_specs=in_specs,
        out_specs=pl.BlockSpec((tm, d), lambda i: (i, 0)),
        out_shape=jax.ShapeDtypeStruct((n, d), F32),
        compiler_params=_cp(("parallel",)),
        name="matmul_residual",
    )(res, *a_list, *w_list)


def _ffn_kernel(x_ref, g_ref, wu_ref, wd_ref, gf_ref, o_ref, hn_ref, acc_ref, *, final_norm):
    j = pl.program_id(1)

    @pl.when(j == 0)
    def _():
        x = x_ref[...]
        hn_ref[...] = _rms(x, g_ref[...]).astype(BF16)
        acc_ref[...] = x

    u = jnp.dot(hn_ref[...], wu_ref[...], preferred_element_type=F32)
    u = jnp.square(jnp.maximum(u, 0.0)).astype(BF16)
    acc_ref[...] += jnp.dot(u, wd_ref[...], preferred_element_type=F32)

    @pl.when(j == pl.num_programs(1) - 1)
    def _():
        y = acc_ref[...]
        if final_norm:
            y = _rms(y, gf_ref[...])
        o_ref[...] = y


def ffn(x, gain, wu_bf16, wd_bf16, gain_final, final_norm, tm, tf):
    n, d = x.shape
    f = wu_bf16.shape[1]
    assert n % tm == 0 and f % tf == 0
    return pl.pallas_call(
        functools.partial(_ffn_kernel, final_norm=final_norm),
        grid=(n // tm, f // tf),
        in_specs=[pl.BlockSpec((tm, d), lambda i, j: (i, 0)),
                  pl.BlockSpec((1, d), lambda i, j: (0, 0)),
                  pl.BlockSpec((d, tf), lambda i, j: (0, j)),
                  pl.BlockSpec((tf, d), lambda i, j: (j, 0)),
                  pl.BlockSpec((1, d), lambda i, j: (0, 0))],
        out_specs=pl.BlockSpec((tm, d), lambda i, j: (i, 0)),
        out_shape=jax.ShapeDtypeStruct((n, d), F32),
        scratch_shapes=[pltpu.VMEM((tm, d), BF16), pltpu.VMEM((tm, d), F32)],
        compiler_params=_cp(("parallel", "arbitrary")),
        name="ffn",
    )(x, gain.reshape(1, d), wu_bf16, wd_bf16, gain_final.reshape(1, d))


def _mem_attn_row_kernel(q_ref, kv_ref, o_ref, *, m, seqs, heads):
    sub = SUBLANES
    ones = jnp.ones((LANES, LANES), BF16)
    for u in range(seqs):
        x = kv_ref[u].reshape(m, 2 * sub, LANES)
        prod = x[:, 0:sub] * q_ref[u][None]
        part = jnp.dot(prod.reshape(m * sub, LANES).astype(BF16), ones, preferred_element_type=F32)
        part = part.reshape(m, sub, LANES)
        s = part + pltpu.roll(part, heads, 1)
        p = jnp.exp2(s - jnp.max(s, axis=0, keepdims=True))
        o_ref[u] = jnp.sum(p * x[:, sub:2 * sub], axis=0) / jnp.sum(p, axis=0)


def mem_attention_row(q, kv, seqs=4):
    b, h, hd = q.shape
    m = kv.shape[1]
    assert kv.shape[2:] == (2, h, hd)
    tiles = hd // LANES
    assert tiles * LANES == hd and tiles == 2 and tiles * h == SUBLANES and b % seqs == 0
    rows = jnp.transpose(kv.reshape(b, m, 2, h, tiles, LANES), (0, 1, 2, 4, 3, 5)).reshape(b, m * 2 * SUBLANES, LANES)
    q8 = jnp.transpose((q * (hd ** -0.5 * LOG2_E)).reshape(b, h, tiles, LANES), (0, 2, 1, 3)).reshape(b, SUBLANES, LANES)
    o8 = pl.pallas_call(
        functools.partial(_mem_attn_row_kernel, m=m, seqs=seqs, heads=h),
        grid=(b // seqs,),
        in_specs=[pl.BlockSpec((seqs, SUBLANES, LANES), lambda i: (i, 0, 0)),
                  pl.BlockSpec((seqs, m * 2 * SUBLANES, LANES), lambda i: (i, 0, 0))],
        out_specs=pl.BlockSpec((seqs, SUBLANES, LANES), lambda i: (i, 0, 0)),
        out_shape=jax.ShapeDtypeStruct((b, SUBLANES, LANES), F32),
        compiler_params=_cp(("parallel",)),
        name="mem_attention_row",
    )(q8, rows)
    return jnp.transpose(o8.reshape(b, tiles, h, LANES), (0, 2, 1, 3)).reshape(b, h, hd)


def _mem_block_kernel(x_ref, a1_ref, a2_ref, w1_ref, w2_ref, g_ref, wq_ref, kv_ref, wo_ref, o_ref, *, heads, hd):
    scale = hd ** -0.5
    x = (x_ref[0] + jnp.dot(a1_ref[0].astype(BF16), w1_ref[...], preferred_element_type=F32)
         + jnp.dot(a2_ref[0].astype(BF16), w2_ref[...], preferred_element_type=F32))
    q = jnp.dot(_rms(x, g_ref[...]).astype(BF16), wq_ref[...], preferred_element_type=F32)
    kt = jnp.swapaxes(kv_ref[0, :, 0], 0, 1)
    vt = jnp.swapaxes(kv_ref[0, :, 1], 0, 1)
    outs = []
    for h in range(heads):
        s = _dot_nt(q[:, h * hd:(h + 1) * hd], kt[h]) * scale
        p = jnp.exp(s - jnp.max(s, axis=-1, keepdims=True))
        p = p / jnp.sum(p, axis=-1, keepdims=True)
        outs.append(_dot(p, vt[h]))
    att = jnp.concatenate(outs, axis=1).astype(BF16)
    o_ref[0] = x + jnp.dot(att, wo_ref[...], preferred_element_type=F32)


def mem_block(x3, a1, a2, w1, w2, gain, wq, kv, wo, tq):
    b, t, d = x3.shape
    m = kv.shape[1]
    hd = d // MEM_HEADS
    assert t % tq == 0 and kv.shape[2:] == (2, MEM_HEADS, hd)
    return pl.pallas_call(
        functools.partial(_mem_block_kernel, heads=MEM_HEADS, hd=hd),
        grid=(b, t // tq),
        in_specs=[pl.BlockSpec((1, tq, d), lambda i, j: (i, j, 0)),
                  pl.BlockSpec((1, tq, a1.shape[2]), lambda i, j: (i, j, 0)),
                  pl.BlockSpec((1, tq, a2.shape[2]), lambda i, j: (i, j, 0)),
                  pl.BlockSpec(w1.shape, lambda i, j: (0, 0)),
                  pl.BlockSpec(w2.shape, lambda i, j: (0, 0)),
                  pl.BlockSpec((1, d), lambda i, j: (0, 0)),
                  pl.BlockSpec((d, d), lambda i, j: (0, 0)),
                  pl.BlockSpec((1, m, 2, MEM_HEADS, hd), lambda i, j: (i, 0, 0, 0, 0)),
                  pl.BlockSpec((d, d), lambda i, j: (0, 0))],
        out_specs=pl.BlockSpec((1, tq, d), lambda i, j: (i, j, 0)),
        out_shape=jax.ShapeDtypeStruct((b, t, d), F32),
        compiler_params=_cp(("parallel", "parallel")),
        name="mem_block",
    )(x3, a1, a2, w1, w2, gain.reshape(1, d), wq, kv, wo)


def _compress_kernel(x_ref, w1_ref, pe_ref, b1_ref, w2_ref, o_ref, acc_ref, *, row_cols, tt):
    j = pl.program_id(1)

    @pl.when(j == 0)
    def _():
        acc_ref[...] = jnp.zeros_like(acc_ref)

    hid2 = 2 * CMP_HIDDEN
    for kind in range(2):
        part = None
        for t in range(tt):
            c0 = t * row_cols + kind * LANES
            xs = x_ref[:, c0:c0 + LANES] + pe_ref[kind, t:t + 1, :]
            d = jnp.dot(xs.astype(BF16), w1_ref[kind, t], preferred_element_type=F32)
            part = d if part is None else part + d
        acc_ref[:, kind * hid2:(kind + 1) * hid2] += part

    @pl.when(j == pl.num_programs(1) - 1)
    def _():
        h = jnp.maximum(acc_ref[...] + b1_ref[...], 0.0)
        for kind in range(2):
            o_ref[:, kind * LANES:(kind + 1) * LANES] = jnp.dot(
                h[:, kind * hid2:(kind + 1) * hid2].astype(BF16), w2_ref[kind], preferred_element_type=F32)


def compress_blocks(x2, row_cols, cw, bt, tt=8):
    nb = x2.shape[0]
    assert nb % bt == 0 and CMP_BLOCK % tt == 0
    w1bd, pe2, b1bd, w2bd = cw
    return pl.pallas_call(
        functools.partial(_compress_kernel, row_cols=row_cols, tt=tt),
        grid=(nb // bt, CMP_BLOCK // tt),
        in_specs=[pl.BlockSpec((bt, tt * row_cols), lambda i, j: (i, j)),
                  pl.BlockSpec((2, tt, LANES, 2 * CMP_HIDDEN), lambda i, j: (0, j, 0, 0)),
                  pl.BlockSpec((2, tt, LANES), lambda i, j: (0, j, 0)),
                  pl.BlockSpec((1, 4 * CMP_HIDDEN), lambda i, j: (0, 0)),
                  pl.BlockSpec((2, 2 * CMP_HIDDEN, LANES), lambda i, j: (0, 0, 0))],
        out_specs=pl.BlockSpec((bt, 2 * LANES), lambda i, j: (i, 0)),
        out_shape=jax.ShapeDtypeStruct((nb, 2 * LANES), F32),
        scratch_shapes=[pltpu.VMEM((bt, 4 * CMP_HIDDEN), F32)],
        compiler_params=_cp(("parallel", "arbitrary")),
        name="compress_blocks",
    )(x2, w1bd, pe2, b1bd, w2bd)


def _compress_weights(cmp_pe, cmp_w1, cmp_b1, cmp_w2):
    w1r = cmp_w1.reshape(2, CMP_BLOCK, HEAD_DIM, CMP_HIDDEN)
    z = jnp.zeros_like(w1r)
    w1bd = jnp.concatenate([jnp.concatenate([w1r, z], -1), jnp.concatenate([z, w1r], -1)], axis=2).astype(BF16)
    pe2 = jnp.concatenate([cmp_pe, cmp_pe], -1)
    b1bd = jnp.concatenate([cmp_b1[0], cmp_b1[0], cmp_b1[1], cmp_b1[1]]).reshape(1, 4 * CMP_HIDDEN)
    z2 = jnp.zeros_like(cmp_w2)
    w2bd = jnp.concatenate([jnp.concatenate([cmp_w2, z2], -1), jnp.concatenate([z2, cmp_w2], -1)], axis=1).astype(BF16)
    return w1bd, pe2, b1bd, w2bd


def _masked_softmax(s, mask):
    s = jnp.where(mask, s, NEG_INF)
    p = jnp.where(mask, jnp.exp(s - jnp.max(s, axis=-1, keepdims=True)), 0.0)
    return p / jnp.maximum(jnp.sum(p, axis=-1, keepdims=True), 1e-30)


def _heads_as_rows(q_ref, tq, scale=ATTN_SCALE):
    qb = q_ref[0] * scale
    return jnp.concatenate([qb[:, h * HEAD_DIM:(h + 1) * HEAD_DIM] for h in range(NSA_GROUP)], axis=0).astype(BF16)


def _cmp_topk_kernel(q_ref, kc_ref, vct_ref, ocmp_ref, sel_ref, *, tq, nblk, topn):
    qi = pl.program_id(2)
    r, dh = NSA_GROUP, HEAD_DIM
    qpos = qi * tq + lax.broadcasted_iota(jnp.int32, (nblk, tq), 1)
    blk = lax.broadcasted_iota(jnp.int32, (nblk, tq), 0)
    vis = (blk + 1) * CMP_BLOCK - 1 <= qpos
    q = _heads_as_rows(q_ref, tq)
    kc = kc_ref[0, 0]
    vct = vct_ref[0, 0]
    imp = jnp.zeros((nblk, tq), F32)
    for h in range(r):
        s = jnp.where(vis, _dot_nt(kc, q[h * tq:(h + 1) * tq]), NEG_INF)
        p = jnp.where(vis, jnp.exp(s - jnp.max(s, axis=0, keepdims=True)), 0.0)
        p = p / jnp.maximum(jnp.sum(p, axis=0, keepdims=True), 1e-30)
        ocmp_ref[0, 0, h * dh:(h + 1) * dh, :] = _dot(vct, p)
        imp = imp + p
    cur = lax.shift_right_logical(qpos, 6)
    valid = blk <= cur
    forced = valid & ((blk == 0) | (cur - blk < N_LOCAL_BLOCKS))
    score = jnp.where(valid, imp + jnp.where(forced, FORCE_SCORE, 0.0), -1.0)
    rank = jnp.zeros((nblk, tq), F32)
    for i in range(nblk):
        si = score[i:i + 1, :]
        rank = rank + jnp.where((si > score) | ((si == score) & (blk > i)), 1.0, 0.0)
    sel_ref[0, 0] = jnp.where(rank < topn, 1.0, 0.0).astype(BF16)


def nsa_cmp_topk(proj3, kc, vct, tq):
    b, sq, _ = proj3.shape
    g = kc.shape[1]
    nblk = kc.shape[2]
    dh = HEAD_DIM
    gw = NSA_GROUP * dh
    assert sq % tq == 0 and SEL_BLOCK == 64 and P_NQ % gw == 0
    return pl.pallas_call(
        functools.partial(_cmp_topk_kernel, tq=tq, nblk=nblk, topn=min(TOP_N, nblk)),
        grid=(b, g, sq // tq),
        in_specs=[pl.BlockSpec((1, tq, gw), lambda i, j, k: (i, k, P_NQ // gw + j)),
                  pl.BlockSpec((1, 1, nblk, dh), lambda i, j, k: (i, j, 0, 0)),
                  pl.BlockSpec((1, 1, dh, nblk), lambda i, j, k: (i, j, 0, 0))],
        out_specs=[pl.BlockSpec((1, 1, gw, tq), lambda i, j, k: (i, j, 0, k)),
                   pl.BlockSpec((1, 1, nblk, tq), lambda i, j, k: (i, j, 0, k))],
        out_shape=[jax.ShapeDtypeStruct((b, g, gw, sq), F32),
                   jax.ShapeDtypeStruct((b, g, nblk, sq), BF16)],
        compiler_params=_cp(("parallel", "parallel", "parallel")),
        name="nsa_cmp_topk",
    )(proj3, kc, vct)


def _nsa_attn_kernel(q_ref, kst_ref, vst_ref, kwt_ref, vwt_ref, selt_ref, ocmp_ref, sm_ref, o_ref,
                     m_sc, acc_sc, s_sc, bias_sc, p_sc, *, tq, tk, nsel, ck):
    grp = pl.program_id(1)
    qi = pl.program_id(2)
    r = NSA_GROUP
    dh = HEAD_DIM
    q = _heads_as_rows(q_ref, tq, ATTN_SCALE * LOG2_E)
    qpos = qi * tq + lax.broadcasted_iota(jnp.int32, (1, tq), 1)
    ones_row = jnp.where(lax.broadcasted_iota(jnp.int32, (8, 1), 0) == 0, 1.0, 0.0)

    def reset():
        m_sc[...] = jnp.full(m_sc.shape, NEG_INF, F32)
        acc_sc[...] = jnp.zeros(acc_sc.shape, F32)

    def with_ones(vt):
        return jnp.concatenate([vt, jnp.broadcast_to(ones_row, (8, vt.shape[1]))], axis=0).astype(BF16)

    def scores(kt, qrows, row0):
        s_sc[row0:row0 + kt.shape[1], :] = _dot_nt(kt.T, qrows)

    def set_bias(mask, row0):
        bias_sc[pl.ds(row0, mask.shape[0]), :] = jnp.where(mask, 0.0, NEG_INF)

    def fold(row0, n, vt, biased):
        def chunk(c, shift):
            rows = pl.ds(row0 + c * ck, ck)
            x = s_sc[rows, :] - shift
            return x + jnp.concatenate([bias_sc[rows, :]] * r, axis=1) if biased else x

        m_prev = m_sc[...]
        top = jnp.full((SUBLANES, r * tq), NEG_INF, F32)
        for c in range(n // ck):
            top = jnp.maximum(top, jnp.max(chunk(c, 0.0).reshape(ck // SUBLANES, SUBLANES, r * tq), axis=0))
        m_new = jnp.maximum(m_prev, jnp.max(top, axis=0, keepdims=True))
        for c in range(n // ck):
            p_sc[c * ck:(c + 1) * ck, :] = jnp.exp2(chunk(c, m_new)).astype(BF16)
        acc_sc[...] = jnp.exp2(m_prev - m_new) * acc_sc[...] + jnp.dot(with_ones(vt), p_sc[0:n, :],
                                                                      preferred_element_type=F32)
        m_sc[...] = m_new

    def result():
        acc = acc_sc[...]
        return acc[0:dh] / jnp.maximum(acc[dh:dh + 1], 1e-30)

    wk = WINDOW + tq
    w0 = pl.multiple_of(jnp.maximum(qi - WINDOW // tq, 0) * tq, tq)
    dpos = qpos - (w0 + lax.broadcasted_iota(jnp.int32, (wk, 1), 0))
    set_bias((dpos >= 0) & (dpos < WINDOW), 0)
    scores(kwt_ref[0, :, pl.ds(w0, wk)], q, 0)

    reset()
    block_bias = jnp.where(selt_ref[0, 0].astype(F32).T > 0.5, 0.0, NEG_INF).astype(BF16)
    q_sel = jnp.concatenate([q, jnp.concatenate([block_bias] * r, axis=0)], axis=1)
    blk = lax.broadcasted_iota(jnp.int32, (nsel, tk), 0)
    krow = lax.broadcasted_iota(jnp.int32, (nsel, tk), 1)

    def slc_scores(k0, row0):
        code = jnp.where(lax.shift_right_logical(k0 + krow, CMP_BLOCK.bit_length() - 1) == blk, 1.0, 0.0)
        scores(jnp.concatenate([kst_ref[0, :, pl.ds(k0, tk)], code], axis=0), q_sel, row0)

    def slc_fold(k0, row0, biased):
        fold(row0, tk, vst_ref[0, :, pl.ds(k0, tk)], biased)

    def slc_pair(c, carry):
        ka = pl.multiple_of(2 * c * tk, tk)
        slc_scores(ka + tk, wk + tk)
        slc_fold(ka, wk, False)
        slc_scores(ka + 2 * tk, wk)
        slc_fold(ka + tk, wk + tk, False)
        return carry

    n_below = (qi * tq) // tk
    kd = pl.multiple_of(n_below * tk, tk)
    odd = n_below % 2
    slc_scores(0, wk)
    lax.fori_loop(0, n_below // 2, slc_pair, 0)

    @pl.when(odd == 1)
    def _():
        slc_scores(kd, wk + tk)
        slc_fold(kd - tk, wk, False)

    row_d = pl.multiple_of(wk + odd * tk, ck)
    set_bias(kd + lax.broadcasted_iota(jnp.int32, (tk, 1), 0) <= qpos, row_d)
    slc_fold(kd, row_d, True)
    o_slc = result()

    reset()
    fold(0, wk, vwt_ref[0, :, pl.ds(w0, wk)], True)
    o_win = result()

    gates_t = jax.nn.sigmoid(sm_ref[0]).T
    g0 = 2 * DN_HEADS
    per = 3 * r
    gt = jnp.where(grp == 0, gates_t[g0:g0 + per], gates_t[g0 + per:g0 + 2 * per])
    outs = []
    for h in range(r):
        outs.append(gt[3 * h:3 * h + 1] * ocmp_ref[0, 0, h * dh:(h + 1) * dh, :]
                    + gt[3 * h + 1:3 * h + 2] * o_slc[:, h * tq:(h + 1) * tq]
                    + gt[3 * h + 2:3 * h + 3] * o_win[:, h * tq:(h + 1) * tq])
    o_ref[0] = jnp.concatenate(outs, axis=0).T


def nsa_attention(proj3, kvt, selt, ocmpt, tq, tk):
    b, sq, _ = proj3.shape
    t = kvt.shape[2]
    g, nsel = selt.shape[1], selt.shape[2]
    r, dh = NSA_GROUP, HEAD_DIM
    gw = r * dh
    assert sq == t and sq % tq == 0 and t % tk == 0 and tk % tq == 0 and WINDOW % tq == 0 and tq % LANES == 0
    assert t >= WINDOW + tq and g == NSA_KV_HEADS and g == 2
    ck = 32
    assert tk % ck == 0 and (WINDOW + tq) % ck == 0 and tk <= WINDOW + tq
    rows = WINDOW + tq + 2 * tk

    def kv_spec(kind):
        return pl.BlockSpec((1, dh, t), lambda i, j, k: (i, kind * g + j, 0))

    return pl.pallas_call(
        functools.partial(_nsa_attn_kernel, tq=tq, tk=tk, nsel=nsel, ck=ck),
        grid=(b, g, sq // tq),
        in_specs=[pl.BlockSpec((1, tq, gw), lambda i, j, k: (i, k, P_NQ // gw + j)),
                  kv_spec(2), kv_spec(3), kv_spec(4), kv_spec(5),
                  pl.BlockSpec((1, 1, nsel, tq), lambda i, j, k: (i, j, 0, k)),
                  pl.BlockSpec((1, 1, gw, tq), lambda i, j, k: (i, j, 0, k)),
                  pl.BlockSpec((1, tq, LANES), lambda i, j, k: (i, k, P_SMALL // LANES))],
        out_specs=pl.BlockSpec((1, tq, gw), lambda i, j, k: (i, k, j)),
        out_shape=jax.ShapeDtypeStruct((b, sq, g * gw), F32),
        scratch_shapes=[pltpu.VMEM((1, r * tq), F32), pltpu.VMEM((dh + 8, r * tq), F32),
                        pltpu.VMEM((rows, r * tq), F32), pltpu.VMEM((rows, tq), F32),
                        pltpu.VMEM((WINDOW + tq, r * tq), BF16)],
        compiler_params=_cp(("parallel", "parallel", "parallel")),
        name="nsa_attention",
    )(proj3, kvt, kvt, kvt, kvt, selt, ocmpt, proj3)


def _bdot(a, b, passes=1):
    dims = (((2,), (1,)), ((0,), (0,)))
    if passes == 1:
        return lax.dot_general(a.astype(BF16), b.astype(BF16), dims, preferred_element_type=F32)
    ah, al = _split2(a)
    bh, bl = _split2(b)
    return (lax.dot_general(ah, bh, dims, preferred_element_type=F32)
            + lax.dot_general(ah, bl, dims, preferred_element_type=F32)
            + lax.dot_general(al, bh, dims, preferred_element_type=F32))


def _bdot_nt(a, b):
    return lax.dot_general(a.astype(BF16), b.astype(BF16), (((2,), (2,)), ((0,), (0,))),
                           preferred_element_type=F32)


def _deltanet_kernel(qkv_ref, z_ref, sm_ref, cw_ref, alog_ref, dtb_ref, gn_ref, o_ref, s_out_ref,
                     xbuf, s_sc):
    c = pl.program_id(0)
    ch = DN_CHUNK
    n_pairs = DN_HEADS // 2
    two = 2 * ch
    n_batch = qkv_ref.shape[0]

    @pl.when(c == 0)
    def _():
        xbuf[:, 0:8, :] = jnp.zeros((n_batch, 8, DN_CONV_DIM), F32)
        s_sc[...] = jnp.zeros_like(s_sc)

    ti = lax.broadcasted_iota(jnp.int32, (ch, ch), 0)
    tj = lax.broadcasted_iota(jnp.int32, (ch, ch), 1)
    tri = jnp.where(ti >= tj, 1.0, 0.0)
    lane = lax.broadcasted_iota(jnp.int32, (ch, LANES), 1)
    lo = lane < HEAD_DIM
    row2 = lax.broadcasted_iota(jnp.int32, (two, two), 0)
    col2 = lax.broadcasted_iota(jnp.int32, (two, two), 1)
    same = (row2 >= ch) == (col2 >= ch)
    incl = (same & (row2 >= col2))[None]
    strict = (same & (row2 > col2))[None]
    top = lax.broadcasted_iota(jnp.int32, (two, 1), 0) < ch

    def seg_sum(x):
        s_lo = jnp.sum(jnp.where(lo, x, 0.0), axis=-1, keepdims=True)
        s_hi = jnp.sum(jnp.where(lo, 0.0, x), axis=-1, keepdims=True)
        return jnp.where(lo, s_lo, s_hi)

    def stack2(x):
        return jnp.concatenate([jnp.where(lo, x, 0.0), jnp.where(lo, 0.0, x)], axis=0)

    def col2x(a, b):
        return jnp.concatenate([jnp.broadcast_to(a, (ch, LANES)), jnp.broadcast_to(b, (ch, LANES))], axis=0)

    q_l, k_l, v_l, beta_l, gc_l, gl_l = [], [], [], [], [], []
    for bi in range(n_batch):
        xbuf[bi, 8:8 + ch, :] = qkv_ref[bi]
        conv = None
        for w in range(CONV_WIDTH):
            term = xbuf[bi, 5 + w:5 + w + ch, :] * cw_ref[w:w + 1, :]
            conv = term if conv is None else conv + term
        xbuf[bi, 0:8, :] = xbuf[bi, ch:ch + 8, :]
        act = _silu(conv)
        sm = sm_ref[bi]
        beta_all = jax.nn.sigmoid(sm)
        g_all = -jnp.exp(alog_ref[...]) * _softplus(sm + dtb_ref[...])
        gcum_all = _dot_exact_lhs01(tri, g_all)
        for p in range(n_pairs):
            c0 = p * LANES
            qp = act[:, c0:c0 + LANES]
            kp = act[:, DN_WIDTH + c0:DN_WIDTH + c0 + LANES]
            vp = act[:, 2 * DN_WIDTH + c0:2 * DN_WIDTH + c0 + LANES]
            qp = qp * lax.rsqrt(seg_sum(qp * qp) + 1e-6) * (HEAD_DIM ** -0.5)
            kp = kp * lax.rsqrt(seg_sum(kp * kp) + 1e-6)
            h0, h1 = DN_HEADS + 2 * p, DN_HEADS + 2 * p + 1
            q_l.append(stack2(qp))
            k_l.append(stack2(kp))
            v_l.append(stack2(vp))
            beta_l.append(col2x(beta_all[:, 2 * p:2 * p + 1], beta_all[:, 2 * p + 1:2 * p + 2]))
            gc_l.append(col2x(gcum_all[:, h0:h0 + 1], gcum_all[:, h1:h1 + 1]))
            gl_l.append(jnp.broadcast_to(jnp.where(top, gcum_all[ch - 1:ch, h0:h0 + 1], gcum_all[ch - 1:ch, h1:h1 + 1]),
                                         (two, LANES)))
    q2, k2, v2 = jnp.stack(q_l), jnp.stack(k_l), jnp.stack(v_l)
    beta2, gc2, gl2 = jnp.stack(beta_l), jnp.stack(gc_l), jnp.stack(gl_l)
    decay = jnp.exp(jnp.where(incl, gc2 - jnp.swapaxes(gc2, 1, 2), NEG_INF))
    kb2 = k2 * beta2
    a_mat = jnp.where(strict, _bdot_nt(kb2, k2) * decay, 0.0)
    aqk = jnp.where(incl, _bdot_nt(q2, k2) * decay, 0.0)
    s_old = s_sc[...]
    egc = jnp.exp(gc2)
    x = beta2 * (v2 - egc * _bdot(k2, s_old))
    pw = -a_mat
    n_lvl = ch.bit_length() - 1
    for lvl in range(n_lvl):
        x = x + _bdot(pw, x, DN_APPLY_PASSES[lvl])
        if lvl + 1 < n_lvl:
            pw = _bdot(pw, pw, DN_SQUARE_PASSES[lvl])
    o2 = _bdot(q2 * egc, s_old) + _bdot(aqk, x)
    kdec = k2 * jnp.exp(gl2 - gc2)
    s_sc[...] = s_old * jnp.exp(gl2) + _bdot(jnp.swapaxes(kdec, 1, 2), x)
    for bi in range(n_batch):
        for p in range(n_pairs):
            c0 = p * LANES
            o_n = o2[bi * n_pairs + p]
            o_pair = o_n[0:ch] + o_n[ch:two]
            inv = lax.rsqrt(seg_sum(o_pair * o_pair) * (1.0 / HEAD_DIM) + RMS_EPS)
            o_ref[bi, :, c0:c0 + LANES] = o_pair * inv * gn_ref[...] * _silu(z_ref[bi, :, c0:c0 + LANES])

    @pl.when(c == pl.num_programs(0) - 1)
    def _():
        s_out_ref[...] = s_sc[...]


def deltanet_prompt(proj3, conv_w, a_log, dt_bias, norm_gain):
    b, t, _ = proj3.shape
    ch = DN_CHUNK
    assert t % ch == 0
    pad = jnp.zeros((LANES - 2 * DN_HEADS,), F32)
    alog_row = jnp.concatenate([jnp.zeros((DN_HEADS,), F32), a_log, pad]).reshape(1, LANES)
    dtb_row = jnp.concatenate([jnp.zeros((DN_HEADS,), F32), dt_bias, pad]).reshape(1, LANES)
    gn_row = jnp.concatenate([norm_gain, norm_gain]).reshape(1, LANES)
    n_pairs = DN_HEADS // 2
    o, s_fin = pl.pallas_call(
        _deltanet_kernel,
        grid=(t // ch,),
        in_specs=[pl.BlockSpec((b, ch, DN_CONV_DIM), lambda j: (0, j, P_QKV // DN_CONV_DIM)),
                  pl.BlockSpec((b, ch, DN_WIDTH), lambda j: (0, j, P_Z // DN_WIDTH)),
                  pl.BlockSpec((b, ch, LANES), lambda j: (0, j, P_SMALL // LANES)),
                  pl.BlockSpec((CONV_WIDTH, DN_CONV_DIM), lambda j: (0, 0)),
                  pl.BlockSpec((1, LANES), lambda j: (0, 0)),
                  pl.BlockSpec((1, LANES), lambda j: (0, 0)),
                  pl.BlockSpec((1, LANES), lambda j: (0, 0))],
        out_specs=[pl.BlockSpec((b, ch, DN_WIDTH), lambda j: (0, j, 0)),
                   pl.BlockSpec((b * n_pairs, 2 * ch, LANES), lambda j: (0, 0, 0))],
        out_shape=[jax.ShapeDtypeStruct((b, t, DN_WIDTH), F32),
                   jax.ShapeDtypeStruct((b * n_pairs, 2 * ch, LANES), F32)],
        scratch_shapes=[pltpu.VMEM((b, ch + 8, DN_CONV_DIM), F32), pltpu.VMEM((b * n_pairs, 2 * ch, LANES), F32)],
        compiler_params=_cp(("arbitrary",)),
        name="deltanet_prompt",
    )(proj3, proj3, proj3, conv_w, alog_row, dtb_row, gn_row)
    return o, s_fin.reshape(b, n_pairs, 2 * ch, LANES)


def _pairs_to_heads(s_pairs):
    d = HEAD_DIM
    return jnp.stack([s_pairs[:, :, :d, :d], s_pairs[:, :, d:, d:]], axis=2).reshape(
        s_pairs.shape[0], DN_HEADS, d, d)


def _dn_step_prep_kernel(qkv_ref, cs_ref, sm_ref, cw_ref, alog_ref, dtb_ref, ones_ref,
                         q_ref, k_ref, v_ref, sc_ref):
    conv = qkv_ref[...] * cw_ref[CONV_WIDTH - 1:CONV_WIDTH, :]
    for w in range(CONV_WIDTH - 1):
        conv = conv + cs_ref[w] * cw_ref[w:w + 1, :]
    act = _silu(conv)
    q = act[:, 0:DN_WIDTH]
    k = act[:, DN_WIDTH:2 * DN_WIDTH]

    def seg_sum(x):
        return _dot_exact_rhs01(x, ones_ref[...])

    q_ref[...] = q * lax.rsqrt(seg_sum(q * q) + 1e-6) * (HEAD_DIM ** -0.5)
    k_ref[...] = k * lax.rsqrt(seg_sum(k * k) + 1e-6)
    v_ref[...] = act[:, 2 * DN_WIDTH:]
    sm = sm_ref[...]
    g = -jnp.exp(alog_ref[...]) * _softplus(sm + dtb_ref[...])
    lane = lax.broadcasted_iota(jnp.int32, sm.shape, 1)
    sc_ref[...] = jnp.where(lane < DN_HEADS, jax.nn.sigmoid(sm), jnp.exp(g))


def _dn_step_kernel(k_ref, q_ref, v_ref, be_ref, eg_ref, z_ref, gn_ref, s_ref, o_ref, s_out_ref):
    s_old = s_ref[0]
    k, q, v = k_ref[0], q_ref[0], v_ref[0]
    beta, eg = be_ref[0], eg_ref[0]
    ks = jnp.sum(k[:, None, :] * s_old, axis=0)
    qs = jnp.sum(q[:, None, :] * s_old, axis=0)
    qk = jnp.sum(q * k, axis=0, keepdims=True)
    v_new = beta * (v - eg * ks)
    o = eg * qs + qk * v_new
    inv = lax.rsqrt(jnp.mean(o * o, axis=0, keepdims=True) + RMS_EPS)
    o_ref[0] = o * inv * gn_ref[...] * _silu(z_ref[0])
    s_out_ref[0] = s_old * eg[None] + k[:, None, :] * v_new[None]


def deltanet_sample(proj_s, conv_state, rec_state, conv_w, a_log, dt_bias, norm_gain):
    n = proj_s.shape[0]
    hds, d = DN_HEADS, HEAD_DIM
    pad = jnp.zeros((LANES - 2 * hds,), F32)
    alog_row = jnp.concatenate([jnp.zeros((hds,), F32), a_log, pad]).reshape(1, LANES)
    dtb_row = jnp.concatenate([jnp.zeros((hds,), F32), dt_bias, pad]).reshape(1, LANES)
    head_of = jnp.arange(DN_WIDTH) // d
    ones_bd = (head_of[:, None] == head_of[None, :]).astype(BF16)
    cs = jnp.transpose(conv_state, (1, 0, 2))
    full = lambda shape: pl.BlockSpec(shape, lambda i: (0,) * len(shape))
    q, k, v, sc = pl.pallas_call(
        _dn_step_prep_kernel,
        grid=(1,),
        in_specs=[pl.BlockSpec((n, DN_CONV_DIM), lambda i: (0, P_QKV // DN_CONV_DIM)),
                  full((CONV_WIDTH - 1, n, DN_CONV_DIM)),
                  pl.BlockSpec((n, LANES), lambda i: (0, P_SMALL // LANES)),
                  full((CONV_WIDTH, DN_CONV_DIM)), full((1, LANES)), full((1, LANES)),
                  full((DN_WIDTH, DN_WIDTH))],
        out_specs=[full((n, DN_WIDTH)), full((n, DN_WIDTH)), full((n, DN_WIDTH)), full((n, LANES))],
        out_shape=[jax.ShapeDtypeStruct((n, DN_WIDTH), F32)] * 3 + [jax.ShapeDtypeStruct((n, LANES), F32)],
        compiler_params=_cp(("arbitrary",)),
        name="dn_step_prep",
    )(proj_s, cs, proj_s, conv_w, alog_row, dtb_row, ones_bd)
    t3 = lambda a: a.T.reshape(hds, d, n)
    sct = sc[:, :2 * hds].T.reshape(2, hds, 1, n)
    head_vec = pl.BlockSpec((1, d, n), lambda i: (i, 0, 0))
    head_scl = pl.BlockSpec((1, 1, n), lambda i: (i, 0, 0))
    state = pl.BlockSpec((1, d, d, n), lambda i: (i, 0, 0, 0))
    o, s_new = pl.pallas_call(
        _dn_step_kernel,
        grid=(hds,),
        in_specs=[head_vec, head_vec, head_vec, head_scl, head_scl, head_vec,
                  pl.BlockSpec((d, 1), lambda i: (0, 0)), state],
        out_specs=[head_vec, state],
        out_shape=[jax.ShapeDtypeStruct((hds, d, n), F32), jax.ShapeDtypeStruct((hds, d, d, n), F32)],
        compiler_params=_cp(("parallel",)),
        name="dn_step",
    )(t3(k), t3(q), t3(v), sct[0], sct[1], t3(proj_s[:, P_Z:P_Z + DN_WIDTH]), norm_gain.reshape(d, 1),
      jnp.transpose(rec_state, (1, 2, 3, 0)))
    return o.reshape(DN_WIDTH, n).T, jnp.transpose(s_new, (3, 0, 1, 2))


def _nsa_decode_kernel(pt_ref, q_ref, new_ref, gate_ref, win_ref, exp_ref, *refs, n_pages, seqs):
    del pt_ref
    o_ref, kc_sc = refs[2 * seqs * n_pages], refs[2 * seqs * n_pages + 1]
    dh, r, g2, nh = HEAD_DIM, NSA_GROUP, NSA_KV_HEADS, NSA_HEADS
    past = n_pages * PAGE_SIZE
    nb = past // CMP_BLOCK
    nsel = nb + 1
    wlen = win_ref.shape[4]
    for u in range(seqs):
        for j in range(n_pages):
            kc_sc[u, j:j + 1, :] = refs[u * n_pages + j][0]
    kcv = kc_sc[...]
    new = new_ref[...]
    q8 = q_ref[...] * ATTN_SCALE
    head = lax.broadcasted_iota(jnp.int32, (1, nh, 1), 1)
    lane = lax.broadcasted_iota(jnp.int32, (1, 1, LANES), 2)
    g0 = head < r
    blk = jnp.where(lane < n_pages, 2 * lane, jnp.where(lane < nb, 2 * (lane - n_pages) + 1, lane))
    wp = lax.broadcasted_iota(jnp.int32, (1, 1, wlen + LANES), 2)
    dpos = wlen - wp
    wmask = (dpos >= 0) & (dpos < WINDOW) & (past - wlen + wp >= 0)

    def both(x):
        return jnp.where(g0, x[:, :, 0:dh], x[:, :, dh:2 * dh])

    def new_part(kind):
        return both(jnp.broadcast_to(new[:, :, 2 * kind * dh:2 * (kind + 1) * dh], (seqs, nh, 2 * dh)))

    def pages(kind):
        return jnp.stack([jnp.concatenate([refs[(seqs + u) * n_pages + j][0, kind].reshape(2 * dh, PAGE_SIZE)
                                           for j in range(n_pages)], axis=1) for u in range(seqs)])

    q_bd = jnp.concatenate([jnp.where(g0, q8, 0.0), jnp.where(g0, 0.0, q8)], axis=2)
    s_new = jnp.sum(q8 * new_part(2), axis=-1, keepdims=True)
    s_all = jnp.concatenate([_bdot(q_bd, pages(0)), jnp.broadcast_to(s_new, (seqs, nh, LANES))], axis=2)
    sw = _bdot(q_bd, win_ref[:, 0].reshape(seqs, 2 * dh, wlen))
    sw_new = jnp.sum(q8 * new_part(4), axis=-1, keepdims=True)

    def cmp_rows(base):
        even = jnp.concatenate([kcv[:, :, base:base + dh], kcv[:, :, base + LANES:base + LANES + dh]], axis=2)
        odd = jnp.concatenate([kcv[:, :, base + dh:base + LANES], kcv[:, :, base + LANES + dh:base + 2 * LANES]], axis=2)
        return jnp.concatenate([even, odd], axis=1)

    s = _bdot_nt(q_bd, cmp_rows(0))
    p = jnp.exp(s - jnp.max(s, axis=-1, keepdims=True))
    p = p / jnp.maximum(jnp.sum(p, axis=-1, keepdims=True), 1e-30)
    o_cmp = both(_bdot(p, cmp_rows(g2 * LANES)))
    bid_row = jnp.broadcast_to(blk.astype(F32), (1, LANES, LANES))
    bid_col = jnp.swapaxes(bid_row, 1, 2)
    valid = lane < nsel
    forced = valid & ((blk == 0) | (nb - blk < N_LOCAL_BLOCKS))
    sels = []
    for g in range(g2):
        in_g = (head >= g * r) & (head < (g + 1) * r)
        imp = jnp.sum(jnp.where(in_g, p, 0.0), axis=1, keepdims=True)
        imp = jnp.concatenate([imp, jnp.zeros((seqs, 1, LANES - nb), F32)], axis=2)
        score = jnp.where(valid, imp + jnp.where(forced, FORCE_SCORE, 0.0), -1.0)
        sc_row = jnp.broadcast_to(score, (seqs, LANES, LANES))
        sc_col = jnp.swapaxes(sc_row, 1, 2)
        beats = (sc_col > sc_row) | ((sc_col == sc_row) & (bid_col < bid_row))
        rank = jnp.sum(jnp.where(beats, 1.0, 0.0), axis=1, keepdims=True)
        sels.append(jnp.where(valid & (rank < min(TOP_N, nsel)), 1.0, 0.0))
    sel8 = jnp.where(g0, sels[0], sels[1]).astype(BF16)
    chosen = jnp.dot(sel8.reshape(seqs * nh, LANES), exp_ref[...], preferred_element_type=F32) > 0.5
    pm = _masked_softmax(s_all, chosen.reshape(seqs, nh, past + LANES))
    o_slc = pm[:, :, past:past + 1] * new_part(3) + both(_bdot_nt(pm[:, :, 0:past], pages(1)))
    pw = _masked_softmax(jnp.concatenate([sw, jnp.broadcast_to(sw_new, (seqs, nh, LANES))], axis=2), wmask)
    o_win = pw[:, :, wlen:wlen + 1] * new_part(5) + both(
        _bdot_nt(pw[:, :, 0:wlen], win_ref[:, 1].reshape(seqs, 2 * dh, wlen)))
    gates = jnp.broadcast_to(jax.nn.sigmoid(gate_ref[...]), (seqs, nh, LANES))

    def gate(branch):
        return jnp.sum(jnp.where(lane == 3 * head + branch, gates, 0.0), axis=-1, keepdims=True)

    o_ref[...] = gate(0) * o_cmp + gate(1) * o_slc + gate(2) * o_win


def nsa_decode(page_table, q3, new_row, gate_row, win_t, kc_phys, cache_t, seqs=8):
    n, n_pages = page_table.shape
    dh = HEAD_DIM
    g2 = NSA_KV_HEADS
    wlen = win_t.shape[4]
    assert 2 * n_pages + 1 <= LANES and PAGE_SIZE == 2 * CMP_BLOCK and PAGE_SIZE == LANES and g2 == 2

    past = n_pages * PAGE_SIZE
    nb = past // CMP_BLOCK
    erow = lax.broadcasted_iota(jnp.int32, (LANES, past + LANES), 0)
    ecol = lax.broadcasted_iota(jnp.int32, (LANES, past + LANES), 1)
    page, second = ecol // PAGE_SIZE, (ecol % PAGE_SIZE) >= CMP_BLOCK
    expand = (((erow < n_pages) & (page == erow) & ~second & (ecol < past))
              | ((erow >= n_pages) & (erow < nb) & (page == erow - n_pages) & second & (ecol < past))
              | ((erow == nb) & (ecol == past))).astype(BF16)

    assert n % seqs == 0

    def kc_map(u, j):
        return lambda i, pt: (pt[seqs * i + u, j], 0, 0)

    def slc_map(u, j):
        return lambda i, pt: (pt[seqs * i + u, j], 1, 0, 0, 0)

    in_specs = [pl.BlockSpec((seqs, NSA_HEADS, dh), lambda i, pt: (i, 0, 0)),
                pl.BlockSpec((seqs, 1, KV_COLS), lambda i, pt: (i, 0, 0)),
                pl.BlockSpec((seqs, 1, LANES), lambda i, pt: (i, 0, 0)),
                pl.BlockSpec((seqs, 2, g2, dh, wlen), lambda i, pt: (i, 0, 0, 0, 0)),
                pl.BlockSpec((LANES, past + LANES), lambda i, pt: (0, 0))]
    in_specs += [pl.BlockSpec((1, 1, 4 * LANES), kc_map(u, j)) for u in range(seqs) for j in range(n_pages)]
    in_specs += [pl.BlockSpec((1, 2, g2, dh, PAGE_SIZE), slc_map(u, j)) for u in range(seqs) for j in range(n_pages)]
    grid_spec = pltpu.PrefetchScalarGridSpec(
        num_scalar_prefetch=1, grid=(n // seqs,), in_specs=in_specs,
        out_specs=pl.BlockSpec((seqs, NSA_HEADS, dh), lambda i, pt: (i, 0, 0)),
        scratch_shapes=[pltpu.VMEM((seqs, n_pages, 4 * LANES), F32)])
    return pl.pallas_call(
        functools.partial(_nsa_decode_kernel, n_pages=n_pages, seqs=seqs),
        grid_spec=grid_spec,
        out_shape=jax.ShapeDtypeStruct((n, NSA_HEADS, dh), F32),
        compiler_params=_cp(("arbitrary",)),
        name="nsa_decode",
    )(page_table, q3, new_row, gate_row, win_t, expand, *([kc_phys] * (seqs * n_pages)), *([cache_t] * (seqs * n_pages)))


def _compress_pages_kernel(x_ref, wd_ref, ped_ref, b1_ref, w2_ref, o_ref, acc_ref, *, dd):
    j = pl.program_id(1)
    g2 = NSA_KV_HEADS

    @pl.when(j == 0)
    def _():
        acc_ref[...] = jnp.zeros_like(acc_ref)

    for kind in range(2):
        for g in range(g2):
            part = None
            xt = jnp.swapaxes(x_ref[:, kind, g], 0, 1)
            for dp in range(dd // 2):
                xs = jnp.concatenate([xt[2 * dp] + ped_ref[kind, 2 * dp:2 * dp + 1, :],
                                      xt[2 * dp + 1] + ped_ref[kind, 2 * dp + 1:2 * dp + 2, :]], axis=1)
                d = jnp.dot(xs.astype(BF16), wd_ref[kind, dp], preferred_element_type=F32)
                part = d if part is None else part + d
            acc_ref[kind * g2 + g] += part

    @pl.when(j == pl.num_programs(1) - 1)
    def _():
        for kind in range(2):
            for g in range(g2):
                h = jnp.maximum(acc_ref[kind * g2 + g] + b1_ref[kind:kind + 1, :], 0.0)
                o_ref[:, 0, (kind * g2 + g) * LANES:(kind * g2 + g + 1) * LANES] = jnp.dot(
                    h.astype(BF16), w2_ref[kind], preferred_element_type=F32)


def compress_pages(cache_t, cwp, bp, dd=8):
    n_phys = cache_t.shape[0]
    g2, dh = NSA_KV_HEADS, HEAD_DIM
    assert n_phys % bp == 0 and dh % dd == 0 and dd % 2 == 0
    wd, ped, b1h, w2h = cwp
    return pl.pallas_call(
        functools.partial(_compress_pages_kernel, dd=dd),
        grid=(n_phys // bp, dh // dd),
        in_specs=[pl.BlockSpec((bp, 2, g2, dd, PAGE_SIZE), lambda i, j: (i, 0, 0, j, 0)),
                  pl.BlockSpec((2, dd // 2, 2 * PAGE_SIZE, 2 * CMP_HIDDEN), lambda i, j: (0, j, 0, 0)),
                  pl.BlockSpec((2, dd, PAGE_SIZE), lambda i, j: (0, j, 0)),
                  pl.BlockSpec((2, 2 * CMP_HIDDEN), lambda i, j: (0, 0)),
                  pl.BlockSpec((2, 2 * CMP_HIDDEN, LANES), lambda i, j: (0, 0, 0))],
        out_specs=pl.BlockSpec((bp, 1, 2 * g2 * LANES), lambda i, j: (i, 0, 0)),
        out_shape=jax.ShapeDtypeStruct((n_phys, 1, 2 * g2 * LANES), F32),
        scratch_shapes=[pltpu.VMEM((2 * g2, bp, 2 * CMP_HIDDEN), F32)],
        compiler_params=_cp(("parallel", "arbitrary")),
        name="compress_pages",
    )(cache_t, wd, ped, b1h, w2h)


def _compress_page_weights(cmp_pe, cmp_w1, cmp_b1, cmp_w2):
    w1t = jnp.transpose(cmp_w1.reshape(2, CMP_BLOCK, HEAD_DIM, CMP_HIDDEN), (0, 2, 1, 3))
    z = jnp.zeros_like(w1t)
    wd = jnp.concatenate([jnp.concatenate([w1t, z], -1), jnp.concatenate([z, w1t], -1)], axis=2).astype(BF16)
    wd = wd.reshape(2, HEAD_DIM // 2, 2 * PAGE_SIZE, 2 * CMP_HIDDEN)
    pet = jnp.transpose(cmp_pe, (0, 2, 1))
    ped = jnp.concatenate([pet, pet], -1)
    b1h = jnp.concatenate([cmp_b1, cmp_b1], -1)
    z2 = jnp.zeros_like(cmp_w2)
    w2h = jnp.concatenate([jnp.concatenate([cmp_w2, z2], -1), jnp.concatenate([z2, cmp_w2], -1)], axis=1).astype(BF16)
    return wd, ped, b1h, w2h


def _split_w_in(w):
    d = w.shape[0]
    off_b = DN_CONV_DIM + DN_WIDTH
    off_q = off_b + 2 * DN_HEADS
    off_kv = off_q + NSA_WIDTH
    off_g = off_kv + KV_COLS
    n_g = 3 * NSA_HEADS
    pad = jnp.zeros((d, P_COLS - P_SMALL - 2 * DN_HEADS - n_g), w.dtype)
    main = jnp.concatenate([w[:, :off_b], w[:, off_q:off_kv], w[:, off_kv:off_kv + CMP_COLS], w[:, off_b:off_q],
                            w[:, off_g:off_g + n_g], pad], axis=1)
    return main.astype(BF16), w[:, off_kv:off_g].T.astype(BF16)


def _row_tile(n, cap):
    t = min(n, cap)
    while n % t:
        t //= 2
    return t


def _trunk_tail(x2, mixer_dn, mixer_nsa, mem_kv3, bshape, lw, final_norm):
    n, d = x2.shape
    b, t = bshape
    if t >= SUBLANES:
        x3 = mem_block(x2.reshape(b, t, d), mixer_dn.reshape(b, t, -1), mixer_nsa.reshape(b, t, -1),
                       lw["w_out_dn"], lw["w_out_nsa"], lw["ln_mem"], lw["w_mem_q"], mem_kv3, lw["w_mem_o"],
                       _row_tile(t, 512))
        x2 = x3.reshape(n, d)
    else:
        assert t == 1
        tm = _row_tile(n, 512)
        x2 = matmul_residual(x2, [mixer_dn, mixer_nsa], [lw["w_out_dn"], lw["w_out_nsa"]], tm)
        qm = rms_matmul(x2, lw["ln_mem"], lw["w_mem_q"], tm, 512)
        att = mem_attention_row(qm.reshape(n, MEM_HEADS, d // MEM_HEADS), mem_kv3).reshape(n, d)
        x2 = matmul_residual(x2, [att], [lw["w_mem_o"]], tm)
    return ffn(x2, lw["ln_ffn"], lw["w_up"], lw["w_down"], lw["ln_final"], final_norm, _row_tile(n, 1024), 512)


def _kv_rows(kvt, lo, hi):
    b, _, s = kvt.shape
    g, dh = NSA_KV_HEADS, HEAD_DIM
    return jnp.transpose(kvt[:, lo * g * dh:hi * g * dh].reshape(b, hi - lo, g, dh, s), (0, 4, 1, 2, 3))


def _prompt_layer(xp, mem_prompt, lw, cw, final_norm):
    b, s, d = xp.shape
    n = b * s
    x2 = xp.reshape(n, d)
    proj, kvt = input_projection(xp, lw["ln_mix"], lw["w_in"], lw["w_kvt"], _row_tile(s, 1024), 1024)
    proj3 = proj.reshape(b, s, P_COLS)
    dn_out, s_pairs = deltanet_prompt(proj3, lw["dn_conv_w"], lw["dn_a_log"], lw["dn_dt_bias"], lw["dn_norm"])
    p_conv = proj3[:, s - (CONV_WIDTH - 1):, P_QKV:P_QKV + DN_CONV_DIM]
    p_rec = _pairs_to_heads(s_pairs)
    nb = s // CMP_BLOCK
    cmp_rows = proj3[:, :nb * CMP_BLOCK, P_CMP:P_CMP + CMP_COLS]
    kcv = compress_blocks(cmp_rows.reshape(b * nb, CMP_BLOCK * CMP_COLS), CMP_COLS, cw, _row_tile(b * nb, 256))
    kcv = kcv.reshape(b, nb, 2, NSA_KV_HEADS, HEAD_DIM)
    kc = jnp.transpose(kcv[:, :, 0], (0, 2, 1, 3))
    vct = jnp.transpose(kcv[:, :, 1], (0, 2, 3, 1))
    ocmpt, selt = nsa_cmp_topk(proj3, kc, vct, _row_tile(s, 512))
    nsa_out = nsa_attention(proj3, kvt, selt, ocmpt, 256, 512).reshape(n, NSA_WIDTH)
    m = mem_prompt.shape[1]
    mem_kv = rms_matmul(mem_prompt.reshape(b * m, d), lw["ln_memkv"], lw["w_mem_kv"], _row_tile(b * m, 512), 512)
    mem_kv5 = mem_kv.reshape(b, m, 2, MEM_HEADS, d // MEM_HEADS)
    y = _trunk_tail(x2, dn_out.reshape(n, DN_WIDTH), nsa_out, mem_kv5, (b, s), lw, final_norm)
    wk = min(WINDOW, s)
    return y.reshape(b, s, d), _kv_rows(kvt, 0, 4), _kv_rows(kvt[:, :, s - wk:], 4, 6), mem_kv5, p_conv, p_rec


def _sample_layer(xs, cache_nsa, cache_win, cache_mem, conv_state, rec_state, page_table, lw, cwp, final_norm):
    db, ds, d = xs.shape
    assert ds == 1
    n = db
    x2 = xs.reshape(n, d)
    proj, kvt = input_projection(x2[None], lw["ln_mix"], lw["w_in"], lw["w_kvt"], n, 1024)
    dn_out, s_rec = deltanet_sample(proj, conv_state, rec_state, lw["dn_conv_w"], lw["dn_a_log"], lw["dn_dt_bias"],
                                    lw["dn_norm"])
    s_conv = jnp.concatenate([conv_state[:, 1:], proj[:, None, P_QKV:P_QKV + DN_CONV_DIM]], axis=1)
    n_phys = cache_nsa.shape[0]
    cache_t = jnp.transpose(cache_nsa, (0, 2, 3, 4, 1))
    win_t = jnp.transpose(cache_win, (0, 2, 3, 4, 1))
    kc_phys = compress_pages(cache_t, cwp, _row_tile(n_phys, 256))
    kv_new = kvt[0].T
    o8 = nsa_decode(page_table, proj[:, P_NQ:P_NQ + NSA_WIDTH].reshape(n, NSA_HEADS, HEAD_DIM),
                    kv_new.reshape(n, 1, KV_COLS),
                    _flat_gates(proj[:, P_SMALL:P_SMALL + LANES]).reshape(n, 1, LANES),
                    win_t, kc_phys, cache_t)
    kv6 = kv_new.reshape(n, 1, NSA_KV_KINDS, NSA_KV_HEADS, HEAD_DIM)
    s_nsa = jnp.transpose(kvt.reshape(NSA_KV_KINDS, NSA_KV_HEADS, HEAD_DIM, n)[:4], (3, 0, 1, 2))[:, None]
    s_win = jnp.concatenate([cache_win, kv6[:, :, 4:]], axis=1)[:, 1:]
    y = _trunk_tail(x2, dn_out, o8.reshape(n, NSA_WIDTH), cache_mem, (n, 1), lw, final_norm)
    return y.reshape(db, ds, d), s_nsa, s_win, s_conv, s_rec


def _flat_gates(small):
    g0 = 2 * DN_HEADS
    n_g = 3 * NSA_HEADS
    gl = small[..., g0:g0 + n_g]
    return jnp.concatenate([gl, jnp.zeros(gl.shape[:-1] + (LANES - n_g,), gl.dtype)], axis=-1)


def kernel(x_prompt, x_sample, mem_prompt, cache_nsa_kv, cache_win_kv, cache_mem_kv, state_dn_conv, state_dn_rec, page_table, ln_mix, w_in, dn_conv_w, dn_a_log, dn_dt_bias, dn_norm, cmp_pe, cmp_w1, cmp_b1, cmp_w2, w_out, ln_mem, ln_memkv, w_mem_q, w_mem_kv, w_mem_o, ln_ffn, w_up, w_down, ln_final):
    depth = w_in.shape[0]
    xp, xs = x_prompt, x_sample
    outs_p = [[] for _ in range(5)]
    outs_s = [[] for _ in range(4)]
    for l in range(depth):
        lw = {
            "ln_mix": ln_mix[l],
            "dn_conv_w": dn_conv_w[l], "dn_a_log": dn_a_log[l], "dn_dt_bias": dn_dt_bias[l], "dn_norm": dn_norm[l],
            "w_out_dn": w_out[l][:DN_WIDTH].astype(BF16), "w_out_nsa": w_out[l][DN_WIDTH:].astype(BF16),
            "ln_mem": ln_mem[l], "ln_memkv": ln_memkv[l], "w_mem_q": w_mem_q[l].astype(BF16),
            "w_mem_kv": w_mem_kv[l].astype(BF16), "w_mem_o": w_mem_o[l].astype(BF16),
            "ln_ffn": ln_ffn[l], "w_up": w_up[l].astype(BF16), "w_down": w_down[l].astype(BF16),
            "ln_final": ln_final,
        }
        lw["w_in"], lw["w_kvt"] = _split_w_in(w_in[l])
        cw = _compress_weights(cmp_pe[l], cmp_w1[l], cmp_b1[l], cmp_w2[l])
        last = l == depth - 1
        xp, p_nsa, p_win, p_mem, p_conv, p_rec = _prompt_layer(xp, mem_prompt, lw, cw, last)
        for acc, val in zip(outs_p, (p_nsa, p_win, p_mem, p_conv, p_rec)):
            acc.append(val)
        xs, s_nsa, s_win, s_conv, s_rec = _sample_layer(
            xs, cache_nsa_kv[l], cache_win_kv[l], cache_mem_kv[l], state_dn_conv[l], state_dn_rec[l],
            page_table, lw, _compress_page_weights(cmp_pe[l], cmp_w1[l], cmp_b1[l], cmp_w2[l]), last)
        for acc, val in zip(outs_s, (s_nsa, s_win, s_conv, s_rec)):
            acc.append(val)
    return (xp, xs) + tuple(jnp.stack(a) for a in outs_p) + tuple(jnp.stack(a) for a in outs_s)
```

```python
import functools

import jax
import jax.numpy as jnp
from jax import lax
from jax.experimental import pallas as pl
from jax.experimental.pallas import tpu as pltpu

F32 = jnp.float32
BF16 = jnp.bfloat16

HEAD_DIM = 64
DN_HEADS = 8
NSA_HEADS = 8
NSA_KV_HEADS = 2
NSA_GROUP = NSA_HEADS // NSA_KV_HEADS
DN_WIDTH = DN_HEADS * HEAD_DIM
NSA_WIDTH = NSA_HEADS * HEAD_DIM
CONV_WIDTH = 4
DN_CONV_DIM = 3 * DN_WIDTH
DN_CHUNK = 64
CMP_BLOCK = 64
SEL_BLOCK = 64
TOP_N = 16
N_LOCAL_BLOCKS = 2
WINDOW = 512
CMP_HIDDEN = 128
NSA_KV_KINDS = 6
MEM_HEADS = 4
PAGE_SIZE = 128
RMS_EPS = 1e-6
FORCE_SCORE = 1e3
NEG_INF = -1e30
ATTN_SCALE = HEAD_DIM ** -0.5
LOG2_E = 1.4426950408889634

LANES = 128
SUBLANES = 8
P_QKV = 0
P_Z = P_QKV + DN_CONV_DIM
P_NQ = P_Z + DN_WIDTH
P_CMP = P_NQ + NSA_WIDTH
CMP_COLS = 2 * NSA_KV_HEADS * HEAD_DIM
P_SMALL = P_CMP + CMP_COLS
P_COLS = P_SMALL + 2 * LANES
KV_COLS = NSA_KV_KINDS * NSA_KV_HEADS * HEAD_DIM
VMEM_LIMIT = 56 * 1024 * 1024
DN_APPLY_PASSES = (3, 3, 3, 1, 1, 1)
DN_SQUARE_PASSES = (3, 3, 1, 1, 1)


def _cp(sem, vmem=VMEM_LIMIT):
    return pltpu.CompilerParams(dimension_semantics=sem, vmem_limit_bytes=vmem)


def _split2(a):
    hi = a.astype(BF16)
    return hi, (a - hi.astype(F32)).astype(BF16)


def _dot(a, b, passes=1):
    if passes == 1:
        return jnp.dot(a.astype(BF16), b.astype(BF16), preferred_element_type=F32)
    ah, al = _split2(a)
    bh, bl = _split2(b)
    return (jnp.dot(ah, bh, preferred_element_type=F32) + jnp.dot(ah, bl, preferred_element_type=F32)
            + jnp.dot(al, bh, preferred_element_type=F32))


def _dot_nt(a, b):
    return lax.dot_general(a.astype(BF16), b.astype(BF16), (((1,), (1,)), ((), ())),
                           preferred_element_type=F32)


def _split3(a):
    hi = a.astype(BF16)
    r1 = a - hi.astype(F32)
    mid = r1.astype(BF16)
    lo = (r1 - mid.astype(F32)).astype(BF16)
    return hi, mid, lo


def _dot_exact_lhs01(a01, b):
    a = a01.astype(BF16)
    hi, mid, lo = _split3(b)
    return (jnp.dot(a, hi, preferred_element_type=F32) + jnp.dot(a, mid, preferred_element_type=F32)
            + jnp.dot(a, lo, preferred_element_type=F32))


def _dot_exact_rhs01(a, b01):
    b = b01.astype(BF16)
    hi, mid, lo = _split3(a)
    return (jnp.dot(hi, b, preferred_element_type=F32) + jnp.dot(mid, b, preferred_element_type=F32)
            + jnp.dot(lo, b, preferred_element_type=F32))


def _rms(x, gain):
    ms = jnp.mean(x * x, axis=-1, keepdims=True)
    return x * lax.rsqrt(ms + RMS_EPS) * gain


def _softplus(x):
    return jnp.maximum(x, 0.0) + jnp.log1p(jnp.exp(-jnp.abs(x)))


def _silu(x):
    return x * jax.nn.sigmoid(x)


def _rms_mm_kernel(x_ref, g_ref, w_ref, o_ref, h_ref):
    @pl.when(pl.program_id(1) == 0)
    def _():
        h_ref[...] = _rms(x_ref[...], g_ref[...]).astype(BF16)

    o_ref[...] = jnp.dot(h_ref[...], w_ref[...], preferred_element_type=F32)


def rms_matmul(x, gain, w_bf16, tm, tn):
    n, d = x.shape
    m = w_bf16.shape[1]
    assert n % tm == 0 and m % tn == 0
    return pl.pallas_call(
        _rms_mm_kernel,
        grid=(n // tm, m // tn),
        in_specs=[pl.BlockSpec((tm, d), lambda i, j: (i, 0)),
                  pl.BlockSpec((1, d), lambda i, j: (0, 0)),
                  pl.BlockSpec((d, tn), lambda i, j: (0, j))],
        out_specs=pl.BlockSpec((tm, tn), lambda i, j: (i, j)),
        out_shape=jax.ShapeDtypeStruct((n, m), F32),
        scratch_shapes=[pltpu.VMEM((tm, d), BF16)],
        compiler_params=_cp(("parallel", "arbitrary")),
        name="rms_matmul",
    )(x, gain.reshape(1, d), w_bf16)


def _proj_kernel(x_ref, g_ref, w_ref, wkv_ref, o_ref, okv_ref, owin_ref, h_ref):
    @pl.when(pl.program_id(1) == 0)
    def _():
        h = _rms(x_ref[...], g_ref[...]).astype(BF16)
        h_ref[...] = h
        kvt = lax.dot_general(wkv_ref[...], h, (((1,), (1,)), ((), ())), preferred_element_type=F32)
        split = okv_ref.shape[1]
        okv_ref[0] = kvt[0:split]
        owin_ref[0] = kvt[split:]

    o_ref[...] = jnp.dot(h_ref[...], w_ref[...], preferred_element_type=F32)


def input_projection(x3, gain, w_main, w_kvt, tm, tn):
    b, s, d = x3.shape
    n = b * s
    m = w_main.shape[1]
    kvc = w_kvt.shape[0]
    win = 2 * NSA_KV_HEADS * HEAD_DIM
    assert s % tm == 0 and m % tn == 0
    spt = s // tm
    return pl.pallas_call(
        _proj_kernel,
        grid=(n // tm, m // tn),
        in_specs=[pl.BlockSpec((tm, d), lambda i, j: (i, 0)),
                  pl.BlockSpec((1, d), lambda i, j: (0, 0)),
                  pl.BlockSpec((d, tn), lambda i, j: (0, j)),
                  pl.BlockSpec((kvc, d), lambda i, j: (0, 0))],
        out_specs=[pl.BlockSpec((tm, tn), lambda i, j: (i, j)),
                   pl.BlockSpec((1, kvc - win, tm), lambda i, j: (i // spt, 0, i % spt)),
                   pl.BlockSpec((1, win, tm), lambda i, j: (i // spt, 0, i % spt))],
        out_shape=[jax.ShapeDtypeStruct((n, m), F32), jax.ShapeDtypeStruct((b, kvc - win, s), F32),
                   jax.ShapeDtypeStruct((b, win, s), F32)],
        scratch_shapes=[pltpu.VMEM((tm, d), BF16)],
        compiler_params=_cp(("parallel", "arbitrary")),
        name="input_projection",
    )(x3.reshape(n, d), gain.reshape(1, d), w_main, w_kvt)


def _mm_res_kernel(*refs, n_in):
    res_ref = refs[0]
    a_refs = refs[1:1 + n_in]
    w_refs = refs[1 + n_in:1 + 2 * n_in]
    o_ref = refs[1 + 2 * n_in]
    acc = res_ref[...]
    for a_ref, w_ref in zip(a_refs, w_refs):
        acc = acc + jnp.dot(a_ref[...].astype(BF16), w_ref[...], preferred_element_type=F32)
    o_ref[...] = acc


def matmul_residual(res, a_list, w_list, tm):
    n, d = res.shape
    assert n % tm == 0
    n_in = len(a_list)
    in_specs = [pl.BlockSpec((tm, d), lambda i: (i, 0))]
    in_specs += [pl.BlockSpec((tm, a.shape[1]), lambda i: (i, 0)) for a in a_list]
    in_specs += [pl.BlockSpec(w.shape, lambda i: (0, 0)) for w in w_list]
    return pl.pallas_call(
        functools.partial(_mm_res_kernel, n_in=n_in),
        grid=(n // tm,),
        in_specs=in_specs,
        out_specs=pl.BlockSpec((tm, d), lambda i: (i, 0)),
        out_shape=jax.ShapeDtypeStruct((n, d), F32),
        compiler_params=_cp(("parallel",)),
        name="matmul_residual",
    )(res, *a_list, *w_list)


def _ffn_kernel(x_ref, g_ref, wu_ref, wd_ref, gf_ref, o_ref, hn_ref, acc_ref, *, final_norm):
    j = pl.program_id(1)

    @pl.when(j == 0)
    def _():
        x = x_ref[...]
        hn_ref[...] = _rms(x, g_ref[...]).astype(BF16)
        acc_ref[...] = x

    u = jnp.dot(hn_ref[...], wu_ref[...], preferred_element_type=F32)
    u = jnp.square(jnp.maximum(u, 0.0)).astype(BF16)
    acc_ref[...] += jnp.dot(u, wd_ref[...], preferred_element_type=F32)

    @pl.when(j == pl.num_programs(1) - 1)
    def _():
        y = acc_ref[...]
        if final_norm:
            y = _rms(y, gf_ref[...])
        o_ref[...] = y


def ffn(x, gain, wu_bf16, wd_bf16, gain_final, final_norm, tm, tf):
    n, d = x.shape
    f = wu_bf16.shape[1]
    assert n % tm == 0 and f % tf == 0
    return pl.pallas_call(
        functools.partial(_ffn_kernel, final_norm=final_norm),
        grid=(n // tm, f // tf),
        in_specs=[pl.BlockSpec((tm, d), lambda i, j: (i, 0)),
                  pl.BlockSpec((1, d), lambda i, j: (0, 0)),
                  pl.BlockSpec((d, tf), lambda i, j: (0, j)),
                  pl.BlockSpec((tf, d), lambda i, j: (j, 0)),
                  pl.BlockSpec((1, d), lambda i, j: (0, 0))],
        out_specs=pl.BlockSpec((tm, d), lambda i, j: (i, 0)),
        out_shape=jax.ShapeDtypeStruct((n, d), F32),
        scratch_shapes=[pltpu.VMEM((tm, d), BF16), pltpu.VMEM((tm, d), F32)],
        compiler_params=_cp(("parallel", "arbitrary")),
        name="ffn",
    )(x, gain.reshape(1, d), wu_bf16, wd_bf16, gain_final.reshape(1, d))


def _mem_attn_row_kernel(q_ref, kv_ref, o_ref, *, m, seqs, heads):
    sub = SUBLANES
    ones = jnp.ones((LANES, LANES), BF16)
    for u in range(seqs):
        x = kv_ref[u].reshape(m, 2 * sub, LANES)
        prod = x[:, 0:sub] * q_ref[u][None]
        part = jnp.dot(prod.reshape(m * sub, LANES).astype(BF16), ones, preferred_element_type=F32)
        part = part.reshape(m, sub, LANES)
        s = part + pltpu.roll(part, heads, 1)
        p = jnp.exp2(s - jnp.max(s, axis=0, keepdims=True))
        o_ref[u] = jnp.sum(p * x[:, sub:2 * sub], axis=0) / jnp.sum(p, axis=0)


def mem_attention_row(q, kv, seqs=4):
    b, h, hd = q.shape
    m = kv.shape[1]
    assert kv.shape[2:] == (2, h, hd)
    tiles = hd // LANES
    assert tiles * LANES == hd and tiles == 2 and tiles * h == SUBLANES and b % seqs == 0
    rows = jnp.transpose(kv.reshape(b, m, 2, h, tiles, LANES), (0, 1, 2, 4, 3, 5)).reshape(b, m * 2 * SUBLANES, LANES)
    q8 = jnp.transpose((q * (hd ** -0.5 * LOG2_E)).reshape(b, h, tiles, LANES), (0, 2, 1, 3)).reshape(b, SUBLANES, LANES)
    o8 = pl.pallas_call(
        functools.partial(_mem_attn_row_kernel, m=m, seqs=seqs, heads=h),
        grid=(b // seqs,),
        in_specs=[pl.BlockSpec((seqs, SUBLANES, LANES), lambda i: (i, 0, 0)),
                  pl.BlockSpec((seqs, m * 2 * SUBLANES, LANES), lambda i: (i, 0, 0))],
        out_specs=pl.BlockSpec((seqs, SUBLANES, LANES), lambda i: (i, 0, 0)),
        out_shape=jax.ShapeDtypeStruct((b, SUBLANES, LANES), F32),
        compiler_params=_cp(("parallel",)),
        name="mem_attention_row",
    )(q8, rows)
    return jnp.transpose(o8.reshape(b, tiles, h, LANES), (0, 2, 1, 3)).reshape(b, h, hd)


def _mem_block_kernel(x_ref, a1_ref, a2_ref, w1_ref, w2_ref, g_ref, wq_ref, kv_ref, wo_ref, o_ref, *, heads, hd):
    scale = hd ** -0.5
    x = (x_ref[0] + jnp.dot(a1_ref[0].astype(BF16), w1_ref[...], preferred_element_type=F32)
         + jnp.dot(a2_ref[0].astype(BF16), w2_ref[...], preferred_element_type=F32))
    q = jnp.dot(_rms(x, g_ref[...]).astype(BF16), wq_ref[...], preferred_element_type=F32)
    kt = jnp.swapaxes(kv_ref[0, :, 0], 0, 1)
    vt = jnp.swapaxes(kv_ref[0, :, 1], 0, 1)
    outs = []
    for h in range(heads):
        s = _dot_nt(q[:, h * hd:(h + 1) * hd], kt[h]) * scale
        p = jnp.exp(s - jnp.max(s, axis=-1, keepdims=True))
        p = p / jnp.sum(p, axis=-1, keepdims=True)
        outs.append(_dot(p, vt[h]))
    att = jnp.concatenate(outs, axis=1).astype(BF16)
    o_ref[0] = x + jnp.dot(att, wo_ref[...], preferred_element_type=F32)


def mem_block(x3, a1, a2, w1, w2, gain, wq, kv, wo, tq):
    b, t, d = x3.shape
    m = kv.shape[1]
    hd = d // MEM_HEADS
    assert t % tq == 0 and kv.shape[2:] == (2, MEM_HEADS, hd)
    return pl.pallas_call(
        functools.partial(_mem_block_kernel, heads=MEM_HEADS, hd=hd),
        grid=(b, t // tq),
        in_specs=[pl.BlockSpec((1, tq, d), lambda i, j: (i, j, 0)),
                  pl.BlockSpec((1, tq, a1.shape[2]), lambda i, j: (i, j, 0)),
                  pl.BlockSpec((1, tq, a2.shape[2]), lambda i, j: (i, j, 0)),
                  pl.BlockSpec(w1.shape, lambda i, j: (0, 0)),
                  pl.BlockSpec(w2.shape, lambda i, j: (0, 0)),
                  pl.BlockSpec((1, d), lambda i, j: (0, 0)),
                  pl.BlockSpec((d, d), lambda i, j: (0, 0)),
                  pl.BlockSpec((1, m, 2, MEM_HEADS, hd), lambda i, j: (i, 0, 0, 0, 0)),
                  pl.BlockSpec((d, d), lambda i, j: (0, 0))],
        out_specs=pl.BlockSpec((1, tq, d), lambda i, j: (i, j, 0)),
        out_shape=jax.ShapeDtypeStruct((b, t, d), F32),
        compiler_params=_cp(("parallel", "parallel")),
        name="mem_block",
    )(x3, a1, a2, w1, w2, gain.reshape(1, d), wq, kv, wo)


def _compress_kernel(x_ref, w1_ref, pe_ref, b1_ref, w2_ref, o_ref, acc_ref, *, row_cols, tt):
    j = pl.program_id(1)

    @pl.when(j == 0)
    def _():
        acc_ref[...] = jnp.zeros_like(acc_ref)

    hid2 = 2 * CMP_HIDDEN
    for kind in range(2):
        part = None
        for t in range(tt):
            c0 = t * row_cols + kind * LANES
            xs = x_ref[:, c0:c0 + LANES] + pe_ref[kind, t:t + 1, :]
            d = jnp.dot(xs.astype(BF16), w1_ref[kind, t], preferred_element_type=F32)
            part = d if part is None else part + d
        acc_ref[:, kind * hid2:(kind + 1) * hid2] += part

    @pl.when(j == pl.num_programs(1) - 1)
    def _():
        h = jnp.maximum(acc_ref[...] + b1_ref[...], 0.0)
        for kind in range(2):
            o_ref[:, kind * LANES:(kind + 1) * LANES] = jnp.dot(
                h[:, kind * hid2:(kind + 1) * hid2].astype(BF16), w2_ref[kind], preferred_element_type=F32)


def compress_blocks(x2, row_cols, cw, bt, tt=8):
    nb = x2.shape[0]
    assert nb % bt == 0 and CMP_BLOCK % tt == 0
    w1bd, pe2, b1bd, w2bd = cw
    return pl.pallas_call(
        functools.partial(_compress_kernel, row_cols=row_cols, tt=tt),
        grid=(nb // bt, CMP_BLOCK // tt),
        in_specs=[pl.BlockSpec((bt, tt * row_cols), lambda i, j: (i, j)),
                  pl.BlockSpec((2, tt, LANES, 2 * CMP_HIDDEN), lambda i, j: (0, j, 0, 0)),
                  pl.BlockSpec((2, tt, LANES), lambda i, j: (0, j, 0)),
                  pl.BlockSpec((1, 4 * CMP_HIDDEN), lambda i, j: (0, 0)),
                  pl.BlockSpec((2, 2 * CMP_HIDDEN, LANES), lambda i, j: (0, 0, 0))],
        out_specs=pl.BlockSpec((bt, 2 * LANES), lambda i, j: (i, 0)),
        out_shape=jax.ShapeDtypeStruct((nb, 2 * LANES), F32),
        scratch_shapes=[pltpu.VMEM((bt, 4 * CMP_HIDDEN), F32)],
        compiler_params=_cp(("parallel", "arbitrary")),
        name="compress_blocks",
    )(x2, w1bd, pe2, b1bd, w2bd)


def _compress_weights(cmp_pe, cmp_w1, cmp_b1, cmp_w2):
    w1r = cmp_w1.reshape(2, CMP_BLOCK, HEAD_DIM, CMP_HIDDEN)
    z = jnp.zeros_like(w1r)
    w1bd = jnp.concatenate([jnp.concatenate([w1r, z], -1), jnp.concatenate([z, w1r], -1)], axis=2).astype(BF16)
    pe2 = jnp.concatenate([cmp_pe, cmp_pe], -1)
    b1bd = jnp.concatenate([cmp_b1[0], cmp_b1[0], cmp_b1[1], cmp_b1[1]]).reshape(1, 4 * CMP_HIDDEN)
    z2 = jnp.zeros_like(cmp_w2)
    w2bd = jnp.concatenate([jnp.concatenate([cmp_w2, z2], -1), jnp.concatenate([z2, cmp_w2], -1)], axis=1).astype(BF16)
    return w1bd, pe2, b1bd, w2bd


def _masked_softmax(s, mask):
    s = jnp.where(mask, s, NEG_INF)
    p = jnp.where(mask, jnp.exp(s - jnp.max(s, axis=-1, keepdims=True)), 0.0)
    return p / jnp.maximum(jnp.sum(p, axis=-1, keepdims=True), 1e-30)


def _heads_as_rows(q_ref, tq, scale=ATTN_SCALE):
    qb = q_ref[0] * scale
    return jnp.concatenate([qb[:, h * HEAD_DIM:(h + 1) * HEAD_DIM] for h in range(NSA_GROUP)], axis=0).astype(BF16)


def _cmp_topk_kernel(q_ref, kc_ref, vct_ref, ocmp_ref, sel_ref, *, tq, nblk, topn):
    qi = pl.program_id(2)
    r, dh = NSA_GROUP, HEAD_DIM
    qpos = qi * tq + lax.broadcasted_iota(jnp.int32, (nblk, tq), 1)
    blk = lax.broadcasted_iota(jnp.int32, (nblk, tq), 0)
    vis = (blk + 1) * CMP_BLOCK - 1 <= qpos
    q = _heads_as_rows(q_ref, tq)
    kc = kc_ref[0, 0]
    vct = vct_ref[0, 0]
    imp = jnp.zeros((nblk, tq), F32)
    for h in range(r):
        s = jnp.where(vis, _dot_nt(kc, q[h * tq:(h + 1) * tq]), NEG_INF)
        p = jnp.where(vis, jnp.exp(s - jnp.max(s, axis=0, keepdims=True)), 0.0)
        p = p / jnp.maximum(jnp.sum(p, axis=0, keepdims=True), 1e-30)
        ocmp_ref[0, 0, h * dh:(h + 1) * dh, :] = _dot(vct, p)
        imp = imp + p
    cur = lax.shift_right_logical(qpos, 6)
    valid = blk <= cur
    forced = valid & ((blk == 0) | (cur - blk < N_LOCAL_BLOCKS))
    score = jnp.where(valid, imp + jnp.where(forced, FORCE_SCORE, 0.0), -1.0)
    rank = jnp.zeros((nblk, tq), F32)
    row8 = lax.broadcasted_iota(jnp.int32, (SUBLANES, tq), 0)
    for i in range(nblk):
        si = score[i:i + 1, :]
        t0 = i // SUBLANES * SUBLANES
        mid = score[t0:t0 + SUBLANES]
        parts = [jnp.where(row8 > i - t0, jnp.where(si >= mid, 1.0, 0.0), jnp.where(si > mid, 1.0, 0.0))]
        if t0 > 0:
            parts.insert(0, jnp.where(si > score[0:t0], 1.0, 0.0))
        if t0 + SUBLANES < nblk:
            parts.append(jnp.where(si >= score[t0 + SUBLANES:], 1.0, 0.0))
        rank = rank + jnp.concatenate(parts, axis=0)
    sel_ref[0, 0] = jnp.where(rank < topn, 1.0, 0.0).astype(BF16)


def nsa_cmp_topk(proj3, kc, vct, tq):
    b, sq, _ = proj3.shape
    g = kc.shape[1]
    nblk = kc.shape[2]
    dh = HEAD_DIM
    gw = NSA_GROUP * dh
    assert sq % tq == 0 and SEL_BLOCK == 64 and P_NQ % gw == 0 and nblk % SUBLANES == 0
    return pl.pallas_call(
        functools.partial(_cmp_topk_kernel, tq=tq, nblk=nblk, topn=min(TOP_N, nblk)),
        grid=(b, g, sq // tq),
        in_specs=[pl.BlockSpec((1, tq, gw), lambda i, j, k: (i, k, P_NQ // gw + j)),
                  pl.BlockSpec((1, 1, nblk, dh), lambda i, j, k: (i, j, 0, 0)),
                  pl.BlockSpec((1, 1, dh, nblk), lambda i, j, k: (i, j, 0, 0))],
        out_specs=[pl.BlockSpec((1, 1, gw, tq), lambda i, j, k: (i, j, 0, k)),
                   pl.BlockSpec((1, 1, nblk, tq), lambda i, j, k: (i, j, 0, k))],
        out_shape=[jax.ShapeDtypeStruct((b, g, gw, sq), F32),
                   jax.ShapeDtypeStruct((b, g, nblk, sq), BF16)],
        compiler_params=_cp(("parallel", "parallel", "parallel")),
        name="nsa_cmp_topk",
    )(proj3, kc, vct)


def _nsa_attn_kernel(q_ref, kst_ref, vst_ref, kwt_ref, vwt_ref, selt_ref, ocmp_ref, sm_ref, o_ref,
                     m_sc, acc_sc, s_sc, bias_sc, p_sc, *, tq, tk, nsel, ck):
    grp = pl.program_id(1)
    qi = pl.program_id(2)
    r = NSA_GROUP
    dh = HEAD_DIM
    q = _heads_as_rows(q_ref, tq, ATTN_SCALE * LOG2_E)
    qpos = qi * tq + lax.broadcasted_iota(jnp.int32, (1, tq), 1)
    ones_row = jnp.where(lax.broadcasted_iota(jnp.int32, (8, 1), 0) == 0, 1.0, 0.0)

    def reset():
        m_sc[...] = jnp.full(m_sc.shape, NEG_INF, F32)
        acc_sc[...] = jnp.zeros(acc_sc.shape, F32)

    def with_ones(vt):
        return jnp.concatenate([vt, jnp.broadcast_to(ones_row, (8, vt.shape[1]))], axis=0).astype(BF16)

    def scores(kt, qrows, row0):
        s_sc[row0:row0 + kt.shape[1], :] = _dot_nt(kt.T, qrows)

    def set_bias(mask, row0):
        bias_sc[pl.ds(row0, mask.shape[0]), :] = jnp.where(mask, 0.0, NEG_INF)

    def fold(row0, n, vt, biased):
        def chunk(c, shift):
            rows = pl.ds(row0 + c * ck, ck)
            x = s_sc[rows, :] - shift
            return x + jnp.concatenate([bias_sc[rows, :]] * r, axis=1) if biased else x

        m_prev = m_sc[...]
        top = jnp.full((SUBLANES, r * tq), NEG_INF, F32)
        for c in range(n // ck):
            top = jnp.maximum(top, jnp.max(chunk(c, 0.0).reshape(ck // SUBLANES, SUBLANES, r * tq), axis=0))
        m_new = jnp.maximum(m_prev, jnp.max(top, axis=0, keepdims=True))
        for c in range(n // ck):
            p_sc[c * ck:(c + 1) * ck, :] = jnp.exp2(chunk(c, m_new)).astype(BF16)
        acc_sc[...] = jnp.exp2(m_prev - m_new) * acc_sc[...] + jnp.dot(with_ones(vt), p_sc[0:n, :],
                                                                      preferred_element_type=F32)
        m_sc[...] = m_new

    def result():
        acc = acc_sc[...]
        return acc[0:dh] / jnp.maximum(acc[dh:dh + 1], 1e-30)

    wk = WINDOW + tq
    w0 = pl.multiple_of(jnp.maximum(qi - WINDOW // tq, 0) * tq, tq)
    dpos = qpos - (w0 + lax.broadcasted_iota(jnp.int32, (wk, 1), 0))
    set_bias((dpos >= 0) & (dpos < WINDOW), 0)
    scores(kwt_ref[0, :, pl.ds(w0, wk)], q, 0)

    reset()
    block_bias = jnp.where(selt_ref[0, 0].astype(F32).T > 0.5, 0.0, NEG_INF).astype(BF16)
    q_sel = jnp.concatenate([q, jnp.concatenate([block_bias] * r, axis=0)], axis=1)
    blk = lax.broadcasted_iota(jnp.int32, (nsel, tk), 0)
    krow = lax.broadcasted_iota(jnp.int32, (nsel, tk), 1)

    def slc_scores(k0, row0):
        code = jnp.where(lax.shift_right_logical(k0 + krow, CMP_BLOCK.bit_length() - 1) == blk, 1.0, 0.0)
        scores(jnp.concatenate([kst_ref[0, :, pl.ds(k0, tk)], code], axis=0), q_sel, row0)

    def slc_fold(k0, row0, biased):
        fold(row0, tk, vst_ref[0, :, pl.ds(k0, tk)], biased)

    def slc_pair(c, carry):
        ka = pl.multiple_of(2 * c * tk, tk)
        slc_scores(ka + tk, wk + tk)
        slc_fold(ka, wk, False)
        slc_scores(ka + 2 * tk, wk)
        slc_fold(ka + tk, wk + tk, False)
        return carry

    n_below = (qi * tq) // tk
    kd = pl.multiple_of(n_below * tk, tk)
    odd = n_below % 2
    slc_scores(0, wk)
    lax.fori_loop(0, n_below // 2, slc_pair, 0)

    @pl.when(odd == 1)
    def _():
        slc_scores(kd, wk + tk)
        slc_fold(kd - tk, wk, False)

    row_d = pl.multiple_of(wk + odd * tk, ck)
    set_bias(kd + lax.broadcasted_iota(jnp.int32, (tk, 1), 0) <= qpos, row_d)
    slc_fold(kd, row_d, True)
    o_slc = result()

    reset()
    fold(0, wk, vwt_ref[0, :, pl.ds(w0, wk)], True)
    o_win = result()

    gates_t = jax.nn.sigmoid(sm_ref[0]).T
    g0 = 2 * DN_HEADS
    per = 3 * r
    gt = jnp.where(grp == 0, gates_t[g0:g0 + per], gates_t[g0 + per:g0 + 2 * per])
    outs = []
    for h in range(r):
        outs.append(gt[3 * h:3 * h + 1] * ocmp_ref[0, 0, h * dh:(h + 1) * dh, :]
                    + gt[3 * h + 1:3 * h + 2] * o_slc[:, h * tq:(h + 1) * tq]
                    + gt[3 * h + 2:3 * h + 3] * o_win[:, h * tq:(h + 1) * tq])
    o_ref[0] = jnp.concatenate(outs, axis=0).T


def nsa_attention(proj3, kvt, kvt_win, selt, ocmpt, tq, tk):
    b, sq, _ = proj3.shape
    t = kvt.shape[2]
    g, nsel = selt.shape[1], selt.shape[2]
    r, dh = NSA_GROUP, HEAD_DIM
    gw = r * dh
    assert sq == t and sq % tq == 0 and t % tk == 0 and tk % tq == 0 and WINDOW % tq == 0 and tq % LANES == 0
    assert t >= WINDOW + tq and g == NSA_KV_HEADS and g == 2
    ck = 32
    assert tk % ck == 0 and (WINDOW + tq) % ck == 0 and tk <= WINDOW + tq
    rows = WINDOW + tq + 2 * tk

    def kv_spec(kind):
        return pl.BlockSpec((1, dh, t), lambda i, j, k: (i, kind * g + j, 0))

    return pl.pallas_call(
        functools.partial(_nsa_attn_kernel, tq=tq, tk=tk, nsel=nsel, ck=ck),
        grid=(b, g, sq // tq),
        in_specs=[pl.BlockSpec((1, tq, gw), lambda i, j, k: (i, k, P_NQ // gw + j)),
                  kv_spec(2), kv_spec(3), kv_spec(0), kv_spec(1),
                  pl.BlockSpec((1, 1, nsel, tq), lambda i, j, k: (i, j, 0, k)),
                  pl.BlockSpec((1, 1, gw, tq), lambda i, j, k: (i, j, 0, k)),
                  pl.BlockSpec((1, tq, LANES), lambda i, j, k: (i, k, P_SMALL // LANES))],
        out_specs=pl.BlockSpec((1, tq, gw), lambda i, j, k: (i, k, j)),
        out_shape=jax.ShapeDtypeStruct((b, sq, g * gw), F32),
        scratch_shapes=[pltpu.VMEM((1, r * tq), F32), pltpu.VMEM((dh + 8, r * tq), F32),
                        pltpu.VMEM((rows, r * tq), F32), pltpu.VMEM((rows, tq), F32),
                        pltpu.VMEM((WINDOW + tq, r * tq), BF16)],
        compiler_params=_cp(("parallel", "parallel", "parallel")),
        name="nsa_attention",
    )(proj3, kvt, kvt, kvt_win, kvt_win, selt, ocmpt, proj3)


def _bdot(a, b, passes=1):
    dims = (((2,), (1,)), ((0,), (0,)))
    if passes == 1:
        return lax.dot_general(a.astype(BF16), b.astype(BF16), dims, preferred_element_type=F32)
    ah, al = _split2(a)
    bh, bl = _split2(b)
    return lax.dot_general(jnp.concatenate([ah, ah, al], axis=2), jnp.concatenate([bh, bl, bh], axis=1), dims,
                           preferred_element_type=F32)


def _bdot_nt(a, b):
    return lax.dot_general(a.astype(BF16), b.astype(BF16), (((2,), (2,)), ((0,), (0,))),
                           preferred_element_type=F32)


def _deltanet_kernel(qkv_ref, z_ref, sm_ref, cw_ref, alog_ref, dtb_ref, gn_ref, o_ref, s_out_ref,
                     xbuf, s_sc):
    c = pl.program_id(0)
    ch = DN_CHUNK
    n_pairs = DN_HEADS // 2
    two = 2 * ch
    n_batch = qkv_ref.shape[0]

    @pl.when(c == 0)
    def _():
        xbuf[:, 0:8, :] = jnp.zeros((n_batch, 8, DN_CONV_DIM), F32)
        s_sc[...] = jnp.zeros_like(s_sc)

    ti = lax.broadcasted_iota(jnp.int32, (ch, ch), 0)
    tj = lax.broadcasted_iota(jnp.int32, (ch, ch), 1)
    tri = jnp.where(ti >= tj, 1.0, 0.0)
    lane = lax.broadcasted_iota(jnp.int32, (ch, LANES), 1)
    lo = lane < HEAD_DIM
    row2 = lax.broadcasted_iota(jnp.int32, (two, two), 0)
    col2 = lax.broadcasted_iota(jnp.int32, (two, two), 1)
    same = (row2 >= ch) == (col2 >= ch)
    incl = (same & (row2 >= col2))[None]
    strict = (same & (row2 > col2))[None]
    top = lax.broadcasted_iota(jnp.int32, (two, 1), 0) < ch

    def seg_sum(x):
        s_lo = jnp.sum(jnp.where(lo, x, 0.0), axis=-1, keepdims=True)
        s_hi = jnp.sum(jnp.where(lo, 0.0, x), axis=-1, keepdims=True)
        return jnp.where(lo, s_lo, s_hi)

    def stack2(x):
        return jnp.concatenate([jnp.where(lo, x, 0.0), jnp.where(lo, 0.0, x)], axis=0)

    def col2x(a, b):
        return jnp.concatenate([jnp.broadcast_to(a, (ch, LANES)), jnp.broadcast_to(b, (ch, LANES))], axis=0)

    q_l, k_l, v_l, beta_l, gc_l, gl_l = [], [], [], [], [], []
    for bi in range(n_batch):
        xbuf[bi, 8:8 + ch, :] = qkv_ref[bi]
        conv = None
        for w in range(CONV_WIDTH):
            term = xbuf[bi, 5 + w:5 + w + ch, :] * cw_ref[w:w + 1, :]
            conv = term if conv is None else conv + term
        xbuf[bi, 0:8, :] = xbuf[bi, ch:ch + 8, :]
        act = _silu(conv)
        sm = sm_ref[bi]
        beta_all = jax.nn.sigmoid(sm)
        g_all = -jnp.exp(alog_ref[...]) * _softplus(sm + dtb_ref[...])
        gcum_all = _dot_exact_lhs01(tri, g_all)
        for p in range(n_pairs):
            c0 = p * LANES
            qp = act[:, c0:c0 + LANES]
            kp = act[:, DN_WIDTH + c0:DN_WIDTH + c0 + LANES]
            vp = act[:, 2 * DN_WIDTH + c0:2 * DN_WIDTH + c0 + LANES]
            qp = qp * lax.rsqrt(seg_sum(qp * qp) + 1e-6) * (HEAD_DIM ** -0.5)
            kp = kp * lax.rsqrt(seg_sum(kp * kp) + 1e-6)
            h0, h1 = DN_HEADS + 2 * p, DN_HEADS + 2 * p + 1
            q_l.append(stack2(qp))
            k_l.append(stack2(kp))
            v_l.append(stack2(vp))
            beta_l.append(col2x(beta_all[:, 2 * p:2 * p + 1], beta_all[:, 2 * p + 1:2 * p + 2]))
            gc_l.append(col2x(gcum_all[:, h0:h0 + 1], gcum_all[:, h1:h1 + 1]))
            gl_l.append(jnp.broadcast_to(jnp.where(top, gcum_all[ch - 1:ch, h0:h0 + 1], gcum_all[ch - 1:ch, h1:h1 + 1]),
                                         (two, LANES)))
    q2, k2, v2 = jnp.stack(q_l), jnp.stack(k_l), jnp.stack(v_l)
    beta2, gc2, gl2 = jnp.stack(beta_l), jnp.stack(gc_l), jnp.stack(gl_l)
    decay = jnp.exp(jnp.where(incl, gc2 - jnp.swapaxes(gc2, 1, 2), NEG_INF))
    kb2 = k2 * beta2
    a_mat = jnp.where(strict, _bdot_nt(kb2, k2) * decay, 0.0)
    aqk = jnp.where(incl, _bdot_nt(q2, k2) * decay, 0.0)
    s_old = s_sc[...]
    egc = jnp.exp(gc2)
    x = beta2 * (v2 - egc * _bdot(k2, s_old))
    pw = -a_mat
    n_lvl = ch.bit_length() - 1
    for lvl in range(n_lvl):
        x = x + _bdot(pw, x, DN_APPLY_PASSES[lvl])
        if lvl + 1 < n_lvl:
            pw = _bdot(pw, pw, DN_SQUARE_PASSES[lvl])
    o2 = _bdot(q2 * egc, s_old) + _bdot(aqk, x)
    kdec = k2 * jnp.exp(gl2 - gc2)
    s_sc[...] = s_old * jnp.exp(gl2) + _bdot(jnp.swapaxes(kdec, 1, 2), x)
    for bi in range(n_batch):
        for p in range(n_pairs):
            c0 = p * LANES
            o_n = o2[bi * n_pairs + p]
            o_pair = o_n[0:ch] + o_n[ch:two]
            inv = lax.rsqrt(seg_sum(o_pair * o_pair) * (1.0 / HEAD_DIM) + RMS_EPS)
            o_ref[bi, :, c0:c0 + LANES] = o_pair * inv * gn_ref[...] * _silu(z_ref[bi, :, c0:c0 + LANES])

    @pl.when(c == pl.num_programs(0) - 1)
    def _():
        s_out_ref[...] = s_sc[...]


def deltanet_prompt(proj3, conv_w, a_log, dt_bias, norm_gain):
    b, t, _ = proj3.shape
    ch = DN_CHUNK
    assert t % ch == 0
    pad = jnp.zeros((LANES - 2 * DN_HEADS,), F32)
    alog_row = jnp.concatenate([jnp.zeros((DN_HEADS,), F32), a_log, pad]).reshape(1, LANES)
    dtb_row = jnp.concatenate([jnp.zeros((DN_HEADS,), F32), dt_bias, pad]).reshape(1, LANES)
    gn_row = jnp.concatenate([norm_gain, norm_gain]).reshape(1, LANES)
    n_pairs = DN_HEADS // 2
    o, s_fin = pl.pallas_call(
        _deltanet_kernel,
        grid=(t // ch,),
        in_specs=[pl.BlockSpec((b, ch, DN_CONV_DIM), lambda j: (0, j, P_QKV // DN_CONV_DIM)),
                  pl.BlockSpec((b, ch, DN_WIDTH), lambda j: (0, j, P_Z // DN_WIDTH)),
                  pl.BlockSpec((b, ch, LANES), lambda j: (0, j, P_SMALL // LANES)),
                  pl.BlockSpec((CONV_WIDTH, DN_CONV_DIM), lambda j: (0, 0)),
                  pl.BlockSpec((1, LANES), lambda j: (0, 0)),
                  pl.BlockSpec((1, LANES), lambda j: (0, 0)),
                  pl.BlockSpec((1, LANES), lambda j: (0, 0))],
        out_specs=[pl.BlockSpec((b, ch, DN_WIDTH), lambda j: (0, j, 0)),
                   pl.BlockSpec((b * n_pairs, 2 * ch, LANES), lambda j: (0, 0, 0))],
        out_shape=[jax.ShapeDtypeStruct((b, t, DN_WIDTH), F32),
                   jax.ShapeDtypeStruct((b * n_pairs, 2 * ch, LANES), F32)],
        scratch_shapes=[pltpu.VMEM((b, ch + 8, DN_CONV_DIM), F32), pltpu.VMEM((b * n_pairs, 2 * ch, LANES), F32)],
        compiler_params=_cp(("arbitrary",)),
        name="deltanet_prompt",
    )(proj3, proj3, proj3, conv_w, alog_row, dtb_row, gn_row)
    return o, s_fin.reshape(b, n_pairs, 2 * ch, LANES)


def _pairs_to_heads(s_pairs):
    d = HEAD_DIM
    return jnp.stack([s_pairs[:, :, :d, :d], s_pairs[:, :, d:, d:]], axis=2).reshape(
        s_pairs.shape[0], DN_HEADS, d, d)


def _dn_step_prep_kernel(qkv_ref, cs_ref, sm_ref, cw_ref, alog_ref, dtb_ref, ones_ref,
                         q_ref, k_ref, v_ref, sc_ref):
    conv = qkv_ref[...] * cw_ref[CONV_WIDTH - 1:CONV_WIDTH, :]
    for w in range(CONV_WIDTH - 1):
        conv = conv + cs_ref[w] * cw_ref[w:w + 1, :]
    act = _silu(conv)
    q = act[:, 0:DN_WIDTH]
    k = act[:, DN_WIDTH:2 * DN_WIDTH]

    def seg_sum(x):
        return _dot_exact_rhs01(x, ones_ref[...])

    q_ref[...] = q * lax.rsqrt(seg_sum(q * q) + 1e-6) * (HEAD_DIM ** -0.5)
    k_ref[...] = k * lax.rsqrt(seg_sum(k * k) + 1e-6)
    v_ref[...] = act[:, 2 * DN_WIDTH:]
    sm = sm_ref[...]
    g = -jnp.exp(alog_ref[...]) * _softplus(sm + dtb_ref[...])
    lane = lax.broadcasted_iota(jnp.int32, sm.shape, 1)
    sc_ref[...] = jnp.where(lane < DN_HEADS, jax.nn.sigmoid(sm), jnp.exp(g))


def _dn_step_kernel(k_ref, q_ref, v_ref, be_ref, eg_ref, z_ref, gn_ref, s_ref, o_ref, s_out_ref):
    s_old = s_ref[0]
    k, q, v = k_ref[0], q_ref[0], v_ref[0]
    beta, eg = be_ref[0], eg_ref[0]
    ks = jnp.sum(k[:, None, :] * s_old, axis=0)
    qs = jnp.sum(q[:, None, :] * s_old, axis=0)
    qk = jnp.sum(q * k, axis=0, keepdims=True)
    v_new = beta * (v - eg * ks)
    o = eg * qs + qk * v_new
    inv = lax.rsqrt(jnp.mean(o * o, axis=0, keepdims=True) + RMS_EPS)
    o_ref[0] = o * inv * gn_ref[...] * _silu(z_ref[0])
    s_out_ref[0] = s_old * eg[None] + k[:, None, :] * v_new[None]


def deltanet_sample(proj_s, conv_state, rec_state, conv_w, a_log, dt_bias, norm_gain):
    n = proj_s.shape[0]
    hds, d = DN_HEADS, HEAD_DIM
    pad = jnp.zeros((LANES - 2 * hds,), F32)
    alog_row = jnp.concatenate([jnp.zeros((hds,), F32), a_log, pad]).reshape(1, LANES)
    dtb_row = jnp.concatenate([jnp.zeros((hds,), F32), dt_bias, pad]).reshape(1, LANES)
    head_of = jnp.arange(DN_WIDTH) // d
    ones_bd = (head_of[:, None] == head_of[None, :]).astype(BF16)
    cs = jnp.transpose(conv_state, (1, 0, 2))
    full = lambda shape: pl.BlockSpec(shape, lambda i: (0,) * len(shape))
    q, k, v, sc = pl.pallas_call(
        _dn_step_prep_kernel,
        grid=(1,),
        in_specs=[pl.BlockSpec((n, DN_CONV_DIM), lambda i: (0, P_QKV // DN_CONV_DIM)),
                  full((CONV_WIDTH - 1, n, DN_CONV_DIM)),
                  pl.BlockSpec((n, LANES), lambda i: (0, P_SMALL // LANES)),
                  full((CONV_WIDTH, DN_CONV_DIM)), full((1, LANES)), full((1, LANES)),
                  full((DN_WIDTH, DN_WIDTH))],
        out_specs=[full((n, DN_WIDTH)), full((n, DN_WIDTH)), full((n, DN_WIDTH)), full((n, LANES))],
        out_shape=[jax.ShapeDtypeStruct((n, DN_WIDTH), F32)] * 3 + [jax.ShapeDtypeStruct((n, LANES), F32)],
        compiler_params=_cp(("arbitrary",)),
        name="dn_step_prep",
    )(proj_s, cs, proj_s, conv_w, alog_row, dtb_row, ones_bd)
    t3 = lambda a: a.T.reshape(hds, d, n)
    sct = sc[:, :2 * hds].T.reshape(2, hds, 1, n)
    head_vec = pl.BlockSpec((1, d, n), lambda i: (i, 0, 0))
    head_scl = pl.BlockSpec((1, 1, n), lambda i: (i, 0, 0))
    state = pl.BlockSpec((1, d, d, n), lambda i: (i, 0, 0, 0))
    o, s_new = pl.pallas_call(
        _dn_step_kernel,
        grid=(hds,),
        in_specs=[head_vec, head_vec, head_vec, head_scl, head_scl, head_vec,
                  pl.BlockSpec((d, 1), lambda i: (0, 0)), state],
        out_specs=[head_vec, state],
        out_shape=[jax.ShapeDtypeStruct((hds, d, n), F32), jax.ShapeDtypeStruct((hds, d, d, n), F32)],
        compiler_params=_cp(("parallel",)),
        name="dn_step",
    )(t3(k), t3(q), t3(v), sct[0], sct[1], t3(proj_s[:, P_Z:P_Z + DN_WIDTH]), norm_gain.reshape(d, 1),
      jnp.transpose(rec_state, (1, 2, 3, 0)))
    return o.reshape(DN_WIDTH, n).T, jnp.transpose(s_new, (3, 0, 1, 2))


def _nsa_decode_kernel(pt_ref, q_ref, new_ref, gate_ref, win_ref, newc_ref, exp_ref, *refs, n_pages, seqs):
    del pt_ref
    o_ref, swin_ref, kc_sc = refs[2 * seqs * n_pages:2 * seqs * n_pages + 3]
    dh, r, g2, nh = HEAD_DIM, NSA_GROUP, NSA_KV_HEADS, NSA_HEADS
    past = n_pages * PAGE_SIZE
    nb = past // CMP_BLOCK
    nsel = nb + 1
    wlen = win_ref.shape[4]
    for u in range(seqs):
        for j in range(n_pages):
            kc_sc[u, j:j + 1, :] = refs[u * n_pages + j][0]
    kcv = kc_sc[...]
    new = new_ref[...]
    q8 = q_ref[...] * ATTN_SCALE
    head = lax.broadcasted_iota(jnp.int32, (1, nh, 1), 1)
    lane = lax.broadcasted_iota(jnp.int32, (1, 1, LANES), 2)
    g0 = head < r
    blk = jnp.where(lane < n_pages, 2 * lane, jnp.where(lane < nb, 2 * (lane - n_pages) + 1, lane))
    wp = lax.broadcasted_iota(jnp.int32, (1, 1, wlen + LANES), 2)
    dpos = wlen - wp
    wmask = (dpos >= 0) & (dpos < WINDOW) & (past - wlen + wp >= 0)

    def both(x):
        return jnp.where(g0, x[:, :, 0:dh], x[:, :, dh:2 * dh])

    def new_part(kind):
        return both(jnp.broadcast_to(new[:, :, 2 * kind * dh:2 * (kind + 1) * dh], (seqs, nh, 2 * dh)))

    def pages(kind):
        return jnp.stack([jnp.concatenate([refs[(seqs + u) * n_pages + j][0, kind].reshape(2 * dh, PAGE_SIZE)
                                           for j in range(n_pages)], axis=1) for u in range(seqs)])

    q_bd = jnp.concatenate([jnp.where(g0, q8, 0.0), jnp.where(g0, 0.0, q8)], axis=2)
    s_new = jnp.sum(q8 * new_part(2), axis=-1, keepdims=True)
    s_all = jnp.concatenate([_bdot(q_bd, pages(0)), jnp.broadcast_to(s_new, (seqs, nh, LANES))], axis=2)
    sw = _bdot(q_bd, win_ref[:, 0].reshape(seqs, 2 * dh, wlen))
    sw_new = jnp.sum(q8 * new_part(4), axis=-1, keepdims=True)

    def cmp_rows(base):
        even = jnp.concatenate([kcv[:, :, base:base + dh], kcv[:, :, base + LANES:base + LANES + dh]], axis=2)
        odd = jnp.concatenate([kcv[:, :, base + dh:base + LANES], kcv[:, :, base + LANES + dh:base + 2 * LANES]], axis=2)
        return jnp.concatenate([even, odd], axis=1)

    s = _bdot_nt(q_bd, cmp_rows(0))
    p = jnp.exp(s - jnp.max(s, axis=-1, keepdims=True))
    p = p / jnp.maximum(jnp.sum(p, axis=-1, keepdims=True), 1e-30)
    o_cmp = both(_bdot(p, cmp_rows(g2 * LANES)))
    bid_row = jnp.broadcast_to(blk.astype(F32), (1, LANES, LANES))
    bid_col = jnp.swapaxes(bid_row, 1, 2)
    valid = lane < nsel
    forced = valid & ((blk == 0) | (nb - blk < N_LOCAL_BLOCKS))
    sels = []
    for g in range(g2):
        in_g = (head >= g * r) & (head < (g + 1) * r)
        imp = jnp.sum(jnp.where(in_g, p, 0.0), axis=1, keepdims=True)
        imp = jnp.concatenate([imp, jnp.zeros((seqs, 1, LANES - nb), F32)], axis=2)
        score = jnp.where(valid, imp + jnp.where(forced, FORCE_SCORE, 0.0), -1.0)
        sc_row = jnp.broadcast_to(score, (seqs, LANES, LANES))
        sc_col = jnp.swapaxes(sc_row, 1, 2)
        beats = (sc_col > sc_row) | ((sc_col == sc_row) & (bid_col < bid_row))
        rank = jnp.sum(jnp.where(beats, 1.0, 0.0), axis=1, keepdims=True)
        sels.append(jnp.where(valid & (rank < min(TOP_N, nsel)), 1.0, 0.0))
    sel8 = jnp.where(g0, sels[0], sels[1]).astype(BF16)
    chosen = jnp.dot(sel8.reshape(seqs * nh, LANES), exp_ref[...], preferred_element_type=F32) > 0.5
    pm = _masked_softmax(s_all, chosen.reshape(seqs, nh, past + LANES))
    o_slc = pm[:, :, past:past + 1] * new_part(3) + both(_bdot_nt(pm[:, :, 0:past], pages(1)))
    pw = _masked_softmax(jnp.concatenate([sw, jnp.broadcast_to(sw_new, (seqs, nh, LANES))], axis=2), wmask)
    o_win = pw[:, :, wlen:wlen + 1] * new_part(5) + both(
        _bdot_nt(pw[:, :, 0:wlen], win_ref[:, 1].reshape(seqs, 2 * dh, wlen)))
    gates = jnp.broadcast_to(jax.nn.sigmoid(gate_ref[...]), (seqs, nh, LANES))

    def gate(branch):
        return jnp.sum(jnp.where(lane == 3 * head + branch, gates, 0.0), axis=-1, keepdims=True)

    o_ref[...] = gate(0) * o_cmp + gate(1) * o_slc + gate(2) * o_win
    wpos = lax.broadcasted_iota(jnp.int32, (1, 1, 1, 1, wlen), 4)
    swin_ref[...] = jnp.where(wpos == wlen - 1, newc_ref[...], pltpu.roll(win_ref[...], wlen - 1, 4))


def nsa_decode(page_table, q3, new_row, gate_row, win_t, new_win, kc_phys, cache_t, seqs=8):
    n, n_pages = page_table.shape
    dh = HEAD_DIM
    g2 = NSA_KV_HEADS
    wlen = win_t.shape[4]
    assert 2 * n_pages + 1 <= LANES and PAGE_SIZE == 2 * CMP_BLOCK and PAGE_SIZE == LANES and g2 == 2

    past = n_pages * PAGE_SIZE
    nb = past // CMP_BLOCK
    erow = lax.broadcasted_iota(jnp.int32, (LANES, past + LANES), 0)
    ecol = lax.broadcasted_iota(jnp.int32, (LANES, past + LANES), 1)
    page, second = ecol // PAGE_SIZE, (ecol % PAGE_SIZE) >= CMP_BLOCK
    expand = (((erow < n_pages) & (page == erow) & ~second & (ecol < past))
              | ((erow >= n_pages) & (erow < nb) & (page == erow - n_pages) & second & (ecol < past))
              | ((erow == nb) & (ecol == past))).astype(BF16)

    assert n % seqs == 0

    def kc_map(u, j):
        return lambda i, pt: (pt[seqs * i + u, j], 0, 0)

    def slc_map(u, j):
        return lambda i, pt: (pt[seqs * i + u, j], 1, 0, 0, 0)

    in_specs = [pl.BlockSpec((seqs, NSA_HEADS, dh), lambda i, pt: (i, 0, 0)),
                pl.BlockSpec((seqs, 1, KV_COLS), lambda i, pt: (i, 0, 0)),
                pl.BlockSpec((seqs, 1, LANES), lambda i, pt: (i, 0, 0)),
                pl.BlockSpec((seqs, 2, g2, dh, wlen), lambda i, pt: (i, 0, 0, 0, 0)),
                pl.BlockSpec((seqs, 2, g2, dh, 1), lambda i, pt: (i, 0, 0, 0, 0)),
                pl.BlockSpec((LANES, past + LANES), lambda i, pt: (0, 0))]
    in_specs += [pl.BlockSpec((1, 1, 4 * LANES), kc_map(u, j)) for u in range(seqs) for j in range(n_pages)]
    in_specs += [pl.BlockSpec((1, 2, g2, dh, PAGE_SIZE), slc_map(u, j)) for u in range(seqs) for j in range(n_pages)]
    grid_spec = pltpu.PrefetchScalarGridSpec(
        num_scalar_prefetch=1, grid=(n // seqs,), in_specs=in_specs,
        out_specs=[pl.BlockSpec((seqs, NSA_HEADS, dh), lambda i, pt: (i, 0, 0)),
                   pl.BlockSpec((seqs, 2, g2, dh, wlen), lambda i, pt: (i, 0, 0, 0, 0))],
        scratch_shapes=[pltpu.VMEM((seqs, n_pages, 4 * LANES), F32)])
    return pl.pallas_call(
        functools.partial(_nsa_decode_kernel, n_pages=n_pages, seqs=seqs),
        grid_spec=grid_spec,
        out_shape=[jax.ShapeDtypeStruct((n, NSA_HEADS, dh), F32), jax.ShapeDtypeStruct(win_t.shape, F32)],
        compiler_params=_cp(("arbitrary",)),
        name="nsa_decode",
    )(page_table, q3, new_row, gate_row, win_t, new_win, expand,
      *([kc_phys] * (seqs * n_pages)), *([cache_t] * (seqs * n_pages)))


def _compress_pages_kernel(x_ref, wd_ref, ped_ref, b1_ref, w2_ref, o_ref, acc_ref, *, dd):
    j = pl.program_id(1)
    g2 = NSA_KV_HEADS

    @pl.when(j == 0)
    def _():
        acc_ref[...] = jnp.zeros_like(acc_ref)

    for kind in range(2):
        for g in range(g2):
            part = None
            xt = jnp.swapaxes(x_ref[:, kind, g], 0, 1)
            for dp in range(dd // 2):
                xs = jnp.concatenate([xt[2 * dp] + ped_ref[kind, 2 * dp:2 * dp + 1, :],
                                      xt[2 * dp + 1] + ped_ref[kind, 2 * dp + 1:2 * dp + 2, :]], axis=1)
                d = jnp.dot(xs.astype(BF16), wd_ref[kind, dp], preferred_element_type=F32)
                part = d if part is None else part + d
            acc_ref[kind * g2 + g] += part

    @pl.when(j == pl.num_programs(1) - 1)
    def _():
        for kind in range(2):
            for g in range(g2):
                h = jnp.maximum(acc_ref[kind * g2 + g] + b1_ref[kind:kind + 1, :], 0.0)
                o_ref[:, 0, (kind * g2 + g) * LANES:(kind * g2 + g + 1) * LANES] = jnp.dot(
                    h.astype(BF16), w2_ref[kind], preferred_element_type=F32)


def compress_pages(cache_t, cwp, bp, dd=8):
    n_phys = cache_t.shape[0]
    g2, dh = NSA_KV_HEADS, HEAD_DIM
    assert n_phys % bp == 0 and dh % dd == 0 and dd % 2 == 0
    wd, ped, b1h, w2h = cwp
    return pl.pallas_call(
        functools.partial(_compress_pages_kernel, dd=dd),
        grid=(n_phys // bp, dh // dd),
        in_specs=[pl.BlockSpec((bp, 2, g2, dd, PAGE_SIZE), lambda i, j: (i, 0, 0, j, 0)),
                  pl.BlockSpec((2, dd // 2, 2 * PAGE_SIZE, 2 * CMP_HIDDEN), lambda i, j: (0, j, 0, 0)),
                  pl.BlockSpec((2, dd, PAGE_SIZE), lambda i, j: (0, j, 0)),
                  pl.BlockSpec((2, 2 * CMP_HIDDEN), lambda i, j: (0, 0)),
                  pl.BlockSpec((2, 2 * CMP_HIDDEN, LANES), lambda i, j: (0, 0, 0))],
        out_specs=pl.BlockSpec((bp, 1, 2 * g2 * LANES), lambda i, j: (i, 0, 0)),
        out_shape=jax.ShapeDtypeStruct((n_phys, 1, 2 * g2 * LANES), F32),
        scratch_shapes=[pltpu.VMEM((2 * g2, bp, 2 * CMP_HIDDEN), F32)],
        compiler_params=_cp(("parallel", "arbitrary")),
        name="compress_pages",
    )(cache_t, wd, ped, b1h, w2h)


def _compress_page_weights(cmp_pe, cmp_w1, cmp_b1, cmp_w2):
    w1t = jnp.transpose(cmp_w1.reshape(2, CMP_BLOCK, HEAD_DIM, CMP_HIDDEN), (0, 2, 1, 3))
    z = jnp.zeros_like(w1t)
    wd = jnp.concatenate([jnp.concatenate([w1t, z], -1), jnp.concatenate([z, w1t], -1)], axis=2).astype(BF16)
    wd = wd.reshape(2, HEAD_DIM // 2, 2 * PAGE_SIZE, 2 * CMP_HIDDEN)
    pet = jnp.transpose(cmp_pe, (0, 2, 1))
    ped = jnp.concatenate([pet, pet], -1)
    b1h = jnp.concatenate([cmp_b1, cmp_b1], -1)
    z2 = jnp.zeros_like(cmp_w2)
    w2h = jnp.concatenate([jnp.concatenate([cmp_w2, z2], -1), jnp.concatenate([z2, cmp_w2], -1)], axis=1).astype(BF16)
    return wd, ped, b1h, w2h


def _split_w_in(w):
    d = w.shape[0]
    off_b = DN_CONV_DIM + DN_WIDTH
    off_q = off_b + 2 * DN_HEADS
    off_kv = off_q + NSA_WIDTH
    off_g = off_kv + KV_COLS
    n_g = 3 * NSA_HEADS
    pad = jnp.zeros((d, P_COLS - P_SMALL - 2 * DN_HEADS - n_g), w.dtype)
    main = jnp.concatenate([w[:, :off_b], w[:, off_q:off_kv], w[:, off_kv:off_kv + CMP_COLS], w[:, off_b:off_q],
                            w[:, off_g:off_g + n_g], pad], axis=1)
    return main.astype(BF16), w[:, off_kv:off_g].T.astype(BF16)


def _row_tile(n, cap):
    t = min(n, cap)
    while n % t:
        t //= 2
    return t


def _trunk_tail(x2, mixer_dn, mixer_nsa, mem_kv3, bshape, lw, final_norm):
    n, d = x2.shape
    b, t = bshape
    if t >= SUBLANES:
        x3 = mem_block(x2.reshape(b, t, d), mixer_dn.reshape(b, t, -1), mixer_nsa.reshape(b, t, -1),
                       lw["w_out_dn"], lw["w_out_nsa"], lw["ln_mem"], lw["w_mem_q"], mem_kv3, lw["w_mem_o"],
                       _row_tile(t, 512))
        x2 = x3.reshape(n, d)
    else:
        assert t == 1
        tm = _row_tile(n, 512)
        x2 = matmul_residual(x2, [mixer_dn, mixer_nsa], [lw["w_out_dn"], lw["w_out_nsa"]], tm)
        qm = rms_matmul(x2, lw["ln_mem"], lw["w_mem_q"], tm, 512)
        att = mem_attention_row(qm.reshape(n, MEM_HEADS, d // MEM_HEADS), mem_kv3).reshape(n, d)
        x2 = matmul_residual(x2, [att], [lw["w_mem_o"]], tm)
    return ffn(x2, lw["ln_ffn"], lw["w_up"], lw["w_down"], lw["ln_final"], final_norm, _row_tile(n, 1024), 512)


def _kv_rows(kvt):
    b, rows, s = kvt.shape
    g, dh = NSA_KV_HEADS, HEAD_DIM
    return jnp.transpose(kvt.reshape(b, rows // (g * dh), g, dh, s), (0, 4, 1, 2, 3))


def _prompt_layer(xp, mem_prompt, lw, cw, final_norm):
    b, s, d = xp.shape
    n = b * s
    x2 = xp.reshape(n, d)
    proj, kvt, kvt_win = input_projection(xp, lw["ln_mix"], lw["w_in"], lw["w_kvt"], _row_tile(s, 1024), 1024)
    proj3 = proj.reshape(b, s, P_COLS)
    dn_out, s_pairs = deltanet_prompt(proj3, lw["dn_conv_w"], lw["dn_a_log"], lw["dn_dt_bias"], lw["dn_norm"])
    p_conv = proj3[:, s - (CONV_WIDTH - 1):, P_QKV:P_QKV + DN_CONV_DIM]
    p_rec = _pairs_to_heads(s_pairs)
    nb = s // CMP_BLOCK
    cmp_rows = proj3[:, :nb * CMP_BLOCK, P_CMP:P_CMP + CMP_COLS]
    kcv = compress_blocks(cmp_rows.reshape(b * nb, CMP_BLOCK * CMP_COLS), CMP_COLS, cw, _row_tile(b * nb, 256))
    kcv = kcv.reshape(b, nb, 2, NSA_KV_HEADS, HEAD_DIM)
    kc = jnp.transpose(kcv[:, :, 0], (0, 2, 1, 3))
    vct = jnp.transpose(kcv[:, :, 1], (0, 2, 3, 1))
    ocmpt, selt = nsa_cmp_topk(proj3, kc, vct, _row_tile(s, 512))
    nsa_out = nsa_attention(proj3, kvt, kvt_win, selt, ocmpt, 256, 512).reshape(n, NSA_WIDTH)
    m = mem_prompt.shape[1]
    mem_kv = rms_matmul(mem_prompt.reshape(b * m, d), lw["ln_memkv"], lw["w_mem_kv"], _row_tile(b * m, 512), 512)
    mem_kv5 = mem_kv.reshape(b, m, 2, MEM_HEADS, d // MEM_HEADS)
    y = _trunk_tail(x2, dn_out.reshape(n, DN_WIDTH), nsa_out, mem_kv5, (b, s), lw, final_norm)
    wk = min(WINDOW, s)
    return y.reshape(b, s, d), _kv_rows(kvt), _kv_rows(kvt_win[:, :, s - wk:]), mem_kv5, p_conv, p_rec


def _sample_layer(xs, cache_nsa, cache_win, cache_mem, conv_state, rec_state, page_table, lw, cwp, final_norm):
    db, ds, d = xs.shape
    assert ds == 1
    n = db
    x2 = xs.reshape(n, d)
    proj, kvt, kvt_win = input_projection(x2[None], lw["ln_mix"], lw["w_in"], lw["w_kvt"], n, 1024)
    dn_out, s_rec = deltanet_sample(proj, conv_state, rec_state, lw["dn_conv_w"], lw["dn_a_log"], lw["dn_dt_bias"],
                                    lw["dn_norm"])
    s_conv = jnp.concatenate([conv_state[:, 1:], proj[:, None, P_QKV:P_QKV + DN_CONV_DIM]], axis=1)
    n_phys = cache_nsa.shape[0]
    cache_t = jnp.transpose(cache_nsa, (0, 2, 3, 4, 1))
    win_t = jnp.transpose(cache_win, (0, 2, 3, 4, 1))
    kc_phys = compress_pages(cache_t, cwp, _row_tile(n_phys, 256))
    kv_new = jnp.concatenate([kvt[0], kvt_win[0]], axis=0).T
    new_win = kvt_win[0].T.reshape(n, 2, NSA_KV_HEADS, HEAD_DIM, 1)
    o8, s_win_t = nsa_decode(page_table, proj[:, P_NQ:P_NQ + NSA_WIDTH].reshape(n, NSA_HEADS, HEAD_DIM),
                             kv_new.reshape(n, 1, KV_COLS),
                             _flat_gates(proj[:, P_SMALL:P_SMALL + LANES]).reshape(n, 1, LANES),
                             win_t, new_win, kc_phys, cache_t)
    s_nsa = jnp.transpose(kvt.reshape(-1, NSA_KV_HEADS, HEAD_DIM, n), (3, 0, 1, 2))[:, None]
    s_win = jnp.transpose(s_win_t, (0, 4, 1, 2, 3))
    y = _trunk_tail(x2, dn_out, o8.reshape(n, NSA_WIDTH), cache_mem, (n, 1), lw, final_norm)
    return y.reshape(db, ds, d), s_nsa, s_win, s_conv, s_rec


def _flat_gates(small):
    g0 = 2 * DN_HEADS
    n_g = 3 * NSA_HEADS
    gl = small[..., g0:g0 + n_g]
    return jnp.concatenate([gl, jnp.zeros(gl.shape[:-1] + (LANES - n_g,), gl.dtype)], axis=-1)


def kernel(x_prompt, x_sample, mem_prompt, cache_nsa_kv, cache_win_kv, cache_mem_kv, state_dn_conv, state_dn_rec, page_table, ln_mix, w_in, dn_conv_w, dn_a_log, dn_dt_bias, dn_norm, cmp_pe, cmp_w1, cmp_b1, cmp_w2, w_out, ln_mem, ln_memkv, w_mem_q, w_mem_kv, w_mem_o, ln_ffn, w_up, w_down, ln_final):
    depth = w_in.shape[0]
    xp, xs = x_prompt, x_sample
    outs_p = [[] for _ in range(5)]
    outs_s = [[] for _ in range(4)]
    for l in range(depth):
        lw = {
            "ln_mix": ln_mix[l],
            "dn_conv_w": dn_conv_w[l], "dn_a_log": dn_a_log[l], "dn_dt_bias": dn_dt_bias[l], "dn_norm": dn_norm[l],
            "w_out_dn": w_out[l][:DN_WIDTH].astype(BF16), "w_out_nsa": w_out[l][DN_WIDTH:].astype(BF16),
            "ln_mem": ln_mem[l], "ln_memkv": ln_memkv[l], "w_mem_q": w_mem_q[l].astype(BF16),
            "w_mem_kv": w_mem_kv[l].astype(BF16), "w_mem_o": w_mem_o[l].astype(BF16),
            "ln_ffn": ln_ffn[l], "w_up": w_up[l].astype(BF16), "w_down": w_down[l].astype(BF16),
            "ln_final": ln_final,
        }
        lw["w_in"], lw["w_kvt"] = _split_w_in(w_in[l])
        cw = _compress_weights(cmp_pe[l], cmp_w1[l], cmp_b1[l], cmp_w2[l])
        last = l == depth - 1
        xp, p_nsa, p_win, p_mem, p_conv, p_rec = _prompt_layer(xp, mem_prompt, lw, cw, last)
        for acc, val in zip(outs_p, (p_nsa, p_win, p_mem, p_conv, p_rec)):
            acc.append(val)
        xs, s_nsa, s_win, s_conv, s_rec = _sample_layer(
            xs, cache_nsa_kv[l], cache_win_kv[l], cache_mem_kv[l], state_dn_conv[l], state_dn_rec[l],
            page_table, lw, _compress_page_weights(cmp_pe[l], cmp_w1[l], cmp_b1[l], cmp_w2[l]), last)
        for acc, val in zip(outs_s, (s_nsa, s_win, s_conv, s_rec)):
            acc.append(val)
    return (xp, xs) + tuple(jnp.stack(a) for a in outs_p) + tuple(jnp.stack(a) for a in outs_s)
```

```python
import functools

import jax
import jax.numpy as jnp
from jax import lax
from jax.experimental import pallas as pl
from jax.experimental.pallas import tpu as pltpu

F32 = jnp.float32
BF16 = jnp.bfloat16

HEAD_DIM = 64
DN_HEADS = 8
NSA_HEADS = 8
NSA_KV_HEADS = 2
NSA_GROUP = NSA_HEADS // NSA_KV_HEADS
DN_WIDTH = DN_HEADS * HEAD_DIM
NSA_WIDTH = NSA_HEADS * HEAD_DIM
CONV_WIDTH = 4
DN_CONV_DIM = 3 * DN_WIDTH
DN_CHUNK = 64
CMP_BLOCK = 64
SEL_BLOCK = 64
TOP_N = 16
N_LOCAL_BLOCKS = 2
WINDOW = 512
CMP_HIDDEN = 128
NSA_KV_KINDS = 6
MEM_HEADS = 4
PAGE_SIZE = 128
RMS_EPS = 1e-6
FORCE_SCORE = 1e3
NEG_INF = -1e30
ATTN_SCALE = HEAD_DIM ** -0.5
LOG2_E = 1.4426950408889634

LANES = 128
SUBLANES = 8
P_QKV = 0
P_Z = P_QKV + DN_CONV_DIM
P_NQ = P_Z + DN_WIDTH
P_CMP = P_NQ + NSA_WIDTH
CMP_COLS = 2 * NSA_KV_HEADS * HEAD_DIM
P_SMALL = P_CMP + CMP_COLS
P_COLS = P_SMALL + 2 * LANES
KV_COLS = NSA_KV_KINDS * NSA_KV_HEADS * HEAD_DIM
VMEM_LIMIT = 56 * 1024 * 1024
DN_APPLY_PASSES = (3, 3, 3, 1, 1, 1)
DN_SQUARE_PASSES = (3, 3, 1, 1, 1)


def _cp(sem, vmem=VMEM_LIMIT):
    return pltpu.CompilerParams(dimension_semantics=sem, vmem_limit_bytes=vmem)


def _split2(a):
    hi = a.astype(BF16)
    return hi, (a - hi.astype(F32)).astype(BF16)


def _dot(a, b, passes=1):
    if passes == 1:
        return jnp.dot(a.astype(BF16), b.astype(BF16), preferred_element_type=F32)
    ah, al = _split2(a)
    bh, bl = _split2(b)
    return (jnp.dot(ah, bh, preferred_element_type=F32) + jnp.dot(ah, bl, preferred_element_type=F32)
            + jnp.dot(al, bh, preferred_element_type=F32))


def _dot_nt(a, b):
    return lax.dot_general(a.astype(BF16), b.astype(BF16), (((1,), (1,)), ((), ())),
                           preferred_element_type=F32)


def _split3(a):
    hi = a.astype(BF16)
    r1 = a - hi.astype(F32)
    mid = r1.astype(BF16)
    lo = (r1 - mid.astype(F32)).astype(BF16)
    return hi, mid, lo


def _dot_exact_lhs01(a01, b):
    a = a01.astype(BF16)
    hi, mid, lo = _split3(b)
    return (jnp.dot(a, hi, preferred_element_type=F32) + jnp.dot(a, mid, preferred_element_type=F32)
            + jnp.dot(a, lo, preferred_element_type=F32))


def _dot_exact_rhs01(a, b01):
    b = b01.astype(BF16)
    hi, mid, lo = _split3(a)
    return (jnp.dot(hi, b, preferred_element_type=F32) + jnp.dot(mid, b, preferred_element_type=F32)
            + jnp.dot(lo, b, preferred_element_type=F32))


def _rms(x, gain):
    ms = jnp.mean(x * x, axis=-1, keepdims=True)
    return x * lax.rsqrt(ms + RMS_EPS) * gain


def _softplus(x):
    return jnp.maximum(x, 0.0) + jnp.log1p(jnp.exp(-jnp.abs(x)))


def _silu(x):
    return x * jax.nn.sigmoid(x)


def _rms_mm_kernel(x_ref, g_ref, w_ref, o_ref, h_ref):
    @pl.when(pl.program_id(1) == 0)
    def _():
        h_ref[...] = _rms(x_ref[...], g_ref[...]).astype(BF16)

    o_ref[...] = jnp.dot(h_ref[...], w_ref[...], preferred_element_type=F32)


def rms_matmul(x, gain, w_bf16, tm, tn):
    n, d = x.shape
    m = w_bf16.shape[1]
    assert n % tm == 0 and m % tn == 0
    return pl.pallas_call(
        _rms_mm_kernel,
        grid=(n // tm, m // tn),
        in_specs=[pl.BlockSpec((tm, d), lambda i, j: (i, 0)),
                  pl.BlockSpec((1, d), lambda i, j: (0, 0)),
                  pl.BlockSpec((d, tn), lambda i, j: (0, j))],
        out_specs=pl.BlockSpec((tm, tn), lambda i, j: (i, j)),
        out_shape=jax.ShapeDtypeStruct((n, m), F32),
        scratch_shapes=[pltpu.VMEM((tm, d), BF16)],
        compiler_params=_cp(("parallel", "arbitrary")),
        name="rms_matmul",
    )(x, gain.reshape(1, d), w_bf16)


def _proj_kernel(x_ref, g_ref, w_ref, wkv_ref, o_ref, okv_ref, owin_ref, h_ref):
    @pl.when(pl.program_id(1) == 0)
    def _():
        h = _rms(x_ref[...], g_ref[...]).astype(BF16)
        h_ref[...] = h
        kvt = lax.dot_general(wkv_ref[...], h, (((1,), (1,)), ((), ())), preferred_element_type=F32)
        split = okv_ref.shape[1]
        okv_ref[0] = kvt[0:split]
        owin_ref[0] = kvt[split:]

    o_ref[...] = jnp.dot(h_ref[...], w_ref[...], preferred_element_type=F32)


def input_projection(x3, gain, w_main, w_kvt, tm, tn):
    b, s, d = x3.shape
    n = b * s
    m = w_main.shape[1]
    kvc = w_kvt.shape[0]
    win = 2 * NSA_KV_HEADS * HEAD_DIM
    assert s % tm == 0 and m % tn == 0
    spt = s // tm
    return pl.pallas_call(
        _proj_kernel,
        grid=(n // tm, m // tn),
        in_specs=[pl.BlockSpec((tm, d), lambda i, j: (i, 0)),
                  pl.BlockSpec((1, d), lambda i, j: (0, 0)),
                  pl.BlockSpec((d, tn), lambda i, j: (0, j)),
                  pl.BlockSpec((kvc, d), lambda i, j: (0, 0))],
        out_specs=[pl.BlockSpec((tm, tn), lambda i, j: (i, j)),
                   pl.BlockSpec((1, kvc - win, tm), lambda i, j: (i // spt, 0, i % spt)),
                   pl.BlockSpec((1, win, tm), lambda i, j: (i // spt, 0, i % spt))],
        out_shape=[jax.ShapeDtypeStruct((n, m), F32), jax.ShapeDtypeStruct((b, kvc - win, s), F32),
                   jax.ShapeDtypeStruct((b, win, s), F32)],
        scratch_shapes=[pltpu.VMEM((tm, d), BF16)],
        compiler_params=_cp(("parallel", "arbitrary")),
        name="input_projection",
    )(x3.reshape(n, d), gain.reshape(1, d), w_main, w_kvt)


def _mm_res_kernel(*refs, n_in):
    res_ref = refs[0]
    a_refs = refs[1:1 + n_in]
    w_refs = refs[1 + n_in:1 + 2 * n_in]
    o_ref = refs[1 + 2 * n_in]
    acc = res_ref[...]
    for a_ref, w_ref in zip(a_refs, w_refs):
        acc = acc + jnp.dot(a_ref[...].astype(BF16), w_ref[...], preferred_element_type=F32)
    o_ref[...] = acc


def matmul_residual(res, a_list, w_list, tm):
    n, d = res.shape
    assert n % tm == 0
    n_in = len(a_list)
    in_specs = [pl.BlockSpec((tm, d), lambda i: (i, 0))]
    in_specs += [pl.BlockSpec((tm, a.shape[1]), lambda i: (i, 0)) for a in a_list]
    in_specs += [pl.BlockSpec(w.shape, lambda i: (0, 0)) for w in w_list]
    return pl.pallas_call(
        functools.partial(_mm_res_kernel, n_in=n_in),
        grid=(n // tm,),
        in_specs=in_specs,
        out_specs=pl.BlockSpec((tm, d), lambda i: (i, 0)),
        out_shape=jax.ShapeDtypeStruct((n, d), F32),
        compiler_params=_cp(("parallel",)),
        name="matmul_residual",
    )(res, *a_list, *w_list)


def _ffn_kernel(x_ref, g_ref, wu_ref, wd_ref, gf_ref, o_ref, hn_ref, acc_ref, *, final_norm):
    j = pl.program_id(1)

    @pl.when(j == 0)
    def _():
        x = x_ref[...]
        hn_ref[...] = _rms(x, g_ref[...]).astype(BF16)
        acc_ref[...] = x

    u = jnp.dot(hn_ref[...], wu_ref[...], preferred_element_type=F32)
    u = jnp.square(jnp.maximum(u, 0.0)).astype(BF16)
    acc_ref[...] += jnp.dot(u, wd_ref[...], preferred_element_type=F32)

    @pl.when(j == pl.num_programs(1) - 1)
    def _():
        y = acc_ref[...]
        if final_norm:
            y = _rms(y, gf_ref[...])
        o_ref[...] = y


def ffn(x, gain, wu_bf16, wd_bf16, gain_final, final_norm, tm, tf):
    n, d = x.shape
    f = wu_bf16.shape[1]
    assert n % tm == 0 and f % tf == 0
    return pl.pallas_call(
        functools.partial(_ffn_kernel, final_norm=final_norm),
        grid=(n // tm, f // tf),
        in_specs=[pl.BlockSpec((tm, d), lambda i, j: (i, 0)),
                  pl.BlockSpec((1, d), lambda i, j: (0, 0)),
                  pl.BlockSpec((d, tf), lambda i, j: (0, j)),
                  pl.BlockSpec((tf, d), lambda i, j: (j, 0)),
                  pl.BlockSpec((1, d), lambda i, j: (0, 0))],
        out_specs=pl.BlockSpec((tm, d), lambda i, j: (i, 0)),
        out_shape=jax.ShapeDtypeStruct((n, d), F32),
        scratch_shapes=[pltpu.VMEM((tm, d), BF16), pltpu.VMEM((tm, d), F32)],
        compiler_params=_cp(("parallel", "arbitrary")),
        name="ffn",
    )(x, gain.reshape(1, d), wu_bf16, wd_bf16, gain_final.reshape(1, d))


def _mem_attn_row_kernel(q_ref, kv_ref, o_ref, *, m, seqs, heads):
    sub = SUBLANES
    ones = jnp.ones((LANES, LANES), BF16)
    for u in range(seqs):
        x = kv_ref[u].reshape(m, 2 * sub, LANES)
        prod = x[:, 0:sub] * q_ref[u][None]
        part = jnp.dot(prod.reshape(m * sub, LANES).astype(BF16), ones, preferred_element_type=F32)
        part = part.reshape(m, sub, LANES)
        s = part + pltpu.roll(part, heads, 1)
        p = jnp.exp2(s - jnp.max(s, axis=0, keepdims=True))
        o_ref[u] = jnp.sum(p * x[:, sub:2 * sub], axis=0) / jnp.sum(p, axis=0)


def mem_attention_row(q, kv, seqs=4):
    b, h, hd = q.shape
    m = kv.shape[1]
    assert kv.shape[2:] == (2, h, hd)
    tiles = hd // LANES
    assert tiles * LANES == hd and tiles == 2 and tiles * h == SUBLANES and b % seqs == 0
    rows = jnp.transpose(kv.reshape(b, m, 2, h, tiles, LANES), (0, 1, 2, 4, 3, 5)).reshape(b, m * 2 * SUBLANES, LANES)
    q8 = jnp.transpose((q * (hd ** -0.5 * LOG2_E)).reshape(b, h, tiles, LANES), (0, 2, 1, 3)).reshape(b, SUBLANES, LANES)
    o8 = pl.pallas_call(
        functools.partial(_mem_attn_row_kernel, m=m, seqs=seqs, heads=h),
        grid=(b // seqs,),
        in_specs=[pl.BlockSpec((seqs, SUBLANES, LANES), lambda i: (i, 0, 0)),
                  pl.BlockSpec((seqs, m * 2 * SUBLANES, LANES), lambda i: (i, 0, 0))],
        out_specs=pl.BlockSpec((seqs, SUBLANES, LANES), lambda i: (i, 0, 0)),
        out_shape=jax.ShapeDtypeStruct((b, SUBLANES, LANES), F32),
        compiler_params=_cp(("parallel",)),
        name="mem_attention_row",
    )(q8, rows)
    return jnp.transpose(o8.reshape(b, tiles, h, LANES), (0, 2, 1, 3)).reshape(b, h, hd)


def _mem_block_kernel(x_ref, a1_ref, a2_ref, w1_ref, w2_ref, g_ref, wq_ref, kv_ref, wo_ref, o_ref, *, heads, hd):
    scale = hd ** -0.5
    x = (x_ref[0] + jnp.dot(a1_ref[0].astype(BF16), w1_ref[...], preferred_element_type=F32)
         + jnp.dot(a2_ref[0].astype(BF16), w2_ref[...], preferred_element_type=F32))
    q = jnp.dot(_rms(x, g_ref[...]).astype(BF16), wq_ref[...], preferred_element_type=F32)
    kt = jnp.swapaxes(kv_ref[0, :, 0], 0, 1)
    vt = jnp.swapaxes(kv_ref[0, :, 1], 0, 1)
    outs = []
    for h in range(heads):
        s = _dot_nt(q[:, h * hd:(h + 1) * hd], kt[h]) * scale
        p = jnp.exp(s - jnp.max(s, axis=-1, keepdims=True))
        p = p / jnp.sum(p, axis=-1, keepdims=True)
        outs.append(_dot(p, vt[h]))
    att = jnp.concatenate(outs, axis=1).astype(BF16)
    o_ref[0] = x + jnp.dot(att, wo_ref[...], preferred_element_type=F32)


def mem_block(x3, a1, a2, w1, w2, gain, wq, kv, wo, tq):
    b, t, d = x3.shape
    m = kv.shape[1]
    hd = d // MEM_HEADS
    assert t % tq == 0 and kv.shape[2:] == (2, MEM_HEADS, hd)
    return pl.pallas_call(
        functools.partial(_mem_block_kernel, heads=MEM_HEADS, hd=hd),
        grid=(b, t // tq),
        in_specs=[pl.BlockSpec((1, tq, d), lambda i, j: (i, j, 0)),
                  pl.BlockSpec((1, tq, a1.shape[2]), lambda i, j: (i, j, 0)),
                  pl.BlockSpec((1, tq, a2.shape[2]), lambda i, j: (i, j, 0)),
                  pl.BlockSpec(w1.shape, lambda i, j: (0, 0)),
                  pl.BlockSpec(w2.shape, lambda i, j: (0, 0)),
                  pl.BlockSpec((1, d), lambda i, j: (0, 0)),
                  pl.BlockSpec((d, d), lambda i, j: (0, 0)),
                  pl.BlockSpec((1, m, 2, MEM_HEADS, hd), lambda i, j: (i, 0, 0, 0, 0)),
                  pl.BlockSpec((d, d), lambda i, j: (0, 0))],
        out_specs=pl.BlockSpec((1, tq, d), lambda i, j: (i, j, 0)),
        out_shape=jax.ShapeDtypeStruct((b, t, d), F32),
        compiler_params=_cp(("parallel", "parallel")),
        name="mem_block",
    )(x3, a1, a2, w1, w2, gain.reshape(1, d), wq, kv, wo)


def _compress_kernel(x_ref, w1_ref, pe_ref, b1_ref, w2_ref, o_ref, acc_ref, *, tt):
    j = pl.program_id(1)

    @pl.when(j == 0)
    def _():
        acc_ref[...] = jnp.zeros_like(acc_ref)

    hid2 = 2 * CMP_HIDDEN
    xt = jnp.swapaxes(x_ref[...], 0, 1)
    for kind in range(2):
        part = None
        for t in range(tt):
            xs = xt[t][:, kind * LANES:(kind + 1) * LANES] + pe_ref[kind, t:t + 1, :]
            d = jnp.dot(xs.astype(BF16), w1_ref[kind, t], preferred_element_type=F32)
            part = d if part is None else part + d
        acc_ref[:, kind * hid2:(kind + 1) * hid2] += part

    @pl.when(j == pl.num_programs(1) - 1)
    def _():
        h = jnp.maximum(acc_ref[...] + b1_ref[...], 0.0)
        for kind in range(2):
            o_ref[:, kind * LANES:(kind + 1) * LANES] = jnp.dot(
                h[:, kind * hid2:(kind + 1) * hid2].astype(BF16), w2_ref[kind], preferred_element_type=F32)


def compress_blocks(x3, col0, cw, bt, tt=8):
    nb = x3.shape[0]
    assert nb % bt == 0 and CMP_BLOCK % tt == 0 and col0 % CMP_COLS == 0 and x3.shape[1] == CMP_BLOCK
    w1bd, pe2, b1bd, w2bd = cw
    return pl.pallas_call(
        functools.partial(_compress_kernel, tt=tt),
        grid=(nb // bt, CMP_BLOCK // tt),
        in_specs=[pl.BlockSpec((bt, tt, CMP_COLS), lambda i, j: (i, j, col0 // CMP_COLS)),
                  pl.BlockSpec((2, tt, LANES, 2 * CMP_HIDDEN), lambda i, j: (0, j, 0, 0)),
                  pl.BlockSpec((2, tt, LANES), lambda i, j: (0, j, 0)),
                  pl.BlockSpec((1, 4 * CMP_HIDDEN), lambda i, j: (0, 0)),
                  pl.BlockSpec((2, 2 * CMP_HIDDEN, LANES), lambda i, j: (0, 0, 0))],
        out_specs=pl.BlockSpec((bt, 2 * LANES), lambda i, j: (i, 0)),
        out_shape=jax.ShapeDtypeStruct((nb, 2 * LANES), F32),
        scratch_shapes=[pltpu.VMEM((bt, 4 * CMP_HIDDEN), F32)],
        compiler_params=_cp(("parallel", "arbitrary")),
        name="compress_blocks",
    )(x3, w1bd, pe2, b1bd, w2bd)


def _compress_weights(cmp_pe, cmp_w1, cmp_b1, cmp_w2):
    w1r = cmp_w1.reshape(2, CMP_BLOCK, HEAD_DIM, CMP_HIDDEN)
    z = jnp.zeros_like(w1r)
    w1bd = jnp.concatenate([jnp.concatenate([w1r, z], -1), jnp.concatenate([z, w1r], -1)], axis=2).astype(BF16)
    pe2 = jnp.concatenate([cmp_pe, cmp_pe], -1)
    b1bd = jnp.concatenate([cmp_b1[0], cmp_b1[0], cmp_b1[1], cmp_b1[1]]).reshape(1, 4 * CMP_HIDDEN)
    z2 = jnp.zeros_like(cmp_w2)
    w2bd = jnp.concatenate([jnp.concatenate([cmp_w2, z2], -1), jnp.concatenate([z2, cmp_w2], -1)], axis=1).astype(BF16)
    return w1bd, pe2, b1bd, w2bd


def _masked_softmax(s, mask):
    s = jnp.where(mask, s, NEG_INF)
    p = jnp.where(mask, jnp.exp(s - jnp.max(s, axis=-1, keepdims=True)), 0.0)
    return p / jnp.maximum(jnp.sum(p, axis=-1, keepdims=True), 1e-30)


def _heads_as_rows(q_ref, tq, scale=ATTN_SCALE):
    qb = q_ref[0] * scale
    return jnp.concatenate([qb[:, h * HEAD_DIM:(h + 1) * HEAD_DIM] for h in range(NSA_GROUP)], axis=0).astype(BF16)


def _cmp_topk_kernel(q_ref, kc_ref, vct_ref, ocmp_ref, sel_ref, *, tq, nblk, topn):
    qi = pl.program_id(2)
    r, dh = NSA_GROUP, HEAD_DIM
    qpos = qi * tq + lax.broadcasted_iota(jnp.int32, (nblk, tq), 1)
    blk = lax.broadcasted_iota(jnp.int32, (nblk, tq), 0)
    vis = (blk + 1) * CMP_BLOCK - 1 <= qpos
    q = _heads_as_rows(q_ref, tq)
    kc = kc_ref[0, 0]
    vct = vct_ref[0, 0]
    imp = jnp.zeros((nblk, tq), F32)
    for h in range(r):
        s = jnp.where(vis, _dot_nt(kc, q[h * tq:(h + 1) * tq]), NEG_INF)
        p = jnp.where(vis, jnp.exp(s - jnp.max(s, axis=0, keepdims=True)), 0.0)
        p = p / jnp.maximum(jnp.sum(p, axis=0, keepdims=True), 1e-30)
        ocmp_ref[0, 0, h * dh:(h + 1) * dh, :] = _dot(vct, p)
        imp = imp + p
    cur = lax.shift_right_logical(qpos, 6)
    valid = blk <= cur
    forced = valid & ((blk == 0) | (cur - blk < N_LOCAL_BLOCKS))
    score = jnp.where(valid, imp + jnp.where(forced, FORCE_SCORE, 0.0), -1.0)
    rank = jnp.zeros((nblk, tq), F32)
    row8 = lax.broadcasted_iota(jnp.int32, (SUBLANES, tq), 0)
    for i in range(nblk):
        si = score[i:i + 1, :]
        t0 = i // SUBLANES * SUBLANES
        mid = score[t0:t0 + SUBLANES]
        parts = [jnp.where(row8 > i - t0, jnp.where(si >= mid, 1.0, 0.0), jnp.where(si > mid, 1.0, 0.0))]
        if t0 > 0:
            parts.insert(0, jnp.where(si > score[0:t0], 1.0, 0.0))
        if t0 + SUBLANES < nblk:
            parts.append(jnp.where(si >= score[t0 + SUBLANES:], 1.0, 0.0))
        rank = rank + jnp.concatenate(parts, axis=0)
    sel_ref[0, 0] = jnp.where(rank < topn, 1.0, 0.0).astype(BF16)


def nsa_cmp_topk(proj3, kc, vct, tq):
    b, sq, _ = proj3.shape
    g = kc.shape[1]
    nblk = kc.shape[2]
    dh = HEAD_DIM
    gw = NSA_GROUP * dh
    assert sq % tq == 0 and SEL_BLOCK == 64 and P_NQ % gw == 0 and nblk % SUBLANES == 0
    return pl.pallas_call(
        functools.partial(_cmp_topk_kernel, tq=tq, nblk=nblk, topn=min(TOP_N, nblk)),
        grid=(b, g, sq // tq),
        in_specs=[pl.BlockSpec((1, tq, gw), lambda i, j, k: (i, k, P_NQ // gw + j)),
                  pl.BlockSpec((1, 1, nblk, dh), lambda i, j, k: (i, j, 0, 0)),
                  pl.BlockSpec((1, 1, dh, nblk), lambda i, j, k: (i, j, 0, 0))],
        out_specs=[pl.BlockSpec((1, 1, gw, tq), lambda i, j, k: (i, j, 0, k)),
                   pl.BlockSpec((1, 1, nblk, tq), lambda i, j, k: (i, j, 0, k))],
        out_shape=[jax.ShapeDtypeStruct((b, g, gw, sq), F32),
                   jax.ShapeDtypeStruct((b, g, nblk, sq), BF16)],
        compiler_params=_cp(("parallel", "parallel", "parallel")),
        name="nsa_cmp_topk",
    )(proj3, kc, vct)


def _nsa_attn_kernel(q_ref, kst_ref, vst_ref, kwt_ref, vwt_ref, selt_ref, ocmp_ref, sm_ref, o_ref,
                     m_sc, acc_sc, s_sc, bias_sc, p_sc, *, tq, tk, nsel, ck):
    grp = pl.program_id(1)
    qi = pl.program_id(2)
    r = NSA_GROUP
    dh = HEAD_DIM
    q = _heads_as_rows(q_ref, tq, ATTN_SCALE * LOG2_E)
    qpos = qi * tq + lax.broadcasted_iota(jnp.int32, (1, tq), 1)
    ones_row = jnp.where(lax.broadcasted_iota(jnp.int32, (8, 1), 0) == 0, 1.0, 0.0)

    def reset():
        m_sc[...] = jnp.full(m_sc.shape, NEG_INF, F32)
        acc_sc[...] = jnp.zeros(acc_sc.shape, F32)

    def with_ones(vt):
        return jnp.concatenate([vt, jnp.broadcast_to(ones_row, (8, vt.shape[1]))], axis=0).astype(BF16)

    def scores(kt, qrows, row0):
        s_sc[row0:row0 + kt.shape[1], :] = _dot_nt(kt.T, qrows)

    def set_bias(mask, row0):
        bias_sc[pl.ds(row0, mask.shape[0]), :] = jnp.where(mask, 0.0, NEG_INF)

    def fold(row0, n, vt, biased):
        def chunk(c, shift):
            rows = pl.ds(row0 + c * ck, ck)
            x = s_sc[rows, :] - shift
            return x + jnp.concatenate([bias_sc[rows, :]] * r, axis=1) if biased else x

        m_prev = m_sc[...]
        top = jnp.full((SUBLANES, r * tq), NEG_INF, F32)
        for c in range(n // ck):
            top = jnp.maximum(top, jnp.max(chunk(c, 0.0).reshape(ck // SUBLANES, SUBLANES, r * tq), axis=0))
        m_new = jnp.maximum(m_prev, jnp.max(top, axis=0, keepdims=True))
        for c in range(n // ck):
            p_sc[c * ck:(c + 1) * ck, :] = jnp.exp2(chunk(c, m_new)).astype(BF16)
        acc_sc[...] = jnp.exp2(m_prev - m_new) * acc_sc[...] + jnp.dot(with_ones(vt), p_sc[0:n, :],
                                                                      preferred_element_type=F32)
        m_sc[...] = m_new

    def result():
        acc = acc_sc[...]
        return acc[0:dh] / jnp.maximum(acc[dh:dh + 1], 1e-30)

    wk = WINDOW + tq
    w0 = pl.multiple_of(jnp.maximum(qi - WINDOW // tq, 0) * tq, tq)
    dpos = qpos - (w0 + lax.broadcasted_iota(jnp.int32, (wk, 1), 0))
    set_bias((dpos >= 0) & (dpos < WINDOW), 0)
    scores(kwt_ref[0, :, pl.ds(w0, wk)], q, 0)

    reset()
    block_bias = jnp.where(selt_ref[0, 0].astype(F32).T > 0.5, 0.0, NEG_INF).astype(BF16)
    q_sel = jnp.concatenate([q, jnp.concatenate([block_bias] * r, axis=0)], axis=1)
    blk = lax.broadcasted_iota(jnp.int32, (nsel, tk), 0)
    krow = lax.broadcasted_iota(jnp.int32, (nsel, tk), 1)

    def slc_scores(k0, row0):
        code = jnp.where(lax.shift_right_logical(k0 + krow, CMP_BLOCK.bit_length() - 1) == blk, 1.0, 0.0)
        scores(jnp.concatenate([kst_ref[0, :, pl.ds(k0, tk)], code], axis=0), q_sel, row0)

    def slc_fold(k0, row0, biased):
        fold(row0, tk, vst_ref[0, :, pl.ds(k0, tk)], biased)

    def slc_pair(c, carry):
        ka = pl.multiple_of(2 * c * tk, tk)
        slc_scores(ka + tk, wk + tk)
        slc_fold(ka, wk, False)
        slc_scores(ka + 2 * tk, wk)
        slc_fold(ka + tk, wk + tk, False)
        return carry

    n_below = (qi * tq) // tk
    kd = pl.multiple_of(n_below * tk, tk)
    odd = n_below % 2
    slc_scores(0, wk)
    lax.fori_loop(0, n_below // 2, slc_pair, 0)

    @pl.when(odd == 1)
    def _():
        slc_scores(kd, wk + tk)
        slc_fold(kd - tk, wk, False)

    row_d = pl.multiple_of(wk + odd * tk, ck)
    set_bias(kd + lax.broadcasted_iota(jnp.int32, (tk, 1), 0) <= qpos, row_d)
    slc_fold(kd, row_d, True)
    o_slc = result()

    reset()
    fold(0, wk, vwt_ref[0, :, pl.ds(w0, wk)], True)
    o_win = result()

    gates_t = jax.nn.sigmoid(sm_ref[0]).T
    g0 = 2 * DN_HEADS
    per = 3 * r
    gt = jnp.where(grp == 0, gates_t[g0:g0 + per], gates_t[g0 + per:g0 + 2 * per])
    outs = []
    for h in range(r):
        outs.append(gt[3 * h:3 * h + 1] * ocmp_ref[0, 0, h * dh:(h + 1) * dh, :]
                    + gt[3 * h + 1:3 * h + 2] * o_slc[:, h * tq:(h + 1) * tq]
                    + gt[3 * h + 2:3 * h + 3] * o_win[:, h * tq:(h + 1) * tq])
    o_ref[0] = jnp.concatenate(outs, axis=0).T


def nsa_attention(proj3, kvt, kvt_win, selt, ocmpt, tq, tk):
    b, sq, _ = proj3.shape
    t = kvt.shape[2]
    g, nsel = selt.shape[1], selt.shape[2]
    r, dh = NSA_GROUP, HEAD_DIM
    gw = r * dh
    assert sq == t and sq % tq == 0 and t % tk == 0 and tk % tq == 0 and WINDOW % tq == 0 and tq % LANES == 0
    assert t >= WINDOW + tq and g == NSA_KV_HEADS and g == 2
    ck = 32
    assert tk % ck == 0 and (WINDOW + tq) % ck == 0 and tk <= WINDOW + tq
    rows = WINDOW + tq + 2 * tk

    def kv_spec(kind):
        return pl.BlockSpec((1, dh, t), lambda i, j, k: (i, kind * g + j, 0))

    return pl.pallas_call(
        functools.partial(_nsa_attn_kernel, tq=tq, tk=tk, nsel=nsel, ck=ck),
        grid=(b, g, sq // tq),
        in_specs=[pl.BlockSpec((1, tq, gw), lambda i, j, k: (i, k, P_NQ // gw + j)),
                  kv_spec(2), kv_spec(3), kv_spec(0), kv_spec(1),
                  pl.BlockSpec((1, 1, nsel, tq), lambda i, j, k: (i, j, 0, k)),
                  pl.BlockSpec((1, 1, gw, tq), lambda i, j, k: (i, j, 0, k)),
                  pl.BlockSpec((1, tq, LANES), lambda i, j, k: (i, k, P_SMALL // LANES))],
        out_specs=pl.BlockSpec((1, tq, gw), lambda i, j, k: (i, k, j)),
        out_shape=jax.ShapeDtypeStruct((b, sq, g * gw), F32),
        scratch_shapes=[pltpu.VMEM((1, r * tq), F32), pltpu.VMEM((dh + 8, r * tq), F32),
                        pltpu.VMEM((rows, r * tq), F32), pltpu.VMEM((rows, tq), F32),
                        pltpu.VMEM((WINDOW + tq, r * tq), BF16)],
        compiler_params=_cp(("parallel", "parallel", "parallel")),
        name="nsa_attention",
    )(proj3, kvt, kvt, kvt_win, kvt_win, selt, ocmpt, proj3)


def _bdot(a, b, passes=1):
    dims = (((2,), (1,)), ((0,), (0,)))
    if passes == 1:
        return lax.dot_general(a.astype(BF16), b.astype(BF16), dims, preferred_element_type=F32)
    ah, al = _split2(a)
    bh, bl = _split2(b)
    return (lax.dot_general(ah, bh, dims, preferred_element_type=F32)
            + lax.dot_general(ah, bl, dims, preferred_element_type=F32)
            + lax.dot_general(al, bh, dims, preferred_element_type=F32))


def _bdot_nt(a, b):
    return lax.dot_general(a.astype(BF16), b.astype(BF16), (((2,), (2,)), ((0,), (0,))),
                           preferred_element_type=F32)


def _deltanet_kernel(qkv_ref, z_ref, sm_ref, cw_ref, alog_ref, dtb_ref, gn_ref, o_ref, s_out_ref,
                     xbuf, s_sc):
    c = pl.program_id(0)
    ch = DN_CHUNK
    n_pairs = DN_HEADS // 2
    two = 2 * ch
    n_batch = qkv_ref.shape[0]

    @pl.when(c == 0)
    def _():
        xbuf[:, 0:8, :] = jnp.zeros((n_batch, 8, DN_CONV_DIM), F32)
        s_sc[...] = jnp.zeros_like(s_sc)

    ti = lax.broadcasted_iota(jnp.int32, (ch, ch), 0)
    tj = lax.broadcasted_iota(jnp.int32, (ch, ch), 1)
    tri = jnp.where(ti >= tj, 1.0, 0.0)
    lane = lax.broadcasted_iota(jnp.int32, (ch, LANES), 1)
    lo = lane < HEAD_DIM
    row2 = lax.broadcasted_iota(jnp.int32, (two, two), 0)
    col2 = lax.broadcasted_iota(jnp.int32, (two, two), 1)
    same = (row2 >= ch) == (col2 >= ch)
    incl = (same & (row2 >= col2))[None]
    strict = (same & (row2 > col2))[None]
    top = lax.broadcasted_iota(jnp.int32, (two, 1), 0) < ch

    def seg_sum(x):
        s_lo = jnp.sum(jnp.where(lo, x, 0.0), axis=-1, keepdims=True)
        s_hi = jnp.sum(jnp.where(lo, 0.0, x), axis=-1, keepdims=True)
        return jnp.where(lo, s_lo, s_hi)

    def stack2(x):
        return jnp.concatenate([jnp.where(lo, x, 0.0), jnp.where(lo, 0.0, x)], axis=0)

    def col2x(a, b):
        return jnp.concatenate([jnp.broadcast_to(a, (ch, LANES)), jnp.broadcast_to(b, (ch, LANES))], axis=0)

    q_l, k_l, v_l, beta_l, gc_l, gl_l = [], [], [], [], [], []
    for bi in range(n_batch):
        xbuf[bi, 8:8 + ch, :] = qkv_ref[bi]
        conv = None
        for w in range(CONV_WIDTH):
            term = xbuf[bi, 5 + w:5 + w + ch, :] * cw_ref[w:w + 1, :]
            conv = term if conv is None else conv + term
        xbuf[bi, 0:8, :] = xbuf[bi, ch:ch + 8, :]
        act = _silu(conv)
        sm = sm_ref[bi]
        beta_all = jax.nn.sigmoid(sm)
        g_all = -jnp.exp(alog_ref[...]) * _softplus(sm + dtb_ref[...])
        gcum_all = _dot_exact_lhs01(tri, g_all)
        for p in range(n_pairs):
            c0 = p * LANES
            qp = act[:, c0:c0 + LANES]
            kp = act[:, DN_WIDTH + c0:DN_WIDTH + c0 + LANES]
            vp = act[:, 2 * DN_WIDTH + c0:2 * DN_WIDTH + c0 + LANES]
            qp = qp * lax.rsqrt(seg_sum(qp * qp) + 1e-6) * (HEAD_DIM ** -0.5)
            kp = kp * lax.rsqrt(seg_sum(kp * kp) + 1e-6)
            h0, h1 = DN_HEADS + 2 * p, DN_HEADS + 2 * p + 1
            q_l.append(stack2(qp))
            k_l.append(stack2(kp))
            v_l.append(stack2(vp))
            beta_l.append(col2x(beta_all[:, 2 * p:2 * p + 1], beta_all[:, 2 * p + 1:2 * p + 2]))
            gc_l.append(col2x(gcum_all[:, h0:h0 + 1], gcum_all[:, h1:h1 + 1]))
            gl_l.append(jnp.broadcast_to(jnp.where(top, gcum_all[ch - 1:ch, h0:h0 + 1], gcum_all[ch - 1:ch, h1:h1 + 1]),
                                         (two, LANES)))
    q2, k2, v2 = jnp.stack(q_l), jnp.stack(k_l), jnp.stack(v_l)
    beta2, gc2, gl2 = jnp.stack(beta_l), jnp.stack(gc_l), jnp.stack(gl_l)
    decay = jnp.exp(jnp.where(incl, gc2 - jnp.swapaxes(gc2, 1, 2), NEG_INF))
    kb2 = k2 * beta2
    a_mat = jnp.where(strict, _bdot_nt(kb2, k2) * decay, 0.0)
    aqk = jnp.where(incl, _bdot_nt(q2, k2) * decay, 0.0)
    s_old = s_sc[...]
    egc = jnp.exp(gc2)
    x = beta2 * (v2 - egc * _bdot(k2, s_old))
    pw = -a_mat
    n_lvl = ch.bit_length() - 1
    for lvl in range(n_lvl):
        x = x + _bdot(pw, x, DN_APPLY_PASSES[lvl])
        if lvl + 1 < n_lvl:
            pw = _bdot(pw, pw, DN_SQUARE_PASSES[lvl])
    o2 = _bdot(q2 * egc, s_old) + _bdot(aqk, x)
    kdec = k2 * jnp.exp(gl2 - gc2)
    s_sc[...] = s_old * jnp.exp(gl2) + _bdot(jnp.swapaxes(kdec, 1, 2), x)
    for bi in range(n_batch):
        for p in range(n_pairs):
            c0 = p * LANES
            o_n = o2[bi * n_pairs + p]
            o_pair = o_n[0:ch] + o_n[ch:two]
            inv = lax.rsqrt(seg_sum(o_pair * o_pair) * (1.0 / HEAD_DIM) + RMS_EPS)
            o_ref[bi, :, c0:c0 + LANES] = o_pair * inv * gn_ref[...] * _silu(z_ref[bi, :, c0:c0 + LANES])

    @pl.when(c == pl.num_programs(0) - 1)
    def _():
        s_out_ref[...] = s_sc[...]


def deltanet_prompt(proj3, conv_w, a_log, dt_bias, norm_gain):
    b, t, _ = proj3.shape
    ch = DN_CHUNK
    assert t % ch == 0
    pad = jnp.zeros((LANES - 2 * DN_HEADS,), F32)
    alog_row = jnp.concatenate([jnp.zeros((DN_HEADS,), F32), a_log, pad]).reshape(1, LANES)
    dtb_row = jnp.concatenate([jnp.zeros((DN_HEADS,), F32), dt_bias, pad]).reshape(1, LANES)
    gn_row = jnp.concatenate([norm_gain, norm_gain]).reshape(1, LANES)
    n_pairs = DN_HEADS // 2
    o, s_fin = pl.pallas_call(
        _deltanet_kernel,
        grid=(t // ch,),
        in_specs=[pl.BlockSpec((b, ch, DN_CONV_DIM), lambda j: (0, j, P_QKV // DN_CONV_DIM)),
                  pl.BlockSpec((b, ch, DN_WIDTH), lambda j: (0, j, P_Z // DN_WIDTH)),
                  pl.BlockSpec((b, ch, LANES), lambda j: (0, j, P_SMALL // LANES)),
                  pl.BlockSpec((CONV_WIDTH, DN_CONV_DIM), lambda j: (0, 0)),
                  pl.BlockSpec((1, LANES), lambda j: (0, 0)),
                  pl.BlockSpec((1, LANES), lambda j: (0, 0)),
                  pl.BlockSpec((1, LANES), lambda j: (0, 0))],
        out_specs=[pl.BlockSpec((b, ch, DN_WIDTH), lambda j: (0, j, 0)),
                   pl.BlockSpec((b * n_pairs, 2 * ch, LANES), lambda j: (0, 0, 0))],
        out_shape=[jax.ShapeDtypeStruct((b, t, DN_WIDTH), F32),
                   jax.ShapeDtypeStruct((b * n_pairs, 2 * ch, LANES), F32)],
        scratch_shapes=[pltpu.VMEM((b, ch + 8, DN_CONV_DIM), F32), pltpu.VMEM((b * n_pairs, 2 * ch, LANES), F32)],
        compiler_params=_cp(("arbitrary",)),
        name="deltanet_prompt",
    )(proj3, proj3, proj3, conv_w, alog_row, dtb_row, gn_row)
    return o, s_fin.reshape(b, n_pairs, 2 * ch, LANES)


def _pairs_to_heads(s_pairs):
    d = HEAD_DIM
    return jnp.stack([s_pairs[:, :, :d, :d], s_pairs[:, :, d:, d:]], axis=2).reshape(
        s_pairs.shape[0], DN_HEADS, d, d)


def _dn_step_prep_kernel(qkv_ref, cs_ref, sm_ref, cw_ref, alog_ref, dtb_ref, ones_ref,
                         q_ref, k_ref, v_ref, sc_ref):
    conv = qkv_ref[...] * cw_ref[CONV_WIDTH - 1:CONV_WIDTH, :]
    for w in range(CONV_WIDTH - 1):
        conv = conv + cs_ref[w] * cw_ref[w:w + 1, :]
    act = _silu(conv)
    q = act[:, 0:DN_WIDTH]
    k = act[:, DN_WIDTH:2 * DN_WIDTH]

    def seg_sum(x):
        return _dot_exact_rhs01(x, ones_ref[...])

    q_ref[...] = q * lax.rsqrt(seg_sum(q * q) + 1e-6) * (HEAD_DIM ** -0.5)
    k_ref[...] = k * lax.rsqrt(seg_sum(k * k) + 1e-6)
    v_ref[...] = act[:, 2 * DN_WIDTH:]
    sm = sm_ref[...]
    g = -jnp.exp(alog_ref[...]) * _softplus(sm + dtb_ref[...])
    lane = lax.broadcasted_iota(jnp.int32, sm.shape, 1)
    sc_ref[...] = jnp.where(lane < DN_HEADS, jax.nn.sigmoid(sm), jnp.exp(g))


def _dn_step_kernel(k_ref, q_ref, v_ref, be_ref, eg_ref, z_ref, gn_ref, s_ref, o_ref, s_out_ref):
    s_old = s_ref[0]
    k, q, v = k_ref[0], q_ref[0], v_ref[0]
    beta, eg = be_ref[0], eg_ref[0]
    ks = jnp.sum(k[:, None, :] * s_old, axis=0)
    qs = jnp.sum(q[:, None, :] * s_old, axis=0)
    qk = jnp.sum(q * k, axis=0, keepdims=True)
    v_new = beta * (v - eg * ks)
    o = eg * qs + qk * v_new
    inv = lax.rsqrt(jnp.mean(o * o, axis=0, keepdims=True) + RMS_EPS)
    o_ref[0] = o * inv * gn_ref[...] * _silu(z_ref[0])
    s_out_ref[0] = s_old * eg[None] + k[:, None, :] * v_new[None]


def deltanet_sample(proj_s, conv_state, rec_state, conv_w, a_log, dt_bias, norm_gain):
    n = proj_s.shape[0]
    hds, d = DN_HEADS, HEAD_DIM
    pad = jnp.zeros((LANES - 2 * hds,), F32)
    alog_row = jnp.concatenate([jnp.zeros((hds,), F32), a_log, pad]).reshape(1, LANES)
    dtb_row = jnp.concatenate([jnp.zeros((hds,), F32), dt_bias, pad]).reshape(1, LANES)
    head_of = jnp.arange(DN_WIDTH) // d
    ones_bd = (head_of[:, None] == head_of[None, :]).astype(BF16)
    cs = jnp.transpose(conv_state, (1, 0, 2))
    full = lambda shape: pl.BlockSpec(shape, lambda i: (0,) * len(shape))
    q, k, v, sc = pl.pallas_call(
        _dn_step_prep_kernel,
        grid=(1,),
        in_specs=[pl.BlockSpec((n, DN_CONV_DIM), lambda i: (0, P_QKV // DN_CONV_DIM)),
                  full((CONV_WIDTH - 1, n, DN_CONV_DIM)),
                  pl.BlockSpec((n, LANES), lambda i: (0, P_SMALL // LANES)),
                  full((CONV_WIDTH, DN_CONV_DIM)), full((1, LANES)), full((1, LANES)),
                  full((DN_WIDTH, DN_WIDTH))],
        out_specs=[full((n, DN_WIDTH)), full((n, DN_WIDTH)), full((n, DN_WIDTH)), full((n, LANES))],
        out_shape=[jax.ShapeDtypeStruct((n, DN_WIDTH), F32)] * 3 + [jax.ShapeDtypeStruct((n, LANES), F32)],
        compiler_params=_cp(("arbitrary",)),
        name="dn_step_prep",
    )(proj_s, cs, proj_s, conv_w, alog_row, dtb_row, ones_bd)
    t3 = lambda a: a.T.reshape(hds, d, n)
    sct = sc[:, :2 * hds].T.reshape(2, hds, 1, n)
    head_vec = pl.BlockSpec((1, d, n), lambda i: (i, 0, 0))
    head_scl = pl.BlockSpec((1, 1, n), lambda i: (i, 0, 0))
    state = pl.BlockSpec((1, d, d, n), lambda i: (i, 0, 0, 0))
    o, s_new = pl.pallas_call(
        _dn_step_kernel,
        grid=(hds,),
        in_specs=[head_vec, head_vec, head_vec, head_scl, head_scl, head_vec,
                  pl.BlockSpec((d, 1), lambda i: (0, 0)), state],
        out_specs=[head_vec, state],
        out_shape=[jax.ShapeDtypeStruct((hds, d, n), F32), jax.ShapeDtypeStruct((hds, d, d, n), F32)],
        compiler_params=_cp(("parallel",)),
        name="dn_step",
    )(t3(k), t3(q), t3(v), sct[0], sct[1], t3(proj_s[:, P_Z:P_Z + DN_WIDTH]), norm_gain.reshape(d, 1),
      jnp.transpose(rec_state, (1, 2, 3, 0)))
    return o.reshape(DN_WIDTH, n).T, jnp.transpose(s_new, (3, 0, 1, 2))


def _nsa_decode_kernel(pt_ref, q_ref, new_ref, gate_ref, win_ref, newc_ref, exp_ref, *refs, n_pages, seqs):
    del pt_ref
    o_ref, swin_ref, kc_sc = refs[2 * seqs * n_pages:2 * seqs * n_pages + 3]
    dh, r, g2, nh = HEAD_DIM, NSA_GROUP, NSA_KV_HEADS, NSA_HEADS
    past = n_pages * PAGE_SIZE
    nb = past // CMP_BLOCK
    nsel = nb + 1
    wlen = win_ref.shape[4]
    for u in range(seqs):
        for j in range(n_pages):
            kc_sc[u, j:j + 1, :] = refs[u * n_pages + j][0]
    kcv = kc_sc[...]
    new = new_ref[...]
    q8 = q_ref[...] * ATTN_SCALE
    head = lax.broadcasted_iota(jnp.int32, (1, nh, 1), 1)
    lane = lax.broadcasted_iota(jnp.int32, (1, 1, LANES), 2)
    g0 = head < r
    blk = jnp.where(lane < n_pages, 2 * lane, jnp.where(lane < nb, 2 * (lane - n_pages) + 1, lane))
    wp = lax.broadcasted_iota(jnp.int32, (1, 1, wlen + LANES), 2)
    dpos = wlen - wp
    wmask = (dpos >= 0) & (dpos < WINDOW) & (past - wlen + wp >= 0)

    def both(x):
        return jnp.where(g0, x[:, :, 0:dh], x[:, :, dh:2 * dh])

    def new_part(kind):
        return both(jnp.broadcast_to(new[:, :, 2 * kind * dh:2 * (kind + 1) * dh], (seqs, nh, 2 * dh)))

    def pages(kind):
        return jnp.stack([jnp.concatenate([refs[(seqs + u) * n_pages + j][0, kind].reshape(2 * dh, PAGE_SIZE)
                                           for j in range(n_pages)], axis=1) for u in range(seqs)])

    q_bd = jnp.concatenate([jnp.where(g0, q8, 0.0), jnp.where(g0, 0.0, q8)], axis=2)
    s_new = jnp.sum(q8 * new_part(2), axis=-1, keepdims=True)
    s_all = jnp.concatenate([_bdot(q_bd, pages(0)), jnp.broadcast_to(s_new, (seqs, nh, LANES))], axis=2)
    sw = _bdot(q_bd, win_ref[:, 0].reshape(seqs, 2 * dh, wlen))
    sw_new = jnp.sum(q8 * new_part(4), axis=-1, keepdims=True)

    def cmp_rows(base):
        even = jnp.concatenate([kcv[:, :, base:base + dh], kcv[:, :, base + LANES:base + LANES + dh]], axis=2)
        odd = jnp.concatenate([kcv[:, :, base + dh:base + LANES], kcv[:, :, base + LANES + dh:base + 2 * LANES]], axis=2)
        return jnp.concatenate([even, odd], axis=1)

    s = _bdot_nt(q_bd, cmp_rows(0))
    p = jnp.exp(s - jnp.max(s, axis=-1, keepdims=True))
    p = p / jnp.maximum(jnp.sum(p, axis=-1, keepdims=True), 1e-30)
    o_cmp = both(_bdot(p, cmp_rows(g2 * LANES)))
    bid_row = jnp.broadcast_to(blk.astype(F32), (1, LANES, LANES))
    bid_col = jnp.swapaxes(bid_row, 1, 2)
    valid = lane < nsel
    forced = valid & ((blk == 0) | (nb - blk < N_LOCAL_BLOCKS))
    sels = []
    for g in range(g2):
        in_g = (head >= g * r) & (head < (g + 1) * r)
        imp = jnp.sum(jnp.where(in_g, p, 0.0), axis=1, keepdims=True)
        imp = jnp.concatenate([imp, jnp.zeros((seqs, 1, LANES - nb), F32)], axis=2)
        score = jnp.where(valid, imp + jnp.where(forced, FORCE_SCORE, 0.0), -1.0)
        sc_row = jnp.broadcast_to(score, (seqs, LANES, LANES))
        sc_col = jnp.swapaxes(sc_row, 1, 2)
        beats = (sc_col > sc_row) | ((sc_col == sc_row) & (bid_col < bid_row))
        rank = jnp.sum(jnp.where(beats, 1.0, 0.0), axis=1, keepdims=True)
        sels.append(jnp.where(valid & (rank < min(TOP_N, nsel)), 1.0, 0.0))
    sel8 = jnp.where(g0, sels[0], sels[1]).astype(BF16)
    chosen = jnp.dot(sel8.reshape(seqs * nh, LANES), exp_ref[...], preferred_element_type=F32) > 0.5
    pm = _masked_softmax(s_all, chosen.reshape(seqs, nh, past + LANES))
    o_slc = pm[:, :, past:past + 1] * new_part(3) + both(_bdot_nt(pm[:, :, 0:past], pages(1)))
    pw = _masked_softmax(jnp.concatenate([sw, jnp.broadcast_to(sw_new, (seqs, nh, LANES))], axis=2), wmask)
    o_win = pw[:, :, wlen:wlen + 1] * new_part(5) + both(
        _bdot_nt(pw[:, :, 0:wlen], win_ref[:, 1].reshape(seqs, 2 * dh, wlen)))
    gates = jnp.broadcast_to(jax.nn.sigmoid(gate_ref[...]), (seqs, nh, LANES))

    def gate(branch):
        return jnp.sum(jnp.where(lane == 3 * head + branch, gates, 0.0), axis=-1, keepdims=True)

    o_ref[...] = gate(0) * o_cmp + gate(1) * o_slc + gate(2) * o_win
    wpos = lax.broadcasted_iota(jnp.int32, (1, 1, 1, wlen), 3)
    nlane = newc_ref.shape[1]
    slane = lax.broadcasted_iota(jnp.int32, (1, nlane), 1)
    for u in range(seqs):
        mine = slane == (pl.program_id(0) * seqs + u) % nlane
        col = jnp.sum(jnp.where(mine, newc_ref[...], 0.0), axis=1, keepdims=True)
        swin_ref[u] = jnp.where(wpos == wlen - 1, col.reshape(2, g2, dh, 1), pltpu.roll(win_ref[u], wlen - 1, 3))


def nsa_decode(page_table, q3, new_row, gate_row, win_t, new_win, kc_phys, cache_t, seqs=8):
    n, n_pages = page_table.shape
    dh = HEAD_DIM
    g2 = NSA_KV_HEADS
    wlen = win_t.shape[4]
    assert 2 * n_pages + 1 <= LANES and PAGE_SIZE == 2 * CMP_BLOCK and PAGE_SIZE == LANES and g2 == 2

    past = n_pages * PAGE_SIZE
    nb = past // CMP_BLOCK
    erow = lax.broadcasted_iota(jnp.int32, (LANES, past + LANES), 0)
    ecol = lax.broadcasted_iota(jnp.int32, (LANES, past + LANES), 1)
    page, second = ecol // PAGE_SIZE, (ecol % PAGE_SIZE) >= CMP_BLOCK
    expand = (((erow < n_pages) & (page == erow) & ~second & (ecol < past))
              | ((erow >= n_pages) & (erow < nb) & (page == erow - n_pages) & second & (ecol < past))
              | ((erow == nb) & (ecol == past))).astype(BF16)

    assert n % seqs == 0
    nlane = LANES if n % LANES == 0 else n
    assert nlane % seqs == 0

    def kc_map(u, j):
        return lambda i, pt: (pt[seqs * i + u, j], 0, 0)

    def slc_map(u, j):
        return lambda i, pt: (pt[seqs * i + u, j], 1, 0, 0, 0)

    in_specs = [pl.BlockSpec((seqs, NSA_HEADS, dh), lambda i, pt: (i, 0, 0)),
                pl.BlockSpec((seqs, 1, KV_COLS), lambda i, pt: (i, 0, 0)),
                pl.BlockSpec((seqs, 1, LANES), lambda i, pt: (i, 0, 0)),
                pl.BlockSpec((seqs, 2, g2, dh, wlen), lambda i, pt: (i, 0, 0, 0, 0)),
                pl.BlockSpec((2 * g2 * dh, nlane), lambda i, pt: (0, seqs * i // nlane)),
                pl.BlockSpec((LANES, past + LANES), lambda i, pt: (0, 0))]
    in_specs += [pl.BlockSpec((1, 1, 4 * LANES), kc_map(u, j)) for u in range(seqs) for j in range(n_pages)]
    in_specs += [pl.BlockSpec((1, 2, g2, dh, PAGE_SIZE), slc_map(u, j)) for u in range(seqs) for j in range(n_pages)]
    grid_spec = pltpu.PrefetchScalarGridSpec(
        num_scalar_prefetch=1, grid=(n // seqs,), in_specs=in_specs,
        out_specs=[pl.BlockSpec((seqs, NSA_HEADS, dh), lambda i, pt: (i, 0, 0)),
                   pl.BlockSpec((seqs, 2, g2, dh, wlen), lambda i, pt: (i, 0, 0, 0, 0))],
        scratch_shapes=[pltpu.VMEM((seqs, n_pages, 4 * LANES), F32)])
    return pl.pallas_call(
        functools.partial(_nsa_decode_kernel, n_pages=n_pages, seqs=seqs),
        grid_spec=grid_spec,
        out_shape=[jax.ShapeDtypeStruct((n, NSA_HEADS, dh), F32), jax.ShapeDtypeStruct(win_t.shape, F32)],
        compiler_params=_cp(("arbitrary",)),
        name="nsa_decode",
    )(page_table, q3, new_row, gate_row, win_t, new_win, expand,
      *([kc_phys] * (seqs * n_pages)), *([cache_t] * (seqs * n_pages)))


def _compress_pages_kernel(x_ref, wd_ref, ped_ref, b1_ref, w2_ref, o_ref, acc_ref, *, dd):
    j = pl.program_id(1)
    g2 = NSA_KV_HEADS

    @pl.when(j == 0)
    def _():
        acc_ref[...] = jnp.zeros_like(acc_ref)

    for kind in range(2):
        for g in range(g2):
            part = None
            xt = jnp.swapaxes(x_ref[:, kind, g], 0, 1)
            for dp in range(dd // 2):
                xs = jnp.concatenate([xt[2 * dp] + ped_ref[kind, 2 * dp:2 * dp + 1, :],
                                      xt[2 * dp + 1] + ped_ref[kind, 2 * dp + 1:2 * dp + 2, :]], axis=1)
                d = jnp.dot(xs.astype(BF16), wd_ref[kind, dp], preferred_element_type=F32)
                part = d if part is None else part + d
            acc_ref[kind * g2 + g] += part

    @pl.when(j == pl.num_programs(1) - 1)
    def _():
        for kind in range(2):
            for g in range(g2):
                h = jnp.maximum(acc_ref[kind * g2 + g] + b1_ref[kind:kind + 1, :], 0.0)
                o_ref[:, 0, (kind * g2 + g) * LANES:(kind * g2 + g + 1) * LANES] = jnp.dot(
                    h.astype(BF16), w2_ref[kind], preferred_element_type=F32)


def compress_pages(cache_t, cwp, bp, dd=8):
    n_phys = cache_t.shape[0]
    g2, dh = NSA_KV_HEADS, HEAD_DIM
    assert n_phys % bp == 0 and dh % dd == 0 and dd % 2 == 0
    wd, ped, b1h, w2h = cwp
    return pl.pallas_call(
        functools.partial(_compress_pages_kernel, dd=dd),
        grid=(n_phys // bp, dh // dd),
        in_specs=[pl.BlockSpec((bp, 2, g2, dd, PAGE_SIZE), lambda i, j: (i, 0, 0, j, 0)),
                  pl.BlockSpec((2, dd // 2, 2 * PAGE_SIZE, 2 * CMP_HIDDEN), lambda i, j: (0, j, 0, 0)),
                  pl.BlockSpec((2, dd, PAGE_SIZE), lambda i, j: (0, j, 0)),
                  pl.BlockSpec((2, 2 * CMP_HIDDEN), lambda i, j: (0, 0)),
                  pl.BlockSpec((2, 2 * CMP_HIDDEN, LANES), lambda i, j: (0, 0, 0))],
        out_specs=pl.BlockSpec((bp, 1, 2 * g2 * LANES), lambda i, j: (i, 0, 0)),
        out_shape=jax.ShapeDtypeStruct((n_phys, 1, 2 * g2 * LANES), F32),
        scratch_shapes=[pltpu.VMEM((2 * g2, bp, 2 * CMP_HIDDEN), F32)],
        compiler_params=_cp(("parallel", "arbitrary")),
        name="compress_pages",
    )(cache_t, wd, ped, b1h, w2h)


def _compress_page_weights(cmp_pe, cmp_w1, cmp_b1, cmp_w2):
    w1t = jnp.transpose(cmp_w1.reshape(2, CMP_BLOCK, HEAD_DIM, CMP_HIDDEN), (0, 2, 1, 3))
    z = jnp.zeros_like(w1t)
    wd = jnp.concatenate([jnp.concatenate([w1t, z], -1), jnp.concatenate([z, w1t], -1)], axis=2).astype(BF16)
    wd = wd.reshape(2, HEAD_DIM // 2, 2 * PAGE_SIZE, 2 * CMP_HIDDEN)
    pet = jnp.transpose(cmp_pe, (0, 2, 1))
    ped = jnp.concatenate([pet, pet], -1)
    b1h = jnp.concatenate([cmp_b1, cmp_b1], -1)
    z2 = jnp.zeros_like(cmp_w2)
    w2h = jnp.concatenate([jnp.concatenate([cmp_w2, z2], -1), jnp.concatenate([z2, cmp_w2], -1)], axis=1).astype(BF16)
    return wd, ped, b1h, w2h


def _split_w_in(w):
    d = w.shape[0]
    off_b = DN_CONV_DIM + DN_WIDTH
    off_q = off_b + 2 * DN_HEADS
    off_kv = off_q + NSA_WIDTH
    off_g = off_kv + KV_COLS
    n_g = 3 * NSA_HEADS
    pad = jnp.zeros((d, P_COLS - P_SMALL - 2 * DN_HEADS - n_g), w.dtype)
    main = jnp.concatenate([w[:, :off_b], w[:, off_q:off_kv], w[:, off_kv:off_kv + CMP_COLS], w[:, off_b:off_q],
                            w[:, off_g:off_g + n_g], pad], axis=1)
    return main.astype(BF16), w[:, off_kv:off_g].T.astype(BF16)


def _row_tile(n, cap):
    t = min(n, cap)
    while n % t:
        t //= 2
    return t


def _trunk_tail(x2, mixer_dn, mixer_nsa, mem_kv3, bshape, lw, final_norm):
    n, d = x2.shape
    b, t = bshape
    if t >= SUBLANES:
        x3 = mem_block(x2.reshape(b, t, d), mixer_dn.reshape(b, t, -1), mixer_nsa.reshape(b, t, -1),
                       lw["w_out_dn"], lw["w_out_nsa"], lw["ln_mem"], lw["w_mem_q"], mem_kv3, lw["w_mem_o"],
                       _row_tile(t, 512))
        x2 = x3.reshape(n, d)
    else:
        assert t == 1
        tm = _row_tile(n, 512)
        x2 = matmul_residual(x2, [mixer_dn, mixer_nsa], [lw["w_out_dn"], lw["w_out_nsa"]], tm)
        qm = rms_matmul(x2, lw["ln_mem"], lw["w_mem_q"], tm, 512)
        att = mem_attention_row(qm.reshape(n, MEM_HEADS, d // MEM_HEADS), mem_kv3).reshape(n, d)
        x2 = matmul_residual(x2, [att], [lw["w_mem_o"]], tm)
    return ffn(x2, lw["ln_ffn"], lw["w_up"], lw["w_down"], lw["ln_final"], final_norm, _row_tile(n, 1024), 512)


def _kv_rows(kvt):
    b, rows, s = kvt.shape
    g, dh = NSA_KV_HEADS, HEAD_DIM
    return jnp.transpose(kvt.reshape(b, rows // (g * dh), g, dh, s), (0, 4, 1, 2, 3))


def _prompt_layer(xp, mem_prompt, lw, cw, final_norm):
    b, s, d = xp.shape
    n = b * s
    x2 = xp.reshape(n, d)
    proj, kvt, kvt_win = input_projection(xp, lw["ln_mix"], lw["w_in"], lw["w_kvt"], _row_tile(s, 1024), 1024)
    proj3 = proj.reshape(b, s, P_COLS)
    dn_out, s_pairs = deltanet_prompt(proj3, lw["dn_conv_w"], lw["dn_a_log"], lw["dn_dt_bias"], lw["dn_norm"])
    p_conv = proj3[:, s - (CONV_WIDTH - 1):, P_QKV:P_QKV + DN_CONV_DIM]
    p_rec = _pairs_to_heads(s_pairs)
    nb = s // CMP_BLOCK
    kcv = compress_blocks(proj3[:, :nb * CMP_BLOCK].reshape(b * nb, CMP_BLOCK, P_COLS), P_CMP, cw,
                          _row_tile(b * nb, 256))
    kcv = kcv.reshape(b, nb, 2, NSA_KV_HEADS, HEAD_DIM)
    kc = jnp.transpose(kcv[:, :, 0], (0, 2, 1, 3))
    vct = jnp.transpose(kcv[:, :, 1], (0, 2, 3, 1))
    ocmpt, selt = nsa_cmp_topk(proj3, kc, vct, _row_tile(s, 512))
    nsa_out = nsa_attention(proj3, kvt, kvt_win, selt, ocmpt, 256, 512).reshape(n, NSA_WIDTH)
    m = mem_prompt.shape[1]
    mem_kv = rms_matmul(mem_prompt.reshape(b * m, d), lw["ln_memkv"], lw["w_mem_kv"], _row_tile(b * m, 512), 512)
    mem_kv5 = mem_kv.reshape(b, m, 2, MEM_HEADS, d // MEM_HEADS)
    y = _trunk_tail(x2, dn_out.reshape(n, DN_WIDTH), nsa_out, mem_kv5, (b, s), lw, final_norm)
    wk = min(WINDOW, s)
    return y.reshape(b, s, d), _kv_rows(kvt), _kv_rows(kvt_win[:, :, s - wk:]), mem_kv5, p_conv, p_rec


def _sample_layer(xs, cache_nsa, cache_win, cache_mem, conv_state, rec_state, page_table, lw, cwp, final_norm):
    db, ds, d = xs.shape
    assert ds == 1
    n = db
    x2 = xs.reshape(n, d)
    proj, kvt, kvt_win = input_projection(x2[None], lw["ln_mix"], lw["w_in"], lw["w_kvt"], n, 1024)
    dn_out, s_rec = deltanet_sample(proj, conv_state, rec_state, lw["dn_conv_w"], lw["dn_a_log"], lw["dn_dt_bias"],
                                    lw["dn_norm"])
    s_conv = jnp.concatenate([conv_state[:, 1:], proj[:, None, P_QKV:P_QKV + DN_CONV_DIM]], axis=1)
    n_phys = cache_nsa.shape[0]
    cache_t = jnp.transpose(cache_nsa, (0, 2, 3, 4, 1))
    win_t = jnp.transpose(cache_win, (0, 2, 3, 4, 1))
    kc_phys = compress_pages(cache_t, cwp, _row_tile(n_phys, 256))
    kv_new = jnp.concatenate([kvt[0], kvt_win[0]], axis=0).T
    o8, s_win_t = nsa_decode(page_table, proj[:, P_NQ:P_NQ + NSA_WIDTH].reshape(n, NSA_HEADS, HEAD_DIM),
                             kv_new.reshape(n, 1, KV_COLS),
                             _flat_gates(proj[:, P_SMALL:P_SMALL + LANES]).reshape(n, 1, LANES),
                             win_t, kvt_win[0], kc_phys, cache_t)
    s_nsa = jnp.transpose(kvt.reshape(-1, NSA_KV_HEADS, HEAD_DIM, n), (3, 0, 1, 2))[:, None]
    s_win = jnp.transpose(s_win_t, (0, 4, 1, 2, 3))
    y = _trunk_tail(x2, dn_out, o8.reshape(n, NSA_WIDTH), cache_mem, (n, 1), lw, final_norm)
    return y.reshape(db, ds, d), s_nsa, s_win, s_conv, s_rec


def _flat_gates(small):
    g0 = 2 * DN_HEADS
    n_g = 3 * NSA_HEADS
    gl = small[..., g0:g0 + n_g]
    return jnp.concatenate([gl, jnp.zeros(gl.shape[:-1] + (LANES - n_g,), gl.dtype)], axis=-1)


def kernel(x_prompt, x_sample, mem_prompt, cache_nsa_kv, cache_win_kv, cache_mem_kv, state_dn_conv, state_dn_rec, page_table, ln_mix, w_in, dn_conv_w, dn_a_log, dn_dt_bias, dn_norm, cmp_pe, cmp_w1, cmp_b1, cmp_w2, w_out, ln_mem, ln_memkv, w_mem_q, w_mem_kv, w_mem_o, ln_ffn, w_up, w_down, ln_final):
    depth = w_in.shape[0]
    xp, xs = x_prompt, x_sample
    outs_p = [[] for _ in range(5)]
    outs_s = [[] for _ in range(4)]
    for l in range(depth):
        lw = {
            "ln_mix": ln_mix[l],
            "dn_conv_w": dn_conv_w[l], "dn_a_log": dn_a_log[l], "dn_dt_bias": dn_dt_bias[l], "dn_norm": dn_norm[l],
            "w_out_dn": w_out[l][:DN_WIDTH].astype(BF16), "w_out_nsa": w_out[l][DN_WIDTH:].astype(BF16),
            "ln_mem": ln_mem[l], "ln_memkv": ln_memkv[l], "w_mem_q": w_mem_q[l].astype(BF16),
            "w_mem_kv": w_mem_kv[l].astype(BF16), "w_mem_o": w_mem_o[l].astype(BF16),
            "ln_ffn": ln_ffn[l], "w_up": w_up[l].astype(BF16), "w_down": w_down[l].astype(BF16),
            "ln_final": ln_final,
        }
        lw["w_in"], lw["w_kvt"] = _split_w_in(w_in[l])
        cw = _compress_weights(cmp_pe[l], cmp_w1[l], cmp_b1[l], cmp_w2[l])
        last = l == depth - 1
        xp, p_nsa, p_win, p_mem, p_conv, p_rec = _prompt_layer(xp, mem_prompt, lw, cw, last)
        for acc, val in zip(outs_p, (p_nsa, p_win, p_mem, p_conv, p_rec)):
            acc.append(val)
        xs, s_nsa, s_win, s_conv, s_rec = _sample_layer(
            xs, cache_nsa_kv[l], cache_win_kv[l], cache_mem_kv[l], state_dn_conv[l], state_dn_rec[l],
            page_table, lw, _compress_page_weights(cmp_pe[l], cmp_w1[l], cmp_b1[l], cmp_w2[l]), last)
        for acc, val in zip(outs_s, (s_nsa, s_win, s_conv, s_rec)):
            acc.append(val)
    return (xp, xs) + tuple(jnp.stack(a) for a in outs_p) + tuple(jnp.stack(a) for a in outs_s)
```

```python
import functools

import jax
import jax.numpy as jnp
from jax import lax
from jax.experimental import pallas as pl
from jax.experimental.pallas import tpu as pltpu

F32 = jnp.float32
BF16 = jnp.bfloat16

HEAD_DIM = 64
DN_HEADS = 8
NSA_HEADS = 8
NSA_KV_HEADS = 2
NSA_GROUP = NSA_HEADS // NSA_KV_HEADS
DN_WIDTH = DN_HEADS * HEAD_DIM
NSA_WIDTH = NSA_HEADS * HEAD_DIM
CONV_WIDTH = 4
DN_CONV_DIM = 3 * DN_WIDTH
DN_CHUNK = 64
CMP_BLOCK = 64
SEL_BLOCK = 64
TOP_N = 16
N_LOCAL_BLOCKS = 2
WINDOW = 512
CMP_HIDDEN = 128
NSA_KV_KINDS = 6
MEM_HEADS = 4
PAGE_SIZE = 128
RMS_EPS = 1e-6
FORCE_SCORE = 1e3
NEG_INF = -1e30
ATTN_SCALE = HEAD_DIM ** -0.5
LOG2_E = 1.4426950408889634

LANES = 128
SUBLANES = 8
P_QKV = 0
P_Z = P_QKV + DN_CONV_DIM
P_NQ = P_Z + DN_WIDTH
P_CMP = P_NQ + NSA_WIDTH
CMP_COLS = 2 * NSA_KV_HEADS * HEAD_DIM
P_SMALL = P_CMP + CMP_COLS
P_COLS = P_SMALL + 2 * LANES
KV_COLS = NSA_KV_KINDS * NSA_KV_HEADS * HEAD_DIM
VMEM_LIMIT = 56 * 1024 * 1024
DN_APPLY_PASSES = (3, 3, 3, 1, 1, 1)
DN_SQUARE_PASSES = (3, 3, 1, 1, 1)


def _cp(sem, vmem=VMEM_LIMIT):
    return pltpu.CompilerParams(dimension_semantics=sem, vmem_limit_bytes=vmem)


def _split2(a):
    hi = a.astype(BF16)
    return hi, (a - hi.astype(F32)).astype(BF16)


def _dot(a, b, passes=1):
    if passes == 1:
        return jnp.dot(a.astype(BF16), b.astype(BF16), preferred_element_type=F32)
    ah, al = _split2(a)
    bh, bl = _split2(b)
    return (jnp.dot(ah, bh, preferred_element_type=F32) + jnp.dot(ah, bl, preferred_element_type=F32)
            + jnp.dot(al, bh, preferred_element_type=F32))


def _dot_nt(a, b):
    return lax.dot_general(a.astype(BF16), b.astype(BF16), (((1,), (1,)), ((), ())),
                           preferred_element_type=F32)


def _split3(a):
    hi = a.astype(BF16)
    r1 = a - hi.astype(F32)
    mid = r1.astype(BF16)
    lo = (r1 - mid.astype(F32)).astype(BF16)
    return hi, mid, lo


def _dot_exact_lhs01(a01, b):
    a = a01.astype(BF16)
    hi, mid, lo = _split3(b)
    return (jnp.dot(a, hi, preferred_element_type=F32) + jnp.dot(a, mid, preferred_element_type=F32)
            + jnp.dot(a, lo, preferred_element_type=F32))


def _dot_exact_rhs01(a, b01):
    b = b01.astype(BF16)
    hi, mid, lo = _split3(a)
    return (jnp.dot(hi, b, preferred_element_type=F32) + jnp.dot(mid, b, preferred_element_type=F32)
            + jnp.dot(lo, b, preferred_element_type=F32))


def _rms(x, gain):
    ms = jnp.mean(x * x, axis=-1, keepdims=True)
    return x * lax.rsqrt(ms + RMS_EPS) * gain


def _softplus(x):
    return jnp.maximum(x, 0.0) + jnp.log1p(jnp.exp(-jnp.abs(x)))


def _silu(x):
    return x * jax.nn.sigmoid(x)


def _rms_mm_kernel(x_ref, g_ref, w_ref, o_ref, h_ref):
    @pl.when(pl.program_id(1) == 0)
    def _():
        h_ref[...] = _rms(x_ref[...], g_ref[...]).astype(BF16)

    o_ref[...] = jnp.dot(h_ref[...], w_ref[...], preferred_element_type=F32)


def rms_matmul(x, gain, w_bf16, tm, tn):
    n, d = x.shape
    m = w_bf16.shape[1]
    assert n % tm == 0 and m % tn == 0
    return pl.pallas_call(
        _rms_mm_kernel,
        grid=(n // tm, m // tn),
        in_specs=[pl.BlockSpec((tm, d), lambda i, j: (i, 0)),
                  pl.BlockSpec((1, d), lambda i, j: (0, 0)),
                  pl.BlockSpec((d, tn), lambda i, j: (0, j))],
        out_specs=pl.BlockSpec((tm, tn), lambda i, j: (i, j)),
        out_shape=jax.ShapeDtypeStruct((n, m), F32),
        scratch_shapes=[pltpu.VMEM((tm, d), BF16)],
        compiler_params=_cp(("parallel", "arbitrary")),
        name="rms_matmul",
    )(x, gain.reshape(1, d), w_bf16)


def _proj_kernel(x_ref, g_ref, w_ref, wkv_ref, o_ref, okv_ref, owin_ref, h_ref):
    @pl.when(pl.program_id(1) == 0)
    def _():
        h = _rms(x_ref[...], g_ref[...]).astype(BF16)
        h_ref[...] = h
        kvt = lax.dot_general(wkv_ref[...], h, (((1,), (1,)), ((), ())), preferred_element_type=F32)
        split = okv_ref.shape[1]
        okv_ref[0] = kvt[0:split]
        owin_ref[0] = kvt[split:]

    o_ref[...] = jnp.dot(h_ref[...], w_ref[...], preferred_element_type=F32)


def input_projection(x3, gain, w_main, w_kvt, tm, tn):
    b, s, d = x3.shape
    n = b * s
    m = w_main.shape[1]
    kvc = w_kvt.shape[0]
    win = 2 * NSA_KV_HEADS * HEAD_DIM
    assert s % tm == 0 and m % tn == 0
    spt = s // tm
    return pl.pallas_call(
        _proj_kernel,
        grid=(n // tm, m // tn),
        in_specs=[pl.BlockSpec((tm, d), lambda i, j: (i, 0)),
                  pl.BlockSpec((1, d), lambda i, j: (0, 0)),
                  pl.BlockSpec((d, tn), lambda i, j: (0, j)),
                  pl.BlockSpec((kvc, d), lambda i, j: (0, 0))],
        out_specs=[pl.BlockSpec((tm, tn), lambda i, j: (i, j)),
                   pl.BlockSpec((1, kvc - win, tm), lambda i, j: (i // spt, 0, i % spt)),
                   pl.BlockSpec((1, win, tm), lambda i, j: (i // spt, 0, i % spt))],
        out_shape=[jax.ShapeDtypeStruct((n, m), F32), jax.ShapeDtypeStruct((b, kvc - win, s), F32),
                   jax.ShapeDtypeStruct((b, win, s), F32)],
        scratch_shapes=[pltpu.VMEM((tm, d), BF16)],
        compiler_params=_cp(("parallel", "arbitrary")),
        name="input_projection",
    )(x3.reshape(n, d), gain.reshape(1, d), w_main, w_kvt)


def _mm_res_kernel(*refs, n_in):
    res_ref = refs[0]
    a_refs = refs[1:1 + n_in]
    w_refs = refs[1 + n_in:1 + 2 * n_in]
    o_ref = refs[1 + 2 * n_in]
    acc = res_ref[...]
    for a_ref, w_ref in zip(a_refs, w_refs):
        acc = acc + jnp.dot(a_ref[...].astype(BF16), w_ref[...], preferred_element_type=F32)
    o_ref[...] = acc


def matmul_residual(res, a_list, w_list, tm):
    n, d = res.shape
    assert n % tm == 0
    n_in = len(a_list)
    in_specs = [pl.BlockSpec((tm, d), lambda i: (i, 0))]
    in_specs += [pl.BlockSpec((tm, a.shape[1]), lambda i: (i, 0)) for a in a_list]
    in_specs += [pl.BlockSpec(w.shape, lambda i: (0, 0)) for w in w_list]
    return pl.pallas_call(
        functools.partial(_mm_res_kernel, n_in=n_in),
        grid=(n // tm,),
        in_specs=in_specs,
        out_specs=pl.BlockSpec((tm, d), lambda i: (i, 0)),
        out_shape=jax.ShapeDtypeStruct((n, d), F32),
        compiler_params=_cp(("parallel",)),
        name="matmul_residual",
    )(res, *a_list, *w_list)


def _ffn_kernel(x_ref, g_ref, wu_ref, wd_ref, gf_ref, o_ref, hn_ref, acc_ref, *, final_norm):
    j = pl.program_id(1)

    @pl.when(j == 0)
    def _():
        x = x_ref[...]
        hn_ref[...] = _rms(x, g_ref[...]).astype(BF16)
        acc_ref[...] = x

    u = jnp.dot(hn_ref[...], wu_ref[...], preferred_element_type=F32)
    u = jnp.square(jnp.maximum(u, 0.0)).astype(BF16)
    acc_ref[...] += jnp.dot(u, wd_ref[...], preferred_element_type=F32)

    @pl.when(j == pl.num_programs(1) - 1)
    def _():
        y = acc_ref[...]
        if final_norm:
            y = _rms(y, gf_ref[...])
        o_ref[...] = y


def ffn(x, gain, wu_bf16, wd_bf16, gain_final, final_norm, tm, tf):
    n, d = x.shape
    f = wu_bf16.shape[1]
    assert n % tm == 0 and f % tf == 0
    return pl.pallas_call(
        functools.partial(_ffn_kernel, final_norm=final_norm),
        grid=(n // tm, f // tf),
        in_specs=[pl.BlockSpec((tm, d), lambda i, j: (i, 0)),
                  pl.BlockSpec((1, d), lambda i, j: (0, 0)),
                  pl.BlockSpec((d, tf), lambda i, j: (0, j)),
                  pl.BlockSpec((tf, d), lambda i, j: (j, 0)),
                  pl.BlockSpec((1, d), lambda i, j: (0, 0))],
        out_specs=pl.BlockSpec((tm, d), lambda i, j: (i, 0)),
        out_shape=jax.ShapeDtypeStruct((n, d), F32),
        scratch_shapes=[pltpu.VMEM((tm, d), BF16), pltpu.VMEM((tm, d), F32)],
        compiler_params=_cp(("parallel", "arbitrary")),
        name="ffn",
    )(x, gain.reshape(1, d), wu_bf16, wd_bf16, gain_final.reshape(1, d))


def _mem_attn_row_kernel(q_ref, kv_ref, o_ref, *, m, seqs, heads):
    sub = SUBLANES
    ones = jnp.ones((LANES, LANES), BF16)
    for u in range(seqs):
        x = kv_ref[u].reshape(m, 2 * sub, LANES)
        prod = x[:, 0:sub] * q_ref[u][None]
        part = jnp.dot(prod.reshape(m * sub, LANES).astype(BF16), ones, preferred_element_type=F32)
        part = part.reshape(m, sub, LANES)
        s = part + pltpu.roll(part, heads, 1)
        p = jnp.exp2(s - jnp.max(s, axis=0, keepdims=True))
        o_ref[u] = jnp.sum(p * x[:, sub:2 * sub], axis=0) / jnp.sum(p, axis=0)


def mem_attention_row(q, kv, seqs=4):
    b, h, hd = q.shape
    m = kv.shape[1]
    assert kv.shape[2:] == (2, h, hd)
    tiles = hd // LANES
    assert tiles * LANES == hd and tiles == 2 and tiles * h == SUBLANES and b % seqs == 0
    rows = jnp.transpose(kv.reshape(b, m, 2, h, tiles, LANES), (0, 1, 2, 4, 3, 5)).reshape(b, m * 2 * SUBLANES, LANES)
    q8 = jnp.transpose((q * (hd ** -0.5 * LOG2_E)).reshape(b, h, tiles, LANES), (0, 2, 1, 3)).reshape(b, SUBLANES, LANES)
    o8 = pl.pallas_call(
        functools.partial(_mem_attn_row_kernel, m=m, seqs=seqs, heads=h),
        grid=(b // seqs,),
        in_specs=[pl.BlockSpec((seqs, SUBLANES, LANES), lambda i: (i, 0, 0)),
                  pl.BlockSpec((seqs, m * 2 * SUBLANES, LANES), lambda i: (i, 0, 0))],
        out_specs=pl.BlockSpec((seqs, SUBLANES, LANES), lambda i: (i, 0, 0)),
        out_shape=jax.ShapeDtypeStruct((b, SUBLANES, LANES), F32),
        compiler_params=_cp(("parallel",)),
        name="mem_attention_row",
    )(q8, rows)
    return jnp.transpose(o8.reshape(b, tiles, h, LANES), (0, 2, 1, 3)).reshape(b, h, hd)


def _mem_block_kernel(x_ref, a1_ref, a2_ref, w1_ref, w2_ref, g_ref, wq_ref, kv_ref, wo_ref, o_ref, *, heads, hd):
    scale = hd ** -0.5
    x = (x_ref[0] + jnp.dot(a1_ref[0].astype(BF16), w1_ref[...], preferred_element_type=F32)
         + jnp.dot(a2_ref[0].astype(BF16), w2_ref[...], preferred_element_type=F32))
    q = jnp.dot(_rms(x, g_ref[...]).astype(BF16), wq_ref[...], preferred_element_type=F32)
    d = heads * hd
    outs = []
    for h in range(heads):
        s = _dot_nt(q[:, h * hd:(h + 1) * hd], kv_ref[0, :, h * hd:(h + 1) * hd]) * scale
        p = jnp.exp(s - jnp.max(s, axis=-1, keepdims=True))
        p = p / jnp.sum(p, axis=-1, keepdims=True)
        outs.append(_dot(p, kv_ref[0, :, d + h * hd:d + (h + 1) * hd]))
    att = jnp.concatenate(outs, axis=1).astype(BF16)
    o_ref[0] = x + jnp.dot(att, wo_ref[...], preferred_element_type=F32)


def mem_block(x3, a1, a2, w1, w2, gain, wq, kv, wo, tq):
    b, t, d = x3.shape
    m = kv.shape[1]
    hd = d // MEM_HEADS
    assert t % tq == 0 and kv.shape[2] == 2 * d
    return pl.pallas_call(
        functools.partial(_mem_block_kernel, heads=MEM_HEADS, hd=hd),
        grid=(b, t // tq),
        in_specs=[pl.BlockSpec((1, tq, d), lambda i, j: (i, j, 0)),
                  pl.BlockSpec((1, tq, a1.shape[2]), lambda i, j: (i, j, 0)),
                  pl.BlockSpec((1, tq, a2.shape[2]), lambda i, j: (i, j, 0)),
                  pl.BlockSpec(w1.shape, lambda i, j: (0, 0)),
                  pl.BlockSpec(w2.shape, lambda i, j: (0, 0)),
                  pl.BlockSpec((1, d), lambda i, j: (0, 0)),
                  pl.BlockSpec((d, d), lambda i, j: (0, 0)),
                  pl.BlockSpec((1, m, 2 * d), lambda i, j: (i, 0, 0)),
                  pl.BlockSpec((d, d), lambda i, j: (0, 0))],
        out_specs=pl.BlockSpec((1, tq, d), lambda i, j: (i, j, 0)),
        out_shape=jax.ShapeDtypeStruct((b, t, d), F32),
        compiler_params=_cp(("parallel", "parallel")),
        name="mem_block",
    )(x3, a1, a2, w1, w2, gain.reshape(1, d), wq, kv, wo)


def _compress_kernel(x_ref, w1_ref, pe_ref, b1_ref, w2_ref, o_ref, acc_ref, *, tt):
    j = pl.program_id(1)

    @pl.when(j == 0)
    def _():
        acc_ref[...] = jnp.zeros_like(acc_ref)

    hid2 = 2 * CMP_HIDDEN
    xt = jnp.swapaxes(x_ref[...], 0, 1)
    for kind in range(2):
        part = None
        for t in range(tt):
            xs = xt[t][:, kind * LANES:(kind + 1) * LANES] + pe_ref[kind, t:t + 1, :]
            d = jnp.dot(xs.astype(BF16), w1_ref[kind, t], preferred_element_type=F32)
            part = d if part is None else part + d
        acc_ref[:, kind * hid2:(kind + 1) * hid2] += part

    @pl.when(j == pl.num_programs(1) - 1)
    def _():
        h = jnp.maximum(acc_ref[...] + b1_ref[...], 0.0)
        for kind in range(2):
            o_ref[:, kind * LANES:(kind + 1) * LANES] = jnp.dot(
                h[:, kind * hid2:(kind + 1) * hid2].astype(BF16), w2_ref[kind], preferred_element_type=F32)


def compress_blocks(x3, col0, cw, bt, tt=8):
    nb = x3.shape[0]
    assert nb % bt == 0 and CMP_BLOCK % tt == 0 and col0 % CMP_COLS == 0 and x3.shape[1] == CMP_BLOCK
    w1bd, pe2, b1bd, w2bd = cw
    return pl.pallas_call(
        functools.partial(_compress_kernel, tt=tt),
        grid=(nb // bt, CMP_BLOCK // tt),
        in_specs=[pl.BlockSpec((bt, tt, CMP_COLS), lambda i, j: (i, j, col0 // CMP_COLS)),
                  pl.BlockSpec((2, tt, LANES, 2 * CMP_HIDDEN), lambda i, j: (0, j, 0, 0)),
                  pl.BlockSpec((2, tt, LANES), lambda i, j: (0, j, 0)),
                  pl.BlockSpec((1, 4 * CMP_HIDDEN), lambda i, j: (0, 0)),
                  pl.BlockSpec((2, 2 * CMP_HIDDEN, LANES), lambda i, j: (0, 0, 0))],
        out_specs=pl.BlockSpec((bt, 2 * LANES), lambda i, j: (i, 0)),
        out_shape=jax.ShapeDtypeStruct((nb, 2 * LANES), F32),
        scratch_shapes=[pltpu.VMEM((bt, 4 * CMP_HIDDEN), F32)],
        compiler_params=_cp(("parallel", "arbitrary")),
        name="compress_blocks",
    )(x3, w1bd, pe2, b1bd, w2bd)


def _compress_weights(cmp_pe, cmp_w1, cmp_b1, cmp_w2):
    w1r = cmp_w1.reshape(2, CMP_BLOCK, HEAD_DIM, CMP_HIDDEN)
    z = jnp.zeros_like(w1r)
    w1bd = jnp.concatenate([jnp.concatenate([w1r, z], -1), jnp.concatenate([z, w1r], -1)], axis=2).astype(BF16)
    pe2 = jnp.concatenate([cmp_pe, cmp_pe], -1)
    b1bd = jnp.concatenate([cmp_b1[0], cmp_b1[0], cmp_b1[1], cmp_b1[1]]).reshape(1, 4 * CMP_HIDDEN)
    z2 = jnp.zeros_like(cmp_w2)
    w2bd = jnp.concatenate([jnp.concatenate([cmp_w2, z2], -1), jnp.concatenate([z2, cmp_w2], -1)], axis=1).astype(BF16)
    return w1bd, pe2, b1bd, w2bd


def _masked_softmax(s, mask):
    s = jnp.where(mask, s, NEG_INF)
    p = jnp.where(mask, jnp.exp(s - jnp.max(s, axis=-1, keepdims=True)), 0.0)
    return p / jnp.maximum(jnp.sum(p, axis=-1, keepdims=True), 1e-30)


def _heads_as_rows(q_ref, tq, scale=ATTN_SCALE):
    qb = q_ref[0] * scale
    return jnp.concatenate([qb[:, h * HEAD_DIM:(h + 1) * HEAD_DIM] for h in range(NSA_GROUP)], axis=0).astype(BF16)


def _cmp_topk_kernel(q_ref, kc_ref, vct_ref, ocmp_ref, sel_ref, *, tq, nblk, topn):
    qi = pl.program_id(2)
    r, dh = NSA_GROUP, HEAD_DIM
    qpos = qi * tq + lax.broadcasted_iota(jnp.int32, (nblk, tq), 1)
    blk = lax.broadcasted_iota(jnp.int32, (nblk, tq), 0)
    vis = (blk + 1) * CMP_BLOCK - 1 <= qpos
    q = _heads_as_rows(q_ref, tq)
    kc = kc_ref[0, 0]
    vct = vct_ref[0, 0]
    imp = jnp.zeros((nblk, tq), F32)
    for h in range(r):
        s = jnp.where(vis, _dot_nt(kc, q[h * tq:(h + 1) * tq]), NEG_INF)
        p = jnp.where(vis, jnp.exp(s - jnp.max(s, axis=0, keepdims=True)), 0.0)
        p = p / jnp.maximum(jnp.sum(p, axis=0, keepdims=True), 1e-30)
        ocmp_ref[0, 0, h * dh:(h + 1) * dh, :] = _dot(vct, p)
        imp = imp + p
    cur = lax.shift_right_logical(qpos, 6)
    valid = blk <= cur
    forced = valid & ((blk == 0) | (cur - blk < N_LOCAL_BLOCKS))
    score = jnp.where(valid, imp + jnp.where(forced, FORCE_SCORE, 0.0), -1.0)
    rank = jnp.zeros((nblk, tq), F32)
    row8 = lax.broadcasted_iota(jnp.int32, (SUBLANES, tq), 0)
    for i in range(nblk):
        si = score[i:i + 1, :]
        t0 = i // SUBLANES * SUBLANES
        mid = score[t0:t0 + SUBLANES]
        parts = [jnp.where(row8 > i - t0, jnp.where(si >= mid, 1.0, 0.0), jnp.where(si > mid, 1.0, 0.0))]
        if t0 > 0:
            parts.insert(0, jnp.where(si > score[0:t0], 1.0, 0.0))
        if t0 + SUBLANES < nblk:
            parts.append(jnp.where(si >= score[t0 + SUBLANES:], 1.0, 0.0))
        rank = rank + jnp.concatenate(parts, axis=0)
    sel_ref[0, 0] = jnp.where(rank < topn, 1.0, 0.0).astype(BF16)


def nsa_cmp_topk(proj3, kc, vct, tq):
    b, sq, _ = proj3.shape
    g = kc.shape[1]
    nblk = kc.shape[2]
    dh = HEAD_DIM
    gw = NSA_GROUP * dh
    assert sq % tq == 0 and SEL_BLOCK == 64 and P_NQ % gw == 0 and nblk % SUBLANES == 0
    return pl.pallas_call(
        functools.partial(_cmp_topk_kernel, tq=tq, nblk=nblk, topn=min(TOP_N, nblk)),
        grid=(b, g, sq // tq),
        in_specs=[pl.BlockSpec((1, tq, gw), lambda i, j, k: (i, k, P_NQ // gw + j)),
                  pl.BlockSpec((1, 1, nblk, dh), lambda i, j, k: (i, j, 0, 0)),
                  pl.BlockSpec((1, 1, dh, nblk), lambda i, j, k: (i, j, 0, 0))],
        out_specs=[pl.BlockSpec((1, 1, gw, tq), lambda i, j, k: (i, j, 0, k)),
                   pl.BlockSpec((1, 1, nblk, tq), lambda i, j, k: (i, j, 0, k))],
        out_shape=[jax.ShapeDtypeStruct((b, g, gw, sq), F32),
                   jax.ShapeDtypeStruct((b, g, nblk, sq), BF16)],
        compiler_params=_cp(("parallel", "parallel", "parallel")),
        name="nsa_cmp_topk",
    )(proj3, kc, vct)


def _nsa_attn_kernel(q_ref, kst_ref, vst_ref, kwt_ref, vwt_ref, selt_ref, ocmp_ref, sm_ref, o_ref,
                     m_sc, acc_sc, s_sc, bias_sc, p_sc, *, tq, tk, nsel, ck):
    grp = pl.program_id(1)
    qi = pl.program_id(2)
    r = NSA_GROUP
    dh = HEAD_DIM
    q = _heads_as_rows(q_ref, tq, ATTN_SCALE * LOG2_E)
    qpos = qi * tq + lax.broadcasted_iota(jnp.int32, (1, tq), 1)
    ones_row = jnp.where(lax.broadcasted_iota(jnp.int32, (8, 1), 0) == 0, 1.0, 0.0)

    def reset():
        m_sc[...] = jnp.full(m_sc.shape, NEG_INF, F32)
        acc_sc[...] = jnp.zeros(acc_sc.shape, F32)

    def with_ones(vt):
        return jnp.concatenate([vt, jnp.broadcast_to(ones_row, (8, vt.shape[1]))], axis=0).astype(BF16)

    def scores(kt, qrows, row0):
        s_sc[row0:row0 + kt.shape[1], :] = _dot_nt(kt.T, qrows)

    def set_bias(mask, row0):
        bias_sc[pl.ds(row0, mask.shape[0]), :] = jnp.where(mask, 0.0, NEG_INF)

    def fold(row0, n, vt, biased):
        def chunk(c, shift):
            rows = pl.ds(row0 + c * ck, ck)
            x = s_sc[rows, :] - shift
            return x + jnp.concatenate([bias_sc[rows, :]] * r, axis=1) if biased else x

        m_prev = m_sc[...]
        top = jnp.full((SUBLANES, r * tq), NEG_INF, F32)
        for c in range(n // ck):
            top = jnp.maximum(top, jnp.max(chunk(c, 0.0).reshape(ck // SUBLANES, SUBLANES, r * tq), axis=0))
        m_new = jnp.maximum(m_prev, jnp.max(top, axis=0, keepdims=True))
        for c in range(n // ck):
            p_sc[c * ck:(c + 1) * ck, :] = jnp.exp2(chunk(c, m_new)).astype(BF16)
        acc_sc[...] = jnp.exp2(m_prev - m_new) * acc_sc[...] + jnp.dot(with_ones(vt), p_sc[0:n, :],
                                                                      preferred_element_type=F32)
        m_sc[...] = m_new

    def result():
        acc = acc_sc[...]
        return acc[0:dh] / jnp.maximum(acc[dh:dh + 1], 1e-30)

    wk = WINDOW + tq
    w0 = pl.multiple_of(jnp.maximum(qi - WINDOW // tq, 0) * tq, tq)
    dpos = qpos - (w0 + lax.broadcasted_iota(jnp.int32, (wk, 1), 0))
    set_bias((dpos >= 0) & (dpos < WINDOW), 0)
    scores(kwt_ref[0, :, pl.ds(w0, wk)], q, 0)

    reset()
    block_bias = jnp.where(selt_ref[0, 0].astype(F32).T > 0.5, 0.0, NEG_INF).astype(BF16)
    q_sel = jnp.concatenate([q, jnp.concatenate([block_bias] * r, axis=0)], axis=1)
    blk = lax.broadcasted_iota(jnp.int32, (nsel, tk), 0)
    krow = lax.broadcasted_iota(jnp.int32, (nsel, tk), 1)

    def slc_scores(k0, row0):
        code = jnp.where(lax.shift_right_logical(k0 + krow, CMP_BLOCK.bit_length() - 1) == blk, 1.0, 0.0)
        scores(jnp.concatenate([kst_ref[0, :, pl.ds(k0, tk)], code], axis=0), q_sel, row0)

    def slc_fold(k0, row0, biased):
        fold(row0, tk, vst_ref[0, :, pl.ds(k0, tk)], biased)

    def slc_pair(c, carry):
        ka = pl.multiple_of(2 * c * tk, tk)
        slc_scores(ka + tk, wk + tk)
        slc_fold(ka, wk, False)
        slc_scores(ka + 2 * tk, wk)
        slc_fold(ka + tk, wk + tk, False)
        return carry

    n_below = (qi * tq) // tk
    kd = pl.multiple_of(n_below * tk, tk)
    odd = n_below % 2
    slc_scores(0, wk)
    lax.fori_loop(0, n_below // 2, slc_pair, 0)

    @pl.when(odd == 1)
    def _():
        slc_scores(kd, wk + tk)
        slc_fold(kd - tk, wk, False)

    row_d = pl.multiple_of(wk + odd * tk, ck)
    set_bias(kd + lax.broadcasted_iota(jnp.int32, (tk, 1), 0) <= qpos, row_d)
    slc_fold(kd, row_d, True)
    o_slc = result()

    reset()
    fold(0, wk, vwt_ref[0, :, pl.ds(w0, wk)], True)
    o_win = result()

    gates_t = jax.nn.sigmoid(sm_ref[0]).T
    g0 = 2 * DN_HEADS
    per = 3 * r
    gt = jnp.where(grp == 0, gates_t[g0:g0 + per], gates_t[g0 + per:g0 + 2 * per])
    outs = []
    for h in range(r):
        outs.append(gt[3 * h:3 * h + 1] * ocmp_ref[0, 0, h * dh:(h + 1) * dh, :]
                    + gt[3 * h + 1:3 * h + 2] * o_slc[:, h * tq:(h + 1) * tq]
                    + gt[3 * h + 2:3 * h + 3] * o_win[:, h * tq:(h + 1) * tq])
    o_ref[0] = jnp.concatenate(outs, axis=0).T


def nsa_attention(proj3, kvt, kvt_win, selt, ocmpt, tq, tk):
    b, sq, _ = proj3.shape
    t = kvt.shape[2]
    g, nsel = selt.shape[1], selt.shape[2]
    r, dh = NSA_GROUP, HEAD_DIM
    gw = r * dh
    assert sq == t and sq % tq == 0 and t % tk == 0 and tk % tq == 0 and WINDOW % tq == 0 and tq % LANES == 0
    assert t >= WINDOW + tq and g == NSA_KV_HEADS and g == 2
    ck = 32
    assert tk % ck == 0 and (WINDOW + tq) % ck == 0 and tk <= WINDOW + tq
    rows = WINDOW + tq + 2 * tk

    def kv_spec(kind):
        return pl.BlockSpec((1, dh, t), lambda i, j, k: (i, kind * g + j, 0))

    return pl.pallas_call(
        functools.partial(_nsa_attn_kernel, tq=tq, tk=tk, nsel=nsel, ck=ck),
        grid=(b, g, sq // tq),
        in_specs=[pl.BlockSpec((1, tq, gw), lambda i, j, k: (i, k, P_NQ // gw + j)),
                  kv_spec(2), kv_spec(3), kv_spec(0), kv_spec(1),
                  pl.BlockSpec((1, 1, nsel, tq), lambda i, j, k: (i, j, 0, k)),
                  pl.BlockSpec((1, 1, gw, tq), lambda i, j, k: (i, j, 0, k)),
                  pl.BlockSpec((1, tq, LANES), lambda i, j, k: (i, k, P_SMALL // LANES))],
        out_specs=pl.BlockSpec((1, tq, gw), lambda i, j, k: (i, k, j)),
        out_shape=jax.ShapeDtypeStruct((b, sq, g * gw), F32),
        scratch_shapes=[pltpu.VMEM((1, r * tq), F32), pltpu.VMEM((dh + 8, r * tq), F32),
                        pltpu.VMEM((rows, r * tq), F32), pltpu.VMEM((rows, tq), F32),
                        pltpu.VMEM((WINDOW + tq, r * tq), BF16)],
        compiler_params=_cp(("parallel", "parallel", "parallel")),
        name="nsa_attention",
    )(proj3, kvt, kvt, kvt_win, kvt_win, selt, ocmpt, proj3)


def _bdot(a, b, passes=1):
    dims = (((2,), (1,)), ((0,), (0,)))
    if passes == 1:
        return lax.dot_general(a.astype(BF16), b.astype(BF16), dims, preferred_element_type=F32)
    ah, al = _split2(a)
    bh, bl = _split2(b)
    return (lax.dot_general(ah, bh, dims, preferred_element_type=F32)
            + lax.dot_general(ah, bl, dims, preferred_element_type=F32)
            + lax.dot_general(al, bh, dims, preferred_element_type=F32))


def _bdot_nt(a, b):
    return lax.dot_general(a.astype(BF16), b.astype(BF16), (((2,), (2,)), ((0,), (0,))),
                           preferred_element_type=F32)


def _deltanet_kernel(qkv_ref, z_ref, sm_ref, cw_ref, alog_ref, dtb_ref, gn_ref, o_ref, s_out_ref,
                     xbuf, s_sc):
    c = pl.program_id(0)
    ch = DN_CHUNK
    n_pairs = DN_HEADS // 2
    two = 2 * ch
    n_batch = qkv_ref.shape[0]

    @pl.when(c == 0)
    def _():
        xbuf[:, 0:8, :] = jnp.zeros((n_batch, 8, DN_CONV_DIM), F32)
        s_sc[...] = jnp.zeros_like(s_sc)

    ti = lax.broadcasted_iota(jnp.int32, (ch, ch), 0)
    tj = lax.broadcasted_iota(jnp.int32, (ch, ch), 1)
    tri = jnp.where(ti >= tj, 1.0, 0.0)
    lane = lax.broadcasted_iota(jnp.int32, (ch, LANES), 1)
    lo = lane < HEAD_DIM
    row2 = lax.broadcasted_iota(jnp.int32, (two, two), 0)
    col2 = lax.broadcasted_iota(jnp.int32, (two, two), 1)
    same = (row2 >= ch) == (col2 >= ch)
    incl = (same & (row2 >= col2))[None]
    strict = (same & (row2 > col2))[None]
    top = lax.broadcasted_iota(jnp.int32, (two, 1), 0) < ch

    def seg_sum(x):
        s_lo = jnp.sum(jnp.where(lo, x, 0.0), axis=-1, keepdims=True)
        s_hi = jnp.sum(jnp.where(lo, 0.0, x), axis=-1, keepdims=True)
        return jnp.where(lo, s_lo, s_hi)

    def stack2(x):
        return jnp.concatenate([jnp.where(lo, x, 0.0), jnp.where(lo, 0.0, x)], axis=0)

    def col2x(a, b):
        return jnp.concatenate([jnp.broadcast_to(a, (ch, LANES)), jnp.broadcast_to(b, (ch, LANES))], axis=0)

    q_l, k_l, v_l, beta_l, gc_l, gl_l = [], [], [], [], [], []
    for bi in range(n_batch):
        xbuf[bi, 8:8 + ch, :] = qkv_ref[bi]
        conv = None
        for w in range(CONV_WIDTH):
            term = xbuf[bi, 5 + w:5 + w + ch, :] * cw_ref[w:w + 1, :]
            conv = term if conv is None else conv + term
        xbuf[bi, 0:8, :] = xbuf[bi, ch:ch + 8, :]
        act = _silu(conv)
        sm = sm_ref[bi]
        beta_all = jax.nn.sigmoid(sm)
        g_all = -jnp.exp(alog_ref[...]) * _softplus(sm + dtb_ref[...])
        gcum_all = _dot_exact_lhs01(tri, g_all)
        for p in range(n_pairs):
            c0 = p * LANES
            qp = act[:, c0:c0 + LANES]
            kp = act[:, DN_WIDTH + c0:DN_WIDTH + c0 + LANES]
            vp = act[:, 2 * DN_WIDTH + c0:2 * DN_WIDTH + c0 + LANES]
            qp = qp * lax.rsqrt(seg_sum(qp * qp) + 1e-6) * (HEAD_DIM ** -0.5)
            kp = kp * lax.rsqrt(seg_sum(kp * kp) + 1e-6)
            h0, h1 = DN_HEADS + 2 * p, DN_HEADS + 2 * p + 1
            q_l.append(stack2(qp))
            k_l.append(stack2(kp))
            v_l.append(stack2(vp))
            beta_l.append(col2x(beta_all[:, 2 * p:2 * p + 1], beta_all[:, 2 * p + 1:2 * p + 2]))
            gc_l.append(col2x(gcum_all[:, h0:h0 + 1], gcum_all[:, h1:h1 + 1]))
            gl_l.append(jnp.broadcast_to(jnp.where(top, gcum_all[ch - 1:ch, h0:h0 + 1], gcum_all[ch - 1:ch, h1:h1 + 1]),
                                         (two, LANES)))
    q2, k2, v2 = jnp.stack(q_l), jnp.stack(k_l), jnp.stack(v_l)
    beta2, gc2, gl2 = jnp.stack(beta_l), jnp.stack(gc_l), jnp.stack(gl_l)
    decay = jnp.exp(jnp.where(incl, gc2 - jnp.swapaxes(gc2, 1, 2), NEG_INF))
    kb2 = k2 * beta2
    a_mat = jnp.where(strict, _bdot_nt(kb2, k2) * decay, 0.0)
    aqk = jnp.where(incl, _bdot_nt(q2, k2) * decay, 0.0)
    s_old = s_sc[...]
    egc = jnp.exp(gc2)
    x = beta2 * (v2 - egc * _bdot(k2, s_old))
    pw = -a_mat
    n_lvl = ch.bit_length() - 1
    for lvl in range(n_lvl):
        x = x + _bdot(pw, x, DN_APPLY_PASSES[lvl])
        if lvl + 1 < n_lvl:
            pw = _bdot(pw, pw, DN_SQUARE_PASSES[lvl])
    o2 = _bdot(q2 * egc, s_old) + _bdot(aqk, x)
    kdec = k2 * jnp.exp(gl2 - gc2)
    s_sc[...] = s_old * jnp.exp(gl2) + _bdot(jnp.swapaxes(kdec, 1, 2), x)
    for bi in range(n_batch):
        for p in range(n_pairs):
            c0 = p * LANES
            o_n = o2[bi * n_pairs + p]
            o_pair = o_n[0:ch] + o_n[ch:two]
            inv = lax.rsqrt(seg_sum(o_pair * o_pair) * (1.0 / HEAD_DIM) + RMS_EPS)
            o_ref[bi, :, c0:c0 + LANES] = o_pair * inv * gn_ref[...] * _silu(z_ref[bi, :, c0:c0 + LANES])

    @pl.when(c == pl.num_programs(0) - 1)
    def _():
        s_out_ref[...] = s_sc[...]


def deltanet_prompt(proj3, conv_w, a_log, dt_bias, norm_gain):
    b, t, _ = proj3.shape
    ch = DN_CHUNK
    assert t % ch == 0
    pad = jnp.zeros((LANES - 2 * DN_HEADS,), F32)
    alog_row = jnp.concatenate([jnp.zeros((DN_HEADS,), F32), a_log, pad]).reshape(1, LANES)
    dtb_row = jnp.concatenate([jnp.zeros((DN_HEADS,), F32), dt_bias, pad]).reshape(1, LANES)
    gn_row = jnp.concatenate([norm_gain, norm_gain]).reshape(1, LANES)
    n_pairs = DN_HEADS // 2
    o, s_fin = pl.pallas_call(
        _deltanet_kernel,
        grid=(t // ch,),
        in_specs=[pl.BlockSpec((b, ch, DN_CONV_DIM), lambda j: (0, j, P_QKV // DN_CONV_DIM)),
                  pl.BlockSpec((b, ch, DN_WIDTH), lambda j: (0, j, P_Z // DN_WIDTH)),
                  pl.BlockSpec((b, ch, LANES), lambda j: (0, j, P_SMALL // LANES)),
                  pl.BlockSpec((CONV_WIDTH, DN_CONV_DIM), lambda j: (0, 0)),
                  pl.BlockSpec((1, LANES), lambda j: (0, 0)),
                  pl.BlockSpec((1, LANES), lambda j: (0, 0)),
                  pl.BlockSpec((1, LANES), lambda j: (0, 0))],
        out_specs=[pl.BlockSpec((b, ch, DN_WIDTH), lambda j: (0, j, 0)),
                   pl.BlockSpec((b * n_pairs, 2 * ch, LANES), lambda j: (0, 0, 0))],
        out_shape=[jax.ShapeDtypeStruct((b, t, DN_WIDTH), F32),
                   jax.ShapeDtypeStruct((b * n_pairs, 2 * ch, LANES), F32)],
        scratch_shapes=[pltpu.VMEM((b, ch + 8, DN_CONV_DIM), F32), pltpu.VMEM((b * n_pairs, 2 * ch, LANES), F32)],
        compiler_params=_cp(("arbitrary",)),
        name="deltanet_prompt",
    )(proj3, proj3, proj3, conv_w, alog_row, dtb_row, gn_row)
    return o, s_fin.reshape(b, n_pairs, 2 * ch, LANES)


def _pairs_to_heads(s_pairs):
    d = HEAD_DIM
    return jnp.stack([s_pairs[:, :, :d, :d], s_pairs[:, :, d:, d:]], axis=2).reshape(
        s_pairs.shape[0], DN_HEADS, d, d)


def _dn_step_prep_kernel(qkv_ref, cs_ref, sm_ref, cw_ref, alog_ref, dtb_ref, ones_ref,
                         q_ref, k_ref, v_ref, sc_ref):
    conv = qkv_ref[...] * cw_ref[CONV_WIDTH - 1:CONV_WIDTH, :]
    for w in range(CONV_WIDTH - 1):
        conv = conv + cs_ref[w] * cw_ref[w:w + 1, :]
    act = _silu(conv)
    q = act[:, 0:DN_WIDTH]
    k = act[:, DN_WIDTH:2 * DN_WIDTH]

    def seg_sum(x):
        return _dot_exact_rhs01(x, ones_ref[...])

    q_ref[...] = q * lax.rsqrt(seg_sum(q * q) + 1e-6) * (HEAD_DIM ** -0.5)
    k_ref[...] = k * lax.rsqrt(seg_sum(k * k) + 1e-6)
    v_ref[...] = act[:, 2 * DN_WIDTH:]
    sm = sm_ref[...]
    g = -jnp.exp(alog_ref[...]) * _softplus(sm + dtb_ref[...])
    lane = lax.broadcasted_iota(jnp.int32, sm.shape, 1)
    sc_ref[...] = jnp.where(lane < DN_HEADS, jax.nn.sigmoid(sm), jnp.exp(g))


def _dn_step_kernel(k_ref, q_ref, v_ref, be_ref, eg_ref, z_ref, gn_ref, s_ref, o_ref, s_out_ref):
    s_old = s_ref[0]
    k, q, v = k_ref[0], q_ref[0], v_ref[0]
    beta, eg = be_ref[0], eg_ref[0]
    ks = jnp.sum(k[:, None, :] * s_old, axis=0)
    qs = jnp.sum(q[:, None, :] * s_old, axis=0)
    qk = jnp.sum(q * k, axis=0, keepdims=True)
    v_new = beta * (v - eg * ks)
    o = eg * qs + qk * v_new
    inv = lax.rsqrt(jnp.mean(o * o, axis=0, keepdims=True) + RMS_EPS)
    o_ref[0] = o * inv * gn_ref[...] * _silu(z_ref[0])
    s_out_ref[0] = s_old * eg[None] + k[:, None, :] * v_new[None]


def deltanet_sample(proj_s, conv_state, rec_state, conv_w, a_log, dt_bias, norm_gain):
    n = proj_s.shape[0]
    hds, d = DN_HEADS, HEAD_DIM
    pad = jnp.zeros((LANES - 2 * hds,), F32)
    alog_row = jnp.concatenate([jnp.zeros((hds,), F32), a_log, pad]).reshape(1, LANES)
    dtb_row = jnp.concatenate([jnp.zeros((hds,), F32), dt_bias, pad]).reshape(1, LANES)
    head_of = jnp.arange(DN_WIDTH) // d
    ones_bd = (head_of[:, None] == head_of[None, :]).astype(BF16)
    cs = jnp.transpose(conv_state, (1, 0, 2))
    full = lambda shape: pl.BlockSpec(shape, lambda i: (0,) * len(shape))
    q, k, v, sc = pl.pallas_call(
        _dn_step_prep_kernel,
        grid=(1,),
        in_specs=[pl.BlockSpec((n, DN_CONV_DIM), lambda i: (0, P_QKV // DN_CONV_DIM)),
                  full((CONV_WIDTH - 1, n, DN_CONV_DIM)),
                  pl.BlockSpec((n, LANES), lambda i: (0, P_SMALL // LANES)),
                  full((CONV_WIDTH, DN_CONV_DIM)), full((1, LANES)), full((1, LANES)),
                  full((DN_WIDTH, DN_WIDTH))],
        out_specs=[full((n, DN_WIDTH)), full((n, DN_WIDTH)), full((n, DN_WIDTH)), full((n, LANES))],
        out_shape=[jax.ShapeDtypeStruct((n, DN_WIDTH), F32)] * 3 + [jax.ShapeDtypeStruct((n, LANES), F32)],
        compiler_params=_cp(("arbitrary",)),
        name="dn_step_prep",
    )(proj_s, cs, proj_s, conv_w, alog_row, dtb_row, ones_bd)
    t3 = lambda a: a.T.reshape(hds, d, n)
    sct = sc[:, :2 * hds].T.reshape(2, hds, 1, n)
    head_vec = pl.BlockSpec((1, d, n), lambda i: (i, 0, 0))
    head_scl = pl.BlockSpec((1, 1, n), lambda i: (i, 0, 0))
    state = pl.BlockSpec((1, d, d, n), lambda i: (i, 0, 0, 0))
    o, s_new = pl.pallas_call(
        _dn_step_kernel,
        grid=(hds,),
        in_specs=[head_vec, head_vec, head_vec, head_scl, head_scl, head_vec,
                  pl.BlockSpec((d, 1), lambda i: (0, 0)), state],
        out_specs=[head_vec, state],
        out_shape=[jax.ShapeDtypeStruct((hds, d, n), F32), jax.ShapeDtypeStruct((hds, d, d, n), F32)],
        compiler_params=_cp(("parallel",)),
        name="dn_step",
    )(t3(k), t3(q), t3(v), sct[0], sct[1], t3(proj_s[:, P_Z:P_Z + DN_WIDTH]), norm_gain.reshape(d, 1),
      jnp.transpose(rec_state, (1, 2, 3, 0)))
    return o.reshape(DN_WIDTH, n).T, jnp.transpose(s_new, (3, 0, 1, 2))


def _nsa_decode_kernel(pt_ref, q_ref, new_ref, gate_ref, win_ref, newc_ref, exp_ref, *refs, n_pages, seqs):
    del pt_ref
    o_ref, swin_ref, kc_sc = refs[2 * seqs * n_pages:2 * seqs * n_pages + 3]
    dh, r, g2, nh = HEAD_DIM, NSA_GROUP, NSA_KV_HEADS, NSA_HEADS
    past = n_pages * PAGE_SIZE
    nb = past // CMP_BLOCK
    nsel = nb + 1
    wlen = win_ref.shape[4]
    for u in range(seqs):
        for j in range(n_pages):
            kc_sc[u, j:j + 1, :] = refs[u * n_pages + j][0]
    kcv = kc_sc[...]
    new = new_ref[...]
    q8 = q_ref[...] * ATTN_SCALE
    head = lax.broadcasted_iota(jnp.int32, (1, nh, 1), 1)
    lane = lax.broadcasted_iota(jnp.int32, (1, 1, LANES), 2)
    g0 = head < r
    blk = jnp.where(lane < n_pages, 2 * lane, jnp.where(lane < nb, 2 * (lane - n_pages) + 1, lane))
    wp = lax.broadcasted_iota(jnp.int32, (1, 1, wlen + LANES), 2)
    dpos = wlen - wp
    wmask = (dpos >= 0) & (dpos < WINDOW) & (past - wlen + wp >= 0)

    def both(x):
        return jnp.where(g0, x[:, :, 0:dh], x[:, :, dh:2 * dh])

    def new_part(kind):
        return both(jnp.broadcast_to(new[:, :, 2 * kind * dh:2 * (kind + 1) * dh], (seqs, nh, 2 * dh)))

    def pages(kind):
        return jnp.stack([jnp.concatenate([refs[(seqs + u) * n_pages + j][0, kind].reshape(2 * dh, PAGE_SIZE)
                                           for j in range(n_pages)], axis=1) for u in range(seqs)])

    q_bd = jnp.concatenate([jnp.where(g0, q8, 0.0), jnp.where(g0, 0.0, q8)], axis=2)
    s_new = jnp.sum(q8 * new_part(2), axis=-1, keepdims=True)
    s_all = jnp.concatenate([_bdot(q_bd, pages(0)), jnp.broadcast_to(s_new, (seqs, nh, LANES))], axis=2)
    sw = _bdot(q_bd, win_ref[:, 0].reshape(seqs, 2 * dh, wlen))
    sw_new = jnp.sum(q8 * new_part(4), axis=-1, keepdims=True)

    def cmp_rows(base):
        even = jnp.concatenate([kcv[:, :, base:base + dh], kcv[:, :, base + LANES:base + LANES + dh]], axis=2)
        odd = jnp.concatenate([kcv[:, :, base + dh:base + LANES], kcv[:, :, base + LANES + dh:base + 2 * LANES]], axis=2)
        return jnp.concatenate([even, odd], axis=1)

    s = _bdot_nt(q_bd, cmp_rows(0))
    p = jnp.exp(s - jnp.max(s, axis=-1, keepdims=True))
    p = p / jnp.maximum(jnp.sum(p, axis=-1, keepdims=True), 1e-30)
    o_cmp = both(_bdot(p, cmp_rows(g2 * LANES)))
    bid_row = jnp.broadcast_to(blk.astype(F32), (1, LANES, LANES))
    bid_col = jnp.swapaxes(bid_row, 1, 2)
    valid = lane < nsel
    forced = valid & ((blk == 0) | (nb - blk < N_LOCAL_BLOCKS))
    sels = []
    for g in range(g2):
        in_g = (head >= g * r) & (head < (g + 1) * r)
        imp = jnp.sum(jnp.where(in_g, p, 0.0), axis=1, keepdims=True)
        imp = jnp.concatenate([imp, jnp.zeros((seqs, 1, LANES - nb), F32)], axis=2)
        score = jnp.where(valid, imp + jnp.where(forced, FORCE_SCORE, 0.0), -1.0)
        sc_row = jnp.broadcast_to(score, (seqs, LANES, LANES))
        sc_col = jnp.swapaxes(sc_row, 1, 2)
        beats = (sc_col > sc_row) | ((sc_col == sc_row) & (bid_col < bid_row))
        rank = jnp.sum(jnp.where(beats, 1.0, 0.0), axis=1, keepdims=True)
        sels.append(jnp.where(valid & (rank < min(TOP_N, nsel)), 1.0, 0.0))
    sel8 = jnp.where(g0, sels[0], sels[1]).astype(BF16)
    chosen = jnp.dot(sel8.reshape(seqs * nh, LANES), exp_ref[...], preferred_element_type=F32) > 0.5
    pm = _masked_softmax(s_all, chosen.reshape(seqs, nh, past + LANES))
    o_slc = pm[:, :, past:past + 1] * new_part(3) + both(_bdot_nt(pm[:, :, 0:past], pages(1)))
    pw = _masked_softmax(jnp.concatenate([sw, jnp.broadcast_to(sw_new, (seqs, nh, LANES))], axis=2), wmask)
    o_win = pw[:, :, wlen:wlen + 1] * new_part(5) + both(
        _bdot_nt(pw[:, :, 0:wlen], win_ref[:, 1].reshape(seqs, 2 * dh, wlen)))
    gates = jnp.broadcast_to(jax.nn.sigmoid(gate_ref[...]), (seqs, nh, LANES))

    def gate(branch):
        return jnp.sum(jnp.where(lane == 3 * head + branch, gates, 0.0), axis=-1, keepdims=True)

    o_ref[...] = gate(0) * o_cmp + gate(1) * o_slc + gate(2) * o_win
    wpos = lax.broadcasted_iota(jnp.int32, (1, 1, 1, wlen), 3)
    nlane = newc_ref.shape[1]
    slane = lax.broadcasted_iota(jnp.int32, (1, nlane), 1)
    for u in range(seqs):
        mine = slane == (pl.program_id(0) * seqs + u) % nlane
        col = jnp.sum(jnp.where(mine, newc_ref[...], 0.0), axis=1, keepdims=True)
        swin_ref[u] = jnp.where(wpos == wlen - 1, col.reshape(2, g2, dh, 1), pltpu.roll(win_ref[u], wlen - 1, 3))


def nsa_decode(page_table, q3, new_row, gate_row, win_t, new_win, kc_phys, cache_t, seqs=8):
    n, n_pages = page_table.shape
    dh = HEAD_DIM
    g2 = NSA_KV_HEADS
    wlen = win_t.shape[4]
    assert 2 * n_pages + 1 <= LANES and PAGE_SIZE == 2 * CMP_BLOCK and PAGE_SIZE == LANES and g2 == 2

    past = n_pages * PAGE_SIZE
    nb = past // CMP_BLOCK
    erow = lax.broadcasted_iota(jnp.int32, (LANES, past + LANES), 0)
    ecol = lax.broadcasted_iota(jnp.int32, (LANES, past + LANES), 1)
    page, second = ecol // PAGE_SIZE, (ecol % PAGE_SIZE) >= CMP_BLOCK
    expand = (((erow < n_pages) & (page == erow) & ~second & (ecol < past))
              | ((erow >= n_pages) & (erow < nb) & (page == erow - n_pages) & second & (ecol < past))
              | ((erow == nb) & (ecol == past))).astype(BF16)

    assert n % seqs == 0
    nlane = LANES if n % LANES == 0 else n
    assert nlane % seqs == 0

    def kc_map(u, j):
        return lambda i, pt: (pt[seqs * i + u, j], 0, 0)

    def slc_map(u, j):
        return lambda i, pt: (pt[seqs * i + u, j], 1, 0, 0, 0)

    in_specs = [pl.BlockSpec((seqs, NSA_HEADS, dh), lambda i, pt: (i, 0, 0)),
                pl.BlockSpec((seqs, 1, KV_COLS), lambda i, pt: (i, 0, 0)),
                pl.BlockSpec((seqs, 1, LANES), lambda i, pt: (i, 0, 0)),
                pl.BlockSpec((seqs, 2, g2, dh, wlen), lambda i, pt: (i, 0, 0, 0, 0)),
                pl.BlockSpec((2 * g2 * dh, nlane), lambda i, pt: (0, seqs * i // nlane)),
                pl.BlockSpec((LANES, past + LANES), lambda i, pt: (0, 0))]
    in_specs += [pl.BlockSpec((1, 1, 4 * LANES), kc_map(u, j)) for u in range(seqs) for j in range(n_pages)]
    in_specs += [pl.BlockSpec((1, 2, g2, dh, PAGE_SIZE), slc_map(u, j)) for u in range(seqs) for j in range(n_pages)]
    grid_spec = pltpu.PrefetchScalarGridSpec(
        num_scalar_prefetch=1, grid=(n // seqs,), in_specs=in_specs,
        out_specs=[pl.BlockSpec((seqs, NSA_HEADS, dh), lambda i, pt: (i, 0, 0)),
                   pl.BlockSpec((seqs, 2, g2, dh, wlen), lambda i, pt: (i, 0, 0, 0, 0))],
        scratch_shapes=[pltpu.VMEM((seqs, n_pages, 4 * LANES), F32)])
    return pl.pallas_call(
        functools.partial(_nsa_decode_kernel, n_pages=n_pages, seqs=seqs),
        grid_spec=grid_spec,
        out_shape=[jax.ShapeDtypeStruct((n, NSA_HEADS, dh), F32), jax.ShapeDtypeStruct(win_t.shape, F32)],
        compiler_params=_cp(("arbitrary",)),
        name="nsa_decode",
    )(page_table, q3, new_row, gate_row, win_t, new_win, expand,
      *([kc_phys] * (seqs * n_pages)), *([cache_t] * (seqs * n_pages)))


def _compress_pages_kernel(x_ref, wd_ref, ped_ref, b1_ref, w2_ref, o_ref):
    xt = jnp.swapaxes((x_ref[...] + ped_ref[0][None]).astype(BF16), 0, 1)
    acc = None
    for dp in range(xt.shape[0] // 2):
        d = jnp.dot(jnp.concatenate([xt[2 * dp], xt[2 * dp + 1]], axis=1), wd_ref[0, dp], preferred_element_type=F32)
        acc = d if acc is None else acc + d
    h = jnp.maximum(acc + b1_ref[0], 0.0)
    o_ref[:, 0, :] = jnp.dot(h.astype(BF16), w2_ref[0], preferred_element_type=F32)


def compress_pages(cache_t, cwp, bp):
    n_phys = cache_t.shape[0]
    g2, dh = NSA_KV_HEADS, HEAD_DIM
    assert n_phys % bp == 0 and dh % 2 == 0
    wd, ped, b1h, w2h = cwp
    rows = cache_t.reshape(n_phys, cache_t.shape[1] * g2 * dh, PAGE_SIZE)
    return pl.pallas_call(
        _compress_pages_kernel,
        grid=(2 * g2, n_phys // bp),
        in_specs=[pl.BlockSpec((bp, dh, PAGE_SIZE), lambda j, i: (i, j, 0)),
                  pl.BlockSpec((1, dh // 2, 2 * PAGE_SIZE, 2 * CMP_HIDDEN), lambda j, i: (j // g2, 0, 0, 0)),
                  pl.BlockSpec((1, dh, PAGE_SIZE), lambda j, i: (j // g2, 0, 0)),
                  pl.BlockSpec((1, 1, 2 * CMP_HIDDEN), lambda j, i: (j // g2, 0, 0)),
                  pl.BlockSpec((1, 2 * CMP_HIDDEN, LANES), lambda j, i: (j // g2, 0, 0))],
        out_specs=pl.BlockSpec((bp, 1, LANES), lambda j, i: (i, 0, j)),
        out_shape=jax.ShapeDtypeStruct((n_phys, 1, 2 * g2 * LANES), F32),
        compiler_params=_cp(("parallel", "parallel")),
        name="compress_pages",
    )(rows, wd, ped, b1h.reshape(2, 1, 2 * CMP_HIDDEN), w2h)


def _compress_page_weights(cmp_pe, cmp_w1, cmp_b1, cmp_w2):
    w1t = jnp.transpose(cmp_w1.reshape(2, CMP_BLOCK, HEAD_DIM, CMP_HIDDEN), (0, 2, 1, 3))
    z = jnp.zeros_like(w1t)
    wd = jnp.concatenate([jnp.concatenate([w1t, z], -1), jnp.concatenate([z, w1t], -1)], axis=2).astype(BF16)
    wd = wd.reshape(2, HEAD_DIM // 2, 2 * PAGE_SIZE, 2 * CMP_HIDDEN)
    pet = jnp.transpose(cmp_pe, (0, 2, 1))
    ped = jnp.concatenate([pet, pet], -1)
    b1h = jnp.concatenate([cmp_b1, cmp_b1], -1)
    z2 = jnp.zeros_like(cmp_w2)
    w2h = jnp.concatenate([jnp.concatenate([cmp_w2, z2], -1), jnp.concatenate([z2, cmp_w2], -1)], axis=1).astype(BF16)
    return wd, ped, b1h, w2h


def _split_w_in(w):
    d = w.shape[0]
    off_b = DN_CONV_DIM + DN_WIDTH
    off_q = off_b + 2 * DN_HEADS
    off_kv = off_q + NSA_WIDTH
    off_g = off_kv + KV_COLS
    n_g = 3 * NSA_HEADS
    pad = jnp.zeros((d, P_COLS - P_SMALL - 2 * DN_HEADS - n_g), w.dtype)
    main = jnp.concatenate([w[:, :off_b], w[:, off_q:off_kv], w[:, off_kv:off_kv + CMP_COLS], w[:, off_b:off_q],
                            w[:, off_g:off_g + n_g], pad], axis=1)
    return main.astype(BF16), w[:, off_kv:off_g].T.astype(BF16)


def _row_tile(n, cap):
    t = min(n, cap)
    while n % t:
        t //= 2
    return t


def _trunk_tail(x2, mixer_dn, mixer_nsa, mem_kv3, bshape, lw, final_norm):
    n, d = x2.shape
    b, t = bshape
    if t >= SUBLANES:
        x3 = mem_block(x2.reshape(b, t, d), mixer_dn.reshape(b, t, -1), mixer_nsa.reshape(b, t, -1),
                       lw["w_out_dn"], lw["w_out_nsa"], lw["ln_mem"], lw["w_mem_q"], mem_kv3, lw["w_mem_o"],
                       _row_tile(t, 512))
        x2 = x3.reshape(n, d)
    else:
        assert t == 1
        tm = _row_tile(n, 512)
        x2 = matmul_residual(x2, [mixer_dn, mixer_nsa], [lw["w_out_dn"], lw["w_out_nsa"]], tm)
        qm = rms_matmul(x2, lw["ln_mem"], lw["w_mem_q"], tm, 512)
        att = mem_attention_row(qm.reshape(n, MEM_HEADS, d // MEM_HEADS), mem_kv3).reshape(n, d)
        x2 = matmul_residual(x2, [att], [lw["w_mem_o"]], tm)
    return ffn(x2, lw["ln_ffn"], lw["w_up"], lw["w_down"], lw["ln_final"], final_norm, _row_tile(n, 1024), 512)


def _kv_rows(kvt):
    b, rows, s = kvt.shape
    g, dh = NSA_KV_HEADS, HEAD_DIM
    return jnp.transpose(kvt.reshape(b, rows // (g * dh), g, dh, s), (0, 4, 1, 2, 3))


def _prompt_layer(xp, mem_prompt, lw, cw, final_norm):
    b, s, d = xp.shape
    n = b * s
    x2 = xp.reshape(n, d)
    proj, kvt, kvt_win = input_projection(xp, lw["ln_mix"], lw["w_in"], lw["w_kvt"], _row_tile(s, 1024), 1024)
    proj3 = proj.reshape(b, s, P_COLS)
    dn_out, s_pairs = deltanet_prompt(proj3, lw["dn_conv_w"], lw["dn_a_log"], lw["dn_dt_bias"], lw["dn_norm"])
    p_conv = proj3[:, s - (CONV_WIDTH - 1):, P_QKV:P_QKV + DN_CONV_DIM]
    p_rec = _pairs_to_heads(s_pairs)
    nb = s // CMP_BLOCK
    kcv = compress_blocks(proj3[:, :nb * CMP_BLOCK].reshape(b * nb, CMP_BLOCK, P_COLS), P_CMP, cw,
                          _row_tile(b * nb, 256))
    kcv = kcv.reshape(b, nb, 2, NSA_KV_HEADS, HEAD_DIM)
    kc = jnp.transpose(kcv[:, :, 0], (0, 2, 1, 3))
    vct = jnp.transpose(kcv[:, :, 1], (0, 2, 3, 1))
    ocmpt, selt = nsa_cmp_topk(proj3, kc, vct, _row_tile(s, 512))
    nsa_out = nsa_attention(proj3, kvt, kvt_win, selt, ocmpt, 256, 512).reshape(n, NSA_WIDTH)
    m = mem_prompt.shape[1]
    mem_kv = rms_matmul(mem_prompt.reshape(b * m, d), lw["ln_memkv"], lw["w_mem_kv"], _row_tile(b * m, 512), 512)
    mem_kv5 = mem_kv.reshape(b, m, 2, MEM_HEADS, d // MEM_HEADS)
    y = _trunk_tail(x2, dn_out.reshape(n, DN_WIDTH), nsa_out, mem_kv.reshape(b, m, 2 * d), (b, s), lw, final_norm)
    wk = min(WINDOW, s)
    return y.reshape(b, s, d), _kv_rows(kvt), _kv_rows(kvt_win[:, :, s - wk:]), mem_kv5, p_conv, p_rec


def _sample_layer(xs, cache_nsa, cache_win, cache_mem, conv_state, rec_state, page_table, lw, cwp, final_norm):
    db, ds, d = xs.shape
    assert ds == 1
    n = db
    x2 = xs.reshape(n, d)
    proj, kvt, kvt_win = input_projection(x2[None], lw["ln_mix"], lw["w_in"], lw["w_kvt"], n, 1024)
    dn_out, s_rec = deltanet_sample(proj, conv_state, rec_state, lw["dn_conv_w"], lw["dn_a_log"], lw["dn_dt_bias"],
                                    lw["dn_norm"])
    s_conv = jnp.concatenate([conv_state[:, 1:], proj[:, None, P_QKV:P_QKV + DN_CONV_DIM]], axis=1)
    n_phys = cache_nsa.shape[0]
    cache_t = jnp.transpose(cache_nsa, (0, 2, 3, 4, 1))
    win_t = jnp.transpose(cache_win, (0, 2, 3, 4, 1))
    kc_phys = compress_pages(cache_t, cwp, _row_tile(n_phys, 128))
    kv_new = jnp.concatenate([kvt[0], kvt_win[0]], axis=0).T
    o8, s_win_t = nsa_decode(page_table, proj[:, P_NQ:P_NQ + NSA_WIDTH].reshape(n, NSA_HEADS, HEAD_DIM),
                             kv_new.reshape(n, 1, KV_COLS),
                             _flat_gates(proj[:, P_SMALL:P_SMALL + LANES]).reshape(n, 1, LANES),
                             win_t, kvt_win[0], kc_phys, cache_t)
    s_nsa = jnp.transpose(kvt.reshape(-1, NSA_KV_HEADS, HEAD_DIM, n), (3, 0, 1, 2))[:, None]
    s_win = jnp.transpose(s_win_t, (0, 4, 1, 2, 3))
    y = _trunk_tail(x2, dn_out, o8.reshape(n, NSA_WIDTH), cache_mem, (n, 1), lw, final_norm)
    return y.reshape(db, ds, d), s_nsa, s_win, s_conv, s_rec


def _flat_gates(small):
    g0 = 2 * DN_HEADS
    n_g = 3 * NSA_HEADS
    gl = small[..., g0:g0 + n_g]
    return jnp.concatenate([gl, jnp.zeros(gl.shape[:-1] + (LANES - n_g,), gl.dtype)], axis=-1)


def kernel(x_prompt, x_sample, mem_prompt, cache_nsa_kv, cache_win_kv, cache_mem_kv, state_dn_conv, state_dn_rec, page_table, ln_mix, w_in, dn_conv_w, dn_a_log, dn_dt_bias, dn_norm, cmp_pe, cmp_w1, cmp_b1, cmp_w2, w_out, ln_mem, ln_memkv, w_mem_q, w_mem_kv, w_mem_o, ln_ffn, w_up, w_down, ln_final):
    depth = w_in.shape[0]
    xp, xs = x_prompt, x_sample
    outs_p = [[] for _ in range(5)]
    outs_s = [[] for _ in range(4)]
    for l in range(depth):
        lw = {
            "ln_mix": ln_mix[l],
            "dn_conv_w": dn_conv_w[l], "dn_a_log": dn_a_log[l], "dn_dt_bias": dn_dt_bias[l], "dn_norm": dn_norm[l],
            "w_out_dn": w_out[l][:DN_WIDTH].astype(BF16), "w_out_nsa": w_out[l][DN_WIDTH:].astype(BF16),
            "ln_mem": ln_mem[l], "ln_memkv": ln_memkv[l], "w_mem_q": w_mem_q[l].astype(BF16),
            "w_mem_kv": w_mem_kv[l].astype(BF16), "w_mem_o": w_mem_o[l].astype(BF16),
            "ln_ffn": ln_ffn[l], "w_up": w_up[l].astype(BF16), "w_down": w_down[l].astype(BF16),
            "ln_final": ln_final,
        }
        lw["w_in"], lw["w_kvt"] = _split_w_in(w_in[l])
        cw = _compress_weights(cmp_pe[l], cmp_w1[l], cmp_b1[l], cmp_w2[l])
        last = l == depth - 1
        xp, p_nsa, p_win, p_mem, p_conv, p_rec = _prompt_layer(xp, mem_prompt, lw, cw, last)
        for acc, val in zip(outs_p, (p_nsa, p_win, p_mem, p_conv, p_rec)):
            acc.append(val)
        xs, s_nsa, s_win, s_conv, s_rec = _sample_layer(
            xs, cache_nsa_kv[l], cache_win_kv[l], cache_mem_kv[l], state_dn_conv[l], state_dn_rec[l],
            page_table, lw, _compress_page_weights(cmp_pe[l], cmp_w1[l], cmp_b1[l], cmp_w2[l]), last)
        for acc, val in zip(outs_s, (s_nsa, s_win, s_conv, s_rec)):
            acc.append(val)
    return (xp, xs) + tuple(jnp.stack(a) for a in outs_p) + tuple(jnp.stack(a) for a in outs_s)
```

```python
import functools

import jax
import jax.numpy as jnp
from jax import lax
from jax.experimental import pallas as pl
from jax.experimental.pallas import tpu as pltpu

F32 = jnp.float32
BF16 = jnp.bfloat16

HEAD_DIM = 64
DN_HEADS = 8
NSA_HEADS = 8
NSA_KV_HEADS = 2
NSA_GROUP = NSA_HEADS // NSA_KV_HEADS
DN_WIDTH = DN_HEADS * HEAD_DIM
NSA_WIDTH = NSA_HEADS * HEAD_DIM
CONV_WIDTH = 4
DN_CONV_DIM = 3 * DN_WIDTH
DN_CHUNK = 64
CMP_BLOCK = 64
SEL_BLOCK = 64
TOP_N = 16
N_LOCAL_BLOCKS = 2
WINDOW = 512
CMP_HIDDEN = 128
NSA_KV_KINDS = 6
MEM_HEADS = 4
PAGE_SIZE = 128
RMS_EPS = 1e-6
FORCE_SCORE = 1e3
NEG_INF = -1e30
ATTN_SCALE = HEAD_DIM ** -0.5
LOG2_E = 1.4426950408889634

LANES = 128
SUBLANES = 8
P_QKV = 0
P_Z = P_QKV + DN_CONV_DIM
P_NQ = P_Z + DN_WIDTH
P_CMP = P_NQ + NSA_WIDTH
CMP_COLS = 2 * NSA_KV_HEADS * HEAD_DIM
P_SMALL = P_CMP + CMP_COLS
P_COLS = P_SMALL + 2 * LANES
KV_COLS = NSA_KV_KINDS * NSA_KV_HEADS * HEAD_DIM
VMEM_LIMIT = 56 * 1024 * 1024
DN_APPLY_PASSES = (3, 3, 3, 1, 1, 1)
DN_SQUARE_PASSES = (3, 3, 1, 1, 1)


def _cp(sem, vmem=VMEM_LIMIT):
    return pltpu.CompilerParams(dimension_semantics=sem, vmem_limit_bytes=vmem)


def _split2(a):
    hi = a.astype(BF16)
    return hi, (a - hi.astype(F32)).astype(BF16)


def _dot(a, b, passes=1):
    if passes == 1:
        return jnp.dot(a.astype(BF16), b.astype(BF16), preferred_element_type=F32)
    ah, al = _split2(a)
    bh, bl = _split2(b)
    return (jnp.dot(ah, bh, preferred_element_type=F32) + jnp.dot(ah, bl, preferred_element_type=F32)
            + jnp.dot(al, bh, preferred_element_type=F32))


def _dot_nt(a, b):
    return lax.dot_general(a.astype(BF16), b.astype(BF16), (((1,), (1,)), ((), ())),
                           preferred_element_type=F32)


def _split3(a):
    hi = a.astype(BF16)
    r1 = a - hi.astype(F32)
    mid = r1.astype(BF16)
    lo = (r1 - mid.astype(F32)).astype(BF16)
    return hi, mid, lo


def _dot_exact_lhs01(a01, b):
    a = a01.astype(BF16)
    hi, mid, lo = _split3(b)
    return (jnp.dot(a, hi, preferred_element_type=F32) + jnp.dot(a, mid, preferred_element_type=F32)
            + jnp.dot(a, lo, preferred_element_type=F32))


def _dot_exact_rhs01(a, b01):
    b = b01.astype(BF16)
    hi, mid, lo = _split3(a)
    return (jnp.dot(hi, b, preferred_element_type=F32) + jnp.dot(mid, b, preferred_element_type=F32)
            + jnp.dot(lo, b, preferred_element_type=F32))


def _rms(x, gain):
    ms = jnp.mean(x * x, axis=-1, keepdims=True)
    return x * lax.rsqrt(ms + RMS_EPS) * gain


def _softplus(x):
    return jnp.maximum(x, 0.0) + jnp.log1p(jnp.exp(-jnp.abs(x)))


def _silu(x):
    return x * jax.nn.sigmoid(x)


def _rms_mm_kernel(x_ref, g_ref, w_ref, o_ref, h_ref):
    @pl.when(pl.program_id(1) == 0)
    def _():
        h_ref[...] = _rms(x_ref[...], g_ref[...]).astype(BF16)

    o_ref[...] = jnp.dot(h_ref[...], w_ref[...], preferred_element_type=F32)


def rms_matmul(x, gain, w_bf16, tm, tn):
    n, d = x.shape
    m = w_bf16.shape[1]
    assert n % tm == 0 and m % tn == 0
    return pl.pallas_call(
        _rms_mm_kernel,
        grid=(n // tm, m // tn),
        in_specs=[pl.BlockSpec((tm, d), lambda i, j: (i, 0)),
                  pl.BlockSpec((1, d), lambda i, j: (0, 0)),
                  pl.BlockSpec((d, tn), lambda i, j: (0, j))],
        out_specs=pl.BlockSpec((tm, tn), lambda i, j: (i, j)),
        out_shape=jax.ShapeDtypeStruct((n, m), F32),
        scratch_shapes=[pltpu.VMEM((tm, d), BF16)],
        compiler_params=_cp(("parallel", "arbitrary")),
        name="rms_matmul",
    )(x, gain.reshape(1, d), w_bf16)


def _proj_kernel(x_ref, g_ref, w_ref, wkv_ref, o_ref, okv_ref, owin_ref, h_ref):
    @pl.when(pl.program_id(1) == 0)
    def _():
        h = _rms(x_ref[...], g_ref[...]).astype(BF16)
        h_ref[...] = h
        kvt = lax.dot_general(wkv_ref[...], h, (((1,), (1,)), ((), ())), preferred_element_type=F32)
        split = okv_ref.shape[1]
        okv_ref[0] = kvt[0:split]
        owin_ref[0] = kvt[split:]

    o_ref[...] = jnp.dot(h_ref[...], w_ref[...], preferred_element_type=F32)


def input_projection(x3, gain, w_main, w_kvt, tm, tn):
    b, s, d = x3.shape
    n = b * s
    m = w_main.shape[1]
    kvc = w_kvt.shape[0]
    win = 2 * NSA_KV_HEADS * HEAD_DIM
    assert s % tm == 0 and m % tn == 0
    spt = s // tm
    return pl.pallas_call(
        _proj_kernel,
        grid=(n // tm, m // tn),
        in_specs=[pl.BlockSpec((tm, d), lambda i, j: (i, 0)),
                  pl.BlockSpec((1, d), lambda i, j: (0, 0)),
                  pl.BlockSpec((d, tn), lambda i, j: (0, j)),
                  pl.BlockSpec((kvc, d), lambda i, j: (0, 0))],
        out_specs=[pl.BlockSpec((tm, tn), lambda i, j: (i, j)),
                   pl.BlockSpec((1, kvc - win, tm), lambda i, j: (i // spt, 0, i % spt)),
                   pl.BlockSpec((1, win, tm), lambda i, j: (i // spt, 0, i % spt))],
        out_shape=[jax.ShapeDtypeStruct((n, m), F32), jax.ShapeDtypeStruct((b, kvc - win, s), F32),
                   jax.ShapeDtypeStruct((b, win, s), F32)],
        scratch_shapes=[pltpu.VMEM((tm, d), BF16)],
        compiler_params=_cp(("parallel", "arbitrary")),
        name="input_projection",
    )(x3.reshape(n, d), gain.reshape(1, d), w_main, w_kvt)


def _mm_res_kernel(*refs, n_in):
    res_ref = refs[0]
    a_refs = refs[1:1 + n_in]
    w_refs = refs[1 + n_in:1 + 2 * n_in]
    o_ref = refs[1 + 2 * n_in]
    acc = res_ref[...]
    for a_ref, w_ref in zip(a_refs, w_refs):
        acc = acc + jnp.dot(a_ref[...].astype(BF16), w_ref[...], preferred_element_type=F32)
    o_ref[...] = acc


def matmul_residual(res, a_list, w_list, tm):
    n, d = res.shape
    assert n % tm == 0
    n_in = len(a_list)
    in_specs = [pl.BlockSpec((tm, d), lambda i: (i, 0))]
    in_specs += [pl.BlockSpec((tm, a.shape[1]), lambda i: (i, 0)) for a in a_list]
    in_specs += [pl.BlockSpec(w.shape, lambda i: (0, 0)) for w in w_list]
    return pl.pallas_call(
        functools.partial(_mm_res_kernel, n_in=n_in),
        grid=(n // tm,),
        in_specs=in_specs,
        out_specs=pl.BlockSpec((tm, d), lambda i: (i, 0)),
        out_shape=jax.ShapeDtypeStruct((n, d), F32),
        compiler_params=_cp(("parallel",)),
        name="matmul_residual",
    )(res, *a_list, *w_list)


def _ffn_kernel(x_ref, g_ref, wu_ref, wd_ref, gf_ref, o_ref, hn_ref, acc_ref, *, final_norm):
    j = pl.program_id(1)

    @pl.when(j == 0)
    def _():
        x = x_ref[...]
        hn_ref[...] = _rms(x, g_ref[...]).astype(BF16)
        acc_ref[...] = x

    u = jnp.dot(hn_ref[...], wu_ref[...], preferred_element_type=F32)
    u = jnp.square(jnp.maximum(u, 0.0)).astype(BF16)
    acc_ref[...] += jnp.dot(u, wd_ref[...], preferred_element_type=F32)

    @pl.when(j == pl.num_programs(1) - 1)
    def _():
        y = acc_ref[...]
        if final_norm:
            y = _rms(y, gf_ref[...])
        o_ref[...] = y


def ffn(x, gain, wu_bf16, wd_bf16, gain_final, final_norm, tm, tf):
    n, d = x.shape
    f = wu_bf16.shape[1]
    assert n % tm == 0 and f % tf == 0
    return pl.pallas_call(
        functools.partial(_ffn_kernel, final_norm=final_norm),
        grid=(n // tm, f // tf),
        in_specs=[pl.BlockSpec((tm, d), lambda i, j: (i, 0)),
                  pl.BlockSpec((1, d), lambda i, j: (0, 0)),
                  pl.BlockSpec((d, tf), lambda i, j: (0, j)),
                  pl.BlockSpec((tf, d), lambda i, j: (j, 0)),
                  pl.BlockSpec((1, d), lambda i, j: (0, 0))],
        out_specs=pl.BlockSpec((tm, d), lambda i, j: (i, 0)),
        out_shape=jax.ShapeDtypeStruct((n, d), F32),
        scratch_shapes=[pltpu.VMEM((tm, d), BF16), pltpu.VMEM((tm, d), F32)],
        compiler_params=_cp(("parallel", "arbitrary")),
        name="ffn",
    )(x, gain.reshape(1, d), wu_bf16, wd_bf16, gain_final.reshape(1, d))


def _mem_attn_row_kernel(q_ref, kv_ref, o_ref, *, m, seqs, heads):
    sub = SUBLANES
    ones = jnp.ones((LANES, LANES), BF16)
    for u in range(seqs):
        x = kv_ref[u].reshape(m, 2 * sub, LANES)
        prod = x[:, 0:sub] * q_ref[u][None]
        part = jnp.dot(prod.reshape(m * sub, LANES).astype(BF16), ones, preferred_element_type=F32)
        part = part.reshape(m, sub, LANES)
        s = part + pltpu.roll(part, heads, 1)
        p = jnp.exp2(s - jnp.max(s, axis=0, keepdims=True))
        o_ref[u] = jnp.sum(p * x[:, sub:2 * sub], axis=0) / jnp.sum(p, axis=0)


def mem_attention_row(q, kv, seqs=8):
    b, h, hd = q.shape
    m = kv.shape[1]
    assert kv.shape[2:] == (2, h, hd)
    tiles = hd // LANES
    assert tiles * LANES == hd and tiles == 2 and tiles * h == SUBLANES and b % seqs == 0
    rows = jnp.transpose(kv.reshape(b, m, 2, h, tiles, LANES), (0, 1, 2, 4, 3, 5)).reshape(b, m * 2 * SUBLANES, LANES)
    q8 = jnp.transpose((q * (hd ** -0.5 * LOG2_E)).reshape(b, h, tiles, LANES), (0, 2, 1, 3)).reshape(b, SUBLANES, LANES)
    o8 = pl.pallas_call(
        functools.partial(_mem_attn_row_kernel, m=m, seqs=seqs, heads=h),
        grid=(b // seqs,),
        in_specs=[pl.BlockSpec((seqs, SUBLANES, LANES), lambda i: (i, 0, 0)),
                  pl.BlockSpec((seqs, m * 2 * SUBLANES, LANES), lambda i: (i, 0, 0))],
        out_specs=pl.BlockSpec((seqs, SUBLANES, LANES), lambda i: (i, 0, 0)),
        out_shape=jax.ShapeDtypeStruct((b, SUBLANES, LANES), F32),
        compiler_params=_cp(("parallel",)),
        name="mem_attention_row",
    )(q8, rows)
    return jnp.transpose(o8.reshape(b, tiles, h, LANES), (0, 2, 1, 3)).reshape(b, h, hd)


def _mem_block_kernel(x_ref, a1_ref, a2_ref, w1_ref, w2_ref, g_ref, wq_ref, kv_ref, wo_ref, o_ref, *, heads, hd):
    scale = hd ** -0.5
    x = (x_ref[0] + jnp.dot(a1_ref[0].astype(BF16), w1_ref[...], preferred_element_type=F32)
         + jnp.dot(a2_ref[0].astype(BF16), w2_ref[...], preferred_element_type=F32))
    q = jnp.dot(_rms(x, g_ref[...]).astype(BF16), wq_ref[...], preferred_element_type=F32)
    d = heads * hd
    outs = []
    for h in range(heads):
        s = _dot_nt(q[:, h * hd:(h + 1) * hd], kv_ref[0, :, h * hd:(h + 1) * hd]) * scale
        p = jnp.exp(s - jnp.max(s, axis=-1, keepdims=True))
        p = p / jnp.sum(p, axis=-1, keepdims=True)
        outs.append(_dot(p, kv_ref[0, :, d + h * hd:d + (h + 1) * hd]))
    att = jnp.concatenate(outs, axis=1).astype(BF16)
    o_ref[0] = x + jnp.dot(att, wo_ref[...], preferred_element_type=F32)


def mem_block(x3, a1, a2, w1, w2, gain, wq, kv, wo, tq):
    b, t, d = x3.shape
    m = kv.shape[1]
    hd = d // MEM_HEADS
    assert t % tq == 0 and kv.shape[2] == 2 * d
    return pl.pallas_call(
        functools.partial(_mem_block_kernel, heads=MEM_HEADS, hd=hd),
        grid=(b, t // tq),
        in_specs=[pl.BlockSpec((1, tq, d), lambda i, j: (i, j, 0)),
                  pl.BlockSpec((1, tq, a1.shape[2]), lambda i, j: (i, j, 0)),
                  pl.BlockSpec((1, tq, a2.shape[2]), lambda i, j: (i, j, 0)),
                  pl.BlockSpec(w1.shape, lambda i, j: (0, 0)),
                  pl.BlockSpec(w2.shape, lambda i, j: (0, 0)),
                  pl.BlockSpec((1, d), lambda i, j: (0, 0)),
                  pl.BlockSpec((d, d), lambda i, j: (0, 0)),
                  pl.BlockSpec((1, m, 2 * d), lambda i, j: (i, 0, 0)),
                  pl.BlockSpec((d, d), lambda i, j: (0, 0))],
        out_specs=pl.BlockSpec((1, tq, d), lambda i, j: (i, j, 0)),
        out_shape=jax.ShapeDtypeStruct((b, t, d), F32),
        compiler_params=_cp(("parallel", "parallel")),
        name="mem_block",
    )(x3, a1, a2, w1, w2, gain.reshape(1, d), wq, kv, wo)


def _compress_kernel(x_ref, w1_ref, pe_ref, b1_ref, w2_ref, o_ref, acc_ref, *, tt):
    j = pl.program_id(1)

    @pl.when(j == 0)
    def _():
        acc_ref[...] = jnp.zeros_like(acc_ref)

    hid2 = 2 * CMP_HIDDEN
    xt = jnp.swapaxes(x_ref[...], 0, 1)
    for kind in range(2):
        part = None
        for t in range(tt):
            xs = xt[t][:, kind * LANES:(kind + 1) * LANES] + pe_ref[kind, t:t + 1, :]
            d = jnp.dot(xs.astype(BF16), w1_ref[kind, t], preferred_element_type=F32)
            part = d if part is None else part + d
        acc_ref[:, kind * hid2:(kind + 1) * hid2] += part

    @pl.when(j == pl.num_programs(1) - 1)
    def _():
        h = jnp.maximum(acc_ref[...] + b1_ref[...], 0.0)
        for kind in range(2):
            o_ref[:, kind * LANES:(kind + 1) * LANES] = jnp.dot(
                h[:, kind * hid2:(kind + 1) * hid2].astype(BF16), w2_ref[kind], preferred_element_type=F32)


def compress_blocks(x3, col0, cw, bt, tt=8):
    nb = x3.shape[0]
    assert nb % bt == 0 and CMP_BLOCK % tt == 0 and col0 % CMP_COLS == 0 and x3.shape[1] == CMP_BLOCK
    w1bd, pe2, b1bd, w2bd = cw
    return pl.pallas_call(
        functools.partial(_compress_kernel, tt=tt),
        grid=(nb // bt, CMP_BLOCK // tt),
        in_specs=[pl.BlockSpec((bt, tt, CMP_COLS), lambda i, j: (i, j, col0 // CMP_COLS)),
                  pl.BlockSpec((2, tt, LANES, 2 * CMP_HIDDEN), lambda i, j: (0, j, 0, 0)),
                  pl.BlockSpec((2, tt, LANES), lambda i, j: (0, j, 0)),
                  pl.BlockSpec((1, 4 * CMP_HIDDEN), lambda i, j: (0, 0)),
                  pl.BlockSpec((2, 2 * CMP_HIDDEN, LANES), lambda i, j: (0, 0, 0))],
        out_specs=pl.BlockSpec((bt, 2 * LANES), lambda i, j: (i, 0)),
        out_shape=jax.ShapeDtypeStruct((nb, 2 * LANES), F32),
        scratch_shapes=[pltpu.VMEM((bt, 4 * CMP_HIDDEN), F32)],
        compiler_params=_cp(("parallel", "arbitrary")),
        name="compress_blocks",
    )(x3, w1bd, pe2, b1bd, w2bd)


def _compress_weights(cmp_pe, cmp_w1, cmp_b1, cmp_w2):
    w1r = cmp_w1.reshape(2, CMP_BLOCK, HEAD_DIM, CMP_HIDDEN)
    z = jnp.zeros_like(w1r)
    w1bd = jnp.concatenate([jnp.concatenate([w1r, z], -1), jnp.concatenate([z, w1r], -1)], axis=2).astype(BF16)
    pe2 = jnp.concatenate([cmp_pe, cmp_pe], -1)
    b1bd = jnp.concatenate([cmp_b1[0], cmp_b1[0], cmp_b1[1], cmp_b1[1]]).reshape(1, 4 * CMP_HIDDEN)
    z2 = jnp.zeros_like(cmp_w2)
    w2bd = jnp.concatenate([jnp.concatenate([cmp_w2, z2], -1), jnp.concatenate([z2, cmp_w2], -1)], axis=1).astype(BF16)
    return w1bd, pe2, b1bd, w2bd


def _masked_softmax(s, mask):
    s = jnp.where(mask, s, NEG_INF)
    p = jnp.where(mask, jnp.exp(s - jnp.max(s, axis=-1, keepdims=True)), 0.0)
    return p / jnp.maximum(jnp.sum(p, axis=-1, keepdims=True), 1e-30)


def _heads_as_rows(q_ref, tq, scale=ATTN_SCALE):
    qb = q_ref[0] * scale
    return jnp.concatenate([qb[:, h * HEAD_DIM:(h + 1) * HEAD_DIM] for h in range(NSA_GROUP)], axis=0).astype(BF16)


def _cmp_topk_kernel(q_ref, kc_ref, vct_ref, ocmp_ref, sel_ref, *, tq, nblk, topn):
    qi = pl.program_id(2)
    r, dh = NSA_GROUP, HEAD_DIM
    qpos = qi * tq + lax.broadcasted_iota(jnp.int32, (nblk, tq), 1)
    blk = lax.broadcasted_iota(jnp.int32, (nblk, tq), 0)
    vis = (blk + 1) * CMP_BLOCK - 1 <= qpos
    q = _heads_as_rows(q_ref, tq)
    kc = kc_ref[0, 0]
    vct = vct_ref[0, 0]
    imp = jnp.zeros((nblk, tq), F32)
    for h in range(r):
        s = jnp.where(vis, _dot_nt(kc, q[h * tq:(h + 1) * tq]), NEG_INF)
        p = jnp.where(vis, jnp.exp(s - jnp.max(s, axis=0, keepdims=True)), 0.0)
        p = p / jnp.maximum(jnp.sum(p, axis=0, keepdims=True), 1e-30)
        ocmp_ref[0, 0, h * dh:(h + 1) * dh, :] = _dot(vct, p)
        imp = imp + p
    cur = lax.shift_right_logical(qpos, 6)
    valid = blk <= cur
    forced = valid & ((blk == 0) | (cur - blk < N_LOCAL_BLOCKS))
    score = jnp.where(valid, imp + jnp.where(forced, FORCE_SCORE, 0.0), -1.0)
    rank = jnp.zeros((nblk, tq), F32)
    row8 = lax.broadcasted_iota(jnp.int32, (SUBLANES, tq), 0)
    for i in range(nblk):
        si = score[i:i + 1, :]
        t0 = i // SUBLANES * SUBLANES
        mid = score[t0:t0 + SUBLANES]
        parts = [jnp.where(row8 > i - t0, jnp.where(si >= mid, 1.0, 0.0), jnp.where(si > mid, 1.0, 0.0))]
        if t0 > 0:
            parts.insert(0, jnp.where(si > score[0:t0], 1.0, 0.0))
        if t0 + SUBLANES < nblk:
            parts.append(jnp.where(si >= score[t0 + SUBLANES:], 1.0, 0.0))
        rank = rank + jnp.concatenate(parts, axis=0)
    sel_ref[0, 0] = jnp.where(rank < topn, 1.0, 0.0).astype(BF16)


def nsa_cmp_topk(proj3, kc, vct, tq):
    b, sq, _ = proj3.shape
    g = kc.shape[1]
    nblk = kc.shape[2]
    dh = HEAD_DIM
    gw = NSA_GROUP * dh
    assert sq % tq == 0 and SEL_BLOCK == 64 and P_NQ % gw == 0 and nblk % SUBLANES == 0
    return pl.pallas_call(
        functools.partial(_cmp_topk_kernel, tq=tq, nblk=nblk, topn=min(TOP_N, nblk)),
        grid=(b, g, sq // tq),
        in_specs=[pl.BlockSpec((1, tq, gw), lambda i, j, k: (i, k, P_NQ // gw + j)),
                  pl.BlockSpec((1, 1, nblk, dh), lambda i, j, k: (i, j, 0, 0)),
                  pl.BlockSpec((1, 1, dh, nblk), lambda i, j, k: (i, j, 0, 0))],
        out_specs=[pl.BlockSpec((1, 1, gw, tq), lambda i, j, k: (i, j, 0, k)),
                   pl.BlockSpec((1, 1, nblk, tq), lambda i, j, k: (i, j, 0, k))],
        out_shape=[jax.ShapeDtypeStruct((b, g, gw, sq), F32),
                   jax.ShapeDtypeStruct((b, g, nblk, sq), BF16)],
        compiler_params=_cp(("parallel", "parallel", "parallel")),
        name="nsa_cmp_topk",
    )(proj3, kc, vct)


def _nsa_attn_kernel(q_ref, kst_ref, vst_ref, kwt_ref, vwt_ref, selt_ref, ocmp_ref, sm_ref, o_ref,
                     m_sc, acc_sc, s_sc, bias_sc, p_sc, *, tq, tk, nsel, ck):
    grp = pl.program_id(1)
    qi = pl.program_id(2)
    r = NSA_GROUP
    dh = HEAD_DIM
    q = _heads_as_rows(q_ref, tq, ATTN_SCALE * LOG2_E)
    qpos = qi * tq + lax.broadcasted_iota(jnp.int32, (1, tq), 1)
    ones_row = jnp.where(lax.broadcasted_iota(jnp.int32, (8, 1), 0) == 0, 1.0, 0.0)

    def reset():
        m_sc[...] = jnp.full(m_sc.shape, NEG_INF, F32)
        acc_sc[...] = jnp.zeros(acc_sc.shape, F32)

    def with_ones(vt):
        return jnp.concatenate([vt, jnp.broadcast_to(ones_row, (8, vt.shape[1]))], axis=0).astype(BF16)

    def scores(kt, qrows, row0):
        s_sc[row0:row0 + kt.shape[1], :] = _dot_nt(kt.T, qrows)

    def set_bias(mask, row0):
        bias_sc[pl.ds(row0, mask.shape[0]), :] = jnp.where(mask, 0.0, NEG_INF)

    def fold(row0, n, vt, biased):
        def chunk(c, shift):
            rows = pl.ds(row0 + c * ck, ck)
            x = s_sc[rows, :] - shift
            return x + jnp.concatenate([bias_sc[rows, :]] * r, axis=1) if biased else x

        m_prev = m_sc[...]
        top = jnp.full((SUBLANES, r * tq), NEG_INF, F32)
        for c in range(n // ck):
            top = jnp.maximum(top, jnp.max(chunk(c, 0.0).reshape(ck // SUBLANES, SUBLANES, r * tq), axis=0))
        m_new = jnp.maximum(m_prev, jnp.max(top, axis=0, keepdims=True))
        for c in range(n // ck):
            p_sc[c * ck:(c + 1) * ck, :] = jnp.exp2(chunk(c, m_new)).astype(BF16)
        acc_sc[...] = jnp.exp2(m_prev - m_new) * acc_sc[...] + jnp.dot(with_ones(vt), p_sc[0:n, :],
                                                                      preferred_element_type=F32)
        m_sc[...] = m_new

    def result():
        acc = acc_sc[...]
        return acc[0:dh] / jnp.maximum(acc[dh:dh + 1], 1e-30)

    wk = WINDOW + tq
    w0 = pl.multiple_of(jnp.maximum(qi - WINDOW // tq, 0) * tq, tq)
    dpos = qpos - (w0 + lax.broadcasted_iota(jnp.int32, (wk, 1), 0))
    set_bias((dpos >= 0) & (dpos < WINDOW), 0)
    scores(kwt_ref[0, :, pl.ds(w0, wk)], q, 0)

    reset()
    block_bias = jnp.where(selt_ref[0, 0].astype(F32).T > 0.5, 0.0, NEG_INF).astype(BF16)
    q_sel = jnp.concatenate([q, jnp.concatenate([block_bias] * r, axis=0)], axis=1)
    blk = lax.broadcasted_iota(jnp.int32, (nsel, tk), 0)
    krow = lax.broadcasted_iota(jnp.int32, (nsel, tk), 1)

    def slc_scores(k0, row0):
        code = jnp.where(lax.shift_right_logical(k0 + krow, CMP_BLOCK.bit_length() - 1) == blk, 1.0, 0.0)
        scores(jnp.concatenate([kst_ref[0, :, pl.ds(k0, tk)], code], axis=0), q_sel, row0)

    def slc_fold(k0, row0, biased):
        fold(row0, tk, vst_ref[0, :, pl.ds(k0, tk)], biased)

    def slc_pair(c, carry):
        ka = pl.multiple_of(2 * c * tk, tk)
        slc_scores(ka + tk, wk + tk)
        slc_fold(ka, wk, False)
        slc_scores(ka + 2 * tk, wk)
        slc_fold(ka + tk, wk + tk, False)
        return carry

    n_below = (qi * tq) // tk
    kd = pl.multiple_of(n_below * tk, tk)
    odd = n_below % 2
    slc_scores(0, wk)
    lax.fori_loop(0, n_below // 2, slc_pair, 0)

    @pl.when(odd == 1)
    def _():
        slc_scores(kd, wk + tk)
        slc_fold(kd - tk, wk, False)

    row_d = pl.multiple_of(wk + odd * tk, ck)
    set_bias(kd + lax.broadcasted_iota(jnp.int32, (tk, 1), 0) <= qpos, row_d)
    slc_fold(kd, row_d, True)
    o_slc = result()

    reset()
    fold(0, wk, vwt_ref[0, :, pl.ds(w0, wk)], True)
    o_win = result()

    gates_t = jax.nn.sigmoid(sm_ref[0]).T
    g0 = 2 * DN_HEADS
    per = 3 * r
    gt = jnp.where(grp == 0, gates_t[g0:g0 + per], gates_t[g0 + per:g0 + 2 * per])
    outs = []
    for h in range(r):
        outs.append(gt[3 * h:3 * h + 1] * ocmp_ref[0, 0, h * dh:(h + 1) * dh, :]
                    + gt[3 * h + 1:3 * h + 2] * o_slc[:, h * tq:(h + 1) * tq]
                    + gt[3 * h + 2:3 * h + 3] * o_win[:, h * tq:(h + 1) * tq])
    o_ref[0] = jnp.concatenate(outs, axis=0).T


def nsa_attention(proj3, kvt, kvt_win, selt, ocmpt, tq, tk):
    b, sq, _ = proj3.shape
    t = kvt.shape[2]
    g, nsel = selt.shape[1], selt.shape[2]
    r, dh = NSA_GROUP, HEAD_DIM
    gw = r * dh
    assert sq == t and sq % tq == 0 and t % tk == 0 and tk % tq == 0 and WINDOW % tq == 0 and tq % LANES == 0
    assert t >= WINDOW + tq and g == NSA_KV_HEADS and g == 2
    ck = 32
    assert tk % ck == 0 and (WINDOW + tq) % ck == 0 and tk <= WINDOW + tq
    rows = WINDOW + tq + 2 * tk

    def kv_spec(kind):
        return pl.BlockSpec((1, dh, t), lambda i, j, k: (i, kind * g + j, 0))

    return pl.pallas_call(
        functools.partial(_nsa_attn_kernel, tq=tq, tk=tk, nsel=nsel, ck=ck),
        grid=(b, g, sq // tq),
        in_specs=[pl.BlockSpec((1, tq, gw), lambda i, j, k: (i, k, P_NQ // gw + j)),
                  kv_spec(2), kv_spec(3), kv_spec(0), kv_spec(1),
                  pl.BlockSpec((1, 1, nsel, tq), lambda i, j, k: (i, j, 0, k)),
                  pl.BlockSpec((1, 1, gw, tq), lambda i, j, k: (i, j, 0, k)),
                  pl.BlockSpec((1, tq, LANES), lambda i, j, k: (i, k, P_SMALL // LANES))],
        out_specs=pl.BlockSpec((1, tq, gw), lambda i, j, k: (i, k, j)),
        out_shape=jax.ShapeDtypeStruct((b, sq, g * gw), F32),
        scratch_shapes=[pltpu.VMEM((1, r * tq), F32), pltpu.VMEM((dh + 8, r * tq), F32),
                        pltpu.VMEM((rows, r * tq), F32), pltpu.VMEM((rows, tq), F32),
                        pltpu.VMEM((WINDOW + tq, r * tq), BF16)],
        compiler_params=_cp(("parallel", "parallel", "parallel")),
        name="nsa_attention",
    )(proj3, kvt, kvt, kvt_win, kvt_win, selt, ocmpt, proj3)


def _bdot(a, b, passes=1):
    dims = (((2,), (1,)), ((0,), (0,)))
    if passes == 1:
        return lax.dot_general(a.astype(BF16), b.astype(BF16), dims, preferred_element_type=F32)
    ah, al = _split2(a)
    bh, bl = _split2(b)
    return (lax.dot_general(ah, bh, dims, preferred_element_type=F32)
            + lax.dot_general(ah, bl, dims, preferred_element_type=F32)
            + lax.dot_general(al, bh, dims, preferred_element_type=F32))


def _bdot_nt(a, b):
    return lax.dot_general(a.astype(BF16), b.astype(BF16), (((2,), (2,)), ((0,), (0,))),
                           preferred_element_type=F32)


def _deltanet_kernel(qkv_ref, z_ref, sm_ref, cw_ref, alog_ref, dtb_ref, gn_ref, o_ref, s_out_ref,
                     xbuf, s_sc):
    c = pl.program_id(0)
    ch = DN_CHUNK
    n_pairs = DN_HEADS // 2
    two = 2 * ch
    n_batch = qkv_ref.shape[0]

    @pl.when(c == 0)
    def _():
        xbuf[:, 0:8, :] = jnp.zeros((n_batch, 8, DN_CONV_DIM), F32)
        s_sc[...] = jnp.zeros_like(s_sc)

    ti = lax.broadcasted_iota(jnp.int32, (ch, ch), 0)
    tj = lax.broadcasted_iota(jnp.int32, (ch, ch), 1)
    tri = jnp.where(ti >= tj, 1.0, 0.0)
    lane = lax.broadcasted_iota(jnp.int32, (ch, LANES), 1)
    lo = lane < HEAD_DIM
    row2 = lax.broadcasted_iota(jnp.int32, (two, two), 0)
    col2 = lax.broadcasted_iota(jnp.int32, (two, two), 1)
    same = (row2 >= ch) == (col2 >= ch)
    incl = (same & (row2 >= col2))[None]
    strict = (same & (row2 > col2))[None]
    top = lax.broadcasted_iota(jnp.int32, (two, 1), 0) < ch

    def seg_sum(x):
        s_lo = jnp.sum(jnp.where(lo, x, 0.0), axis=-1, keepdims=True)
        s_hi = jnp.sum(jnp.where(lo, 0.0, x), axis=-1, keepdims=True)
        return jnp.where(lo, s_lo, s_hi)

    def stack2(x):
        return jnp.concatenate([jnp.where(lo, x, 0.0), jnp.where(lo, 0.0, x)], axis=0)

    def col2x(a, b):
        return jnp.concatenate([jnp.broadcast_to(a, (ch, LANES)), jnp.broadcast_to(b, (ch, LANES))], axis=0)

    q_l, k_l, v_l, beta_l, gc_l, gl_l = [], [], [], [], [], []
    for bi in range(n_batch):
        xbuf[bi, 8:8 + ch, :] = qkv_ref[bi]
        conv = None
        for w in range(CONV_WIDTH):
            term = xbuf[bi, 5 + w:5 + w + ch, :] * cw_ref[w:w + 1, :]
            conv = term if conv is None else conv + term
        xbuf[bi, 0:8, :] = xbuf[bi, ch:ch + 8, :]
        act = _silu(conv)
        sm = sm_ref[bi]
        beta_all = jax.nn.sigmoid(sm)
        g_all = -jnp.exp(alog_ref[...]) * _softplus(sm + dtb_ref[...])
        gcum_all = _dot_exact_lhs01(tri, g_all)
        for p in range(n_pairs):
            c0 = p * LANES
            qp = act[:, c0:c0 + LANES]
            kp = act[:, DN_WIDTH + c0:DN_WIDTH + c0 + LANES]
            vp = act[:, 2 * DN_WIDTH + c0:2 * DN_WIDTH + c0 + LANES]
            qp = qp * lax.rsqrt(seg_sum(qp * qp) + 1e-6) * (HEAD_DIM ** -0.5)
            kp = kp * lax.rsqrt(seg_sum(kp * kp) + 1e-6)
            h0, h1 = DN_HEADS + 2 * p, DN_HEADS + 2 * p + 1
            q_l.append(stack2(qp))
            k_l.append(stack2(kp))
            v_l.append(stack2(vp))
            beta_l.append(col2x(beta_all[:, 2 * p:2 * p + 1], beta_all[:, 2 * p + 1:2 * p + 2]))
            gc_l.append(col2x(gcum_all[:, h0:h0 + 1], gcum_all[:, h1:h1 + 1]))
            gl_l.append(jnp.broadcast_to(jnp.where(top, gcum_all[ch - 1:ch, h0:h0 + 1], gcum_all[ch - 1:ch, h1:h1 + 1]),
                                         (two, LANES)))
    q2, k2, v2 = jnp.stack(q_l), jnp.stack(k_l), jnp.stack(v_l)
    beta2, gc2, gl2 = jnp.stack(beta_l), jnp.stack(gc_l), jnp.stack(gl_l)
    decay = jnp.exp(jnp.where(incl, gc2 - jnp.swapaxes(gc2, 1, 2), NEG_INF))
    kb2 = k2 * beta2
    a_mat = jnp.where(strict, _bdot_nt(kb2, k2) * decay, 0.0)
    aqk = jnp.where(incl, _bdot_nt(q2, k2) * decay, 0.0)
    s_old = s_sc[...]
    egc = jnp.exp(gc2)
    x = beta2 * (v2 - egc * _bdot(k2, s_old))
    pw = -a_mat
    n_lvl = ch.bit_length() - 1
    for lvl in range(n_lvl):
        x = x + _bdot(pw, x, DN_APPLY_PASSES[lvl])
        if lvl + 1 < n_lvl:
            pw = _bdot(pw, pw, DN_SQUARE_PASSES[lvl])
    o2 = _bdot(q2 * egc, s_old) + _bdot(aqk, x)
    kdec = k2 * jnp.exp(gl2 - gc2)
    s_sc[...] = s_old * jnp.exp(gl2) + _bdot(jnp.swapaxes(kdec, 1, 2), x)
    for bi in range(n_batch):
        for p in range(n_pairs):
            c0 = p * LANES
            o_n = o2[bi * n_pairs + p]
            o_pair = o_n[0:ch] + o_n[ch:two]
            inv = lax.rsqrt(seg_sum(o_pair * o_pair) * (1.0 / HEAD_DIM) + RMS_EPS)
            o_ref[bi, :, c0:c0 + LANES] = o_pair * inv * gn_ref[...] * _silu(z_ref[bi, :, c0:c0 + LANES])

    @pl.when(c == pl.num_programs(0) - 1)
    def _():
        s_out_ref[...] = s_sc[...]


def deltanet_prompt(proj3, conv_w, a_log, dt_bias, norm_gain):
    b, t, _ = proj3.shape
    ch = DN_CHUNK
    assert t % ch == 0
    pad = jnp.zeros((LANES - 2 * DN_HEADS,), F32)
    alog_row = jnp.concatenate([jnp.zeros((DN_HEADS,), F32), a_log, pad]).reshape(1, LANES)
    dtb_row = jnp.concatenate([jnp.zeros((DN_HEADS,), F32), dt_bias, pad]).reshape(1, LANES)
    gn_row = jnp.concatenate([norm_gain, norm_gain]).reshape(1, LANES)
    n_pairs = DN_HEADS // 2
    o, s_fin = pl.pallas_call(
        _deltanet_kernel,
        grid=(t // ch,),
        in_specs=[pl.BlockSpec((b, ch, DN_CONV_DIM), lambda j: (0, j, P_QKV // DN_CONV_DIM)),
                  pl.BlockSpec((b, ch, DN_WIDTH), lambda j: (0, j, P_Z // DN_WIDTH)),
                  pl.BlockSpec((b, ch, LANES), lambda j: (0, j, P_SMALL // LANES)),
                  pl.BlockSpec((CONV_WIDTH, DN_CONV_DIM), lambda j: (0, 0)),
                  pl.BlockSpec((1, LANES), lambda j: (0, 0)),
                  pl.BlockSpec((1, LANES), lambda j: (0, 0)),
                  pl.BlockSpec((1, LANES), lambda j: (0, 0))],
        out_specs=[pl.BlockSpec((b, ch, DN_WIDTH), lambda j: (0, j, 0)),
                   pl.BlockSpec((b * n_pairs, 2 * ch, LANES), lambda j: (0, 0, 0))],
        out_shape=[jax.ShapeDtypeStruct((b, t, DN_WIDTH), F32),
                   jax.ShapeDtypeStruct((b * n_pairs, 2 * ch, LANES), F32)],
        scratch_shapes=[pltpu.VMEM((b, ch + 8, DN_CONV_DIM), F32), pltpu.VMEM((b * n_pairs, 2 * ch, LANES), F32)],
        compiler_params=_cp(("arbitrary",)),
        name="deltanet_prompt",
    )(proj3, proj3, proj3, conv_w, alog_row, dtb_row, gn_row)
    return o, s_fin.reshape(b, n_pairs, 2 * ch, LANES)


def _pairs_to_heads(s_pairs):
    d = HEAD_DIM
    return jnp.stack([s_pairs[:, :, :d, :d], s_pairs[:, :, d:, d:]], axis=2).reshape(
        s_pairs.shape[0], DN_HEADS, d, d)


def _dn_step_prep_kernel(qkv_ref, cs_ref, sm_ref, cw_ref, alog_ref, dtb_ref, ones_ref,
                         q_ref, k_ref, v_ref, sc_ref):
    conv = qkv_ref[...] * cw_ref[CONV_WIDTH - 1:CONV_WIDTH, :]
    for w in range(CONV_WIDTH - 1):
        conv = conv + cs_ref[w] * cw_ref[w:w + 1, :]
    act = _silu(conv)
    q = act[:, 0:DN_WIDTH]
    k = act[:, DN_WIDTH:2 * DN_WIDTH]

    def seg_sum(x):
        return _dot_exact_rhs01(x, ones_ref[...])

    q_ref[...] = q * lax.rsqrt(seg_sum(q * q) + 1e-6) * (HEAD_DIM ** -0.5)
    k_ref[...] = k * lax.rsqrt(seg_sum(k * k) + 1e-6)
    v_ref[...] = act[:, 2 * DN_WIDTH:]
    sm = sm_ref[...]
    g = -jnp.exp(alog_ref[...]) * _softplus(sm + dtb_ref[...])
    lane = lax.broadcasted_iota(jnp.int32, sm.shape, 1)
    sc_ref[...] = jnp.where(lane < DN_HEADS, jax.nn.sigmoid(sm), jnp.exp(g))


def _dn_step_kernel(k_ref, q_ref, v_ref, be_ref, eg_ref, z_ref, gn_ref, s_ref, o_ref, s_out_ref):
    s_old = s_ref[0]
    k, q, v = k_ref[0], q_ref[0], v_ref[0]
    beta, eg = be_ref[0], eg_ref[0]
    ks = jnp.sum(k[:, None, :] * s_old, axis=0)
    qs = jnp.sum(q[:, None, :] * s_old, axis=0)
    qk = jnp.sum(q * k, axis=0, keepdims=True)
    v_new = beta * (v - eg * ks)
    o = eg * qs + qk * v_new
    inv = lax.rsqrt(jnp.mean(o * o, axis=0, keepdims=True) + RMS_EPS)
    o_ref[0] = o * inv * gn_ref[...] * _silu(z_ref[0])
    s_out_ref[0] = s_old * eg[None] + k[:, None, :] * v_new[None]


def deltanet_sample(proj_s, conv_state, rec_state, conv_w, a_log, dt_bias, norm_gain):
    n = proj_s.shape[0]
    hds, d = DN_HEADS, HEAD_DIM
    pad = jnp.zeros((LANES - 2 * hds,), F32)
    alog_row = jnp.concatenate([jnp.zeros((hds,), F32), a_log, pad]).reshape(1, LANES)
    dtb_row = jnp.concatenate([jnp.zeros((hds,), F32), dt_bias, pad]).reshape(1, LANES)
    head_of = jnp.arange(DN_WIDTH) // d
    ones_bd = (head_of[:, None] == head_of[None, :]).astype(BF16)
    cs = jnp.transpose(conv_state, (1, 0, 2))
    full = lambda shape: pl.BlockSpec(shape, lambda i: (0,) * len(shape))
    q, k, v, sc = pl.pallas_call(
        _dn_step_prep_kernel,
        grid=(1,),
        in_specs=[pl.BlockSpec((n, DN_CONV_DIM), lambda i: (0, P_QKV // DN_CONV_DIM)),
                  full((CONV_WIDTH - 1, n, DN_CONV_DIM)),
                  pl.BlockSpec((n, LANES), lambda i: (0, P_SMALL // LANES)),
                  full((CONV_WIDTH, DN_CONV_DIM)), full((1, LANES)), full((1, LANES)),
                  full((DN_WIDTH, DN_WIDTH))],
        out_specs=[full((n, DN_WIDTH)), full((n, DN_WIDTH)), full((n, DN_WIDTH)), full((n, LANES))],
        out_shape=[jax.ShapeDtypeStruct((n, DN_WIDTH), F32)] * 3 + [jax.ShapeDtypeStruct((n, LANES), F32)],
        compiler_params=_cp(("arbitrary",)),
        name="dn_step_prep",
    )(proj_s, cs, proj_s, conv_w, alog_row, dtb_row, ones_bd)
    t3 = lambda a: a.T.reshape(hds, d, n)
    sct = sc[:, :2 * hds].T.reshape(2, hds, 1, n)
    head_vec = pl.BlockSpec((1, d, n), lambda i: (i, 0, 0))
    head_scl = pl.BlockSpec((1, 1, n), lambda i: (i, 0, 0))
    state = pl.BlockSpec((1, d, d, n), lambda i: (i, 0, 0, 0))
    o, s_new = pl.pallas_call(
        _dn_step_kernel,
        grid=(hds,),
        in_specs=[head_vec, head_vec, head_vec, head_scl, head_scl, head_vec,
                  pl.BlockSpec((d, 1), lambda i: (0, 0)), state],
        out_specs=[head_vec, state],
        out_shape=[jax.ShapeDtypeStruct((hds, d, n), F32), jax.ShapeDtypeStruct((hds, d, d, n), F32)],
        compiler_params=_cp(("parallel",)),
        name="dn_step",
    )(t3(k), t3(q), t3(v), sct[0], sct[1], t3(proj_s[:, P_Z:P_Z + DN_WIDTH]), norm_gain.reshape(d, 1),
      jnp.transpose(rec_state, (1, 2, 3, 0)))
    return o.reshape(DN_WIDTH, n).T, jnp.transpose(s_new, (3, 0, 1, 2))


def _nsa_decode_kernel(pt_ref, q_ref, new_ref, gate_ref, win_ref, newc_ref, exp_ref, *refs, n_pages, seqs):
    del pt_ref
    o_ref, swin_ref, kc_sc = refs[2 * seqs * n_pages:2 * seqs * n_pages + 3]
    dh, r, g2, nh = HEAD_DIM, NSA_GROUP, NSA_KV_HEADS, NSA_HEADS
    past = n_pages * PAGE_SIZE
    nb = past // CMP_BLOCK
    nsel = nb + 1
    wlen = win_ref.shape[4]
    for u in range(seqs):
        for j in range(n_pages):
            kc_sc[u, j:j + 1, :] = refs[u * n_pages + j][0]
    kcv = kc_sc[...]
    new = new_ref[...]
    q8 = q_ref[...] * ATTN_SCALE
    head = lax.broadcasted_iota(jnp.int32, (1, nh, 1), 1)
    lane = lax.broadcasted_iota(jnp.int32, (1, 1, LANES), 2)
    g0 = head < r
    blk = jnp.where(lane < n_pages, 2 * lane, jnp.where(lane < nb, 2 * (lane - n_pages) + 1, lane))
    wp = lax.broadcasted_iota(jnp.int32, (1, 1, wlen + LANES), 2)
    dpos = wlen - wp
    wmask = (dpos >= 0) & (dpos < WINDOW) & (past - wlen + wp >= 0)

    def both(x):
        return jnp.where(g0, x[:, :, 0:dh], x[:, :, dh:2 * dh])

    def new_part(kind):
        return both(jnp.broadcast_to(new[:, :, 2 * kind * dh:2 * (kind + 1) * dh], (seqs, nh, 2 * dh)))

    def pages(kind):
        return jnp.stack([jnp.concatenate([refs[(seqs + u) * n_pages + j][0, kind].reshape(2 * dh, PAGE_SIZE)
                                           for j in range(n_pages)], axis=1) for u in range(seqs)])

    q_bd = jnp.concatenate([jnp.where(g0, q8, 0.0), jnp.where(g0, 0.0, q8)], axis=2)
    s_new = jnp.sum(q8 * new_part(2), axis=-1, keepdims=True)
    s_all = jnp.concatenate([_bdot(q_bd, pages(0)), jnp.broadcast_to(s_new, (seqs, nh, LANES))], axis=2)
    sw = _bdot(q_bd, win_ref[:, 0].reshape(seqs, 2 * dh, wlen))
    sw_new = jnp.sum(q8 * new_part(4), axis=-1, keepdims=True)

    def cmp_rows(base):
        even = jnp.concatenate([kcv[:, :, base:base + dh], kcv[:, :, base + LANES:base + LANES + dh]], axis=2)
        odd = jnp.concatenate([kcv[:, :, base + dh:base + LANES], kcv[:, :, base + LANES + dh:base + 2 * LANES]], axis=2)
        return jnp.concatenate([even, odd], axis=1)

    s = _bdot_nt(q_bd, cmp_rows(0))
    p = jnp.exp(s - jnp.max(s, axis=-1, keepdims=True))
    p = p / jnp.maximum(jnp.sum(p, axis=-1, keepdims=True), 1e-30)
    o_cmp = both(_bdot(p, cmp_rows(g2 * LANES)))
    bid_row = jnp.broadcast_to(blk.astype(F32), (1, LANES, LANES))
    bid_col = jnp.swapaxes(bid_row, 1, 2)
    valid = lane < nsel
    forced = valid & ((blk == 0) | (nb - blk < N_LOCAL_BLOCKS))
    sels = []
    for g in range(g2):
        in_g = (head >= g * r) & (head < (g + 1) * r)
        imp = jnp.sum(jnp.where(in_g, p, 0.0), axis=1, keepdims=True)
        imp = jnp.concatenate([imp, jnp.zeros((seqs, 1, LANES - nb), F32)], axis=2)
        score = jnp.where(valid, imp + jnp.where(forced, FORCE_SCORE, 0.0), -1.0)
        sc_row = jnp.broadcast_to(score, (seqs, LANES, LANES))
        sc_col = jnp.swapaxes(sc_row, 1, 2)
        beats = (sc_col > sc_row) | ((sc_col == sc_row) & (bid_col < bid_row))
        rank = jnp.sum(jnp.where(beats, 1.0, 0.0), axis=1, keepdims=True)
        sels.append(jnp.where(valid & (rank < min(TOP_N, nsel)), 1.0, 0.0))
    sel8 = jnp.where(g0, sels[0], sels[1]).astype(BF16)
    chosen = jnp.dot(sel8.reshape(seqs * nh, LANES), exp_ref[...], preferred_element_type=F32) > 0.5
    pm = _masked_softmax(s_all, chosen.reshape(seqs, nh, past + LANES))
    o_slc = pm[:, :, past:past + 1] * new_part(3) + both(_bdot_nt(pm[:, :, 0:past], pages(1)))
    pw = _masked_softmax(jnp.concatenate([sw, jnp.broadcast_to(sw_new, (seqs, nh, LANES))], axis=2), wmask)
    o_win = pw[:, :, wlen:wlen + 1] * new_part(5) + both(
        _bdot_nt(pw[:, :, 0:wlen], win_ref[:, 1].reshape(seqs, 2 * dh, wlen)))
    gates = jnp.broadcast_to(jax.nn.sigmoid(gate_ref[...]), (seqs, nh, LANES))

    def gate(branch):
        return jnp.sum(jnp.where(lane == 3 * head + branch, gates, 0.0), axis=-1, keepdims=True)

    o_ref[...] = gate(0) * o_cmp + gate(1) * o_slc + gate(2) * o_win
    wpos = lax.broadcasted_iota(jnp.int32, (1, 1, 1, wlen), 3)
    nlane = newc_ref.shape[1]
    slane = lax.broadcasted_iota(jnp.int32, (1, nlane), 1)
    for u in range(seqs):
        mine = slane == (pl.program_id(0) * seqs + u) % nlane
        col = jnp.sum(jnp.where(mine, newc_ref[...], 0.0), axis=1, keepdims=True)
        swin_ref[u] = jnp.where(wpos == wlen - 1, col.reshape(2, g2, dh, 1), pltpu.roll(win_ref[u], wlen - 1, 3))


def nsa_decode(page_table, q3, new_row, gate_row, win_t, new_win, kc_phys, cache_t, seqs=8):
    n, n_pages = page_table.shape
    dh = HEAD_DIM
    g2 = NSA_KV_HEADS
    wlen = win_t.shape[4]
    assert 2 * n_pages + 1 <= LANES and PAGE_SIZE == 2 * CMP_BLOCK and PAGE_SIZE == LANES and g2 == 2

    past = n_pages * PAGE_SIZE
    nb = past // CMP_BLOCK
    erow = lax.broadcasted_iota(jnp.int32, (LANES, past + LANES), 0)
    ecol = lax.broadcasted_iota(jnp.int32, (LANES, past + LANES), 1)
    page, second = ecol // PAGE_SIZE, (ecol % PAGE_SIZE) >= CMP_BLOCK
    expand = (((erow < n_pages) & (page == erow) & ~second & (ecol < past))
              | ((erow >= n_pages) & (erow < nb) & (page == erow - n_pages) & second & (ecol < past))
              | ((erow == nb) & (ecol == past))).astype(BF16)

    assert n % seqs == 0
    nlane = LANES if n % LANES == 0 else n
    assert nlane % seqs == 0

    def kc_map(u, j):
        return lambda i, pt: (pt[seqs * i + u, j], 0, 0)

    def slc_map(u, j):
        return lambda i, pt: (pt[seqs * i + u, j], 1, 0, 0, 0)

    in_specs = [pl.BlockSpec((seqs, NSA_HEADS, dh), lambda i, pt: (i, 0, 0)),
                pl.BlockSpec((seqs, 1, KV_COLS), lambda i, pt: (i, 0, 0)),
                pl.BlockSpec((seqs, 1, LANES), lambda i, pt: (i, 0, 0)),
                pl.BlockSpec((seqs, 2, g2, dh, wlen), lambda i, pt: (i, 0, 0, 0, 0)),
                pl.BlockSpec((2 * g2 * dh, nlane), lambda i, pt: (0, seqs * i // nlane)),
                pl.BlockSpec((LANES, past + LANES), lambda i, pt: (0, 0))]
    in_specs += [pl.BlockSpec((1, 1, 4 * LANES), kc_map(u, j)) for u in range(seqs) for j in range(n_pages)]
    in_specs += [pl.BlockSpec((1, 2, g2, dh, PAGE_SIZE), slc_map(u, j)) for u in range(seqs) for j in range(n_pages)]
    grid_spec = pltpu.PrefetchScalarGridSpec(
        num_scalar_prefetch=1, grid=(n // seqs,), in_specs=in_specs,
        out_specs=[pl.BlockSpec((seqs, NSA_HEADS, dh), lambda i, pt: (i, 0, 0)),
                   pl.BlockSpec((seqs, 2, g2, dh, wlen), lambda i, pt: (i, 0, 0, 0, 0))],
        scratch_shapes=[pltpu.VMEM((seqs, n_pages, 4 * LANES), F32)])
    return pl.pallas_call(
        functools.partial(_nsa_decode_kernel, n_pages=n_pages, seqs=seqs),
        grid_spec=grid_spec,
        out_shape=[jax.ShapeDtypeStruct((n, NSA_HEADS, dh), F32), jax.ShapeDtypeStruct(win_t.shape, F32)],
        compiler_params=_cp(("arbitrary",)),
        name="nsa_decode",
    )(page_table, q3, new_row, gate_row, win_t, new_win, expand,
      *([kc_phys] * (seqs * n_pages)), *([cache_t] * (seqs * n_pages)))


def _compress_pages_kernel(x_ref, wd_ref, ped_ref, b1_ref, w2_ref, o_ref):
    xt = jnp.swapaxes((x_ref[...] + ped_ref[0][None]).astype(BF16), 0, 1)
    acc = None
    for dp in range(xt.shape[0] // 2):
        d = jnp.dot(jnp.concatenate([xt[2 * dp], xt[2 * dp + 1]], axis=1), wd_ref[0, dp], preferred_element_type=F32)
        acc = d if acc is None else acc + d
    h = jnp.maximum(acc + b1_ref[0], 0.0)
    o_ref[:, 0, :] = jnp.dot(h.astype(BF16), w2_ref[0], preferred_element_type=F32)


def compress_pages(cache_t, cwp, bp):
    n_phys = cache_t.shape[0]
    g2, dh = NSA_KV_HEADS, HEAD_DIM
    assert n_phys % bp == 0 and dh % 2 == 0
    wd, ped, b1h, w2h = cwp
    rows = cache_t.reshape(n_phys, cache_t.shape[1] * g2 * dh, PAGE_SIZE)
    return pl.pallas_call(
        _compress_pages_kernel,
        grid=(2 * g2, n_phys // bp),
        in_specs=[pl.BlockSpec((bp, dh, PAGE_SIZE), lambda j, i: (i, j, 0)),
                  pl.BlockSpec((1, dh // 2, 2 * PAGE_SIZE, 2 * CMP_HIDDEN), lambda j, i: (j // g2, 0, 0, 0)),
                  pl.BlockSpec((1, dh, PAGE_SIZE), lambda j, i: (j // g2, 0, 0)),
                  pl.BlockSpec((1, 1, 2 * CMP_HIDDEN), lambda j, i: (j // g2, 0, 0)),
                  pl.BlockSpec((1, 2 * CMP_HIDDEN, LANES), lambda j, i: (j // g2, 0, 0))],
        out_specs=pl.BlockSpec((bp, 1, LANES), lambda j, i: (i, 0, j)),
        out_shape=jax.ShapeDtypeStruct((n_phys, 1, 2 * g2 * LANES), F32),
        compiler_params=_cp(("parallel", "parallel")),
        name="compress_pages",
    )(rows, wd, ped, b1h.reshape(2, 1, 2 * CMP_HIDDEN), w2h)


def _compress_page_weights(cmp_pe, cmp_w1, cmp_b1, cmp_w2):
    w1t = jnp.transpose(cmp_w1.reshape(2, CMP_BLOCK, HEAD_DIM, CMP_HIDDEN), (0, 2, 1, 3))
    z = jnp.zeros_like(w1t)
    wd = jnp.concatenate([jnp.concatenate([w1t, z], -1), jnp.concatenate([z, w1t], -1)], axis=2).astype(BF16)
    wd = wd.reshape(2, HEAD_DIM // 2, 2 * PAGE_SIZE, 2 * CMP_HIDDEN)
    pet = jnp.transpose(cmp_pe, (0, 2, 1))
    ped = jnp.concatenate([pet, pet], -1)
    b1h = jnp.concatenate([cmp_b1, cmp_b1], -1)
    z2 = jnp.zeros_like(cmp_w2)
    w2h = jnp.concatenate([jnp.concatenate([cmp_w2, z2], -1), jnp.concatenate([z2, cmp_w2], -1)], axis=1).astype(BF16)
    return wd, ped, b1h, w2h


def _split_w_in(w):
    d = w.shape[0]
    off_b = DN_CONV_DIM + DN_WIDTH
    off_q = off_b + 2 * DN_HEADS
    off_kv = off_q + NSA_WIDTH
    off_g = off_kv + KV_COLS
    n_g = 3 * NSA_HEADS
    pad = jnp.zeros((d, P_COLS - P_SMALL - 2 * DN_HEADS - n_g), w.dtype)
    main = jnp.concatenate([w[:, :off_b], w[:, off_q:off_kv], w[:, off_kv:off_kv + CMP_COLS], w[:, off_b:off_q],
                            w[:, off_g:off_g + n_g], pad], axis=1)
    return main.astype(BF16), w[:, off_kv:off_g].T.astype(BF16)


def _row_tile(n, cap):
    t = min(n, cap)
    while n % t:
        t //= 2
    return t


def _trunk_tail(x2, mixer_dn, mixer_nsa, mem_kv3, bshape, lw, final_norm):
    n, d = x2.shape
    b, t = bshape
    if t >= SUBLANES:
        x3 = mem_block(x2.reshape(b, t, d), mixer_dn.reshape(b, t, -1), mixer_nsa.reshape(b, t, -1),
                       lw["w_out_dn"], lw["w_out_nsa"], lw["ln_mem"], lw["w_mem_q"], mem_kv3, lw["w_mem_o"],
                       _row_tile(t, 512))
        x2 = x3.reshape(n, d)
    else:
        assert t == 1
        tm = _row_tile(n, 512)
        x2 = matmul_residual(x2, [mixer_dn, mixer_nsa], [lw["w_out_dn"], lw["w_out_nsa"]], tm)
        qm = rms_matmul(x2, lw["ln_mem"], lw["w_mem_q"], tm, 512)
        att = mem_attention_row(qm.reshape(n, MEM_HEADS, d // MEM_HEADS), mem_kv3).reshape(n, d)
        x2 = matmul_residual(x2, [att], [lw["w_mem_o"]], tm)
    return ffn(x2, lw["ln_ffn"], lw["w_up"], lw["w_down"], lw["ln_final"], final_norm, _row_tile(n, 1024), 512)


def _kv_rows(kvt):
    b, rows, s = kvt.shape
    g, dh = NSA_KV_HEADS, HEAD_DIM
    return jnp.transpose(kvt.reshape(b, rows // (g * dh), g, dh, s), (0, 4, 1, 2, 3))


def _prompt_layer(xp, mem_prompt, lw, cw, final_norm):
    b, s, d = xp.shape
    n = b * s
    x2 = xp.reshape(n, d)
    proj, kvt, kvt_win = input_projection(xp, lw["ln_mix"], lw["w_in"], lw["w_kvt"], _row_tile(s, 1024), 1024)
    proj3 = proj.reshape(b, s, P_COLS)
    dn_out, s_pairs = deltanet_prompt(proj3, lw["dn_conv_w"], lw["dn_a_log"], lw["dn_dt_bias"], lw["dn_norm"])
    p_conv = proj3[:, s - (CONV_WIDTH - 1):, P_QKV:P_QKV + DN_CONV_DIM]
    p_rec = _pairs_to_heads(s_pairs)
    nb = s // CMP_BLOCK
    kcv = compress_blocks(proj3[:, :nb * CMP_BLOCK].reshape(b * nb, CMP_BLOCK, P_COLS), P_CMP, cw,
                          _row_tile(b * nb, 256))
    kcv = kcv.reshape(b, nb, 2, NSA_KV_HEADS, HEAD_DIM)
    kc = jnp.transpose(kcv[:, :, 0], (0, 2, 1, 3))
    vct = jnp.transpose(kcv[:, :, 1], (0, 2, 3, 1))
    ocmpt, selt = nsa_cmp_topk(proj3, kc, vct, _row_tile(s, 512))
    nsa_out = nsa_attention(proj3, kvt, kvt_win, selt, ocmpt, 256, 512).reshape(n, NSA_WIDTH)
    m = mem_prompt.shape[1]
    mem_kv = rms_matmul(mem_prompt.reshape(b * m, d), lw["ln_memkv"], lw["w_mem_kv"], _row_tile(b * m, 512), 512)
    mem_kv5 = mem_kv.reshape(b, m, 2, MEM_HEADS, d // MEM_HEADS)
    y = _trunk_tail(x2, dn_out.reshape(n, DN_WIDTH), nsa_out, mem_kv.reshape(b, m, 2 * d), (b, s), lw, final_norm)
    wk = min(WINDOW, s)
    return y.reshape(b, s, d), _kv_rows(kvt), _kv_rows(kvt_win[:, :, s - wk:]), mem_kv5, p_conv, p_rec


def _sample_layer(xs, cache_nsa, cache_win, cache_mem, conv_state, rec_state, page_table, lw, cwp, final_norm):
    db, ds, d = xs.shape
    assert ds == 1
    n = db
    x2 = xs.reshape(n, d)
    proj, kvt, kvt_win = input_projection(x2[None], lw["ln_mix"], lw["w_in"], lw["w_kvt"], n, 1024)
    dn_out, s_rec = deltanet_sample(proj, conv_state, rec_state, lw["dn_conv_w"], lw["dn_a_log"], lw["dn_dt_bias"],
                                    lw["dn_norm"])
    s_conv = jnp.concatenate([conv_state[:, 1:], proj[:, None, P_QKV:P_QKV + DN_CONV_DIM]], axis=1)
    n_phys = cache_nsa.shape[0]
    cache_t = jnp.transpose(cache_nsa, (0, 2, 3, 4, 1))
    win_t = jnp.transpose(cache_win, (0, 2, 3, 4, 1))
    kc_phys = compress_pages(cache_t, cwp, _row_tile(n_phys, 256))
    kv_new = jnp.concatenate([kvt[0], kvt_win[0]], axis=0).T
    o8, s_win_t = nsa_decode(page_table, proj[:, P_NQ:P_NQ + NSA_WIDTH].reshape(n, NSA_HEADS, HEAD_DIM),
                             kv_new.reshape(n, 1, KV_COLS),
                             _flat_gates(proj[:, P_SMALL:P_SMALL + LANES]).reshape(n, 1, LANES),
                             win_t, kvt_win[0], kc_phys, cache_t)
    s_nsa = jnp.transpose(kvt.reshape(-1, NSA_KV_HEADS, HEAD_DIM, n), (3, 0, 1, 2))[:, None]
    s_win = jnp.transpose(s_win_t, (0, 4, 1, 2, 3))
    y = _trunk_tail(x2, dn_out, o8.reshape(n, NSA_WIDTH), cache_mem, (n, 1), lw, final_norm)
    return y.reshape(db, ds, d), s_nsa, s_win, s_conv, s_rec


def _flat_gates(small):
    g0 = 2 * DN_HEADS
    n_g = 3 * NSA_HEADS
    gl = small[..., g0:g0 + n_g]
    return jnp.concatenate([gl, jnp.zeros(gl.shape[:-1] + (LANES - n_g,), gl.dtype)], axis=-1)


def kernel(x_prompt, x_sample, mem_prompt, cache_nsa_kv, cache_win_kv, cache_mem_kv, state_dn_conv, state_dn_rec, page_table, ln_mix, w_in, dn_conv_w, dn_a_log, dn_dt_bias, dn_norm, cmp_pe, cmp_w1, cmp_b1, cmp_w2, w_out, ln_mem, ln_memkv, w_mem_q, w_mem_kv, w_mem_o, ln_ffn, w_up, w_down, ln_final):
    depth = w_in.shape[0]
    xp, xs = x_prompt, x_sample
    outs_p = [[] for _ in range(5)]
    outs_s = [[] for _ in range(4)]
    for l in range(depth):
        lw = {
            "ln_mix": ln_mix[l],
            "dn_conv_w": dn_conv_w[l], "dn_a_log": dn_a_log[l], "dn_dt_bias": dn_dt_bias[l], "dn_norm": dn_norm[l],
            "w_out_dn": w_out[l][:DN_WIDTH].astype(BF16), "w_out_nsa": w_out[l][DN_WIDTH:].astype(BF16),
            "ln_mem": ln_mem[l], "ln_memkv": ln_memkv[l], "w_mem_q": w_mem_q[l].astype(BF16),
            "w_mem_kv": w_mem_kv[l].astype(BF16), "w_mem_o": w_mem_o[l].astype(BF16),
            "ln_ffn": ln_ffn[l], "w_up": w_up[l].astype(BF16), "w_down": w_down[l].astype(BF16),
            "ln_final": ln_final,
        }
        lw["w_in"], lw["w_kvt"] = _split_w_in(w_in[l])
        cw = _compress_weights(cmp_pe[l], cmp_w1[l], cmp_b1[l], cmp_w2[l])
        last = l == depth - 1
        xp, p_nsa, p_win, p_mem, p_conv, p_rec = _prompt_layer(xp, mem_prompt, lw, cw, last)
        for acc, val in zip(outs_p, (p_nsa, p_win, p_mem, p_conv, p_rec)):
            acc.append(val)
        xs, s_nsa, s_win, s_conv, s_rec = _sample_layer(
            xs, cache_nsa_kv[l], cache_win_kv[l], cache_mem_kv[l], state_dn_conv[l], state_dn_rec[l],
            page_table, lw, _compress_page_weights(cmp_pe[l], cmp_w1[l], cmp_b1[l], cmp_w2[l]), last)
        for acc, val in zip(outs_s, (s_nsa, s_win, s_conv, s_rec)):
            acc.append(val)
    return (xp, xs) + tuple(jnp.stack(a) for a in outs_p) + tuple(jnp.stack(a) for a in outs_s)
```

```python
import functools

import jax
import jax.numpy as jnp
from jax import lax
from jax.experimental import pallas as pl
from jax.experimental.pallas import tpu as pltpu

F32 = jnp.float32
BF16 = jnp.bfloat16

HEAD_DIM = 64
DN_HEADS = 8
NSA_HEADS = 8
NSA_KV_HEADS = 2
NSA_GROUP = NSA_HEADS // NSA_KV_HEADS
DN_WIDTH = DN_HEADS * HEAD_DIM
NSA_WIDTH = NSA_HEADS * HEAD_DIM
CONV_WIDTH = 4
DN_CONV_DIM = 3 * DN_WIDTH
DN_CHUNK = 64
CMP_BLOCK = 64
SEL_BLOCK = 64
TOP_N = 16
N_LOCAL_BLOCKS = 2
WINDOW = 512
CMP_HIDDEN = 128
NSA_KV_KINDS = 6
MEM_HEADS = 4
PAGE_SIZE = 128
RMS_EPS = 1e-6
FORCE_SCORE = 1e3
NEG_INF = -1e30
ATTN_SCALE = HEAD_DIM ** -0.5
LOG2_E = 1.4426950408889634

LANES = 128
SUBLANES = 8
P_QKV = 0
P_Z = P_QKV + DN_CONV_DIM
P_NQ = P_Z + DN_WIDTH
P_CMP = P_NQ + NSA_WIDTH
CMP_COLS = 2 * NSA_KV_HEADS * HEAD_DIM
P_SMALL = P_CMP + CMP_COLS
P_COLS = P_SMALL + 2 * LANES
KV_COLS = NSA_KV_KINDS * NSA_KV_HEADS * HEAD_DIM
VMEM_LIMIT = 56 * 1024 * 1024
DN_APPLY_PASSES = (3, 3, 3, 1, 1, 1)
DN_SQUARE_PASSES = (3, 3, 1, 1, 1)


def _cp(sem, vmem=VMEM_LIMIT):
    return pltpu.CompilerParams(dimension_semantics=sem, vmem_limit_bytes=vmem)


def _split2(a):
    hi = a.astype(BF16)
    return hi, (a - hi.astype(F32)).astype(BF16)


def _dot(a, b, passes=1):
    if passes == 1:
        return jnp.dot(a.astype(BF16), b.astype(BF16), preferred_element_type=F32)
    ah, al = _split2(a)
    bh, bl = _split2(b)
    return (jnp.dot(ah, bh, preferred_element_type=F32) + jnp.dot(ah, bl, preferred_element_type=F32)
            + jnp.dot(al, bh, preferred_element_type=F32))


def _dot_nt(a, b):
    return lax.dot_general(a.astype(BF16), b.astype(BF16), (((1,), (1,)), ((), ())),
                           preferred_element_type=F32)


def _split3(a):
    hi = a.astype(BF16)
    r1 = a - hi.astype(F32)
    mid = r1.astype(BF16)
    lo = (r1 - mid.astype(F32)).astype(BF16)
    return hi, mid, lo


def _dot_exact_lhs01(a01, b):
    a = a01.astype(BF16)
    hi, mid, lo = _split3(b)
    return (jnp.dot(a, hi, preferred_element_type=F32) + jnp.dot(a, mid, preferred_element_type=F32)
            + jnp.dot(a, lo, preferred_element_type=F32))


def _dot_exact_rhs01(a, b01):
    b = b01.astype(BF16)
    hi, mid, lo = _split3(a)
    return (jnp.dot(hi, b, preferred_element_type=F32) + jnp.dot(mid, b, preferred_element_type=F32)
            + jnp.dot(lo, b, preferred_element_type=F32))


def _rms(x, gain):
    ms = jnp.mean(x * x, axis=-1, keepdims=True)
    return x * lax.rsqrt(ms + RMS_EPS) * gain


def _softplus(x):
    return jnp.maximum(x, 0.0) + jnp.log1p(jnp.exp(-jnp.abs(x)))


def _silu(x):
    return x * jax.nn.sigmoid(x)


def _rms_mm_kernel(x_ref, g_ref, w_ref, o_ref, h_ref):
    @pl.when(pl.program_id(1) == 0)
    def _():
        h_ref[...] = _rms(x_ref[...], g_ref[...]).astype(BF16)

    o_ref[...] = jnp.dot(h_ref[...], w_ref[...], preferred_element_type=F32)


def rms_matmul(x, gain, w_bf16, tm, tn):
    n, d = x.shape
    m = w_bf16.shape[1]
    assert n % tm == 0 and m % tn == 0
    return pl.pallas_call(
        _rms_mm_kernel,
        grid=(n // tm, m // tn),
        in_specs=[pl.BlockSpec((tm, d), lambda i, j: (i, 0)),
                  pl.BlockSpec((1, d), lambda i, j: (0, 0)),
                  pl.BlockSpec((d, tn), lambda i, j: (0, j))],
        out_specs=pl.BlockSpec((tm, tn), lambda i, j: (i, j)),
        out_shape=jax.ShapeDtypeStruct((n, m), F32),
        scratch_shapes=[pltpu.VMEM((tm, d), BF16)],
        compiler_params=_cp(("parallel", "arbitrary")),
        name="rms_matmul",
    )(x, gain.reshape(1, d), w_bf16)


def _proj_kernel(x_ref, g_ref, w_ref, wkv_ref, o_ref, okv_ref, owin_ref, h_ref):
    @pl.when(pl.program_id(1) == 0)
    def _():
        h = _rms(x_ref[...], g_ref[...]).astype(BF16)
        h_ref[...] = h
        kvt = lax.dot_general(wkv_ref[...], h, (((1,), (1,)), ((), ())), preferred_element_type=F32)
        split = okv_ref.shape[1]
        okv_ref[0] = kvt[0:split]
        owin_ref[0] = kvt[split:]

    o_ref[...] = jnp.dot(h_ref[...], w_ref[...], preferred_element_type=F32)


def input_projection(x3, gain, w_main, w_kvt, tm, tn):
    b, s, d = x3.shape
    n = b * s
    m = w_main.shape[1]
    kvc = w_kvt.shape[0]
    win = 2 * NSA_KV_HEADS * HEAD_DIM
    assert s % tm == 0 and m % tn == 0
    spt = s // tm
    return pl.pallas_call(
        _proj_kernel,
        grid=(n // tm, m // tn),
        in_specs=[pl.BlockSpec((tm, d), lambda i, j: (i, 0)),
                  pl.BlockSpec((1, d), lambda i, j: (0, 0)),
                  pl.BlockSpec((d, tn), lambda i, j: (0, j)),
                  pl.BlockSpec((kvc, d), lambda i, j: (0, 0))],
        out_specs=[pl.BlockSpec((tm, tn), lambda i, j: (i, j)),
                   pl.BlockSpec((1, kvc - win, tm), lambda i, j: (i // spt, 0, i % spt)),
                   pl.BlockSpec((1, win, tm), lambda i, j: (i // spt, 0, i % spt))],
        out_shape=[jax.ShapeDtypeStruct((n, m), F32), jax.ShapeDtypeStruct((b, kvc - win, s), F32),
                   jax.ShapeDtypeStruct((b, win, s), F32)],
        scratch_shapes=[pltpu.VMEM((tm, d), BF16)],
        compiler_params=_cp(("parallel", "arbitrary")),
        name="input_projection",
    )(x3.reshape(n, d), gain.reshape(1, d), w_main, w_kvt)


def _mm_res_kernel(*refs, n_in):
    res_ref = refs[0]
    a_refs = refs[1:1 + n_in]
    w_refs = refs[1 + n_in:1 + 2 * n_in]
    o_ref = refs[1 + 2 * n_in]
    acc = res_ref[...]
    for a_ref, w_ref in zip(a_refs, w_refs):
        acc = acc + jnp.dot(a_ref[...].astype(BF16), w_ref[...], preferred_element_type=F32)
    o_ref[...] = acc


def matmul_residual(res, a_list, w_list, tm):
    n, d = res.shape
    assert n % tm == 0
    n_in = len(a_list)
    in_specs = [pl.BlockSpec((tm, d), lambda i: (i, 0))]
    in_specs += [pl.BlockSpec((tm, a.shape[1]), lambda i: (i, 0)) for a in a_list]
    in_specs += [pl.BlockSpec(w.shape, lambda i: (0, 0)) for w in w_list]
    return pl.pallas_call(
        functools.partial(_mm_res_kernel, n_in=n_in),
        grid=(n // tm,),
        in_specs=in_specs,
        out_specs=pl.BlockSpec((tm, d), lambda i: (i, 0)),
        out_shape=jax.ShapeDtypeStruct((n, d), F32),
        compiler_params=_cp(("parallel",)),
        name="matmul_residual",
    )(res, *a_list, *w_list)


def _ffn_kernel(x_ref, g_ref, wu_ref, wd_ref, gf_ref, o_ref, hn_ref, acc_ref, *, final_norm):
    j = pl.program_id(1)

    @pl.when(j == 0)
    def _():
        x = x_ref[...]
        hn_ref[...] = _rms(x, g_ref[...]).astype(BF16)
        acc_ref[...] = x

    u = jnp.dot(hn_ref[...], wu_ref[...], preferred_element_type=F32)
    u = jnp.square(jnp.maximum(u, 0.0)).astype(BF16)
    acc_ref[...] += jnp.dot(u, wd_ref[...], preferred_element_type=F32)

    @pl.when(j == pl.num_programs(1) - 1)
    def _():
        y = acc_ref[...]
        if final_norm:
            y = _rms(y, gf_ref[...])
        o_ref[...] = y


def ffn(x, gain, wu_bf16, wd_bf16, gain_final, final_norm, tm, tf):
    n, d = x.shape
    f = wu_bf16.shape[1]
    assert n % tm == 0 and f % tf == 0
    return pl.pallas_call(
        functools.partial(_ffn_kernel, final_norm=final_norm),
        grid=(n // tm, f // tf),
        in_specs=[pl.BlockSpec((tm, d), lambda i, j: (i, 0)),
                  pl.BlockSpec((1, d), lambda i, j: (0, 0)),
                  pl.BlockSpec((d, tf), lambda i, j: (0, j)),
                  pl.BlockSpec((tf, d), lambda i, j: (j, 0)),
                  pl.BlockSpec((1, d), lambda i, j: (0, 0))],
        out_specs=pl.BlockSpec((tm, d), lambda i, j: (i, 0)),
        out_shape=jax.ShapeDtypeStruct((n, d), F32),
        scratch_shapes=[pltpu.VMEM((tm, d), BF16), pltpu.VMEM((tm, d), F32)],
        compiler_params=_cp(("parallel", "arbitrary")),
        name="ffn",
    )(x, gain.reshape(1, d), wu_bf16, wd_bf16, gain_final.reshape(1, d))


def _mem_attn_row_kernel(q_ref, kv_ref, o_ref, *, m, seqs, heads):
    sub = SUBLANES
    ones = jnp.ones((LANES, LANES), BF16)
    for u in range(seqs):
        x = kv_ref[u].reshape(m, 2 * sub, LANES)
        prod = x[:, 0:sub] * q_ref[u][None]
        part = jnp.dot(prod.reshape(m * sub, LANES).astype(BF16), ones, preferred_element_type=F32)
        part = part.reshape(m, sub, LANES)
        s = part + pltpu.roll(part, heads, 1)
        p = jnp.exp2(s - jnp.max(s, axis=0, keepdims=True))
        o_ref[u] = jnp.sum(p * x[:, sub:2 * sub], axis=0) / jnp.sum(p, axis=0)


def mem_attention_row(q, kv, seqs=8):
    b, h, hd = q.shape
    m = kv.shape[1]
    assert kv.shape[2:] == (2, h, hd)
    tiles = hd // LANES
    assert tiles * LANES == hd and tiles == 2 and tiles * h == SUBLANES and b % seqs == 0
    rows = jnp.transpose(kv.reshape(b, m, 2, h, tiles, LANES), (0, 1, 2, 4, 3, 5)).reshape(b, m * 2 * SUBLANES, LANES)
    q8 = jnp.transpose((q * (hd ** -0.5 * LOG2_E)).reshape(b, h, tiles, LANES), (0, 2, 1, 3)).reshape(b, SUBLANES, LANES)
    o8 = pl.pallas_call(
        functools.partial(_mem_attn_row_kernel, m=m, seqs=seqs, heads=h),
        grid=(b // seqs,),
        in_specs=[pl.BlockSpec((seqs, SUBLANES, LANES), lambda i: (i, 0, 0)),
                  pl.BlockSpec((seqs, m * 2 * SUBLANES, LANES), lambda i: (i, 0, 0))],
        out_specs=pl.BlockSpec((seqs, SUBLANES, LANES), lambda i: (i, 0, 0)),
        out_shape=jax.ShapeDtypeStruct((b, SUBLANES, LANES), F32),
        compiler_params=_cp(("parallel",)),
        name="mem_attention_row",
    )(q8, rows)
    return jnp.transpose(o8.reshape(b, tiles, h, LANES), (0, 2, 1, 3)).reshape(b, h, hd)


def _mem_block_kernel(x_ref, a1_ref, a2_ref, w1_ref, w2_ref, g_ref, wq_ref, kv_ref, wo_ref, o_ref, *, heads, hd):
    scale = hd ** -0.5
    x = (x_ref[0] + jnp.dot(a1_ref[0].astype(BF16), w1_ref[...], preferred_element_type=F32)
         + jnp.dot(a2_ref[0].astype(BF16), w2_ref[...], preferred_element_type=F32))
    q = jnp.dot(_rms(x, g_ref[...]).astype(BF16), wq_ref[...], preferred_element_type=F32)
    d = heads * hd
    outs = []
    for h in range(heads):
        s = _dot_nt(q[:, h * hd:(h + 1) * hd], kv_ref[0, :, h * hd:(h + 1) * hd]) * scale
        p = jnp.exp(s - jnp.max(s, axis=-1, keepdims=True))
        p = p / jnp.sum(p, axis=-1, keepdims=True)
        outs.append(_dot(p, kv_ref[0, :, d + h * hd:d + (h + 1) * hd]))
    att = jnp.concatenate(outs, axis=1).astype(BF16)
    o_ref[0] = x + jnp.dot(att, wo_ref[...], preferred_element_type=F32)


def mem_block(x3, a1, a2, w1, w2, gain, wq, kv, wo, tq):
    b, t, d = x3.shape
    m = kv.shape[1]
    hd = d // MEM_HEADS
    assert t % tq == 0 and kv.shape[2] == 2 * d
    return pl.pallas_call(
        functools.partial(_mem_block_kernel, heads=MEM_HEADS, hd=hd),
        grid=(b, t // tq),
        in_specs=[pl.BlockSpec((1, tq, d), lambda i, j: (i, j, 0)),
                  pl.BlockSpec((1, tq, a1.shape[2]), lambda i, j: (i, j, 0)),
                  pl.BlockSpec((1, tq, a2.shape[2]), lambda i, j: (i, j, 0)),
                  pl.BlockSpec(w1.shape, lambda i, j: (0, 0)),
                  pl.BlockSpec(w2.shape, lambda i, j: (0, 0)),
                  pl.BlockSpec((1, d), lambda i, j: (0, 0)),
                  pl.BlockSpec((d, d), lambda i, j: (0, 0)),
                  pl.BlockSpec((1, m, 2 * d), lambda i, j: (i, 0, 0)),
                  pl.BlockSpec((d, d), lambda i, j: (0, 0))],
        out_specs=pl.BlockSpec((1, tq, d), lambda i, j: (i, j, 0)),
        out_shape=jax.ShapeDtypeStruct((b, t, d), F32),
        compiler_params=_cp(("parallel", "parallel")),
        name="mem_block",
    )(x3, a1, a2, w1, w2, gain.reshape(1, d), wq, kv, wo)


def _compress_kernel(x_ref, w1_ref, pe_ref, b1_ref, w2_ref, o_ref, acc_ref, *, tt):
    j = pl.program_id(1)

    @pl.when(j == 0)
    def _():
        acc_ref[...] = jnp.zeros_like(acc_ref)

    hid2 = 2 * CMP_HIDDEN
    xt = jnp.swapaxes(x_ref[...], 0, 1)
    for kind in range(2):
        part = None
        for t in range(tt):
            xs = xt[t][:, kind * LANES:(kind + 1) * LANES] + pe_ref[kind, t:t + 1, :]
            d = jnp.dot(xs.astype(BF16), w1_ref[kind, t], preferred_element_type=F32)
            part = d if part is None else part + d
        acc_ref[:, kind * hid2:(kind + 1) * hid2] += part

    @pl.when(j == pl.num_programs(1) - 1)
    def _():
        h = jnp.maximum(acc_ref[...] + b1_ref[...], 0.0)
        for kind in range(2):
            o_ref[:, kind * LANES:(kind + 1) * LANES] = jnp.dot(
                h[:, kind * hid2:(kind + 1) * hid2].astype(BF16), w2_ref[kind], preferred_element_type=F32)


def compress_blocks(x3, col0, cw, bt, tt=8):
    nb = x3.shape[0]
    assert nb % bt == 0 and CMP_BLOCK % tt == 0 and col0 % CMP_COLS == 0 and x3.shape[1] == CMP_BLOCK
    w1bd, pe2, b1bd, w2bd = cw
    return pl.pallas_call(
        functools.partial(_compress_kernel, tt=tt),
        grid=(nb // bt, CMP_BLOCK // tt),
        in_specs=[pl.BlockSpec((bt, tt, CMP_COLS), lambda i, j: (i, j, col0 // CMP_COLS)),
                  pl.BlockSpec((2, tt, LANES, 2 * CMP_HIDDEN), lambda i, j: (0, j, 0, 0)),
                  pl.BlockSpec((2, tt, LANES), lambda i, j: (0, j, 0)),
                  pl.BlockSpec((1, 4 * CMP_HIDDEN), lambda i, j: (0, 0)),
                  pl.BlockSpec((2, 2 * CMP_HIDDEN, LANES), lambda i, j: (0, 0, 0))],
        out_specs=pl.BlockSpec((bt, 2 * LANES), lambda i, j: (i, 0)),
        out_shape=jax.ShapeDtypeStruct((nb, 2 * LANES), F32),
        scratch_shapes=[pltpu.VMEM((bt, 4 * CMP_HIDDEN), F32)],
        compiler_params=_cp(("parallel", "arbitrary")),
        name="compress_blocks",
    )(x3, w1bd, pe2, b1bd, w2bd)


def _compress_weights(cmp_pe, cmp_w1, cmp_b1, cmp_w2):
    w1r = cmp_w1.reshape(2, CMP_BLOCK, HEAD_DIM, CMP_HIDDEN)
    z = jnp.zeros_like(w1r)
    w1bd = jnp.concatenate([jnp.concatenate([w1r, z], -1), jnp.concatenate([z, w1r], -1)], axis=2).astype(BF16)
    pe2 = jnp.concatenate([cmp_pe, cmp_pe], -1)
    b1bd = jnp.concatenate([cmp_b1[0], cmp_b1[0], cmp_b1[1], cmp_b1[1]]).reshape(1, 4 * CMP_HIDDEN)
    z2 = jnp.zeros_like(cmp_w2)
    w2bd = jnp.concatenate([jnp.concatenate([cmp_w2, z2], -1), jnp.concatenate([z2, cmp_w2], -1)], axis=1).astype(BF16)
    return w1bd, pe2, b1bd, w2bd


def _masked_softmax(s, mask):
    s = jnp.where(mask, s, NEG_INF)
    p = jnp.where(mask, jnp.exp(s - jnp.max(s, axis=-1, keepdims=True)), 0.0)
    return p / jnp.maximum(jnp.sum(p, axis=-1, keepdims=True), 1e-30)


def _heads_as_rows(q_ref, tq, scale=ATTN_SCALE):
    qb = q_ref[0] * scale
    return jnp.concatenate([qb[:, h * HEAD_DIM:(h + 1) * HEAD_DIM] for h in range(NSA_GROUP)], axis=0).astype(BF16)


def _cmp_topk_kernel(q_ref, kc_ref, vct_ref, ocmp_ref, sel_ref, *, tq, nblk, topn):
    qi = pl.program_id(2)
    r, dh = NSA_GROUP, HEAD_DIM
    qpos = qi * tq + lax.broadcasted_iota(jnp.int32, (nblk, tq), 1)
    blk = lax.broadcasted_iota(jnp.int32, (nblk, tq), 0)
    vis = (blk + 1) * CMP_BLOCK - 1 <= qpos
    q = _heads_as_rows(q_ref, tq)
    kc = kc_ref[0, 0]
    vct = vct_ref[0, 0]
    imp = jnp.zeros((nblk, tq), F32)
    for h in range(r):
        s = jnp.where(vis, _dot_nt(kc, q[h * tq:(h + 1) * tq]), NEG_INF)
        p = jnp.where(vis, jnp.exp(s - jnp.max(s, axis=0, keepdims=True)), 0.0)
        p = p / jnp.maximum(jnp.sum(p, axis=0, keepdims=True), 1e-30)
        ocmp_ref[0, 0, h * dh:(h + 1) * dh, :] = _dot(vct, p)
        imp = imp + p
    cur = lax.shift_right_logical(qpos, 6)
    valid = blk <= cur
    forced = valid & ((blk == 0) | (cur - blk < N_LOCAL_BLOCKS))
    score = jnp.where(valid, imp + jnp.where(forced, FORCE_SCORE, 0.0), -1.0)
    rank = jnp.zeros((nblk, tq), F32)
    row8 = lax.broadcasted_iota(jnp.int32, (SUBLANES, tq), 0)
    for i in range(nblk):
        si = score[i:i + 1, :]
        t0 = i // SUBLANES * SUBLANES
        mid = score[t0:t0 + SUBLANES]
        parts = [jnp.where(row8 > i - t0, jnp.where(si >= mid, 1.0, 0.0), jnp.where(si > mid, 1.0, 0.0))]
        if t0 > 0:
            parts.insert(0, jnp.where(si > score[0:t0], 1.0, 0.0))
        if t0 + SUBLANES < nblk:
            parts.append(jnp.where(si >= score[t0 + SUBLANES:], 1.0, 0.0))
        rank = rank + jnp.concatenate(parts, axis=0)
    sel_ref[0, 0] = jnp.where(rank < topn, 1.0, 0.0).astype(BF16)


def nsa_cmp_topk(proj3, kc, vct, tq):
    b, sq, _ = proj3.shape
    g = kc.shape[1]
    nblk = kc.shape[2]
    dh = HEAD_DIM
    gw = NSA_GROUP * dh
    assert sq % tq == 0 and SEL_BLOCK == 64 and P_NQ % gw == 0 and nblk % SUBLANES == 0
    return pl.pallas_call(
        functools.partial(_cmp_topk_kernel, tq=tq, nblk=nblk, topn=min(TOP_N, nblk)),
        grid=(b, g, sq // tq),
        in_specs=[pl.BlockSpec((1, tq, gw), lambda i, j, k: (i, k, P_NQ // gw + j)),
                  pl.BlockSpec((1, 1, nblk, dh), lambda i, j, k: (i, j, 0, 0)),
                  pl.BlockSpec((1, 1, dh, nblk), lambda i, j, k: (i, j, 0, 0))],
        out_specs=[pl.BlockSpec((1, 1, gw, tq), lambda i, j, k: (i, j, 0, k)),
                   pl.BlockSpec((1, 1, nblk, tq), lambda i, j, k: (i, j, 0, k))],
        out_shape=[jax.ShapeDtypeStruct((b, g, gw, sq), F32),
                   jax.ShapeDtypeStruct((b, g, nblk, sq), BF16)],
        compiler_params=_cp(("parallel", "parallel", "parallel")),
        name="nsa_cmp_topk",
    )(proj3, kc, vct)


def _nsa_attn_kernel(q_ref, kst_ref, vst_ref, kwt_ref, vwt_ref, selt_ref, ocmp_ref, sm_ref, o_ref,
                     m_sc, acc_sc, s_sc, bias_sc, p_sc, *, tq, tk, nsel, ck):
    grp = pl.program_id(1)
    qi = pl.program_id(2)
    r = NSA_GROUP
    dh = HEAD_DIM
    q = _heads_as_rows(q_ref, tq, ATTN_SCALE * LOG2_E)
    qpos = qi * tq + lax.broadcasted_iota(jnp.int32, (1, tq), 1)
    ones_row = jnp.where(lax.broadcasted_iota(jnp.int32, (8, 1), 0) == 0, 1.0, 0.0)

    def reset():
        m_sc[...] = jnp.full(m_sc.shape, NEG_INF, F32)
        acc_sc[...] = jnp.zeros(acc_sc.shape, F32)

    def with_ones(vt):
        return jnp.concatenate([vt, jnp.broadcast_to(ones_row, (8, vt.shape[1]))], axis=0).astype(BF16)

    def scores(kt, qrows, row0):
        s_sc[row0:row0 + kt.shape[1], :] = _dot_nt(kt.T, qrows)

    def set_bias(mask, row0):
        bias_sc[pl.ds(row0, mask.shape[0]), :] = jnp.where(mask, 0.0, NEG_INF)

    def fold(row0, n, vt, biased):
        def chunk(c, shift):
            rows = pl.ds(row0 + c * ck, ck)
            x = s_sc[rows, :] - shift
            return x + jnp.concatenate([bias_sc[rows, :]] * r, axis=1) if biased else x

        m_prev = m_sc[...]
        top = jnp.full((SUBLANES, r * tq), NEG_INF, F32)
        for c in range(n // ck):
            top = jnp.maximum(top, jnp.max(chunk(c, 0.0).reshape(ck // SUBLANES, SUBLANES, r * tq), axis=0))
        m_new = jnp.maximum(m_prev, jnp.max(top, axis=0, keepdims=True))
        for c in range(n // ck):
            p_sc[c * ck:(c + 1) * ck, :] = jnp.exp2(chunk(c, m_new)).astype(BF16)
        acc_sc[...] = jnp.exp2(m_prev - m_new) * acc_sc[...] + jnp.dot(with_ones(vt), p_sc[0:n, :],
                                                                      preferred_element_type=F32)
        m_sc[...] = m_new

    def result():
        acc = acc_sc[...]
        return acc[0:dh] / jnp.maximum(acc[dh:dh + 1], 1e-30)

    wk = WINDOW + tq
    w0 = pl.multiple_of(jnp.maximum(qi - WINDOW // tq, 0) * tq, tq)
    dpos = qpos - (w0 + lax.broadcasted_iota(jnp.int32, (wk, 1), 0))
    set_bias((dpos >= 0) & (dpos < WINDOW), 0)
    scores(kwt_ref[0, :, pl.ds(w0, wk)], q, 0)

    reset()
    block_bias = jnp.where(selt_ref[0, 0].astype(F32).T > 0.5, 0.0, NEG_INF).astype(BF16)
    q_sel = jnp.concatenate([q, jnp.concatenate([block_bias] * r, axis=0)], axis=1)
    blk = lax.broadcasted_iota(jnp.int32, (nsel, tk), 0)
    krow = lax.broadcasted_iota(jnp.int32, (nsel, tk), 1)

    def slc_scores(k0, row0):
        code = jnp.where(lax.shift_right_logical(k0 + krow, CMP_BLOCK.bit_length() - 1) == blk, 1.0, 0.0)
        scores(jnp.concatenate([kst_ref[0, :, pl.ds(k0, tk)], code], axis=0), q_sel, row0)

    def slc_fold(k0, row0, biased):
        fold(row0, tk, vst_ref[0, :, pl.ds(k0, tk)], biased)

    def slc_pair(c, carry):
        ka = pl.multiple_of(2 * c * tk, tk)
        slc_scores(ka + tk, wk + tk)
        slc_fold(ka, wk, False)
        slc_scores(ka + 2 * tk, wk)
        slc_fold(ka + tk, wk + tk, False)
        return carry

    n_below = (qi * tq) // tk
    kd = pl.multiple_of(n_below * tk, tk)
    odd = n_below % 2
    slc_scores(0, wk)
    lax.fori_loop(0, n_below // 2, slc_pair, 0)

    @pl.when(odd == 1)
    def _():
        slc_scores(kd, wk + tk)
        slc_fold(kd - tk, wk, False)

    row_d = pl.multiple_of(wk + odd * tk, ck)
    set_bias(kd + lax.broadcasted_iota(jnp.int32, (tk, 1), 0) <= qpos, row_d)
    slc_fold(kd, row_d, True)
    o_slc = result()

    reset()
    fold(0, wk, vwt_ref[0, :, pl.ds(w0, wk)], True)
    o_win = result()

    gates_t = jax.nn.sigmoid(sm_ref[0]).T
    g0 = 2 * DN_HEADS
    per = 3 * r
    gt = jnp.where(grp == 0, gates_t[g0:g0 + per], gates_t[g0 + per:g0 + 2 * per])
    outs = []
    for h in range(r):
        outs.append(gt[3 * h:3 * h + 1] * ocmp_ref[0, 0, h * dh:(h + 1) * dh, :]
                    + gt[3 * h + 1:3 * h + 2] * o_slc[:, h * tq:(h + 1) * tq]
                    + gt[3 * h + 2:3 * h + 3] * o_win[:, h * tq:(h + 1) * tq])
    o_ref[0] = jnp.concatenate(outs, axis=0).T


def nsa_attention(proj3, kvt, kvt_win, selt, ocmpt, tq, tk):
    b, sq, _ = proj3.shape
    t = kvt.shape[2]
    g, nsel = selt.shape[1], selt.shape[2]
    r, dh = NSA_GROUP, HEAD_DIM
    gw = r * dh
    assert sq == t and sq % tq == 0 and t % tk == 0 and tk % tq == 0 and WINDOW % tq == 0 and tq % LANES == 0
    assert t >= WINDOW + tq and g == NSA_KV_HEADS and g == 2
    ck = 32
    assert tk % ck == 0 and (WINDOW + tq) % ck == 0 and tk <= WINDOW + tq
    rows = WINDOW + tq + 2 * tk

    def kv_spec(kind):
        return pl.BlockSpec((1, dh, t), lambda i, j, k: (i, kind * g + j, 0))

    return pl.pallas_call(
        functools.partial(_nsa_attn_kernel, tq=tq, tk=tk, nsel=nsel, ck=ck),
        grid=(b, g, sq // tq),
        in_specs=[pl.BlockSpec((1, tq, gw), lambda i, j, k: (i, k, P_NQ // gw + j)),
                  kv_spec(2), kv_spec(3), kv_spec(0), kv_spec(1),
                  pl.BlockSpec((1, 1, nsel, tq), lambda i, j, k: (i, j, 0, k)),
                  pl.BlockSpec((1, 1, gw, tq), lambda i, j, k: (i, j, 0, k)),
                  pl.BlockSpec((1, tq, LANES), lambda i, j, k: (i, k, P_SMALL // LANES))],
        out_specs=pl.BlockSpec((1, tq, gw), lambda i, j, k: (i, k, j)),
        out_shape=jax.ShapeDtypeStruct((b, sq, g * gw), F32),
        scratch_shapes=[pltpu.VMEM((1, r * tq), F32), pltpu.VMEM((dh + 8, r * tq), F32),
                        pltpu.VMEM((rows, r * tq), F32), pltpu.VMEM((rows, tq), F32),
                        pltpu.VMEM((WINDOW + tq, r * tq), BF16)],
        compiler_params=_cp(("parallel", "parallel", "parallel")),
        name="nsa_attention",
    )(proj3, kvt, kvt, kvt_win, kvt_win, selt, ocmpt, proj3)


def _bdot(a, b, passes=1):
    dims = (((2,), (1,)), ((0,), (0,)))
    if passes == 1:
        return lax.dot_general(a.astype(BF16), b.astype(BF16), dims, preferred_element_type=F32)
    ah, al = _split2(a)
    bh, bl = _split2(b)
    return (lax.dot_general(ah, bh, dims, preferred_element_type=F32)
            + lax.dot_general(ah, bl, dims, preferred_element_type=F32)
            + lax.dot_general(al, bh, dims, preferred_element_type=F32))


def _bdot_nt(a, b):
    return lax.dot_general(a.astype(BF16), b.astype(BF16), (((2,), (2,)), ((0,), (0,))),
                           preferred_element_type=F32)


def _deltanet_kernel(qkv_ref, z_ref, sm_ref, cw_ref, alog_ref, dtb_ref, gn_ref, o_ref, s_out_ref,
                     xbuf, s_sc):
    c = pl.program_id(0)
    ch = DN_CHUNK
    n_pairs = DN_HEADS // 2
    two = 2 * ch
    n_batch = qkv_ref.shape[0]

    @pl.when(c == 0)
    def _():
        xbuf[:, 0:8, :] = jnp.zeros((n_batch, 8, DN_CONV_DIM), F32)
        s_sc[...] = jnp.zeros_like(s_sc)

    ti = lax.broadcasted_iota(jnp.int32, (ch, ch), 0)
    tj = lax.broadcasted_iota(jnp.int32, (ch, ch), 1)
    tri = jnp.where(ti >= tj, 1.0, 0.0)
    lane = lax.broadcasted_iota(jnp.int32, (ch, LANES), 1)
    lo = lane < HEAD_DIM
    row2 = lax.broadcasted_iota(jnp.int32, (two, two), 0)
    col2 = lax.broadcasted_iota(jnp.int32, (two, two), 1)
    same = (row2 >= ch) == (col2 >= ch)
    incl = (same & (row2 >= col2))[None]
    strict = (same & (row2 > col2))[None]
    top = lax.broadcasted_iota(jnp.int32, (two, 1), 0) < ch

    def seg_sum(x):
        s_lo = jnp.sum(jnp.where(lo, x, 0.0), axis=-1, keepdims=True)
        s_hi = jnp.sum(jnp.where(lo, 0.0, x), axis=-1, keepdims=True)
        return jnp.where(lo, s_lo, s_hi)

    def stack2(x):
        return jnp.concatenate([jnp.where(lo, x, 0.0), jnp.where(lo, 0.0, x)], axis=0)

    def col2x(a, b):
        return jnp.concatenate([jnp.broadcast_to(a, (ch, LANES)), jnp.broadcast_to(b, (ch, LANES))], axis=0)

    q_l, k_l, v_l, beta_l, gc_l, gl_l = [], [], [], [], [], []
    for bi in range(n_batch):
        xbuf[bi, 8:8 + ch, :] = qkv_ref[bi]
        conv = None
        for w in range(CONV_WIDTH):
            term = xbuf[bi, 5 + w:5 + w + ch, :] * cw_ref[w:w + 1, :]
            conv = term if conv is None else conv + term
        xbuf[bi, 0:8, :] = xbuf[bi, ch:ch + 8, :]
        act = _silu(conv)
        sm = sm_ref[bi]
        beta_all = jax.nn.sigmoid(sm)
        g_all = -jnp.exp(alog_ref[...]) * _softplus(sm + dtb_ref[...])
        gcum_all = _dot_exact_lhs01(tri, g_all)
        for p in range(n_pairs):
            c0 = p * LANES
            qp = act[:, c0:c0 + LANES]
            kp = act[:, DN_WIDTH + c0:DN_WIDTH + c0 + LANES]
            vp = act[:, 2 * DN_WIDTH + c0:2 * DN_WIDTH + c0 + LANES]
            qp = qp * lax.rsqrt(seg_sum(qp * qp) + 1e-6) * (HEAD_DIM ** -0.5)
            kp = kp * lax.rsqrt(seg_sum(kp * kp) + 1e-6)
            h0, h1 = DN_HEADS + 2 * p, DN_HEADS + 2 * p + 1
            q_l.append(stack2(qp))
            k_l.append(stack2(kp))
            v_l.append(stack2(vp))
            beta_l.append(col2x(beta_all[:, 2 * p:2 * p + 1], beta_all[:, 2 * p + 1:2 * p + 2]))
            gc_l.append(col2x(gcum_all[:, h0:h0 + 1], gcum_all[:, h1:h1 + 1]))
            gl_l.append(jnp.broadcast_to(jnp.where(top, gcum_all[ch - 1:ch, h0:h0 + 1], gcum_all[ch - 1:ch, h1:h1 + 1]),
                                         (two, LANES)))
    q2, k2, v2 = jnp.stack(q_l), jnp.stack(k_l), jnp.stack(v_l)
    beta2, gc2, gl2 = jnp.stack(beta_l), jnp.stack(gc_l), jnp.stack(gl_l)
    decay = jnp.exp(jnp.where(incl, gc2 - jnp.swapaxes(gc2, 1, 2), NEG_INF))
    kb2 = k2 * beta2
    a_mat = jnp.where(strict, _bdot_nt(kb2, k2) * decay, 0.0)
    aqk = jnp.where(incl, _bdot_nt(q2, k2) * decay, 0.0)
    s_old = s_sc[...]
    egc = jnp.exp(gc2)
    x = beta2 * (v2 - egc * _bdot(k2, s_old))
    pw = -a_mat
    n_lvl = ch.bit_length() - 1
    for lvl in range(n_lvl):
        x = x + _bdot(pw, x, DN_APPLY_PASSES[lvl])
        if lvl + 1 < n_lvl:
            pw = _bdot(pw, pw, DN_SQUARE_PASSES[lvl])
    o2 = _bdot(q2 * egc, s_old) + _bdot(aqk, x)
    kdec = k2 * jnp.exp(gl2 - gc2)
    s_sc[...] = s_old * jnp.exp(gl2) + _bdot(jnp.swapaxes(kdec, 1, 2), x)
    for bi in range(n_batch):
        for p in range(n_pairs):
            c0 = p * LANES
            o_n = o2[bi * n_pairs + p]
            o_pair = o_n[0:ch] + o_n[ch:two]
            inv = lax.rsqrt(seg_sum(o_pair * o_pair) * (1.0 / HEAD_DIM) + RMS_EPS)
            o_ref[bi, :, c0:c0 + LANES] = o_pair * inv * gn_ref[...] * _silu(z_ref[bi, :, c0:c0 + LANES])

    @pl.when(c == pl.num_programs(0) - 1)
    def _():
        s_out_ref[...] = s_sc[...]


def deltanet_prompt(proj3, conv_w, a_log, dt_bias, norm_gain):
    b, t, _ = proj3.shape
    ch = DN_CHUNK
    assert t % ch == 0
    pad = jnp.zeros((LANES - 2 * DN_HEADS,), F32)
    alog_row = jnp.concatenate([jnp.zeros((DN_HEADS,), F32), a_log, pad]).reshape(1, LANES)
    dtb_row = jnp.concatenate([jnp.zeros((DN_HEADS,), F32), dt_bias, pad]).reshape(1, LANES)
    gn_row = jnp.concatenate([norm_gain, norm_gain]).reshape(1, LANES)
    n_pairs = DN_HEADS // 2
    o, s_fin = pl.pallas_call(
        _deltanet_kernel,
        grid=(t // ch,),
        in_specs=[pl.BlockSpec((b, ch, DN_CONV_DIM), lambda j: (0, j, P_QKV // DN_CONV_DIM)),
                  pl.BlockSpec((b, ch, DN_WIDTH), lambda j: (0, j, P_Z // DN_WIDTH)),
                  pl.BlockSpec((b, ch, LANES), lambda j: (0, j, P_SMALL // LANES)),
                  pl.BlockSpec((CONV_WIDTH, DN_CONV_DIM), lambda j: (0, 0)),
                  pl.BlockSpec((1, LANES), lambda j: (0, 0)),
                  pl.BlockSpec((1, LANES), lambda j: (0, 0)),
                  pl.BlockSpec((1, LANES), lambda j: (0, 0))],
        out_specs=[pl.BlockSpec((b, ch, DN_WIDTH), lambda j: (0, j, 0)),
                   pl.BlockSpec((b * n_pairs, 2 * ch, LANES), lambda j: (0, 0, 0))],
        out_shape=[jax.ShapeDtypeStruct((b, t, DN_WIDTH), F32),
                   jax.ShapeDtypeStruct((b * n_pairs, 2 * ch, LANES), F32)],
        scratch_shapes=[pltpu.VMEM((b, ch + 8, DN_CONV_DIM), F32), pltpu.VMEM((b * n_pairs, 2 * ch, LANES), F32)],
        compiler_params=_cp(("arbitrary",)),
        name="deltanet_prompt",
    )(proj3, proj3, proj3, conv_w, alog_row, dtb_row, gn_row)
    return o, s_fin.reshape(b, n_pairs, 2 * ch, LANES)


def _pairs_to_heads(s_pairs):
    d = HEAD_DIM
    return jnp.stack([s_pairs[:, :, :d, :d], s_pairs[:, :, d:, d:]], axis=2).reshape(
        s_pairs.shape[0], DN_HEADS, d, d)


def _dn_step_prep_kernel(qkv_ref, cs_ref, sm_ref, cw_ref, alog_ref, dtb_ref, ones_ref,
                         q_ref, k_ref, v_ref, sc_ref):
    conv = qkv_ref[...] * cw_ref[CONV_WIDTH - 1:CONV_WIDTH, :]
    for w in range(CONV_WIDTH - 1):
        conv = conv + cs_ref[w] * cw_ref[w:w + 1, :]
    act = _silu(conv)
    q = act[:, 0:DN_WIDTH]
    k = act[:, DN_WIDTH:2 * DN_WIDTH]

    def seg_sum(x):
        return _dot_exact_rhs01(x, ones_ref[...])

    q_ref[...] = q * lax.rsqrt(seg_sum(q * q) + 1e-6) * (HEAD_DIM ** -0.5)
    k_ref[...] = k * lax.rsqrt(seg_sum(k * k) + 1e-6)
    v_ref[...] = act[:, 2 * DN_WIDTH:]
    sm = sm_ref[...]
    g = -jnp.exp(alog_ref[...]) * _softplus(sm + dtb_ref[...])
    lane = lax.broadcasted_iota(jnp.int32, sm.shape, 1)
    sc_ref[...] = jnp.where(lane < DN_HEADS, jax.nn.sigmoid(sm), jnp.exp(g))


def _dn_step_kernel(k_ref, q_ref, v_ref, be_ref, eg_ref, z_ref, gn_ref, s_ref, o_ref, s_out_ref):
    s_old = s_ref[0]
    k, q, v = k_ref[0], q_ref[0], v_ref[0]
    beta, eg = be_ref[0], eg_ref[0]
    ks = jnp.sum(k[:, None, :] * s_old, axis=0)
    qs = jnp.sum(q[:, None, :] * s_old, axis=0)
    qk = jnp.sum(q * k, axis=0, keepdims=True)
    v_new = beta * (v - eg * ks)
    o = eg * qs + qk * v_new
    inv = lax.rsqrt(jnp.mean(o * o, axis=0, keepdims=True) + RMS_EPS)
    o_ref[0] = o * inv * gn_ref[...] * _silu(z_ref[0])
    s_out_ref[0] = s_old * eg[None] + k[:, None, :] * v_new[None]


def deltanet_sample(proj_s, conv_state, rec_state, conv_w, a_log, dt_bias, norm_gain):
    n = proj_s.shape[0]
    hds, d = DN_HEADS, HEAD_DIM
    pad = jnp.zeros((LANES - 2 * hds,), F32)
    alog_row = jnp.concatenate([jnp.zeros((hds,), F32), a_log, pad]).reshape(1, LANES)
    dtb_row = jnp.concatenate([jnp.zeros((hds,), F32), dt_bias, pad]).reshape(1, LANES)
    head_of = jnp.arange(DN_WIDTH) // d
    ones_bd = (head_of[:, None] == head_of[None, :]).astype(BF16)
    cs = jnp.transpose(conv_state, (1, 0, 2))
    full = lambda shape: pl.BlockSpec(shape, lambda i: (0,) * len(shape))
    q, k, v, sc = pl.pallas_call(
        _dn_step_prep_kernel,
        grid=(1,),
        in_specs=[pl.BlockSpec((n, DN_CONV_DIM), lambda i: (0, P_QKV // DN_CONV_DIM)),
                  full((CONV_WIDTH - 1, n, DN_CONV_DIM)),
                  pl.BlockSpec((n, LANES), lambda i: (0, P_SMALL // LANES)),
                  full((CONV_WIDTH, DN_CONV_DIM)), full((1, LANES)), full((1, LANES)),
                  full((DN_WIDTH, DN_WIDTH))],
        out_specs=[full((n, DN_WIDTH)), full((n, DN_WIDTH)), full((n, DN_WIDTH)), full((n, LANES))],
        out_shape=[jax.ShapeDtypeStruct((n, DN_WIDTH), F32)] * 3 + [jax.ShapeDtypeStruct((n, LANES), F32)],
        compiler_params=_cp(("arbitrary",)),
        name="dn_step_prep",
    )(proj_s, cs, proj_s, conv_w, alog_row, dtb_row, ones_bd)
    t3 = lambda a: a.T.reshape(hds, d, n)
    sct = sc[:, :2 * hds].T.reshape(2, hds, 1, n)
    head_vec = pl.BlockSpec((1, d, n), lambda i: (i, 0, 0))
    head_scl = pl.BlockSpec((1, 1, n), lambda i: (i, 0, 0))
    state = pl.BlockSpec((1, d, d, n), lambda i: (i, 0, 0, 0))
    o, s_new = pl.pallas_call(
        _dn_step_kernel,
        grid=(hds,),
        in_specs=[head_vec, head_vec, head_vec, head_scl, head_scl, head_vec,
                  pl.BlockSpec((d, 1), lambda i: (0, 0)), state],
        out_specs=[head_vec, state],
        out_shape=[jax.ShapeDtypeStruct((hds, d, n), F32), jax.ShapeDtypeStruct((hds, d, d, n), F32)],
        compiler_params=_cp(("parallel",)),
        name="dn_step",
    )(t3(k), t3(q), t3(v), sct[0], sct[1], t3(proj_s[:, P_Z:P_Z + DN_WIDTH]), norm_gain.reshape(d, 1),
      jnp.transpose(rec_state, (1, 2, 3, 0)))
    return o.reshape(DN_WIDTH, n).T, jnp.transpose(s_new, (3, 0, 1, 2))


def _nsa_decode_kernel(pt_ref, q_ref, new_ref, gate_ref, win_ref, newc_ref, exp_ref, *refs, n_pages, seqs):
    del pt_ref
    o_ref, swin_ref, kc_sc = refs[2 * seqs * n_pages:2 * seqs * n_pages + 3]
    dh, r, g2, nh = HEAD_DIM, NSA_GROUP, NSA_KV_HEADS, NSA_HEADS
    past = n_pages * PAGE_SIZE
    nb = past // CMP_BLOCK
    nsel = nb + 1
    wlen = win_ref.shape[4]
    for u in range(seqs):
        for j in range(n_pages):
            kc_sc[u, j:j + 1, :] = refs[u * n_pages + j][0]
    kcv = kc_sc[...]
    new = new_ref[...]
    q8 = q_ref[...] * ATTN_SCALE
    head = lax.broadcasted_iota(jnp.int32, (1, nh, 1), 1)
    lane = lax.broadcasted_iota(jnp.int32, (1, 1, LANES), 2)
    g0 = head < r
    blk = jnp.where(lane < n_pages, 2 * lane, jnp.where(lane < nb, 2 * (lane - n_pages) + 1, lane))
    wp = lax.broadcasted_iota(jnp.int32, (1, 1, wlen + LANES), 2)
    dpos = wlen - wp
    wmask = (dpos >= 0) & (dpos < WINDOW) & (past - wlen + wp >= 0)

    def both(x):
        return jnp.where(g0, x[:, :, 0:dh], x[:, :, dh:2 * dh])

    def new_part(kind):
        return both(jnp.broadcast_to(new[:, :, 2 * kind * dh:2 * (kind + 1) * dh], (seqs, nh, 2 * dh)))

    def pages(kind):
        return jnp.stack([jnp.concatenate([refs[(seqs + u) * n_pages + j][0, kind].reshape(2 * dh, PAGE_SIZE)
                                           for j in range(n_pages)], axis=1) for u in range(seqs)])

    q_bd = jnp.concatenate([jnp.where(g0, q8, 0.0), jnp.where(g0, 0.0, q8)], axis=2)
    s_new = jnp.sum(q8 * new_part(2), axis=-1, keepdims=True)
    s_all = jnp.concatenate([_bdot(q_bd, pages(0)), jnp.broadcast_to(s_new, (seqs, nh, LANES))], axis=2)
    sw = _bdot(q_bd, win_ref[:, 0].reshape(seqs, 2 * dh, wlen))
    sw_new = jnp.sum(q8 * new_part(4), axis=-1, keepdims=True)

    def cmp_rows(base):
        even = jnp.concatenate([kcv[:, :, base:base + dh], kcv[:, :, base + LANES:base + LANES + dh]], axis=2)
        odd = jnp.concatenate([kcv[:, :, base + dh:base + LANES], kcv[:, :, base + LANES + dh:base + 2 * LANES]], axis=2)
        return jnp.concatenate([even, odd], axis=1)

    s = _bdot_nt(q_bd, cmp_rows(0))
    p = jnp.exp(s - jnp.max(s, axis=-1, keepdims=True))
    p = p / jnp.maximum(jnp.sum(p, axis=-1, keepdims=True), 1e-30)
    o_cmp = both(_bdot(p, cmp_rows(g2 * LANES)))
    bid_row = jnp.broadcast_to(blk.astype(F32), (1, LANES, LANES))
    bid_col = jnp.swapaxes(bid_row, 1, 2)
    valid = lane < nsel
    forced = valid & ((blk == 0) | (nb - blk < N_LOCAL_BLOCKS))
    sels = []
    for g in range(g2):
        in_g = (head >= g * r) & (head < (g + 1) * r)
        imp = jnp.sum(jnp.where(in_g, p, 0.0), axis=1, keepdims=True)
        imp = jnp.concatenate([imp, jnp.zeros((seqs, 1, LANES - nb), F32)], axis=2)
        score = jnp.where(valid, imp + jnp.where(forced, FORCE_SCORE, 0.0), -1.0)
        sc_row = jnp.broadcast_to(score, (seqs, LANES, LANES))
        sc_col = jnp.swapaxes(sc_row, 1, 2)
        beats = (sc_col > sc_row) | ((sc_col == sc_row) & (bid_col < bid_row))
        rank = jnp.sum(jnp.where(beats, 1.0, 0.0), axis=1, keepdims=True)
        sels.append(jnp.where(valid & (rank < min(TOP_N, nsel)), 1.0, 0.0))
    sel8 = jnp.where(g0, sels[0], sels[1]).astype(BF16)
    chosen = jnp.dot(sel8.reshape(seqs * nh, LANES), exp_ref[...], preferred_element_type=F32) > 0.5
    pm = _masked_softmax(s_all, chosen.reshape(seqs, nh, past + LANES))
    o_slc = pm[:, :, past:past + 1] * new_part(3) + both(_bdot_nt(pm[:, :, 0:past], pages(1)))
    pw = _masked_softmax(jnp.concatenate([sw, jnp.broadcast_to(sw_new, (seqs, nh, LANES))], axis=2), wmask)
    o_win = pw[:, :, wlen:wlen + 1] * new_part(5) + both(
        _bdot_nt(pw[:, :, 0:wlen], win_ref[:, 1].reshape(seqs, 2 * dh, wlen)))
    gates = jnp.broadcast_to(jax.nn.sigmoid(gate_ref[...]), (seqs, nh, LANES))

    def gate(branch):
        return jnp.sum(jnp.where(lane == 3 * head + branch, gates, 0.0), axis=-1, keepdims=True)

    o_ref[...] = gate(0) * o_cmp + gate(1) * o_slc + gate(2) * o_win
    wpos = lax.broadcasted_iota(jnp.int32, (1, 1, 1, wlen), 3)
    nlane = newc_ref.shape[1]
    slane = lax.broadcasted_iota(jnp.int32, (1, nlane), 1)
    for u in range(seqs):
        mine = slane == (pl.program_id(0) * seqs + u) % nlane
        col = jnp.sum(jnp.where(mine, newc_ref[...], 0.0), axis=1, keepdims=True)
        swin_ref[u] = jnp.where(wpos == wlen - 1, col.reshape(2, g2, dh, 1), pltpu.roll(win_ref[u], wlen - 1, 3))


def nsa_decode(page_table, q3, new_row, gate_row, win_t, new_win, kc_phys, cache_t, seqs=8):
    n, n_pages = page_table.shape
    dh = HEAD_DIM
    g2 = NSA_KV_HEADS
    wlen = win_t.shape[4]
    assert 2 * n_pages + 1 <= LANES and PAGE_SIZE == 2 * CMP_BLOCK and PAGE_SIZE == LANES and g2 == 2

    past = n_pages * PAGE_SIZE
    nb = past // CMP_BLOCK
    erow = lax.broadcasted_iota(jnp.int32, (LANES, past + LANES), 0)
    ecol = lax.broadcasted_iota(jnp.int32, (LANES, past + LANES), 1)
    page, second = ecol // PAGE_SIZE, (ecol % PAGE_SIZE) >= CMP_BLOCK
    expand = (((erow < n_pages) & (page == erow) & ~second & (ecol < past))
              | ((erow >= n_pages) & (erow < nb) & (page == erow - n_pages) & second & (ecol < past))
              | ((erow == nb) & (ecol == past))).astype(BF16)

    assert n % seqs == 0
    nlane = LANES if n % LANES == 0 else n
    assert nlane % seqs == 0

    def kc_map(u, j):
        return lambda i, pt: (pt[seqs * i + u, j], 0, 0)

    def slc_map(u, j):
        return lambda i, pt: (pt[seqs * i + u, j], 1, 0, 0, 0)

    in_specs = [pl.BlockSpec((seqs, NSA_HEADS, dh), lambda i, pt: (i, 0, 0)),
                pl.BlockSpec((seqs, 1, KV_COLS), lambda i, pt: (i, 0, 0)),
                pl.BlockSpec((seqs, 1, LANES), lambda i, pt: (i, 0, 0)),
                pl.BlockSpec((seqs, 2, g2, dh, wlen), lambda i, pt: (i, 0, 0, 0, 0)),
                pl.BlockSpec((2 * g2 * dh, nlane), lambda i, pt: (0, seqs * i // nlane)),
                pl.BlockSpec((LANES, past + LANES), lambda i, pt: (0, 0))]
    in_specs += [pl.BlockSpec((1, 1, 4 * LANES), kc_map(u, j)) for u in range(seqs) for j in range(n_pages)]
    in_specs += [pl.BlockSpec((1, 2, g2, dh, PAGE_SIZE), slc_map(u, j)) for u in range(seqs) for j in range(n_pages)]
    grid_spec = pltpu.PrefetchScalarGridSpec(
        num_scalar_prefetch=1, grid=(n // seqs,), in_specs=in_specs,
        out_specs=[pl.BlockSpec((seqs, NSA_HEADS, dh), lambda i, pt: (i, 0, 0)),
                   pl.BlockSpec((seqs, 2, g2, dh, wlen), lambda i, pt: (i, 0, 0, 0, 0))],
        scratch_shapes=[pltpu.VMEM((seqs, n_pages, 4 * LANES), F32)])
    return pl.pallas_call(
        functools.partial(_nsa_decode_kernel, n_pages=n_pages, seqs=seqs),
        grid_spec=grid_spec,
        out_shape=[jax.ShapeDtypeStruct((n, NSA_HEADS, dh), F32), jax.ShapeDtypeStruct(win_t.shape, F32)],
        compiler_params=_cp(("arbitrary",)),
        name="nsa_decode",
    )(page_table, q3, new_row, gate_row, win_t, new_win, expand,
      *([kc_phys] * (seqs * n_pages)), *([cache_t] * (seqs * n_pages)))


def _compress_pages_kernel(x_ref, wd_ref, ped_ref, b1_ref, w2_ref, o_ref):
    xt = jnp.swapaxes((x_ref[...] + ped_ref[0][None]).astype(BF16), 0, 1)
    acc = None
    for dp in range(xt.shape[0] // 2):
        d = jnp.dot(jnp.concatenate([xt[2 * dp], xt[2 * dp + 1]], axis=1), wd_ref[0, dp], preferred_element_type=F32)
        acc = d if acc is None else acc + d
    h = jnp.maximum(acc + b1_ref[0], 0.0)
    o_ref[:, 0, :] = jnp.dot(h.astype(BF16), w2_ref[0], preferred_element_type=F32)


def compress_pages(cache_t, cwp, bp):
    n_phys = cache_t.shape[0]
    g2, dh = NSA_KV_HEADS, HEAD_DIM
    assert n_phys % bp == 0 and dh % 2 == 0
    wd, ped, b1h, w2h = cwp
    rows = cache_t.reshape(n_phys, cache_t.shape[1] * g2 * dh, PAGE_SIZE)
    return pl.pallas_call(
        _compress_pages_kernel,
        grid=(2 * g2, n_phys // bp),
        in_specs=[pl.BlockSpec((bp, dh, PAGE_SIZE), lambda j, i: (i, j, 0)),
                  pl.BlockSpec((1, dh // 2, 2 * PAGE_SIZE, 2 * CMP_HIDDEN), lambda j, i: (j // g2, 0, 0, 0)),
                  pl.BlockSpec((1, dh, PAGE_SIZE), lambda j, i: (j // g2, 0, 0)),
                  pl.BlockSpec((1, 1, 2 * CMP_HIDDEN), lambda j, i: (j // g2, 0, 0)),
                  pl.BlockSpec((1, 2 * CMP_HIDDEN, LANES), lambda j, i: (j // g2, 0, 0))],
        out_specs=pl.BlockSpec((bp, 1, LANES), lambda j, i: (i, 0, j)),
        out_shape=jax.ShapeDtypeStruct((n_phys, 1, 2 * g2 * LANES), F32),
        compiler_params=_cp(("parallel", "parallel")),
        name="compress_pages",
    )(rows, wd, ped, b1h.reshape(2, 1, 2 * CMP_HIDDEN), w2h)


def _compress_page_weights(cmp_pe, cmp_w1, cmp_b1, cmp_w2):
    w1t = jnp.transpose(cmp_w1.reshape(2, CMP_BLOCK, HEAD_DIM, CMP_HIDDEN), (0, 2, 1, 3))
    z = jnp.zeros_like(w1t)
    wd = jnp.concatenate([jnp.concatenate([w1t, z], -1), jnp.concatenate([z, w1t], -1)], axis=2).astype(BF16)
    wd = wd.reshape(2, HEAD_DIM // 2, 2 * PAGE_SIZE, 2 * CMP_HIDDEN)
    pet = jnp.transpose(cmp_pe, (0, 2, 1))
    ped = jnp.concatenate([pet, pet], -1)
    b1h = jnp.concatenate([cmp_b1, cmp_b1], -1)
    z2 = jnp.zeros_like(cmp_w2)
    w2h = jnp.concatenate([jnp.concatenate([cmp_w2, z2], -1), jnp.concatenate([z2, cmp_w2], -1)], axis=1).astype(BF16)
    return wd, ped, b1h, w2h


def _split_w_in(w):
    d = w.shape[0]
    off_b = DN_CONV_DIM + DN_WIDTH
    off_q = off_b + 2 * DN_HEADS
    off_kv = off_q + NSA_WIDTH
    off_g = off_kv + KV_COLS
    n_g = 3 * NSA_HEADS
    pad = jnp.zeros((d, P_COLS - P_SMALL - 2 * DN_HEADS - n_g), w.dtype)
    main = jnp.concatenate([w[:, :off_b], w[:, off_q:off_kv], w[:, off_kv:off_kv + CMP_COLS], w[:, off_b:off_q],
                            w[:, off_g:off_g + n_g], pad], axis=1)
    return main.astype(BF16), w[:, off_kv:off_g].T.astype(BF16)


def _row_tile(n, cap):
    t = min(n, cap)
    while n % t:
        t //= 2
    return t


def _trunk_tail(x2, mixer_dn, mixer_nsa, mem_kv3, bshape, lw, final_norm):
    n, d = x2.shape
    b, t = bshape
    if t >= SUBLANES:
        x3 = mem_block(x2.reshape(b, t, d), mixer_dn.reshape(b, t, -1), mixer_nsa.reshape(b, t, -1),
                       lw["w_out_dn"], lw["w_out_nsa"], lw["ln_mem"], lw["w_mem_q"], mem_kv3, lw["w_mem_o"],
                       _row_tile(t, 512))
        x2 = x3.reshape(n, d)
    else:
        assert t == 1
        tm = _row_tile(n, 512)
        x2 = matmul_residual(x2, [mixer_dn, mixer_nsa], [lw["w_out_dn"], lw["w_out_nsa"]], tm)
        qm = rms_matmul(x2, lw["ln_mem"], lw["w_mem_q"], tm, 512)
        att = mem_attention_row(qm.reshape(n, MEM_HEADS, d // MEM_HEADS), mem_kv3).reshape(n, d)
        x2 = matmul_residual(x2, [att], [lw["w_mem_o"]], tm)
    return ffn(x2, lw["ln_ffn"], lw["w_up"], lw["w_down"], lw["ln_final"], final_norm, _row_tile(n, 1024), 1024)


def _kv_rows(kvt):
    b, rows, s = kvt.shape
    g, dh = NSA_KV_HEADS, HEAD_DIM
    return jnp.transpose(kvt.reshape(b, rows // (g * dh), g, dh, s), (0, 4, 1, 2, 3))


def _prompt_layer(xp, mem_prompt, lw, cw, final_norm):
    b, s, d = xp.shape
    n = b * s
    x2 = xp.reshape(n, d)
    proj, kvt, kvt_win = input_projection(xp, lw["ln_mix"], lw["w_in"], lw["w_kvt"], _row_tile(s, 1024), 1024)
    proj3 = proj.reshape(b, s, P_COLS)
    dn_out, s_pairs = deltanet_prompt(proj3, lw["dn_conv_w"], lw["dn_a_log"], lw["dn_dt_bias"], lw["dn_norm"])
    p_conv = proj3[:, s - (CONV_WIDTH - 1):, P_QKV:P_QKV + DN_CONV_DIM]
    p_rec = _pairs_to_heads(s_pairs)
    nb = s // CMP_BLOCK
    kcv = compress_blocks(proj3[:, :nb * CMP_BLOCK].reshape(b * nb, CMP_BLOCK, P_COLS), P_CMP, cw,
                          _row_tile(b * nb, 256))
    kcv = kcv.reshape(b, nb, 2, NSA_KV_HEADS, HEAD_DIM)
    kc = jnp.transpose(kcv[:, :, 0], (0, 2, 1, 3))
    vct = jnp.transpose(kcv[:, :, 1], (0, 2, 3, 1))
    ocmpt, selt = nsa_cmp_topk(proj3, kc, vct, _row_tile(s, 512))
    nsa_out = nsa_attention(proj3, kvt, kvt_win, selt, ocmpt, 256, 512).reshape(n, NSA_WIDTH)
    m = mem_prompt.shape[1]
    mem_kv = rms_matmul(mem_prompt.reshape(b * m, d), lw["ln_memkv"], lw["w_mem_kv"], _row_tile(b * m, 512), 512)
    mem_kv5 = mem_kv.reshape(b, m, 2, MEM_HEADS, d // MEM_HEADS)
    y = _trunk_tail(x2, dn_out.reshape(n, DN_WIDTH), nsa_out, mem_kv.reshape(b, m, 2 * d), (b, s), lw, final_norm)
    wk = min(WINDOW, s)
    return y.reshape(b, s, d), _kv_rows(kvt), _kv_rows(kvt_win[:, :, s - wk:]), mem_kv5, p_conv, p_rec


def _sample_layer(xs, cache_nsa, cache_win, cache_mem, conv_state, rec_state, page_table, lw, cwp, final_norm):
    db, ds, d = xs.shape
    assert ds == 1
    n = db
    x2 = xs.reshape(n, d)
    proj, kvt, kvt_win = input_projection(x2[None], lw["ln_mix"], lw["w_in"], lw["w_kvt"], n, 1024)
    dn_out, s_rec = deltanet_sample(proj, conv_state, rec_state, lw["dn_conv_w"], lw["dn_a_log"], lw["dn_dt_bias"],
                                    lw["dn_norm"])
    s_conv = jnp.concatenate([conv_state[:, 1:], proj[:, None, P_QKV:P_QKV + DN_CONV_DIM]], axis=1)
    n_phys = cache_nsa.shape[0]
    cache_t = jnp.transpose(cache_nsa, (0, 2, 3, 4, 1))
    win_t = jnp.transpose(cache_win, (0, 2, 3, 4, 1))
    kc_phys = compress_pages(cache_t, cwp, _row_tile(n_phys, 256))
    kv_new = jnp.concatenate([kvt[0], kvt_win[0]], axis=0).T
    o8, s_win_t = nsa_decode(page_table, proj[:, P_NQ:P_NQ + NSA_WIDTH].reshape(n, NSA_HEADS, HEAD_DIM),
                             kv_new.reshape(n, 1, KV_COLS),
                             _flat_gates(proj[:, P_SMALL:P_SMALL + LANES]).reshape(n, 1, LANES),
                             win_t, kvt_win[0], kc_phys, cache_t)
    s_nsa = jnp.transpose(kvt.reshape(-1, NSA_KV_HEADS, HEAD_DIM, n), (3, 0, 1, 2))[:, None]
    s_win = jnp.transpose(s_win_t, (0, 4, 1, 2, 3))
    y = _trunk_tail(x2, dn_out, o8.reshape(n, NSA_WIDTH), cache_mem, (n, 1), lw, final_norm)
    return y.reshape(db, ds, d), s_nsa, s_win, s_conv, s_rec


def _flat_gates(small):
    g0 = 2 * DN_HEADS
    n_g = 3 * NSA_HEADS
    gl = small[..., g0:g0 + n_g]
    return jnp.concatenate([gl, jnp.zeros(gl.shape[:-1] + (LANES - n_g,), gl.dtype)], axis=-1)


def kernel(x_prompt, x_sample, mem_prompt, cache_nsa_kv, cache_win_kv, cache_mem_kv, state_dn_conv, state_dn_rec, page_table, ln_mix, w_in, dn_conv_w, dn_a_log, dn_dt_bias, dn_norm, cmp_pe, cmp_w1, cmp_b1, cmp_w2, w_out, ln_mem, ln_memkv, w_mem_q, w_mem_kv, w_mem_o, ln_ffn, w_up, w_down, ln_final):
    depth = w_in.shape[0]
    xp, xs = x_prompt, x_sample
    outs_p = [[] for _ in range(5)]
    outs_s = [[] for _ in range(4)]
    for l in range(depth):
        lw = {
            "ln_mix": ln_mix[l],
            "dn_conv_w": dn_conv_w[l], "dn_a_log": dn_a_log[l], "dn_dt_bias": dn_dt_bias[l], "dn_norm": dn_norm[l],
            "w_out_dn": w_out[l][:DN_WIDTH].astype(BF16), "w_out_nsa": w_out[l][DN_WIDTH:].astype(BF16),
            "ln_mem": ln_mem[l], "ln_memkv": ln_memkv[l], "w_mem_q": w_mem_q[l].astype(BF16),
            "w_mem_kv": w_mem_kv[l].astype(BF16), "w_mem_o": w_mem_o[l].astype(BF16),
            "ln_ffn": ln_ffn[l], "w_up": w_up[l].astype(BF16), "w_down": w_down[l].astype(BF16),
            "ln_final": ln_final,
        }
        lw["w_in"], lw["w_kvt"] = _split_w_in(w_in[l])
        cw = _compress_weights(cmp_pe[l], cmp_w1[l], cmp_b1[l], cmp_w2[l])
        last = l == depth - 1
        xp, p_nsa, p_win, p_mem, p_conv, p_rec = _prompt_layer(xp, mem_prompt, lw, cw, last)
        for acc, val in zip(outs_p, (p_nsa, p_win, p_mem, p_conv, p_rec)):
            acc.append(val)
        xs, s_nsa, s_win, s_conv, s_rec = _sample_layer(
            xs, cache_nsa_kv[l], cache_win_kv[l], cache_mem_kv[l], state_dn_conv[l], state_dn_rec[l],
            page_table, lw, _compress_page_weights(cmp_pe[l], cmp_w1[l], cmp_b1[l], cmp_w2[l]), last)
        for acc, val in zip(outs_s, (s_nsa, s_win, s_conv, s_rec)):
            acc.append(val)
    return (xp, xs) + tuple(jnp.stack(a) for a in outs_p) + tuple(jnp.stack(a) for a in outs_s)
```
